```python
import math
import jax, jax.numpy as jnp
from jax import lax
import numpy as np

D_MODEL = 1024
BATCH = 8
SEQ = 8192
DEPTH = 2
DEC_BATCH = 32
DEC_SEQ = 64
PAST_LEN = 4096

CHUNK = 64
EPS = 1e-6
BRANCH_W = D_MODEL // 2
GLA_HEADS = 4
GLA_DK = BRANCH_W // 8
GLA_DV = BRANCH_W // GLA_HEADS
GLA_LOWRANK = 16
GLA_TAU = 16.0
GDN_HEADS = 4
GDN_DK = BRANCH_W // GDN_HEADS
GDN_DV = BRANCH_W // GDN_HEADS
CONV_W = 4
S5_WIDTH = BRANCH_W
S5_GROUP = 16
S5_GROUPS = S5_WIDTH // S5_GROUP
S5_STATE = 64
GLA_QK = GLA_HEADS * GLA_DK
GLA_V = GLA_HEADS * GLA_DV
GDN_QK = GDN_HEADS * GDN_DK
GDN_V = GDN_HEADS * GDN_DV
GDN_CONV_CH = 2 * GDN_QK + GDN_V
IN_SIZES = (GLA_QK, GLA_QK, GLA_V, GLA_V, GLA_LOWRANK, GDN_CONV_CH, GDN_HEADS, GDN_HEADS, GDN_V, S5_WIDTH)
D_IN = 2 * GLA_QK + 2 * GLA_V + GLA_LOWRANK + GDN_CONV_CH + 2 * GDN_HEADS + GDN_V + S5_WIDTH
MOE_GROUPS = 4
EXPERTS_PER_GROUP = 4
N_EXPERTS = MOE_GROUPS * EXPERTS_PER_GROUP
TOP_K = 2
D_FF_EXPERT = D_MODEL // 4

kernel_name = 'hybrid_gla_gdn_s5_hmoe_streaming_step'


def rmsnorm(x, w):
    xf = x.astype(jnp.float32)
    y = xf * lax.rsqrt(jnp.mean(xf * xf, axis=-1, keepdims=True) + EPS)
    return (y * w.astype(jnp.float32)).astype(x.dtype)


def l2norm(x):
    xf = x.astype(jnp.float32)
    return xf * lax.rsqrt(jnp.sum(xf * xf, axis=-1, keepdims=True) + EPS)


def split_cols(z, sizes):
    idx, acc = [], 0
    for s in sizes[:-1]:
        acc += s
        idx.append(acc)
    return jnp.split(z, idx, axis=-1)


def to_chunks(t, L):
    B, T, H, d = t.shape
    return t.reshape(B, T // L, L, H, d).transpose(1, 0, 3, 2, 4)


def from_chunks(t):
    n, B, H, L, d = t.shape
    return t.transpose(1, 0, 3, 2, 4).reshape(B, n * L, H, d)


def gla_recurrence(q, k, v, log_a, s0):
    T = q.shape[1]
    L = min(CHUNK, T)
    causal = jnp.tril(jnp.ones((L, L), dtype=bool))

    def step(S, inp):
        qc, kc, vc, ac = inp
        b = jnp.cumsum(ac, axis=2)
        diff = b[:, :, :, None, :] - b[:, :, None, :, :]
        decay = jnp.exp(jnp.where(causal[:, :, None], diff, -jnp.inf))
        att = jnp.einsum('bhtd,bhsd,bhtsd->bhts', qc, kc, decay)
        o = jnp.einsum('bhts,bhsv->bhtv', att, vc) + jnp.einsum('bhtd,bhdv->bhtv', qc * jnp.exp(b), S)
        b_last = b[:, :, -1:, :]
        S = S * jnp.exp(b_last[:, :, 0, :, None]) + jnp.einsum('bhsd,bhsv->bhdv', kc * jnp.exp(b_last - b), vc)
        return S, o

    f = lambda t: to_chunks(t.astype(jnp.float32), L)
    S, o = lax.scan(step, s0.astype(jnp.float32), (f(q), f(k), f(v), f(log_a)))
    return from_chunks(o), S


def gdn_recurrence(q, k, v, beta, log_g, s0):
    T = q.shape[1]
    L = min(CHUNK, T)
    causal = jnp.tril(jnp.ones((L, L), dtype=bool))
    strict = jnp.tril(jnp.ones((L, L), dtype=bool), k=-1)
    eye = jnp.eye(L, dtype=jnp.float32)

    def step(S, inp):
        qc, kc, vc, bc, gc = inp
        G = jnp.cumsum(gc, axis=-1)
        dec = jnp.exp(jnp.where(causal, G[..., :, None] - G[..., None, :], -jnp.inf))
        a = jnp.where(strict, bc[..., :, None] * jnp.einsum('bhtd,bhsd->bhts', kc, kc) * dec, 0.0)
        rhs = jnp.concatenate([bc[..., None] * vc, (bc * jnp.exp(G))[..., None] * kc], axis=-1)
        sol = lax.linalg.triangular_solve(a + eye, rhs, left_side=True, lower=True, unit_diagonal=True)
        w_v, w_k = sol[..., :GDN_DV], sol[..., GDN_DV:]
        u = w_v - jnp.einsum('bhtd,bhdv->bhtv', w_k, S)
        qk = jnp.einsum('bhtd,bhsd->bhts', qc, kc) * dec
        o = jnp.einsum('bhts,bhsv->bhtv', qk, u) + jnp.einsum('bhtd,bhdv->bhtv', qc * jnp.exp(G)[..., None], S)
        g_last = G[..., -1:]
        S = S * jnp.exp(g_last)[..., None] + jnp.einsum('bhsd,bhsv->bhdv', kc * jnp.exp(g_last - G)[..., None], u)
        return S, o

    f = lambda t: to_chunks(t.astype(jnp.float32), L)
    fs = lambda t: to_chunks(t.astype(jnp.float32)[..., None], L)[..., 0]
    S, o = lax.scan(step, s0.astype(jnp.float32), (f(q), f(k), f(v), fs(beta), fs(log_g)))
    return from_chunks(o), S


def s5_scan(u, a_re, a_im, log_dt, b_re, b_im, c_re, c_im, d, h0_re, h0_im):
    B, T, _ = u.shape
    L = min(CHUNK, T)
    f32 = jnp.float32
    lam = lax.complex(a_re.astype(f32), a_im.astype(f32))
    lam_bar = jnp.exp(lam * jnp.exp(log_dt.astype(f32))[:, None])
    b_bar = ((lam_bar - 1.0) / lam)[..., None] * lax.complex(b_re.astype(f32), b_im.astype(f32))
    c = lax.complex(c_re.astype(f32), c_im.astype(f32))
    u_blocks = jnp.moveaxis(u.astype(f32).reshape(B, T // L, L, S5_GROUPS, S5_GROUP), 1, 0)

    def combine(e1, e2):
        a1, b1 = e1
        a2, b2 = e2
        return a1 * a2, a2 * b1 + b2

    def step(h, u_blk):
        bu = jnp.einsum('gpc,blgc->blgp', b_bar, u_blk.astype(jnp.complex64))
        a_cum, h_loc = lax.associative_scan(combine, (jnp.broadcast_to(lam_bar, bu.shape), bu), axis=1)
        hs = h_loc + a_cum * h[:, None]
        y = jnp.einsum('gcp,blgp->blgc', c, hs).real
        return hs[:, -1], y

    h0 = lax.complex(h0_re.astype(f32), h0_im.astype(f32))
    h, y = lax.scan(step, h0, u_blocks)
    y = jnp.moveaxis(y, 0, 1).reshape(B, T, S5_WIDTH) + d.astype(f32) * u.astype(f32)
    return y, jnp.real(h), jnp.imag(h)


def mixer(h, p, s_gla, s_gdn, conv_buf, s5_re, s5_im):
    B, T, _ = h.shape
    f32 = jnp.float32
    z = h @ p['w_in']
    qa, ka, va, ga, aa, qkv_b, beta_b, a_b, gb, u_c = split_cols(z, IN_SIZES)
    qa = qa.reshape(B, T, GLA_HEADS, GLA_DK) * GLA_DK ** -0.5
    ka = ka.reshape(B, T, GLA_HEADS, GLA_DK)
    va = va.reshape(B, T, GLA_HEADS, GLA_DV)
    log_a = jax.nn.log_sigmoid((aa @ p['w_alpha_up'] + p['b_alpha']).astype(f32)) / GLA_TAU
    o_a, s_gla_new = gla_recurrence(qa, ka, va, log_a.reshape(B, T, GLA_HEADS, GLA_DK), s_gla)
    o_a = rmsnorm(o_a, p['gla_norm_w']) * jax.nn.silu(ga.reshape(B, T, GLA_HEADS, GLA_DV).astype(f32))
    y_a = o_a.reshape(B, T, GLA_V).astype(h.dtype) @ p['w_br_a']
    xp = jnp.concatenate([conv_buf.astype(qkv_b.dtype), qkv_b], axis=1)
    conv = xp[:, 0:T] * p['conv_w'][0]
    for j in range(1, CONV_W):
        conv = conv + xp[:, j:j + T] * p['conv_w'][j]
    conv_buf_new = xp[:, T:]
    qb, kb, vb = split_cols(jax.nn.silu(conv), (GDN_QK, GDN_QK, GDN_V))
    qb = l2norm(qb.reshape(B, T, GDN_HEADS, GDN_DK)) * GDN_DK ** -0.5
    kb = l2norm(kb.reshape(B, T, GDN_HEADS, GDN_DK))
    vb = vb.reshape(B, T, GDN_HEADS, GDN_DV)
    beta = jax.nn.sigmoid(beta_b.astype(f32))
    log_g = -jnp.exp(p['gdn_a_log'].astype(f32)) * jax.nn.softplus(a_b.astype(f32) + p['gdn_dt_bias'].astype(f32))
    o_b, s_gdn_new = gdn_recurrence(qb, kb, vb, beta, log_g, s_gdn)
    o_b = rmsnorm(o_b, p['gdn_norm_w']) * jax.nn.silu(gb.reshape(B, T, GDN_HEADS, GDN_DV).astype(f32))
    y_b = o_b.reshape(B, T, GDN_V).astype(h.dtype) @ p['w_br_b']
    y_c, s5_re_new, s5_im_new = s5_scan(u_c, p['s5_a_re'], p['s5_a_im'], p['s5_log_dt'], p['s5_b_re'], p['s5_b_im'],
                                        p['s5_c_re'], p['s5_c_im'], p['s5_d'], s5_re, s5_im)
    y_c = jax.nn.gelu(y_c).astype(h.dtype)
    y_c = (y_c * jax.nn.sigmoid(y_c @ p['w_glu'] + p['b_glu'])) @ p['w_br_c']
    g_a, g_b, g_c = jnp.split(jax.nn.sigmoid(h @ p['w_gate'] + p['b_gate']), 3, axis=-1)
    out = (g_a * y_a + g_b * y_b + g_c * y_c) @ p['w_out']
    return out, (s_gla_new, s_gdn_new, conv_buf_new, s5_re_new, s5_im_new)


def hier_moe(h, p):
    B, T, D = h.shape
    hf = h.reshape(B * T, D)
    g_logits = (hf @ p['w_route_group'] + p['b_route_group']).astype(jnp.float32)
    p_group = jax.nn.softmax(g_logits, axis=-1)
    g_sel = jnp.argmax(g_logits, axis=-1)
    e_logits = (hf @ p['w_route_expert'] + p['b_route_expert']).astype(jnp.float32)
    e_logits = e_logits.reshape(B * T, MOE_GROUPS, EXPERTS_PER_GROUP)
    in_group = jnp.take_along_axis(e_logits, g_sel[:, None, None], axis=1)[:, 0]
    top_v, top_i = lax.top_k(in_group, TOP_K)
    w = jnp.take_along_axis(p_group, g_sel[:, None], axis=1) * jax.nn.softmax(top_v, axis=-1)
    e_idx = g_sel[:, None] * EXPERTS_PER_GROUP + top_i
    combine = jnp.einsum('nk,nke->ne', w, jax.nn.one_hot(e_idx, N_EXPERTS, dtype=jnp.float32)).astype(h.dtype)
    out = jnp.zeros_like(hf)
    for e in range(N_EXPERTS):
        he = jax.nn.silu(hf @ p['w_exp_gate'][e]) * (hf @ p['w_exp_up'][e])
        out = out + combine[:, e:e + 1] * (he @ p['w_exp_down'][e])
    return out.reshape(B, T, D)


def layer(x, c, p, st):
    mod = jax.nn.silu(c) @ p['w_ada'] + p['b_ada']
    sh1, sc1, g1, sh2, sc2, g2 = jnp.split(mod[:, None, :], 6, axis=-1)
    h = rmsnorm(x, p['norm1_w']) * (1.0 + sc1) + sh1
    m, new_st = mixer(h, p, *st)
    x = x + g1 * m
    h = rmsnorm(x, p['norm2_w']) * (1.0 + sc2) + sh2
    x = x + g2 * hier_moe(h, p)
    return x, new_st


def trunk(x, c, params, states, final_norm_w):
    new = []
    for l in range(DEPTH):
        x, st = layer(x, c, params[l], states[l])
        new.append(st)
    y = rmsnorm(x, final_norm_w)
    return y, [jnp.stack([new[l][i] for l in range(DEPTH)]) for i in range(5)]


def setup_inputs(seed: int = 0) -> dict:
    key = jax.random.key(seed)
    keys = jax.random.split(key, 64)
    kit = iter([keys[i] for i in range(64)])
    f32 = jnp.float32
    D = D_MODEL

    def nrm(shape, scale):
        return jax.random.normal(next(kit), shape, f32) * scale

    def uni(shape, lo, hi):
        return jax.random.uniform(next(kit), shape, f32, lo, hi)

    gdn_dt = jnp.exp(uni((DEPTH, GDN_HEADS), math.log(1e-3), math.log(1e-1)))
    return {
        'x_prompt': nrm((BATCH, SEQ, D), 1.0),
        'x_sample': nrm((DEC_BATCH, DEC_SEQ, D), 1.0),
        'c_prompt': nrm((BATCH, D), 1.0),
        'c_sample': nrm((DEC_BATCH, D), 1.0),
        'state_gla': nrm((DEPTH, DEC_BATCH, GLA_HEADS, GLA_DK, GLA_DV), 0.1),
        'state_gdn': nrm((DEPTH, DEC_BATCH, GDN_HEADS, GDN_DK, GDN_DV), 0.1),
        'cache_gdn_conv': nrm((DEPTH, DEC_BATCH, CONV_W - 1, GDN_CONV_CH), 1.0),
        'state_s5_re': nrm((DEPTH, DEC_BATCH, S5_GROUPS, S5_STATE), 0.1),
        'state_s5_im': nrm((DEPTH, DEC_BATCH, S5_GROUPS, S5_STATE), 0.1),
        'norm1_w': 1.0 + nrm((DEPTH, D), 0.01),
        'norm2_w': 1.0 + nrm((DEPTH, D), 0.01),
        'final_norm_w': 1.0 + nrm((D,), 0.01),
        'w_ada': nrm((DEPTH, D, 6 * D), 0.5 * D ** -0.5),
        'b_ada': nrm((DEPTH, 6 * D), 0.01),
        'w_in': nrm((DEPTH, D, D_IN), D ** -0.5),
        'w_alpha_up': nrm((DEPTH, GLA_LOWRANK, GLA_QK), GLA_LOWRANK ** -0.5),
        'b_alpha': nrm((DEPTH, GLA_QK), 0.1),
        'gla_norm_w': 1.0 + nrm((DEPTH, GLA_DV), 0.01),
        'conv_w': nrm((DEPTH, CONV_W, GDN_CONV_CH), CONV_W ** -0.5),
        'gdn_a_log': jnp.log(uni((DEPTH, GDN_HEADS), 1.0, 16.0)),
        'gdn_dt_bias': gdn_dt + jnp.log(-jnp.expm1(-gdn_dt)),
        'gdn_norm_w': 1.0 + nrm((DEPTH, GDN_DV), 0.01),
        's5_a_re': -0.5 + nrm((DEPTH, S5_GROUPS, S5_STATE), 0.01),
        's5_a_im': jnp.pi * jnp.arange(S5_STATE, dtype=f32) + nrm((DEPTH, S5_GROUPS, S5_STATE), 0.01),
        's5_log_dt': uni((DEPTH, S5_GROUPS), math.log(1e-3), math.log(1e-1)),
        's5_b_re': nrm((DEPTH, S5_GROUPS, S5_STATE, S5_GROUP), (2 * S5_GROUP) ** -0.5),
        's5_b_im': nrm((DEPTH, S5_GROUPS, S5_STATE, S5_GROUP), (2 * S5_GROUP) ** -0.5),
        's5_c_re': nrm((DEPTH, S5_GROUPS, S5_GROUP, S5_STATE), (2 * S5_STATE) ** -0.5),
        's5_c_im': nrm((DEPTH, S5_GROUPS, S5_GROUP, S5_STATE), (2 * S5_STATE) ** -0.5),
        's5_d': nrm((DEPTH, S5_WIDTH), 1.0),
        'w_glu': nrm((DEPTH, S5_WIDTH, S5_WIDTH), S5_WIDTH ** -0.5),
        'b_glu': nrm((DEPTH, S5_WIDTH), 0.01),
        'w_br_a': nrm((DEPTH, GLA_V, D), GLA_V ** -0.5),
        'w_br_b': nrm((DEPTH, GDN_V, D), GDN_V ** -0.5),
        'w_br_c': nrm((DEPTH, S5_WIDTH, D), S5_WIDTH ** -0.5),
        'w_gate': nrm((DEPTH, D, 3 * D), D ** -0.5),
        'b_gate': nrm((DEPTH, 3 * D), 0.01),
        'w_out': nrm((DEPTH, D, D), D ** -0.5),
        'w_route_group': nrm((DEPTH, D, MOE_GROUPS), D ** -0.5),
        'b_route_group': nrm((DEPTH, MOE_GROUPS), 0.01),
        'w_route_expert': nrm((DEPTH, D, N_EXPERTS), D ** -0.5),
        'b_route_expert': nrm((DEPTH, N_EXPERTS), 0.01),
        'w_exp_gate': nrm((DEPTH, N_EXPERTS, D, D_FF_EXPERT), D ** -0.5),
        'w_exp_up': nrm((DEPTH, N_EXPERTS, D, D_FF_EXPERT), D ** -0.5),
        'w_exp_down': nrm((DEPTH, N_EXPERTS, D_FF_EXPERT, D), D_FF_EXPERT ** -0.5),
    }


def reference(x_prompt, x_sample, c_prompt, c_sample, state_gla, state_gdn, cache_gdn_conv, state_s5_re, state_s5_im,
              norm1_w, norm2_w, final_norm_w, w_ada, b_ada, w_in, w_alpha_up, b_alpha, gla_norm_w, conv_w,
              gdn_a_log, gdn_dt_bias, gdn_norm_w, s5_a_re, s5_a_im, s5_log_dt, s5_b_re, s5_b_im, s5_c_re, s5_c_im,
              s5_d, w_glu, b_glu, w_br_a, w_br_b, w_br_c, w_gate, b_gate, w_out, w_route_group, b_route_group,
              w_route_expert, b_route_expert, w_exp_gate, w_exp_up, w_exp_down):
    params = [dict(norm1_w=norm1_w[l], norm2_w=norm2_w[l], w_ada=w_ada[l], b_ada=b_ada[l], w_in=w_in[l],
                   w_alpha_up=w_alpha_up[l], b_alpha=b_alpha[l], gla_norm_w=gla_norm_w[l], conv_w=conv_w[l],
                   gdn_a_log=gdn_a_log[l], gdn_dt_bias=gdn_dt_bias[l], gdn_norm_w=gdn_norm_w[l],
                   s5_a_re=s5_a_re[l], s5_a_im=s5_a_im[l], s5_log_dt=s5_log_dt[l], s5_b_re=s5_b_re[l],
                   s5_b_im=s5_b_im[l], s5_c_re=s5_c_re[l], s5_c_im=s5_c_im[l], s5_d=s5_d[l], w_glu=w_glu[l],
                   b_glu=b_glu[l], w_br_a=w_br_a[l], w_br_b=w_br_b[l], w_br_c=w_br_c[l], w_gate=w_gate[l],
                   b_gate=b_gate[l], w_out=w_out[l], w_route_group=w_route_group[l],
                   b_route_group=b_route_group[l], w_route_expert=w_route_expert[l],
                   b_route_expert=b_route_expert[l], w_exp_gate=w_exp_gate[l], w_exp_up=w_exp_up[l],
                   w_exp_down=w_exp_down[l])
              for l in range(DEPTH)]
    bp = x_prompt.shape[0]
    f32 = jnp.float32
    prompt_states = [(jnp.zeros((bp, GLA_HEADS, GLA_DK, GLA_DV), f32),
                      jnp.zeros((bp, GDN_HEADS, GDN_DK, GDN_DV), f32),
                      jnp.zeros((bp, CONV_W - 1, GDN_CONV_CH), x_prompt.dtype),
                      jnp.zeros((bp, S5_GROUPS, S5_STATE), f32),
                      jnp.zeros((bp, S5_GROUPS, S5_STATE), f32)) for _ in range(DEPTH)]
    y_prompt, new_p = trunk(x_prompt, c_prompt, params, prompt_states, final_norm_w)
    gla_p, gdn_p, conv_p, s5re_p, s5im_p = new_p
    sample_states = [(state_gla[l], state_gdn[l], cache_gdn_conv[l], state_s5_re[l], state_s5_im[l])
                     for l in range(DEPTH)]
    y_sample, new_s = trunk(x_sample, c_sample, params, sample_states, final_norm_w)
    gla_s, gdn_s, conv_s, s5re_s, s5im_s = new_s
    return (y_prompt, y_sample, gla_p, gdn_p, conv_p, s5re_p, s5im_p, gla_s, gdn_s, conv_s, s5re_s, s5im_s)
```

```python
import functools
import math

import jax
import jax.numpy as jnp
from jax import lax
from jax.experimental import pallas as pl
from jax.experimental.pallas import tpu as pltpu

F32 = jnp.float32
BF16 = jnp.bfloat16
HIGHEST = lax.Precision.HIGHEST

D_MODEL = 1024
EPS = 1e-6
CHUNK = 64
BRANCH_W = 512
GLA_HEADS, GLA_DK, GLA_DV, GLA_LOWRANK, GLA_TAU = 4, 64, 128, 16, 16.0
GDN_HEADS, GDN_DK, GDN_DV, CONV_W = 4, 128, 128, 4
S5_GROUP, S5_GROUPS, S5_STATE = 16, 32, 64
GLA_QK, GLA_V, GDN_QK, GDN_V = 256, 512, 512, 512
GDN_CONV_CH = 1536
MOE_GROUPS, EXPERTS_PER_GROUP, N_EXPERTS, D_FF = 4, 4, 16, 256

LANES = 128
SUBLANES = 8
VMEM_LIMIT = 56 * 1024 * 1024
S5_HALF = S5_GROUPS * S5_STATE
SMALL_BETA0, SMALL_A0 = 16, 20
ROUTE_E0 = 4


def _mm(a, b):
    return jnp.dot(a.astype(BF16), b.astype(BF16), preferred_element_type=F32)


def _mm_nt(a, b):
    return lax.dot_general(a.astype(BF16), b.astype(BF16), (((1,), (1,)), ((), ())), preferred_element_type=F32)


def _mm_tn(a, b):
    return lax.dot_general(a.astype(BF16), b.astype(BF16), (((0,), (0,)), ((), ())), preferred_element_type=F32)


def _mm_f32(a, b):
    return jnp.dot(a, b, preferred_element_type=F32, precision=HIGHEST)


def _mm_nt_f32(a, b):
    return lax.dot_general(a, b, (((1,), (1,)), ((), ())), preferred_element_type=F32, precision=HIGHEST)


def _sigmoid(x):
    return 1.0 / (1.0 + jnp.exp(-x))


def _silu(x):
    return x * _sigmoid(x)


def _softplus(x):
    return jnp.maximum(x, 0.0) + jnp.log1p(jnp.exp(-jnp.abs(x)))


def _rms_rows(x, w):
    ms = jnp.mean(x * x, axis=-1, keepdims=True)
    return x * lax.rsqrt(ms + EPS) * w


def _gelu_tanh(x):
    c = math.sqrt(2.0 / math.pi)
    return x * (0.5 * (1.0 + jnp.tanh(c * (x + 0.044715 * (x * x * x)))))


def _const_spec(shape):
    nd = len(shape)
    return pl.BlockSpec(shape, lambda *_: (0,) * nd)


def _params(sem):
    return pltpu.CompilerParams(dimension_semantics=sem, vmem_limit_bytes=VMEM_LIMIT)


def _ada_kernel(c_ref, w_ref, b_ref, o_ref):
    c = c_ref[...]
    o_ref[0] = _mm(_silu(c), w_ref[0]) + b_ref[0]


def _ada_call(c_all, w_ada, b_ada):
    depth = w_ada.shape[0]
    nrow = c_all.shape[0]
    return pl.pallas_call(
        _ada_kernel,
        grid=(depth, 6),
        in_specs=[pl.BlockSpec((nrow, D_MODEL), lambda l, j: (0, 0)),
                  pl.BlockSpec((1, D_MODEL, D_MODEL), lambda l, j: (l, 0, j)),
                  pl.BlockSpec((1, 1, D_MODEL), lambda l, j: (l, 0, j))],
        out_specs=pl.BlockSpec((1, nrow, D_MODEL), lambda l, j: (l, 0, j)),
        out_shape=jax.ShapeDtypeStruct((depth, nrow, 6 * D_MODEL), F32),
        compiler_params=_params(("parallel", "parallel")),
        name="adaln",
    )(c_all, w_ada, b_ada.reshape(depth, 1, 6 * D_MODEL))


def _inproj_kernel(x_ref, mod_ref, nw_ref, wg_ref, wd_ref, ws_ref, wsm_ref, wup_ref, bal_ref, sp_ref,
                   zg_ref, zd_ref, zs_ref):
    nb, lt, d = x_ref.shape
    x = x_ref[...]
    h = _rms_rows(x, nw_ref[...]) * (1.0 + mod_ref[:, 1:2, :]) + mod_ref[:, 0:1, :]
    h2 = h.reshape(nb * lt, d).astype(BF16)

    zg = _mm(h2, wg_ref[...])
    zsm = _mm(h2, wsm_ref[...])
    la_pre = _mm(zsm, wup_ref[...]) + bal_ref[...]
    log_a = (jnp.minimum(la_pre, 0.0) - jnp.log1p(jnp.exp(-jnp.abs(la_pre)))) * (1.0 / GLA_TAU)
    zg_ref[:, :, 0:512] = zg[:, 0:512].reshape(nb, lt, 512)
    zg_ref[:, :, 512:768] = log_a.reshape(nb, lt, 256)
    zg_ref[:, :, 768:1792] = zg[:, 512:1536].reshape(nb, lt, 1024)

    zd_ref[:, :, 0:2048] = _mm(h2, wd_ref[...]).reshape(nb, lt, 2048)
    lane = lax.broadcasted_iota(jnp.int32, zsm.shape, 1)
    beta = _sigmoid(zsm)
    log_g = -sp_ref[0:1, :] * _softplus(zsm + sp_ref[1:2, :])
    is_beta = (lane >= SMALL_BETA0) & (lane < SMALL_A0)
    is_a = (lane >= SMALL_A0) & (lane < SMALL_A0 + GDN_HEADS)
    small = jnp.where(is_beta, beta, jnp.where(is_a, log_g, 0.0))
    zd_ref[:, :, 2048:2176] = small.reshape(nb, lt, LANES)

    zs_ref[...] = _mm(h2, ws_ref[...]).reshape(nb, lt, BRANCH_W)


def _inproj_call(x, mod, p, nb, lt):
    B, T, D = x.shape
    grid = (B // nb, T // lt)
    tok = lambda w: pl.BlockSpec((nb, lt, w), lambda b, t: (b, t, 0))
    return pl.pallas_call(
        _inproj_kernel,
        grid=grid,
        in_specs=[tok(D),
                  pl.BlockSpec((nb, 6, D), lambda b, t: (b, 0, 0)),
                  _const_spec((1, D)),
                  _const_spec(p["w_gla"].shape), _const_spec(p["w_gdn"].shape), _const_spec(p["w_s5"].shape),
                  _const_spec(p["w_small"].shape), _const_spec(p["w_up"].shape), _const_spec((1, GLA_QK)),
                  _const_spec((2, LANES))],
        out_specs=[tok(1792), tok(2176), tok(BRANCH_W)],
        out_shape=[jax.ShapeDtypeStruct((B, T, 1792), F32),
                   jax.ShapeDtypeStruct((B, T, 2176), F32),
                   jax.ShapeDtypeStruct((B, T, BRANCH_W), F32)],
        compiler_params=_params(("parallel", "parallel")),
        name="inproj",
    )(x, mod, p["norm1_w"], p["w_gla"], p["w_gdn"], p["w_s5"], p["w_small"], p["w_up"], p["b_alpha"],
      p["gdn_small"])


def _gla_kernel(z_ref, s0_ref, nw_ref, o_ref, s_ref, *, n_chunks):
    L = CHUNK

    @pl.when(pl.program_id(1) == 0)
    def _():
        s_ref[...] = s0_ref[...]

    row = lax.broadcasted_iota(jnp.int32, (L, L), 0)
    col = lax.broadcasted_iota(jnp.int32, (L, L), 1)
    causal = row >= col
    tril = causal.astype(F32)
    lane = lax.broadcasted_iota(jnp.int32, (L, LANES), 1)
    own_half = (lane < GLA_DK, lane >= GLA_DK)
    nw = nw_ref[...]

    def chunk(c, carry):
        r0 = pl.multiple_of(c * L, L)
        rows = pl.ds(r0, L)
        q = z_ref[0, rows, 0:256]
        k = z_ref[0, rows, 256:512]
        la = z_ref[0, rows, 512:768]
        b = _mm_f32(tril, la)
        b_mid = b[L // 2 - 1:L // 2, :]
        b_last = b[L - 1:L, :]
        qi = q * jnp.exp(b - b_mid)
        ki = k * jnp.exp(b_mid - b)
        qs = q * jnp.exp(b)
        kl = k * jnp.exp(b_last - b)
        dl = jnp.exp(b_last)
        for h in range(GLA_HEADS):
            pair = slice(LANES * (h // 2), LANES * (h // 2) + LANES)
            mine = own_half[h % 2]
            v_h = z_ref[0, rows, 768 + GLA_DV * h:768 + GLA_DV * (h + 1)]
            g_h = z_ref[0, rows, 1280 + GLA_DV * h:1280 + GLA_DV * (h + 1)]
            ki_h = jnp.where(mine, ki[:, pair], 0.0)
            kl_h = jnp.where(mine, kl[:, pair], 0.0)
            att = jnp.where(causal, _mm_nt(qi[:, pair], ki_h), 0.0)
            st = s_ref[0, h]
            o_h = _mm(att, v_h) + _mm_nt(qs[:, pair], st)
            s_ref[0, h] = st * dl[:, pair] + _mm_tn(v_h, kl_h)
            o_ref[0, rows, GLA_DV * h:GLA_DV * (h + 1)] = _rms_rows(o_h, nw) * _silu(g_h)
        return carry

    lax.fori_loop(0, n_chunks, chunk, 0)


def _gla_call(zg, s0t, norm_w, lt):
    B, T, _ = zg.shape
    st_spec = pl.BlockSpec((1, GLA_HEADS, GLA_DV, LANES), lambda b, t: (b, 0, 0, 0))
    return pl.pallas_call(
        functools.partial(_gla_kernel, n_chunks=lt // CHUNK),
        grid=(B, T // lt),
        in_specs=[pl.BlockSpec((1, lt, 1792), lambda b, t: (b, t, 0)), st_spec, _const_spec((1, GLA_DV))],
        out_specs=[pl.BlockSpec((1, lt, GLA_V), lambda b, t: (b, t, 0)), st_spec],
        out_shape=[jax.ShapeDtypeStruct((B, T, GLA_V), F32),
                   jax.ShapeDtypeStruct((B, GLA_HEADS, GLA_DV, LANES), F32)],
        compiler_params=_params(("parallel", "arbitrary")),
        name="gla",
    )(zg, s0t, norm_w)


def _gdn_kernel(z_ref, cb_ref, s0_ref, cw_ref, nw_ref, o_ref, s_ref, cbo_ref, xbuf, qkv, *, lt):
    L = CHUNK
    n_chunks = lt // L
    t = pl.program_id(1)

    @pl.when(t == 0)
    def _():
        s_ref[...] = s0_ref[...]
        xbuf[0:SUBLANES, :] = cb_ref[0]

    @pl.when(t > 0)
    def _():
        xbuf[0:SUBLANES, :] = xbuf[lt:lt + SUBLANES, :]

    xbuf[SUBLANES:SUBLANES + lt, :] = z_ref[0, :, 0:GDN_CONV_CH]
    cbo_ref[0] = xbuf[lt:lt + SUBLANES, :]

    base = SUBLANES - (CONV_W - 1)
    conv = xbuf[base:base + lt, :] * cw_ref[0:1, :]
    for j in range(1, CONV_W):
        conv = conv + xbuf[base + j:base + j + lt, :] * cw_ref[j:j + 1, :]
    act = _silu(conv)
    for h in range(GDN_HEADS):
        for part, scale in ((0, GDN_DK ** -0.5), (1, 1.0)):
            sl = slice(part * GDN_QK + GDN_DK * h, part * GDN_QK + GDN_DK * (h + 1))
            xh = act[:, sl]
            ss = jnp.sum(xh * xh, axis=-1, keepdims=True)
            qkv[:, sl] = xh * (lax.rsqrt(ss + EPS) * scale)
    qkv[:, 2 * GDN_QK:] = act[:, 2 * GDN_QK:]

    row = lax.broadcasted_iota(jnp.int32, (L, L), 0)
    col = lax.broadcasted_iota(jnp.int32, (L, L), 1)
    causal = row >= col
    strict = row > col
    tril = causal.astype(F32)
    eye = (row == col).astype(F32)
    sel_r = lax.broadcasted_iota(jnp.int32, (SUBLANES, LANES), 0)
    sel_c = lax.broadcasted_iota(jnp.int32, (SUBLANES, LANES), 1)
    sel = (sel_c == sel_r + SMALL_A0).astype(F32)
    nw = nw_ref[...]

    def chunk(c, carry):
        r0 = pl.multiple_of(c * L, L)
        rows = pl.ds(r0, L)
        small = z_ref[0, rows, 2048:2176]
        gs = _mm_f32(tril, small)
        g_rows = _mm_nt_f32(sel, gs)
        for h in range(GDN_HEADS):
            q_h = qkv[rows, GDN_DK * h:GDN_DK * (h + 1)]
            k_h = qkv[rows, GDN_QK + GDN_DK * h:GDN_QK + GDN_DK * (h + 1)]
            v_h = qkv[rows, 2 * GDN_QK + GDN_DV * h:2 * GDN_QK + GDN_DV * (h + 1)]
            gate_h = z_ref[0, rows, GDN_CONV_CH + GDN_DV * h:GDN_CONV_CH + GDN_DV * (h + 1)]
            beta_c = small[:, SMALL_BETA0 + h:SMALL_BETA0 + h + 1]
            g_c = gs[:, SMALL_A0 + h:SMALL_A0 + h + 1]
            g_r = g_rows[h:h + 1, :]
            g_last = g_c[L - 1:L, :]
            dec = jnp.exp(jnp.where(causal, g_c - g_r, -jnp.inf))
            kk = _mm_nt(k_h, k_h)
            a = jnp.where(strict, beta_c * kk * dec, 0.0)
            pw = -a
            tinv = eye + pw
            for _ in range(5):
                pw = _mm_f32(pw, pw)
                tinv = tinv + _mm_f32(tinv, pw)
            eg = jnp.exp(g_c)
            rhs = jnp.concatenate([beta_c * v_h, (beta_c * eg) * k_h], axis=-1)
            sol = _mm(tinv, rhs)
            st = s_ref[0, h]
            u = sol[:, 0:GDN_DV] - _mm(sol[:, GDN_DV:], st)
            qk = jnp.where(causal, _mm_nt(q_h, k_h) * dec, 0.0)
            o_h = _mm(qk, u) + _mm(q_h * eg, st)
            s_ref[0, h] = st * jnp.exp(g_last) + _mm_tn(k_h * jnp.exp(g_last - g_c), u)
            o_ref[0, rows, GDN_DV * h:GDN_DV * (h + 1)] = _rms_rows(o_h, nw) * _silu(gate_h)
        return carry

    lax.fori_loop(0, n_chunks, chunk, 0)


def _gdn_call(zd, cb8, s0, conv_w, norm_w, lt):
    B, T, _ = zd.shape
    st_spec = pl.BlockSpec((1, GDN_HEADS, GDN_DK, GDN_DV), lambda b, t: (b, 0, 0, 0))
    cb_spec = pl.BlockSpec((1, SUBLANES, GDN_CONV_CH), lambda b, t: (b, 0, 0))
    return pl.pallas_call(
        functools.partial(_gdn_kernel, lt=lt),
        grid=(B, T // lt),
        in_specs=[pl.BlockSpec((1, lt, 2176), lambda b, t: (b, t, 0)), cb_spec, st_spec,
                  _const_spec((CONV_W, GDN_CONV_CH)), _const_spec((1, GDN_DV))],
        out_specs=[pl.BlockSpec((1, lt, GDN_V), lambda b, t: (b, t, 0)), st_spec, cb_spec],
        out_shape=[jax.ShapeDtypeStruct((B, T, GDN_V), F32),
                   jax.ShapeDtypeStruct((B, GDN_HEADS, GDN_DK, GDN_DV), F32),
                   jax.ShapeDtypeStruct((B, SUBLANES, GDN_CONV_CH), F32)],
        scratch_shapes=[pltpu.VMEM((lt + SUBLANES, GDN_CONV_CH), F32), pltpu.VMEM((lt, GDN_CONV_CH), F32)],
        compiler_params=_params(("parallel", "arbitrary")),
        name="gdn",
    )(zd, cb8, s0, conv_w, norm_w)


S5_LANE_CHUNK = 512


def _s5_kernel(u_ref, h0_ref, bb_ref, cc_ref, lam_ref, d_ref, y_ref, h_ref, bu_ref, *, lt):
    nb = SUBLANES

    @pl.when(pl.program_id(1) == 0)
    def _():
        h_ref[...] = h0_ref[...]

    u = u_ref[...].reshape(nb * lt, BRANCH_W)
    n_blk = S5_HALF // LANES
    bu = _mm(u, bb_ref[...])
    for jb in range(2 * n_blk):
        bu_ref[jb] = bu[:, LANES * jb:LANES * (jb + 1)]
    per = S5_LANE_CHUNK // LANES
    for j in range(S5_HALF // S5_LANE_CHUNK):
        re = slice(S5_LANE_CHUNK * j, S5_LANE_CHUNK * (j + 1))
        im = slice(S5_HALF + S5_LANE_CHUNK * j, S5_HALF + S5_LANE_CHUNK * (j + 1))
        lr = jnp.broadcast_to(lam_ref[0:1, re], (nb, S5_LANE_CHUNK))
        li = jnp.broadcast_to(lam_ref[1:2, re], (nb, S5_LANE_CHUNK))

        def step(tt, carry, j=j, lr=lr, li=li):
            hr, hi = carry
            rows = pl.ds(tt, nb, stride=lt)
            br = jnp.concatenate([bu_ref[per * j + i, rows, :] for i in range(per)], axis=-1)
            bi = jnp.concatenate([bu_ref[n_blk + per * j + i, rows, :] for i in range(per)], axis=-1)
            nr = lr * hr - li * hi + br
            ni = lr * hi + li * hr + bi
            for i in range(per):
                bu_ref[per * j + i, rows, :] = nr[:, LANES * i:LANES * (i + 1)]
                bu_ref[n_blk + per * j + i, rows, :] = ni[:, LANES * i:LANES * (i + 1)]
            return nr, ni

        hr, hi = lax.fori_loop(0, lt, step, (h_ref[:, re], h_ref[:, im]))
        h_ref[:, re] = hr
        h_ref[:, im] = hi
    hs = jnp.concatenate([bu_ref[jb] for jb in range(2 * n_blk)], axis=-1)
    y = _mm(hs, cc_ref[...]) + d_ref[...] * u
    y_ref[...] = y.reshape(nb, lt, BRANCH_W)


def _s5_call(zs, h0, p, lt):
    B, T, _ = zs.shape
    nb = SUBLANES
    h_spec = pl.BlockSpec((nb, 2 * S5_HALF), lambda b, t: (b, 0))
    return pl.pallas_call(
        functools.partial(_s5_kernel, lt=lt),
        grid=(B // nb, T // lt),
        in_specs=[pl.BlockSpec((nb, lt, BRANCH_W), lambda b, t: (b, t, 0)), h_spec,
                  _const_spec((BRANCH_W, 2 * S5_HALF)), _const_spec((2 * S5_HALF, BRANCH_W)),
                  _const_spec((2, S5_HALF)), _const_spec((1, BRANCH_W))],
        out_specs=[pl.BlockSpec((nb, lt, BRANCH_W), lambda b, t: (b, t, 0)), h_spec],
        out_shape=[jax.ShapeDtypeStruct((B, T, BRANCH_W), F32),
                   jax.ShapeDtypeStruct((B, 2 * S5_HALF), F32)],
        scratch_shapes=[pltpu.VMEM((2 * S5_HALF // LANES, nb * lt, LANES), F32)],
        compiler_params=_params(("parallel", "arbitrary")),
        name="s5",
    )(zs, h0, p["s5_bblk"], p["s5_cblk"], p["s5_lam"], p["s5_d"])


def _merge_kernel(x_ref, mod_ref, nw_ref, oa_ref, ob_ref, ys_ref, wgate_ref, bgate_ref, wa_ref, wb_ref, wc_ref,
                  wglu_ref, bglu_ref, wout_ref, o_ref):
    nb, lt, d = x_ref.shape
    n = nb * lt
    x = x_ref[...]
    h = _rms_rows(x, nw_ref[...]) * (1.0 + mod_ref[:, 1:2, :]) + mod_ref[:, 0:1, :]
    gate = _sigmoid(_mm(h.reshape(n, d), wgate_ref[...]) + bgate_ref[...])
    y_a = _mm(oa_ref[...].reshape(n, BRANCH_W), wa_ref[...])
    y_b = _mm(ob_ref[...].reshape(n, BRANCH_W), wb_ref[...])
    yc = _gelu_tanh(ys_ref[...].reshape(n, BRANCH_W))
    yc = yc * _sigmoid(_mm(yc, wglu_ref[...]) + bglu_ref[...])
    y_c = _mm(yc, wc_ref[...])
    mix = gate[:, 0:d] * y_a + gate[:, d:2 * d] * y_b + gate[:, 2 * d:3 * d] * y_c
    m = _mm(mix, wout_ref[...])
    o_ref[...] = x + mod_ref[:, 2:3, :] * m.reshape(nb, lt, d)


def _merge_call(x, mod, oa, ob, ys, p, nb, lt):
    B, T, D = x.shape
    tok = lambda w: pl.BlockSpec((nb, lt, w), lambda b, t: (b, t, 0))
    return pl.pallas_call(
        _merge_kernel,
        grid=(B // nb, T // lt),
        in_specs=[tok(D), pl.BlockSpec((nb, 6, D), lambda b, t: (b, 0, 0)), _const_spec((1, D)),
                  tok(BRANCH_W), tok(BRANCH_W), tok(BRANCH_W),
                  _const_spec((D, 3 * D)), _const_spec((1, 3 * D)),
                  _const_spec((BRANCH_W, D)), _const_spec((BRANCH_W, D)), _const_spec((BRANCH_W, D)),
                  _const_spec((BRANCH_W, BRANCH_W)), _const_spec((1, BRANCH_W)), _const_spec((D, D))],
        out_specs=tok(D),
        out_shape=jax.ShapeDtypeStruct((B, T, D), F32),
        compiler_params=_params(("parallel", "parallel")),
        name="merge",
    )(x, mod, p["norm1_w"], oa, ob, ys, p["w_gate"], p["b_gate"], p["w_br_a"], p["w_br_b"], p["w_br_c"],
      p["w_glu"], p["b_glu"], p["w_out"])


def _moe_kernel(x_ref, mod_ref, nw_ref, wr_ref, br_ref, ex_ref, wg_ref, wu_ref, wd_ref, fw_ref, o_ref,
                hn_ref, cexp_ref, acc_ref, *, final_norm):
    nb, lt, d = x_ref.shape
    n = nb * lt
    j = pl.program_id(2)
    gw = EXPERTS_PER_GROUP * D_FF

    @pl.when(j == 0)
    def _():
        x = x_ref[...]
        hn = (_rms_rows(x, nw_ref[...]) * (1.0 + mod_ref[:, 4:5, :]) + mod_ref[:, 3:4, :]).reshape(n, d)
        hn_ref[...] = hn.astype(BF16)
        logits = _mm_f32(hn, wr_ref[...]) + br_ref[...]
        lane = lax.broadcasted_iota(jnp.int32, (n, LANES), 1)
        neg = -jnp.inf
        gl = jnp.where(lane < MOE_GROUPS, logits, neg)
        gmax = jnp.max(gl, axis=-1, keepdims=True)
        g_sel = jnp.min(jnp.where(gl == gmax, lane, LANES), axis=-1, keepdims=True)
        p_sel = 1.0 / jnp.sum(jnp.exp(gl - gmax), axis=-1, keepdims=True)
        e_id = lane - ROUTE_E0
        in_group = (e_id >= 0) & (e_id < N_EXPERTS) & ((e_id >> 2) == g_sel)
        el = jnp.where(in_group, logits, neg)
        m1 = jnp.max(el, axis=-1, keepdims=True)
        i1 = jnp.min(jnp.where(el == m1, lane, LANES), axis=-1, keepdims=True)
        el2 = jnp.where(lane == i1, neg, el)
        m2 = jnp.max(el2, axis=-1, keepdims=True)
        i2 = jnp.min(jnp.where(el2 == m2, lane, LANES), axis=-1, keepdims=True)
        e2 = jnp.exp(m2 - m1)
        w1 = p_sel / (1.0 + e2)
        w2 = p_sel * e2 / (1.0 + e2)
        comb = jnp.where(lane == i1, w1, jnp.where(lane == i2, w2, 0.0))
        for g in range(MOE_GROUPS):
            cexp_ref[g] = _mm_f32(comb, ex_ref[:, gw * g:gw * (g + 1)])
        acc_ref[...] = jnp.zeros_like(acc_ref)

    hn = hn_ref[...]
    he = _silu(_mm(hn, wg_ref[0])) * _mm(hn, wu_ref[0])
    acc_ref[...] += _mm(he * cexp_ref[j], wd_ref[0])

    @pl.when(j == MOE_GROUPS - 1)
    def _():
        out = x_ref[...] + mod_ref[:, 5:6, :] * acc_ref[...].reshape(nb, lt, d)
        if final_norm:
            out = _rms_rows(out, fw_ref[...])
        o_ref[...] = out


def _moe_call(x, mod, p, final_w, nb, lt, final_norm):
    B, T, D = x.shape
    n = nb * lt
    gw = EXPERTS_PER_GROUP * D_FF
    tok = pl.BlockSpec((nb, lt, D), lambda b, t, j: (b, t, 0))
    return pl.pallas_call(
        functools.partial(_moe_kernel, final_norm=final_norm),
        grid=(B // nb, T // lt, MOE_GROUPS),
        in_specs=[tok, pl.BlockSpec((nb, 6, D), lambda b, t, j: (b, 0, 0)), _const_spec((1, D)),
                  _const_spec((D, LANES)), _const_spec((1, LANES)), _const_spec((LANES, N_EXPERTS * D_FF)),
                  pl.BlockSpec((1, D, gw), lambda b, t, j: (j, 0, 0)),
                  pl.BlockSpec((1, D, gw), lambda b, t, j: (j, 0, 0)),
                  pl.BlockSpec((1, gw, D), lambda b, t, j: (j, 0, 0)),
                  _const_spec((1, D))],
        out_specs=tok,
        out_shape=jax.ShapeDtypeStruct((B, T, D), F32),
        scratch_shapes=[pltpu.VMEM((n, D), BF16), pltpu.VMEM((MOE_GROUPS, n, gw), F32), pltpu.VMEM((n, D), F32)],
        compiler_params=_params(("parallel", "parallel", "arbitrary")),
        name="moe",
    )(x, mod, p["norm2_w"], p["w_route"], p["b_route"], p["route_expand"], p["w_eg"], p["w_eu"], p["w_ed"],
      final_w)


def _prep_layer(l, norm1_w, norm2_w, w_in, w_alpha_up, b_alpha, gla_norm_w, conv_w, gdn_a_log, gdn_dt_bias,
                gdn_norm_w, s5_a_re, s5_a_im, s5_log_dt, s5_b_re, s5_b_im, s5_c_re, s5_c_im, s5_d, w_glu, b_glu,
                w_br_a, w_br_b, w_br_c, w_gate, b_gate, w_out, w_route_group, b_route_group, w_route_expert,
                b_route_expert, w_exp_gate, w_exp_up, w_exp_down):
    D = D_MODEL
    w = w_in[l]
    o = 0
    cols = {}
    for name, size in (("qa", GLA_QK), ("ka", GLA_QK), ("va", GLA_V), ("ga", GLA_V), ("aa", GLA_LOWRANK),
                       ("qkv", GDN_CONV_CH), ("beta", GDN_HEADS), ("a", GDN_HEADS), ("gb", GDN_V),
                       ("u", BRANCH_W)):
        cols[name] = w[:, o:o + size]
        o += size
    p = {}
    p["norm1_w"] = norm1_w[l].reshape(1, D)
    p["norm2_w"] = norm2_w[l].reshape(1, D)
    p["w_gla"] = jnp.concatenate([cols["qa"] * (GLA_DK ** -0.5), cols["ka"], cols["va"], cols["ga"]],
                                 axis=1).astype(BF16)
    p["w_gdn"] = jnp.concatenate([cols["qkv"], cols["gb"]], axis=1).astype(BF16)
    p["w_s5"] = cols["u"].astype(BF16)
    small = jnp.zeros((D, LANES), F32)
    small = small.at[:, 0:GLA_LOWRANK].set(cols["aa"])
    small = small.at[:, SMALL_BETA0:SMALL_BETA0 + GDN_HEADS].set(cols["beta"])
    small = small.at[:, SMALL_A0:SMALL_A0 + GDN_HEADS].set(cols["a"])
    p["w_small"] = small.astype(BF16)
    p["w_up"] = jnp.zeros((LANES, GLA_QK), F32).at[0:GLA_LOWRANK].set(w_alpha_up[l]).astype(BF16)
    p["b_alpha"] = b_alpha[l].reshape(1, GLA_QK)
    gs = jnp.zeros((2, LANES), F32)
    gs = gs.at[0, SMALL_A0:SMALL_A0 + GDN_HEADS].set(jnp.exp(gdn_a_log[l]))
    gs = gs.at[1, SMALL_A0:SMALL_A0 + GDN_HEADS].set(gdn_dt_bias[l])
    p["gdn_small"] = gs
    p["gla_norm_w"] = gla_norm_w[l].reshape(1, GLA_DV)
    p["gdn_norm_w"] = gdn_norm_w[l].reshape(1, GDN_DV)
    p["conv_w"] = conv_w[l]

    a_re, a_im = s5_a_re[l], s5_a_im[l]
    dt = jnp.exp(s5_log_dt[l])[:, None]
    mag = jnp.exp(a_re * dt)
    lr, li = mag * jnp.cos(a_im * dt), mag * jnp.sin(a_im * dt)
    den = a_re * a_re + a_im * a_im
    f_re = ((lr - 1.0) * a_re + li * a_im) / den
    f_im = (li * a_re - (lr - 1.0) * a_im) / den
    bb_re = f_re[..., None] * s5_b_re[l] - f_im[..., None] * s5_b_im[l]
    bb_im = f_re[..., None] * s5_b_im[l] + f_im[..., None] * s5_b_re[l]
    eye_g = jnp.eye(S5_GROUPS, dtype=F32)
    blk = lambda m: jnp.einsum("gpc,gh->gchp", m, eye_g).reshape(BRANCH_W, S5_HALF)
    p["s5_bblk"] = jnp.concatenate([blk(bb_re), blk(bb_im)], axis=1).astype(BF16)
    cblk = lambda m: jnp.einsum("gcp,gh->hpgc", m, eye_g).reshape(S5_HALF, BRANCH_W)
    p["s5_cblk"] = jnp.concatenate([cblk(s5_c_re[l]), -cblk(s5_c_im[l])], axis=0).astype(BF16)
    p["s5_lam"] = jnp.stack([lr.reshape(S5_HALF), li.reshape(S5_HALF)])
    p["s5_d"] = s5_d[l].reshape(1, BRANCH_W)

    p["w_gate"] = w_gate[l].astype(BF16)
    p["b_gate"] = b_gate[l].reshape(1, 3 * D)
    p["w_br_a"] = w_br_a[l].astype(BF16)
    p["w_br_b"] = w_br_b[l].astype(BF16)
    p["w_br_c"] = w_br_c[l].astype(BF16)
    p["w_glu"] = w_glu[l].astype(BF16)
    p["b_glu"] = b_glu[l].reshape(1, BRANCH_W)
    p["w_out"] = w_out[l].astype(BF16)

    wr = jnp.zeros((D, LANES), F32)
    wr = wr.at[:, 0:MOE_GROUPS].set(w_route_group[l]).at[:, ROUTE_E0:ROUTE_E0 + N_EXPERTS].set(w_route_expert[l])
    br = jnp.zeros((1, LANES), F32)
    br = br.at[0, 0:MOE_GROUPS].set(b_route_group[l]).at[0, ROUTE_E0:ROUTE_E0 + N_EXPERTS].set(b_route_expert[l])
    p["w_route"], p["b_route"] = wr, br
    e_of_col = jnp.arange(N_EXPERTS * D_FF) // D_FF
    p["route_expand"] = (jnp.arange(LANES)[:, None] == (e_of_col[None, :] + ROUTE_E0)).astype(F32)
    gw = EXPERTS_PER_GROUP * D_FF
    p["w_eg"] = w_exp_gate[l].reshape(MOE_GROUPS, EXPERTS_PER_GROUP, D, D_FF).transpose(0, 2, 1, 3) \
        .reshape(MOE_GROUPS, D, gw).astype(BF16)
    p["w_eu"] = w_exp_up[l].reshape(MOE_GROUPS, EXPERTS_PER_GROUP, D, D_FF).transpose(0, 2, 1, 3) \
        .reshape(MOE_GROUPS, D, gw).astype(BF16)
    p["w_ed"] = w_exp_down[l].reshape(MOE_GROUPS, gw, D).astype(BF16)
    return p


def _gla_state_in(s):
    st = jnp.swapaxes(s, -1, -2)
    z = jnp.zeros_like(st)
    halves = [jnp.concatenate([st[:, h], z[:, h]] if h % 2 == 0 else [z[:, h], st[:, h]], axis=-1)
              for h in range(GLA_HEADS)]
    return jnp.stack(halves, axis=1)


def _gla_state_out(st):
    parts = [st[:, h, :, GLA_DK * (h % 2):GLA_DK * (h % 2 + 1)] for h in range(GLA_HEADS)]
    return jnp.swapaxes(jnp.stack(parts, axis=1), -1, -2)


def _trunk(x, mods, params, states, final_w, nb, lt, lt_rec, lt_s5):
    new = []
    for l, p in enumerate(params):
        s_gla, s_gdn, cbuf, s5_re, s5_im = states[l]
        B = x.shape[0]
        zg, zd, zs = _inproj_call(x, mods[l], p, nb, lt)
        oa, gla_t = _gla_call(zg, _gla_state_in(s_gla), p["gla_norm_w"], lt_rec)
        cb8 = jnp.concatenate([jnp.zeros((B, SUBLANES - (CONV_W - 1), GDN_CONV_CH), F32), cbuf], axis=1)
        ob, gdn_new, cb_new = _gdn_call(zd, cb8, s_gdn, p["conv_w"], p["gdn_norm_w"], lt_rec)
        h0 = jnp.concatenate([s5_re.reshape(B, S5_HALF), s5_im.reshape(B, S5_HALF)], axis=1)
        ys, h_new = _s5_call(zs, h0, p, lt_s5)
        x = _merge_call(x, mods[l], oa, ob, ys, p, nb, lt)
        x = _moe_call(x, mods[l], p, final_w, nb, lt, final_norm=(l == len(params) - 1))
        new.append((_gla_state_out(gla_t), gdn_new, cb_new[:, SUBLANES - (CONV_W - 1):],
                    h_new[:, :S5_HALF].reshape(B, S5_GROUPS, S5_STATE),
                    h_new[:, S5_HALF:].reshape(B, S5_GROUPS, S5_STATE)))
    return x, [jnp.stack([new[l][i] for l in range(len(params))]) for i in range(5)]


def kernel(x_prompt, x_sample, c_prompt, c_sample, state_gla, state_gdn, cache_gdn_conv, state_s5_re, state_s5_im, norm1_w, norm2_w, final_norm_w, w_ada, b_ada, w_in, w_alpha_up, b_alpha, gla_norm_w, conv_w, gdn_a_log, gdn_dt_bias, gdn_norm_w, s5_a_re, s5_a_im, s5_log_dt, s5_b_re, s5_b_im, s5_c_re, s5_c_im, s5_d, w_glu, b_glu, w_br_a, w_br_b, w_br_c, w_gate, b_gate, w_out, w_route_group, b_route_group, w_route_expert, b_route_expert, w_exp_gate, w_exp_up, w_exp_down):
    depth = w_in.shape[0]
    bp, tp, _ = x_prompt.shape
    bs, ts, _ = x_sample.shape
    params = [_prep_layer(l, norm1_w, norm2_w, w_in, w_alpha_up, b_alpha, gla_norm_w, conv_w, gdn_a_log,
                          gdn_dt_bias, gdn_norm_w, s5_a_re, s5_a_im, s5_log_dt, s5_b_re, s5_b_im, s5_c_re,
                          s5_c_im, s5_d, w_glu, b_glu, w_br_a, w_br_b, w_br_c, w_gate, b_gate, w_out,
                          w_route_group, b_route_group, w_route_expert, b_route_expert, w_exp_gate, w_exp_up,
                          w_exp_down) for l in range(depth)]
    final_w = final_norm_w.reshape(1, D_MODEL)

    mod = _ada_call(jnp.concatenate([c_prompt, c_sample], axis=0), w_ada, b_ada)
    mod = mod.reshape(depth, bp + bs, 6, D_MODEL)
    mods_p = [mod[l, :bp] for l in range(depth)]
    mods_s = [mod[l, bp:] for l in range(depth)]

    zeros_p = [(jnp.zeros((bp, GLA_HEADS, GLA_DK, GLA_DV), F32), jnp.zeros((bp, GDN_HEADS, GDN_DK, GDN_DV), F32),
                jnp.zeros((bp, CONV_W - 1, GDN_CONV_CH), F32), jnp.zeros((bp, S5_GROUPS, S5_STATE), F32),
                jnp.zeros((bp, S5_GROUPS, S5_STATE), F32)) for _ in range(depth)]
    lt_p = min(256, tp)
    y_p, new_p = _trunk(x_prompt, mods_p, params, zeros_p, final_w, nb=1, lt=lt_p, lt_rec=lt_p,
                        lt_s5=min(CHUNK, tp))

    states_s = [(state_gla[l], state_gdn[l], cache_gdn_conv[l], state_s5_re[l], state_s5_im[l])
                for l in range(depth)]
    nb_s = max(1, min(bs, 256 // ts))
    y_s, new_s = _trunk(x_sample, mods_s, params, states_s, final_w, nb=nb_s, lt=ts, lt_rec=ts, lt_s5=ts)
    return (y_p, y_s, *new_p, *new_s)
```

```python
import functools
import math

import jax
import jax.numpy as jnp
from jax import lax
from jax.experimental import pallas as pl
from jax.experimental.pallas import tpu as pltpu

F32 = jnp.float32
BF16 = jnp.bfloat16
HIGHEST = lax.Precision.HIGHEST

D_MODEL = 1024
EPS = 1e-6
CHUNK = 64
BRANCH_W = 512
GLA_HEADS, GLA_DK, GLA_DV, GLA_LOWRANK, GLA_TAU = 4, 64, 128, 16, 16.0
GDN_HEADS, GDN_DK, GDN_DV, CONV_W = 4, 128, 128, 4
S5_GROUP, S5_GROUPS, S5_STATE = 16, 32, 64
GLA_QK, GLA_V, GDN_QK, GDN_V = 256, 512, 512, 512
GDN_CONV_CH = 1536
MOE_GROUPS, EXPERTS_PER_GROUP, N_EXPERTS, D_FF = 4, 4, 16, 256

LANES = 128
SUBLANES = 8
VMEM_LIMIT = 56 * 1024 * 1024
S5_HALF = S5_GROUPS * S5_STATE
SMALL_BETA0, SMALL_A0 = 16, 20
ROUTE_E0 = 4


def _mm(a, b):
    return jnp.dot(a.astype(BF16), b.astype(BF16), preferred_element_type=F32)


def _mm_nt(a, b):
    return lax.dot_general(a.astype(BF16), b.astype(BF16), (((1,), (1,)), ((), ())), preferred_element_type=F32)


def _mm_tn(a, b):
    return lax.dot_general(a.astype(BF16), b.astype(BF16), (((0,), (0,)), ((), ())), preferred_element_type=F32)


def _mm_f32(a, b):
    return jnp.dot(a, b, preferred_element_type=F32, precision=HIGHEST)


def _mm_nt_f32(a, b):
    return lax.dot_general(a, b, (((1,), (1,)), ((), ())), preferred_element_type=F32, precision=HIGHEST)


def _split3(x):
    a = x.astype(BF16)
    r = x - a.astype(F32)
    b = r.astype(BF16)
    c = (r - b.astype(F32)).astype(BF16)
    return a, b, c


def _mm_sel(m01, x):
    a, b, c = _split3(x)
    return _mm(m01, a) + _mm(m01, b) + _mm(m01, c)


def _mm_sel_r(x, m01):
    a, b, c = _split3(x)
    return _mm(a, m01) + _mm(b, m01) + _mm(c, m01)


def _mm_nt_sel(m01, x):
    a, b, c = _split3(x)
    return _mm_nt(m01, a) + _mm_nt(m01, b) + _mm_nt(m01, c)


def _sigmoid(x):
    return 1.0 / (1.0 + jnp.exp(-x))


def _silu(x):
    return x * _sigmoid(x)


def _softplus(x):
    return jnp.maximum(x, 0.0) + jnp.log1p(jnp.exp(-jnp.abs(x)))


def _rms_rows(x, w):
    ms = jnp.mean(x * x, axis=-1, keepdims=True)
    return x * lax.rsqrt(ms + EPS) * w


def _gelu_tanh(x):
    c = math.sqrt(2.0 / math.pi)
    return x * (0.5 * (1.0 + jnp.tanh(c * (x + 0.044715 * (x * x * x)))))


def _const_spec(shape):
    nd = len(shape)
    return pl.BlockSpec(shape, lambda *_: (0,) * nd)


def _params(sem):
    return pltpu.CompilerParams(dimension_semantics=sem, vmem_limit_bytes=VMEM_LIMIT)


def _ada_kernel(c_ref, w_ref, b_ref, o_ref):
    c = c_ref[...]
    o_ref[0] = _mm(_silu(c), w_ref[0]) + b_ref[0]


def _ada_call(c_all, w_ada, b_ada):
    depth = w_ada.shape[0]
    nrow = c_all.shape[0]
    return pl.pallas_call(
        _ada_kernel,
        grid=(depth, 6),
        in_specs=[pl.BlockSpec((nrow, D_MODEL), lambda l, j: (0, 0)),
                  pl.BlockSpec((1, D_MODEL, D_MODEL), lambda l, j: (l, 0, j)),
                  pl.BlockSpec((1, 1, D_MODEL), lambda l, j: (l, 0, j))],
        out_specs=pl.BlockSpec((1, nrow, D_MODEL), lambda l, j: (l, 0, j)),
        out_shape=jax.ShapeDtypeStruct((depth, nrow, 6 * D_MODEL), F32),
        compiler_params=_params(("parallel", "parallel")),
        name="adaln",
    )(c_all, w_ada, b_ada.reshape(depth, 1, 6 * D_MODEL))


def _inproj_kernel(x_ref, mod_ref, nw_ref, wg_ref, wd_ref, ws_ref, wsm_ref, wup_ref, bal_ref, sp_ref,
                   zg_ref, zd_ref, zs_ref):
    nb, lt, d = x_ref.shape
    x = x_ref[...]
    h = _rms_rows(x, nw_ref[...]) * (1.0 + mod_ref[:, 1:2, :]) + mod_ref[:, 0:1, :]
    h2 = h.reshape(nb * lt, d).astype(BF16)

    zg = _mm(h2, wg_ref[...])
    zsm = _mm(h2, wsm_ref[...])
    la_pre = _mm(zsm, wup_ref[...]) + bal_ref[...]
    log_a = (jnp.minimum(la_pre, 0.0) - jnp.log1p(jnp.exp(-jnp.abs(la_pre)))) * (1.0 / GLA_TAU)
    zg_ref[:, :, 0:512] = zg[:, 0:512].reshape(nb, lt, 512)
    zg_ref[:, :, 512:768] = log_a.reshape(nb, lt, 256)
    zg_ref[:, :, 768:1792] = zg[:, 512:1536].reshape(nb, lt, 1024)

    zd_ref[:, :, 0:2048] = _mm(h2, wd_ref[...]).reshape(nb, lt, 2048)
    lane = lax.broadcasted_iota(jnp.int32, zsm.shape, 1)
    beta = _sigmoid(zsm)
    log_g = -sp_ref[0:1, :] * _softplus(zsm + sp_ref[1:2, :])
    is_beta = (lane >= SMALL_BETA0) & (lane < SMALL_A0)
    is_a = (lane >= SMALL_A0) & (lane < SMALL_A0 + GDN_HEADS)
    small = jnp.where(is_beta, beta, jnp.where(is_a, log_g, 0.0))
    zd_ref[:, :, 2048:2176] = small.reshape(nb, lt, LANES)

    zs = _mm(h2, ws_ref[...])
    for i in range(nb):
        zs_ref[:, BRANCH_W * i:BRANCH_W * (i + 1)] = zs[lt * i:lt * (i + 1), :]


def _inproj_call(x, mod, p, nb, lt):
    B, T, D = x.shape
    grid = (B // nb, T // lt)
    tok = lambda w: pl.BlockSpec((nb, lt, w), lambda b, t: (b, t, 0))
    return pl.pallas_call(
        _inproj_kernel,
        grid=grid,
        in_specs=[tok(D),
                  pl.BlockSpec((nb, 6, D), lambda b, t: (b, 0, 0)),
                  _const_spec((1, D)),
                  _const_spec(p["w_gla"].shape), _const_spec(p["w_gdn"].shape), _const_spec(p["w_s5"].shape),
                  _const_spec(p["w_small"].shape), _const_spec(p["w_up"].shape), _const_spec((1, GLA_QK)),
                  _const_spec((2, LANES))],
        out_specs=[tok(1792), tok(2176), pl.BlockSpec((lt, nb * BRANCH_W), lambda b, t: (t, b))],
        out_shape=[jax.ShapeDtypeStruct((B, T, 1792), F32),
                   jax.ShapeDtypeStruct((B, T, 2176), F32),
                   jax.ShapeDtypeStruct((T, B * BRANCH_W), F32)],
        compiler_params=_params(("parallel", "parallel")),
        name="inproj",
    )(x, mod, p["norm1_w"], p["w_gla"], p["w_gdn"], p["w_s5"], p["w_small"], p["w_up"], p["b_alpha"],
      p["gdn_small"])


def _gla_kernel(z_ref, s0_ref, nw_ref, o_ref, s_ref, *, n_chunks):
    L = CHUNK

    @pl.when(pl.program_id(1) == 0)
    def _():
        s_ref[...] = s0_ref[...]

    row = lax.broadcasted_iota(jnp.int32, (L, L), 0)
    col = lax.broadcasted_iota(jnp.int32, (L, L), 1)
    causal = row >= col
    tril = causal.astype(F32)
    lane = lax.broadcasted_iota(jnp.int32, (L, LANES), 1)
    own_half = (lane < GLA_DK, lane >= GLA_DK)
    nw = nw_ref[...]

    def chunk(c, carry):
        r0 = pl.multiple_of(c * L, L)
        rows = pl.ds(r0, L)
        q = z_ref[0, rows, 0:256]
        k = z_ref[0, rows, 256:512]
        la = z_ref[0, rows, 512:768]
        b = _mm_f32(tril, la)
        b_mid = b[L // 2 - 1:L // 2, :]
        b_last = b[L - 1:L, :]
        qi = q * jnp.exp(b - b_mid)
        ki = k * jnp.exp(b_mid - b)
        qs = q * jnp.exp(b)
        kl = k * jnp.exp(b_last - b)
        dl = jnp.exp(b_last)
        loaded = [(z_ref[0, rows, 768 + GLA_DV * h:768 + GLA_DV * (h + 1)],
                   z_ref[0, rows, 1280 + GLA_DV * h:1280 + GLA_DV * (h + 1)],
                   s_ref[0, h]) for h in range(GLA_HEADS)]
        results = []
        for h in range(GLA_HEADS):
            pair = slice(LANES * (h // 2), LANES * (h // 2) + LANES)
            mine = own_half[h % 2]
            v_h, g_h, st = loaded[h]
            ki_h = jnp.where(mine, ki[:, pair], 0.0)
            kl_h = jnp.where(mine, kl[:, pair], 0.0)
            att = jnp.where(causal, _mm_nt(qi[:, pair], ki_h), 0.0)
            o_h = _mm(att, v_h) + _mm_nt(qs[:, pair], st)
            results.append((st * dl[:, pair] + _mm_tn(v_h, kl_h), _rms_rows(o_h, nw) * _silu(g_h)))
        for h in range(GLA_HEADS):
            s_ref[0, h] = results[h][0]
            o_ref[0, rows, GLA_DV * h:GLA_DV * (h + 1)] = results[h][1]
        return carry

    lax.fori_loop(0, n_chunks, chunk, 0)


def _gla_call(zg, s0t, norm_w, lt):
    B, T, _ = zg.shape
    st_spec = pl.BlockSpec((1, GLA_HEADS, GLA_DV, LANES), lambda b, t: (b, 0, 0, 0))
    return pl.pallas_call(
        functools.partial(_gla_kernel, n_chunks=lt // CHUNK),
        grid=(B, T // lt),
        in_specs=[pl.BlockSpec((1, lt, 1792), lambda b, t: (b, t, 0)), st_spec, _const_spec((1, GLA_DV))],
        out_specs=[pl.BlockSpec((1, lt, GLA_V), lambda b, t: (b, t, 0)), st_spec],
        out_shape=[jax.ShapeDtypeStruct((B, T, GLA_V), F32),
                   jax.ShapeDtypeStruct((B, GLA_HEADS, GLA_DV, LANES), F32)],
        compiler_params=_params(("parallel", "arbitrary")),
        name="gla",
    )(zg, s0t, norm_w)


def _gdn_kernel(z_ref, cb_ref, s0_ref, cw_ref, nw_ref, o_ref, s_ref, cbo_ref, xbuf, qkv, bfull_ref, gfull_ref,
                grow_ref, glast_ref, *, lt):
    L = CHUNK
    n_chunks = lt // L
    t = pl.program_id(1)

    @pl.when(t == 0)
    def _():
        s_ref[...] = s0_ref[...]
        xbuf[0:SUBLANES, :] = cb_ref[0]

    @pl.when(t > 0)
    def _():
        xbuf[0:SUBLANES, :] = xbuf[lt:lt + SUBLANES, :]

    xbuf[SUBLANES:SUBLANES + lt, :] = z_ref[0, :, 0:GDN_CONV_CH]
    cbo_ref[0] = xbuf[lt:lt + SUBLANES, :]

    base = SUBLANES - (CONV_W - 1)
    conv = xbuf[base:base + lt, :] * cw_ref[0:1, :]
    for j in range(1, CONV_W):
        conv = conv + xbuf[base + j:base + j + lt, :] * cw_ref[j:j + 1, :]
    act = _silu(conv)
    for h in range(GDN_HEADS):
        for part, scale in ((0, GDN_DK ** -0.5), (1, 1.0)):
            sl = slice(part * GDN_QK + GDN_DK * h, part * GDN_QK + GDN_DK * (h + 1))
            xh = act[:, sl]
            ss = jnp.sum(xh * xh, axis=-1, keepdims=True)
            qkv[:, sl] = xh * (lax.rsqrt(ss + EPS) * scale)
    qkv[:, 2 * GDN_QK:] = act[:, 2 * GDN_QK:]

    row = lax.broadcasted_iota(jnp.int32, (L, L), 0)
    col = lax.broadcasted_iota(jnp.int32, (L, L), 1)
    causal = row >= col
    strict = row > col
    eye = (row == col).astype(F32)
    diag16 = (row >> 4) == (col >> 4)
    diag32 = (row >> 5) == (col >> 5)
    off32 = diag32 & jnp.logical_not(diag16)
    off64 = jnp.logical_not(diag32)
    sel_r = lax.broadcasted_iota(jnp.int32, (SUBLANES, LANES), 0)
    sel_c = lax.broadcasted_iota(jnp.int32, (SUBLANES, LANES), 1)
    sel = (sel_c == sel_r + SMALL_A0).astype(BF16)
    brow = lax.broadcasted_iota(jnp.int32, (lt, lt), 0)
    bcol = lax.broadcasted_iota(jnp.int32, (lt, lt), 1)
    shift = L.bit_length() - 1
    chunk_tril = ((brow >= bcol) & ((brow >> shift) == (bcol >> shift))).astype(BF16)
    small_all = z_ref[0, :, 2048:2176]
    gs_all = _mm_sel(chunk_tril, small_all)
    grow_all = _mm_nt_sel(sel, gs_all)
    e_r = lax.broadcasted_iota(jnp.int32, (LANES, GDN_HEADS * LANES), 0)
    e_h = lax.broadcasted_iota(jnp.int32, (LANES, GDN_HEADS * LANES), 1) >> (LANES.bit_length() - 1)
    bfull_ref[...] = _mm_sel_r(small_all, (e_r == e_h + SMALL_BETA0).astype(BF16))
    gfull = _mm_sel_r(gs_all, (e_r == e_h + SMALL_A0).astype(BF16))
    gfull_ref[...] = gfull
    for c in range(n_chunks):
        grow_ref[c] = grow_all[:, L * c:L * (c + 1)]
        glast_ref[c:c + 1, :] = gfull[L * (c + 1) - 1:L * (c + 1), :]
    nw = nw_ref[...]

    probs = [(c, h) for c in range(n_chunks) for h in range(GDN_HEADS)]
    rows_of = lambda c: slice(L * c, L * (c + 1))
    head_of = lambda h: slice(LANES * h, LANES * (h + 1))
    q_ = {p: qkv[rows_of(p[0]), GDN_DK * p[1]:GDN_DK * (p[1] + 1)] for p in probs}
    k_ = {p: qkv[rows_of(p[0]), GDN_QK + GDN_DK * p[1]:GDN_QK + GDN_DK * (p[1] + 1)] for p in probs}
    beta_ = {p: bfull_ref[rows_of(p[0]), head_of(p[1])] for p in probs}
    g_ = {p: gfull_ref[rows_of(p[0]), head_of(p[1])] for p in probs}
    dec_ = {p: jnp.exp(jnp.where(causal, g_[p][:, 0:L] - grow_ref[p[0]][p[1]:p[1] + 1, :], -jnp.inf))
            for p in probs}
    kk_ = {p: _mm_nt(k_[p], k_[p]) for p in probs}
    qk_ = {p: _mm_nt(q_[p], k_[p]) for p in probs}
    a_ = {p: jnp.where(strict, beta_[p][:, 0:L] * kk_[p] * dec_[p], 0.0) for p in probs}
    qk_ = {p: jnp.where(causal, qk_[p] * dec_[p], 0.0) for p in probs}
    pw_ = {p: jnp.where(diag16, -a_[p], 0.0) for p in probs}
    tinv_ = {p: eye + pw_[p] for p in probs}
    for _ in range(3):
        pw_ = {p: _mm(pw_[p], pw_[p]) for p in probs}
        tinv_ = {p: tinv_[p] + _mm(tinv_[p], pw_[p]) for p in probs}
    for off in (off32, off64):
        te_ = {p: _mm(tinv_[p], jnp.where(off, a_[p], 0.0)) for p in probs}
        tinv_ = {p: tinv_[p] - _mm(te_[p], tinv_[p]) for p in probs}
    eg_ = {p: jnp.exp(g_[p]) for p in probs}
    sol_ = {}
    for p in probs:
        v_p = qkv[rows_of(p[0]), 2 * GDN_QK + GDN_DV * p[1]:2 * GDN_QK + GDN_DV * (p[1] + 1)]
        rhs = jnp.concatenate([beta_[p] * v_p, (beta_[p] * eg_[p]) * k_[p]], axis=-1)
        sol_[p] = _mm(tinv_[p], rhs)

    st_ = [s_ref[0, h] for h in range(GDN_HEADS)]
    for c in range(n_chunks):
        heads = [(c, h) for h in range(GDN_HEADS)]
        g_last = glast_ref[c:c + 1, :]
        u_ = {p: sol_[p][:, 0:GDN_DV] - _mm(sol_[p][:, GDN_DV:], st_[p[1]]) for p in heads}
        o_ = {p: _mm(qk_[p], u_[p]) + _mm(q_[p] * eg_[p], st_[p[1]]) for p in heads}
        st_ = [st_[h] * jnp.exp(g_last[:, head_of(h)])
               + _mm_tn(k_[(c, h)] * jnp.exp(g_last[:, head_of(h)] - g_[(c, h)]), u_[(c, h)])
               for h in range(GDN_HEADS)]
        for h in range(GDN_HEADS):
            gate = z_ref[0, rows_of(c), GDN_CONV_CH + GDN_DV * h:GDN_CONV_CH + GDN_DV * (h + 1)]
            o_ref[0, rows_of(c), GDN_DV * h:GDN_DV * (h + 1)] = _rms_rows(o_[(c, h)], nw) * _silu(gate)
    for h in range(GDN_HEADS):
        s_ref[0, h] = st_[h]


def _gdn_call(zd, cb8, s0, conv_w, norm_w, lt):
    B, T, _ = zd.shape
    st_spec = pl.BlockSpec((1, GDN_HEADS, GDN_DK, GDN_DV), lambda b, t: (b, 0, 0, 0))
    cb_spec = pl.BlockSpec((1, SUBLANES, GDN_CONV_CH), lambda b, t: (b, 0, 0))
    return pl.pallas_call(
        functools.partial(_gdn_kernel, lt=lt),
        grid=(B, T // lt),
        in_specs=[pl.BlockSpec((1, lt, 2176), lambda b, t: (b, t, 0)), cb_spec, st_spec,
                  _const_spec((CONV_W, GDN_CONV_CH)), _const_spec((1, GDN_DV))],
        out_specs=[pl.BlockSpec((1, lt, GDN_V), lambda b, t: (b, t, 0)), st_spec, cb_spec],
        out_shape=[jax.ShapeDtypeStruct((B, T, GDN_V), F32),
                   jax.ShapeDtypeStruct((B, GDN_HEADS, GDN_DK, GDN_DV), F32),
                   jax.ShapeDtypeStruct((B, SUBLANES, GDN_CONV_CH), F32)],
        scratch_shapes=[pltpu.VMEM((lt + SUBLANES, GDN_CONV_CH), F32), pltpu.VMEM((lt, GDN_CONV_CH), F32),
                        pltpu.VMEM((lt, GDN_HEADS * LANES), F32), pltpu.VMEM((lt, GDN_HEADS * LANES), F32),
                        pltpu.VMEM((lt // CHUNK, SUBLANES, CHUNK), F32),
                        pltpu.VMEM((SUBLANES, GDN_HEADS * LANES), F32)],
        compiler_params=_params(("parallel", "arbitrary")),
        name="gdn",
    )(zd, cb8, s0, conv_w, norm_w)


S5_LANE_CHUNK = 512


def _s5_kernel(u_ref, h0_ref, bb_ref, cc_ref, lam_ref, d_ref, y_ref, h_ref, bu_ref, *, lt):
    nb = SUBLANES

    @pl.when(pl.program_id(1) == 0)
    def _():
        h_ref[...] = h0_ref[...]

    u = u_ref[...].reshape(lt * nb, BRANCH_W)
    bu_ref[...] = _mm(u, bb_ref[...])
    for j in range(S5_HALF // S5_LANE_CHUNK):
        re = slice(S5_LANE_CHUNK * j, S5_LANE_CHUNK * (j + 1))
        im = slice(S5_HALF + S5_LANE_CHUNK * j, S5_HALF + S5_LANE_CHUNK * (j + 1))
        lr = jnp.broadcast_to(lam_ref[0:1, re], (nb, S5_LANE_CHUNK))
        li = jnp.broadcast_to(lam_ref[1:2, re], (nb, S5_LANE_CHUNK))

        def step(tt, carry, re=re, im=im, lr=lr, li=li):
            hr, hi = carry
            rows = pl.ds(pl.multiple_of(tt * nb, nb), nb)
            nr = lr * hr - li * hi + bu_ref[rows, re]
            ni = lr * hi + li * hr + bu_ref[rows, im]
            bu_ref[rows, re] = nr
            bu_ref[rows, im] = ni
            return nr, ni

        hr, hi = lax.fori_loop(0, lt, step, (h_ref[:, re], h_ref[:, im]))
        h_ref[:, re] = hr
        h_ref[:, im] = hi
    y = _mm(bu_ref[...], cc_ref[...]) + d_ref[...] * u
    y_ref[...] = y.reshape(lt, nb, BRANCH_W)


def _s5_call(zs, h0, p, lt):
    T = zs.shape[0]
    B = h0.shape[0]
    nb = SUBLANES
    h_spec = pl.BlockSpec((nb, 2 * S5_HALF), lambda b, t: (b, 0))
    tok_spec = pl.BlockSpec((lt, None, nb, BRANCH_W), lambda b, t: (t, b, 0, 0))
    y, h = pl.pallas_call(
        functools.partial(_s5_kernel, lt=lt),
        grid=(B // nb, T // lt),
        in_specs=[tok_spec, h_spec,
                  _const_spec((BRANCH_W, 2 * S5_HALF)), _const_spec((2 * S5_HALF, BRANCH_W)),
                  _const_spec((2, S5_HALF)), _const_spec((1, BRANCH_W))],
        out_specs=[tok_spec, h_spec],
        out_shape=[jax.ShapeDtypeStruct((T, B // nb, nb, BRANCH_W), F32),
                   jax.ShapeDtypeStruct((B, 2 * S5_HALF), F32)],
        scratch_shapes=[pltpu.VMEM((lt * nb, 2 * S5_HALF), F32)],
        compiler_params=_params(("parallel", "arbitrary")),
        name="s5",
    )(zs.reshape(T, B // nb, nb, BRANCH_W), h0, p["s5_bblk"], p["s5_cblk"], p["s5_lam"], p["s5_d"])
    return y.reshape(T, B * BRANCH_W), h


def _merge_kernel(x_ref, mod_ref, nw_ref, oa_ref, ob_ref, ys_ref, wgate_ref, bgate_ref, wa_ref, wb_ref, wc_ref,
                  wglu_ref, bglu_ref, wout_ref, o_ref):
    nb, lt, d = x_ref.shape
    n = nb * lt
    x = x_ref[...]
    h = _rms_rows(x, nw_ref[...]) * (1.0 + mod_ref[:, 1:2, :]) + mod_ref[:, 0:1, :]
    gate = _sigmoid(_mm(h.reshape(n, d), wgate_ref[...]) + bgate_ref[...])
    y_a = _mm(oa_ref[...].reshape(n, BRANCH_W), wa_ref[...])
    y_b = _mm(ob_ref[...].reshape(n, BRANCH_W), wb_ref[...])
    ys = [ys_ref[:, BRANCH_W * i:BRANCH_W * (i + 1)] for i in range(nb)]
    yc = _gelu_tanh(ys[0] if nb == 1 else jnp.concatenate(ys, axis=0))
    yc = yc * _sigmoid(_mm(yc, wglu_ref[...]) + bglu_ref[...])
    y_c = _mm(yc, wc_ref[...])
    mix = gate[:, 0:d] * y_a + gate[:, d:2 * d] * y_b + gate[:, 2 * d:3 * d] * y_c
    m = _mm(mix, wout_ref[...])
    o_ref[...] = x + mod_ref[:, 2:3, :] * m.reshape(nb, lt, d)


def _merge_call(x, mod, oa, ob, ys, p, nb, lt):
    B, T, D = x.shape
    tok = lambda w: pl.BlockSpec((nb, lt, w), lambda b, t: (b, t, 0))
    return pl.pallas_call(
        _merge_kernel,
        grid=(B // nb, T // lt),
        in_specs=[tok(D), pl.BlockSpec((nb, 6, D), lambda b, t: (b, 0, 0)), _const_spec((1, D)),
                  tok(BRANCH_W), tok(BRANCH_W), pl.BlockSpec((lt, nb * BRANCH_W), lambda b, t: (t, b)),
                  _const_spec((D, 3 * D)), _const_spec((1, 3 * D)),
                  _const_spec((BRANCH_W, D)), _const_spec((BRANCH_W, D)), _const_spec((BRANCH_W, D)),
                  _const_spec((BRANCH_W, BRANCH_W)), _const_spec((1, BRANCH_W)), _const_spec((D, D))],
        out_specs=tok(D),
        out_shape=jax.ShapeDtypeStruct((B, T, D), F32),
        compiler_params=_params(("parallel", "parallel")),
        name="merge",
    )(x, mod, p["norm1_w"], oa, ob, ys, p["w_gate"], p["b_gate"], p["w_br_a"], p["w_br_b"], p["w_br_c"],
      p["w_glu"], p["b_glu"], p["w_out"])


def _moe_kernel(x_ref, mod_ref, nw_ref, wr_ref, br_ref, ex_ref, wg_ref, wu_ref, wd_ref, fw_ref, o_ref,
                hn_ref, cexp_ref, acc_ref, *, final_norm):
    nb, lt, d = x_ref.shape
    n = nb * lt
    j = pl.program_id(2)
    gw = EXPERTS_PER_GROUP * D_FF

    @pl.when(j == 0)
    def _():
        x = x_ref[...]
        hn = (_rms_rows(x, nw_ref[...]) * (1.0 + mod_ref[:, 4:5, :]) + mod_ref[:, 3:4, :]).reshape(n, d)
        hn_ref[...] = hn.astype(BF16)
        logits = _mm_f32(hn, wr_ref[...]) + br_ref[...]
        lane = lax.broadcasted_iota(jnp.int32, (n, LANES), 1)
        neg = -jnp.inf
        gl = jnp.where(lane < MOE_GROUPS, logits, neg)
        gmax = jnp.max(gl, axis=-1, keepdims=True)
        g_sel = jnp.min(jnp.where(gl == gmax, lane, LANES), axis=-1, keepdims=True)
        p_sel = 1.0 / jnp.sum(jnp.exp(gl - gmax), axis=-1, keepdims=True)
        e_id = lane - ROUTE_E0
        in_group = (e_id >= 0) & (e_id < N_EXPERTS) & ((e_id >> 2) == g_sel)
        el = jnp.where(in_group, logits, neg)
        m1 = jnp.max(el, axis=-1, keepdims=True)
        i1 = jnp.min(jnp.where(el == m1, lane, LANES), axis=-1, keepdims=True)
        el2 = jnp.where(lane == i1, neg, el)
        m2 = jnp.max(el2, axis=-1, keepdims=True)
        i2 = jnp.min(jnp.where(el2 == m2, lane, LANES), axis=-1, keepdims=True)
        e2 = jnp.exp(m2 - m1)
        w1 = p_sel / (1.0 + e2)
        w2 = p_sel * e2 / (1.0 + e2)
        comb = jnp.where(lane == i1, w1, jnp.where(lane == i2, w2, 0.0))
        for g in range(MOE_GROUPS):
            cexp_ref[g] = _mm(comb, ex_ref[:, gw * g:gw * (g + 1)])
        acc_ref[...] = jnp.zeros_like(acc_ref)

    hn = hn_ref[...]
    he = _silu(_mm(hn, wg_ref[0])) * _mm(hn, wu_ref[0])
    acc_ref[...] += _mm(he * cexp_ref[j], wd_ref[0])

    @pl.when(j == MOE_GROUPS - 1)
    def _():
        out = x_ref[...] + mod_ref[:, 5:6, :] * acc_ref[...].reshape(nb, lt, d)
        if final_norm:
            out = _rms_rows(out, fw_ref[...])
        o_ref[...] = out


def _moe_call(x, mod, p, final_w, nb, lt, final_norm):
    B, T, D = x.shape
    n = nb * lt
    gw = EXPERTS_PER_GROUP * D_FF
    tok = pl.BlockSpec((nb, lt, D), lambda b, t, j: (b, t, 0))
    return pl.pallas_call(
        functools.partial(_moe_kernel, final_norm=final_norm),
        grid=(B // nb, T // lt, MOE_GROUPS),
        in_specs=[tok, pl.BlockSpec((nb, 6, D), lambda b, t, j: (b, 0, 0)), _const_spec((1, D)),
                  _const_spec((D, LANES)), _const_spec((1, LANES)), _const_spec((LANES, N_EXPERTS * D_FF)),
                  pl.BlockSpec((1, D, gw), lambda b, t, j: (j, 0, 0)),
                  pl.BlockSpec((1, D, gw), lambda b, t, j: (j, 0, 0)),
                  pl.BlockSpec((1, gw, D), lambda b, t, j: (j, 0, 0)),
                  _const_spec((1, D))],
        out_specs=tok,
        out_shape=jax.ShapeDtypeStruct((B, T, D), F32),
        scratch_shapes=[pltpu.VMEM((n, D), BF16), pltpu.VMEM((MOE_GROUPS, n, gw), F32), pltpu.VMEM((n, D), F32)],
        compiler_params=_params(("parallel", "parallel", "arbitrary")),
        name="moe",
    )(x, mod, p["norm2_w"], p["w_route"], p["b_route"], p["route_expand"], p["w_eg"], p["w_eu"], p["w_ed"],
      final_w)


def _prep_layer(l, norm1_w, norm2_w, w_in, w_alpha_up, b_alpha, gla_norm_w, conv_w, gdn_a_log, gdn_dt_bias,
                gdn_norm_w, s5_a_re, s5_a_im, s5_log_dt, s5_b_re, s5_b_im, s5_c_re, s5_c_im, s5_d, w_glu, b_glu,
                w_br_a, w_br_b, w_br_c, w_gate, b_gate, w_out, w_route_group, b_route_group, w_route_expert,
                b_route_expert, w_exp_gate, w_exp_up, w_exp_down):
    D = D_MODEL
    w = w_in[l]
    o = 0
    cols = {}
    for name, size in (("qa", GLA_QK), ("ka", GLA_QK), ("va", GLA_V), ("ga", GLA_V), ("aa", GLA_LOWRANK),
                       ("qkv", GDN_CONV_CH), ("beta", GDN_HEADS), ("a", GDN_HEADS), ("gb", GDN_V),
                       ("u", BRANCH_W)):
        cols[name] = w[:, o:o + size]
        o += size
    p = {}
    p["norm1_w"] = norm1_w[l].reshape(1, D)
    p["norm2_w"] = norm2_w[l].reshape(1, D)
    p["w_gla"] = jnp.concatenate([cols["qa"] * (GLA_DK ** -0.5), cols["ka"], cols["va"], cols["ga"]],
                                 axis=1).astype(BF16)
    p["w_gdn"] = jnp.concatenate([cols["qkv"], cols["gb"]], axis=1).astype(BF16)
    p["w_s5"] = cols["u"].astype(BF16)
    small = jnp.zeros((D, LANES), F32)
    small = small.at[:, 0:GLA_LOWRANK].set(cols["aa"])
    small = small.at[:, SMALL_BETA0:SMALL_BETA0 + GDN_HEADS].set(cols["beta"])
    small = small.at[:, SMALL_A0:SMALL_A0 + GDN_HEADS].set(cols["a"])
    p["w_small"] = small.astype(BF16)
    p["w_up"] = jnp.zeros((LANES, GLA_QK), F32).at[0:GLA_LOWRANK].set(w_alpha_up[l]).astype(BF16)
    p["b_alpha"] = b_alpha[l].reshape(1, GLA_QK)
    gs = jnp.zeros((2, LANES), F32)
    gs = gs.at[0, SMALL_A0:SMALL_A0 + GDN_HEADS].set(jnp.exp(gdn_a_log[l]))
    gs = gs.at[1, SMALL_A0:SMALL_A0 + GDN_HEADS].set(gdn_dt_bias[l])
    p["gdn_small"] = gs
    p["gla_norm_w"] = gla_norm_w[l].reshape(1, GLA_DV)
    p["gdn_norm_w"] = gdn_norm_w[l].reshape(1, GDN_DV)
    p["conv_w"] = conv_w[l]

    a_re, a_im = s5_a_re[l], s5_a_im[l]
    dt = jnp.exp(s5_log_dt[l])[:, None]
    mag = jnp.exp(a_re * dt)
    lr, li = mag * jnp.cos(a_im * dt), mag * jnp.sin(a_im * dt)
    den = a_re * a_re + a_im * a_im
    f_re = ((lr - 1.0) * a_re + li * a_im) / den
    f_im = (li * a_re - (lr - 1.0) * a_im) / den
    bb_re = f_re[..., None] * s5_b_re[l] - f_im[..., None] * s5_b_im[l]
    bb_im = f_re[..., None] * s5_b_im[l] + f_im[..., None] * s5_b_re[l]
    eye_g = jnp.eye(S5_GROUPS, dtype=F32)
    blk = lambda m: jnp.einsum("gpc,gh->gchp", m, eye_g).reshape(BRANCH_W, S5_HALF)
    p["s5_bblk"] = jnp.concatenate([blk(bb_re), blk(bb_im)], axis=1).astype(BF16)
    cblk = lambda m: jnp.einsum("gcp,gh->hpgc", m, eye_g).reshape(S5_HALF, BRANCH_W)
    p["s5_cblk"] = jnp.concatenate([cblk(s5_c_re[l]), -cblk(s5_c_im[l])], axis=0).astype(BF16)
    p["s5_lam"] = jnp.stack([lr.reshape(S5_HALF), li.reshape(S5_HALF)])
    p["s5_d"] = s5_d[l].reshape(1, BRANCH_W)

    p["w_gate"] = w_gate[l].astype(BF16)
    p["b_gate"] = b_gate[l].reshape(1, 3 * D)
    p["w_br_a"] = w_br_a[l].astype(BF16)
    p["w_br_b"] = w_br_b[l].astype(BF16)
    p["w_br_c"] = w_br_c[l].astype(BF16)
    p["w_glu"] = w_glu[l].astype(BF16)
    p["b_glu"] = b_glu[l].reshape(1, BRANCH_W)
    p["w_out"] = w_out[l].astype(BF16)

    wr = jnp.zeros((D, LANES), F32)
    wr = wr.at[:, 0:MOE_GROUPS].set(w_route_group[l]).at[:, ROUTE_E0:ROUTE_E0 + N_EXPERTS].set(w_route_expert[l])
    br = jnp.zeros((1, LANES), F32)
    br = br.at[0, 0:MOE_GROUPS].set(b_route_group[l]).at[0, ROUTE_E0:ROUTE_E0 + N_EXPERTS].set(b_route_expert[l])
    p["w_route"], p["b_route"] = wr, br
    e_of_col = jnp.arange(N_EXPERTS * D_FF) // D_FF
    p["route_expand"] = (jnp.arange(LANES)[:, None] == (e_of_col[None, :] + ROUTE_E0)).astype(F32)
    gw = EXPERTS_PER_GROUP * D_FF
    p["w_eg"] = w_exp_gate[l].reshape(MOE_GROUPS, EXPERTS_PER_GROUP, D, D_FF).transpose(0, 2, 1, 3) \
        .reshape(MOE_GROUPS, D, gw).astype(BF16)
    p["w_eu"] = w_exp_up[l].reshape(MOE_GROUPS, EXPERTS_PER_GROUP, D, D_FF).transpose(0, 2, 1, 3) \
        .reshape(MOE_GROUPS, D, gw).astype(BF16)
    p["w_ed"] = w_exp_down[l].reshape(MOE_GROUPS, gw, D).astype(BF16)
    return p


def _gla_state_in(s):
    st = jnp.swapaxes(s, -1, -2)
    z = jnp.zeros_like(st)
    halves = [jnp.concatenate([st[:, h], z[:, h]] if h % 2 == 0 else [z[:, h], st[:, h]], axis=-1)
              for h in range(GLA_HEADS)]
    return jnp.stack(halves, axis=1)


def _gla_state_out(st):
    parts = [st[:, h, :, GLA_DK * (h % 2):GLA_DK * (h % 2 + 1)] for h in range(GLA_HEADS)]
    return jnp.swapaxes(jnp.stack(parts, axis=1), -1, -2)


def _trunk(x, mods, params, states, final_w, nb, lt, lt_rec, lt_s5):
    new = []
    for l, p in enumerate(params):
        s_gla, s_gdn, cbuf, s5_re, s5_im = states[l]
        B = x.shape[0]
        zg, zd, zs = _inproj_call(x, mods[l], p, nb, lt)
        oa, gla_t = _gla_call(zg, _gla_state_in(s_gla), p["gla_norm_w"], lt_rec)
        cb8 = jnp.concatenate([jnp.zeros((B, SUBLANES - (CONV_W - 1), GDN_CONV_CH), F32), cbuf], axis=1)
        ob, gdn_new, cb_new = _gdn_call(zd, cb8, s_gdn, p["conv_w"], p["gdn_norm_w"], lt_rec)
        h0 = jnp.concatenate([s5_re.reshape(B, S5_HALF), s5_im.reshape(B, S5_HALF)], axis=1)
        ys, h_new = _s5_call(zs, h0, p, lt_s5)
        x = _merge_call(x, mods[l], oa, ob, ys, p, nb, lt)
        x = _moe_call(x, mods[l], p, final_w, nb, lt, final_norm=(l == len(params) - 1))
        new.append((_gla_state_out(gla_t), gdn_new, cb_new[:, SUBLANES - (CONV_W - 1):],
                    h_new[:, :S5_HALF].reshape(B, S5_GROUPS, S5_STATE),
                    h_new[:, S5_HALF:].reshape(B, S5_GROUPS, S5_STATE)))
    return x, [jnp.stack([new[l][i] for l in range(len(params))]) for i in range(5)]


def kernel(x_prompt, x_sample, c_prompt, c_sample, state_gla, state_gdn, cache_gdn_conv, state_s5_re, state_s5_im, norm1_w, norm2_w, final_norm_w, w_ada, b_ada, w_in, w_alpha_up, b_alpha, gla_norm_w, conv_w, gdn_a_log, gdn_dt_bias, gdn_norm_w, s5_a_re, s5_a_im, s5_log_dt, s5_b_re, s5_b_im, s5_c_re, s5_c_im, s5_d, w_glu, b_glu, w_br_a, w_br_b, w_br_c, w_gate, b_gate, w_out, w_route_group, b_route_group, w_route_expert, b_route_expert, w_exp_gate, w_exp_up, w_exp_down):
    depth = w_in.shape[0]
    bp, tp, _ = x_prompt.shape
    bs, ts, _ = x_sample.shape
    params = [_prep_layer(l, norm1_w, norm2_w, w_in, w_alpha_up, b_alpha, gla_norm_w, conv_w, gdn_a_log,
                          gdn_dt_bias, gdn_norm_w, s5_a_re, s5_a_im, s5_log_dt, s5_b_re, s5_b_im, s5_c_re,
                          s5_c_im, s5_d, w_glu, b_glu, w_br_a, w_br_b, w_br_c, w_gate, b_gate, w_out,
                          w_route_group, b_route_group, w_route_expert, b_route_expert, w_exp_gate, w_exp_up,
                          w_exp_down) for l in range(depth)]
    final_w = final_norm_w.reshape(1, D_MODEL)

    mod = _ada_call(jnp.concatenate([c_prompt, c_sample], axis=0), w_ada, b_ada)
    mod = mod.reshape(depth, bp + bs, 6, D_MODEL)
    mods_p = [mod[l, :bp] for l in range(depth)]
    mods_s = [mod[l, bp:] for l in range(depth)]

    zeros_p = [(jnp.zeros((bp, GLA_HEADS, GLA_DK, GLA_DV), F32), jnp.zeros((bp, GDN_HEADS, GDN_DK, GDN_DV), F32),
                jnp.zeros((bp, CONV_W - 1, GDN_CONV_CH), F32), jnp.zeros((bp, S5_GROUPS, S5_STATE), F32),
                jnp.zeros((bp, S5_GROUPS, S5_STATE), F32)) for _ in range(depth)]
    lt_p = min(256, tp)
    y_p, new_p = _trunk(x_prompt, mods_p, params, zeros_p, final_w, nb=1, lt=lt_p, lt_rec=lt_p,
                        lt_s5=min(CHUNK, tp))

    states_s = [(state_gla[l], state_gdn[l], cache_gdn_conv[l], state_s5_re[l], state_s5_im[l])
                for l in range(depth)]
    nb_s = max(1, min(bs, 256 // ts))
    y_s, new_s = _trunk(x_sample, mods_s, params, states_s, final_w, nb=nb_s, lt=ts, lt_rec=ts, lt_s5=ts)
    return (y_p, y_s, *new_p, *new_s)
```

```python
import functools
import math

import jax
import jax.numpy as jnp
from jax import lax
from jax.experimental import pallas as pl
from jax.experimental.pallas import tpu as pltpu

F32 = jnp.float32
BF16 = jnp.bfloat16
HIGHEST = lax.Precision.HIGHEST

D_MODEL = 1024
EPS = 1e-6
CHUNK = 64
BRANCH_W = 512
GLA_HEADS, GLA_DK, GLA_DV, GLA_LOWRANK, GLA_TAU = 4, 64, 128, 16, 16.0
GDN_HEADS, GDN_DK, GDN_DV, CONV_W = 4, 128, 128, 4
S5_GROUP, S5_GROUPS, S5_STATE = 16, 32, 64
GLA_QK, GLA_V, GDN_QK, GDN_V = 256, 512, 512, 512
GDN_CONV_CH = 1536
MOE_GROUPS, EXPERTS_PER_GROUP, N_EXPERTS, D_FF = 4, 4, 16, 256

LANES = 128
SUBLANES = 8
VMEM_LIMIT = 56 * 1024 * 1024
S5_HALF = S5_GROUPS * S5_STATE
SMALL_BETA0, SMALL_A0 = 16, 20
ROUTE_E0 = 4


def _mm(a, b):
    return jnp.dot(a.astype(BF16), b.astype(BF16), preferred_element_type=F32)


def _mm_nt(a, b):
    return lax.dot_general(a.astype(BF16), b.astype(BF16), (((1,), (1,)), ((), ())), preferred_element_type=F32)


def _mm_tn(a, b):
    return lax.dot_general(a.astype(BF16), b.astype(BF16), (((0,), (0,)), ((), ())), preferred_element_type=F32)


def _mm_f32(a, b):
    return jnp.dot(a, b, preferred_element_type=F32, precision=HIGHEST)


def _mm_nt_f32(a, b):
    return lax.dot_general(a, b, (((1,), (1,)), ((), ())), preferred_element_type=F32, precision=HIGHEST)


def _split3(x):
    a = x.astype(BF16)
    r = x - a.astype(F32)
    b = r.astype(BF16)
    c = (r - b.astype(F32)).astype(BF16)
    return a, b, c


def _mm_sel(m01, x):
    a, b, c = _split3(x)
    return _mm(m01, a) + _mm(m01, b) + _mm(m01, c)


def _mm_sel_r(x, m01):
    a, b, c = _split3(x)
    return _mm(a, m01) + _mm(b, m01) + _mm(c, m01)


def _mm_nt_sel(m01, x):
    a, b, c = _split3(x)
    return _mm_nt(m01, a) + _mm_nt(m01, b) + _mm_nt(m01, c)


def _sigmoid(x):
    return 1.0 / (1.0 + jnp.exp(-x))


def _silu(x):
    return x * _sigmoid(x)


def _softplus(x):
    return jnp.maximum(x, 0.0) + jnp.log1p(jnp.exp(-jnp.abs(x)))


def _rms_rows(x, w):
    ms = jnp.mean(x * x, axis=-1, keepdims=True)
    return x * lax.rsqrt(ms + EPS) * w


def _gelu_tanh(x):
    c = math.sqrt(2.0 / math.pi)
    return x * (0.5 * (1.0 + jnp.tanh(c * (x + 0.044715 * (x * x * x)))))


def _const_spec(shape):
    nd = len(shape)
    return pl.BlockSpec(shape, lambda *_: (0,) * nd)


def _params(sem):
    return pltpu.CompilerParams(dimension_semantics=sem, vmem_limit_bytes=VMEM_LIMIT)


def _ada_kernel(c_ref, w_ref, b_ref, o_ref):
    c = c_ref[...]
    o_ref[0] = _mm(_silu(c), w_ref[0]) + b_ref[0]


def _ada_call(c_all, w_ada, b_ada):
    depth = w_ada.shape[0]
    nrow = c_all.shape[0]
    return pl.pallas_call(
        _ada_kernel,
        grid=(depth, 6),
        in_specs=[pl.BlockSpec((nrow, D_MODEL), lambda l, j: (0, 0)),
                  pl.BlockSpec((1, D_MODEL, D_MODEL), lambda l, j: (l, 0, j)),
                  pl.BlockSpec((1, 1, D_MODEL), lambda l, j: (l, 0, j))],
        out_specs=pl.BlockSpec((1, nrow, D_MODEL), lambda l, j: (l, 0, j)),
        out_shape=jax.ShapeDtypeStruct((depth, nrow, 6 * D_MODEL), F32),
        compiler_params=_params(("parallel", "parallel")),
        name="adaln",
    )(c_all, w_ada, b_ada.reshape(depth, 1, 6 * D_MODEL))


def _inproj_kernel(x_ref, mod_ref, nw_ref, wg_ref, wd_ref, ws_ref, wsm_ref, wup_ref, bal_ref, sp_ref,
                   zg_ref, zd_ref, zs_ref):
    nb, lt, d = x_ref.shape
    x = x_ref[...]
    h = _rms_rows(x, nw_ref[...]) * (1.0 + mod_ref[:, 1:2, :]) + mod_ref[:, 0:1, :]
    h2 = h.reshape(nb * lt, d).astype(BF16)

    zg = _mm(h2, wg_ref[...])
    zsm = _mm(h2, wsm_ref[...])
    la_pre = _mm(zsm, wup_ref[...]) + bal_ref[...]
    log_a = (jnp.minimum(la_pre, 0.0) - jnp.log1p(jnp.exp(-jnp.abs(la_pre)))) * (1.0 / GLA_TAU)
    zg_ref[:, :, 0:512] = zg[:, 0:512].reshape(nb, lt, 512)
    zg_ref[:, :, 512:768] = log_a.reshape(nb, lt, 256)
    zg_ref[:, :, 768:1792] = zg[:, 512:1536].reshape(nb, lt, 1024)

    zd_ref[:, :, 0:2048] = _mm(h2, wd_ref[...]).reshape(nb, lt, 2048)
    lane = lax.broadcasted_iota(jnp.int32, zsm.shape, 1)
    beta = _sigmoid(zsm)
    log_g = -sp_ref[0:1, :] * _softplus(zsm + sp_ref[1:2, :])
    is_beta = (lane >= SMALL_BETA0) & (lane < SMALL_A0)
    is_a = (lane >= SMALL_A0) & (lane < SMALL_A0 + GDN_HEADS)
    small = jnp.where(is_beta, beta, jnp.where(is_a, log_g, 0.0))
    zd_ref[:, :, 2048:2176] = small.reshape(nb, lt, LANES)

    zs = _mm(h2, ws_ref[...])
    for i in range(nb):
        zs_ref[:, BRANCH_W * i:BRANCH_W * (i + 1)] = zs[lt * i:lt * (i + 1), :]


def _inproj_call(x, mod, p, nb, lt):
    B, T, D = x.shape
    grid = (B // nb, T // lt)
    tok = lambda w: pl.BlockSpec((nb, lt, w), lambda b, t: (b, t, 0))
    return pl.pallas_call(
        _inproj_kernel,
        grid=grid,
        in_specs=[tok(D),
                  pl.BlockSpec((nb, 6, D), lambda b, t: (b, 0, 0)),
                  _const_spec((1, D)),
                  _const_spec(p["w_gla"].shape), _const_spec(p["w_gdn"].shape), _const_spec(p["w_s5"].shape),
                  _const_spec(p["w_small"].shape), _const_spec(p["w_up"].shape), _const_spec((1, GLA_QK)),
                  _const_spec((2, LANES))],
        out_specs=[tok(1792), tok(2176), pl.BlockSpec((lt, nb * BRANCH_W), lambda b, t: (t, b))],
        out_shape=[jax.ShapeDtypeStruct((B, T, 1792), F32),
                   jax.ShapeDtypeStruct((B, T, 2176), F32),
                   jax.ShapeDtypeStruct((T, B * BRANCH_W), F32)],
        compiler_params=_params(("parallel", "parallel")),
        name="inproj",
    )(x, mod, p["norm1_w"], p["w_gla"], p["w_gdn"], p["w_s5"], p["w_small"], p["w_up"], p["b_alpha"],
      p["gdn_small"])


def _gla_kernel(z_ref, s0_ref, nw_ref, o_ref, s_ref, *, n_chunks):
    L = CHUNK

    @pl.when(pl.program_id(1) == 0)
    def _():
        s_ref[...] = s0_ref[...]

    row = lax.broadcasted_iota(jnp.int32, (L, L), 0)
    col = lax.broadcasted_iota(jnp.int32, (L, L), 1)
    causal = row >= col
    lane = lax.broadcasted_iota(jnp.int32, (L, LANES), 1)
    own_half = (lane < GLA_DK, lane >= GLA_DK)
    nw = nw_ref[...]
    lt = n_chunks * L
    brow = lax.broadcasted_iota(jnp.int32, (lt, lt), 0)
    bcol = lax.broadcasted_iota(jnp.int32, (lt, lt), 1)
    shift = L.bit_length() - 1
    chunk_tril = ((brow >= bcol) & ((brow >> shift) == (bcol >> shift))).astype(BF16)
    b_all = _mm_sel(chunk_tril, z_ref[0, :, 512:768])

    probs = [(c, h) for c in range(n_chunks) for h in range(GLA_HEADS)]
    rows_of = lambda c: slice(L * c, L * (c + 1))
    pair_of = lambda h: slice(LANES * (h // 2), LANES * (h // 2) + LANES)
    qi_, ki_, qs_, kl_, dl_ = {}, {}, {}, {}, {}
    for c in range(n_chunks):
        q = z_ref[0, rows_of(c), 0:256]
        k = z_ref[0, rows_of(c), 256:512]
        b = b_all[rows_of(c), :]
        b_mid = b[L // 2 - 1:L // 2, :]
        b_last = b[L - 1:L, :]
        qi_[c] = q * jnp.exp(b - b_mid)
        ki_[c] = k * jnp.exp(b_mid - b)
        qs_[c] = q * jnp.exp(b)
        kl_[c] = k * jnp.exp(b_last - b)
        dl_[c] = jnp.exp(b_last)
    v_ = {p: z_ref[0, rows_of(p[0]), 768 + GLA_DV * p[1]:768 + GLA_DV * (p[1] + 1)] for p in probs}
    att_ = {p: jnp.where(causal, _mm_nt(qi_[p[0]][:, pair_of(p[1])],
                                        jnp.where(own_half[p[1] % 2], ki_[p[0]][:, pair_of(p[1])], 0.0)), 0.0)
            for p in probs}
    kv_ = {p: _mm_tn(v_[p], jnp.where(own_half[p[1] % 2], kl_[p[0]][:, pair_of(p[1])], 0.0)) for p in probs}
    av_ = {p: _mm(att_[p], v_[p]) for p in probs}
    st_ = {(0, h): s_ref[0, h] for h in range(GLA_HEADS)}
    for c in range(n_chunks):
        for h in range(GLA_HEADS):
            st_[(c + 1, h)] = st_[(c, h)] * dl_[c][:, pair_of(h)] + kv_[(c, h)]
    for p in probs:
        c, h = p
        o_h = av_[p] + _mm_nt(qs_[c][:, pair_of(h)], st_[p])
        gate = z_ref[0, rows_of(c), 1280 + GLA_DV * h:1280 + GLA_DV * (h + 1)]
        o_ref[0, rows_of(c), GLA_DV * h:GLA_DV * (h + 1)] = _rms_rows(o_h, nw) * _silu(gate)
    for h in range(GLA_HEADS):
        s_ref[0, h] = st_[(n_chunks, h)]


def _gla_call(zg, s0t, norm_w, lt):
    B, T, _ = zg.shape
    st_spec = pl.BlockSpec((1, GLA_HEADS, GLA_DV, LANES), lambda b, t: (b, 0, 0, 0))
    return pl.pallas_call(
        functools.partial(_gla_kernel, n_chunks=lt // CHUNK),
        grid=(B, T // lt),
        in_specs=[pl.BlockSpec((1, lt, 1792), lambda b, t: (b, t, 0)), st_spec, _const_spec((1, GLA_DV))],
        out_specs=[pl.BlockSpec((1, lt, GLA_V), lambda b, t: (b, t, 0)), st_spec],
        out_shape=[jax.ShapeDtypeStruct((B, T, GLA_V), F32),
                   jax.ShapeDtypeStruct((B, GLA_HEADS, GLA_DV, LANES), F32)],
        compiler_params=_params(("parallel", "arbitrary")),
        name="gla",
    )(zg, s0t, norm_w)


def _gdn_kernel(z_ref, cb_ref, s0_ref, cw_ref, nw_ref, o_ref, s_ref, cbo_ref, xbuf, qkv, bfull_ref, gfull_ref,
                grow_ref, glast_ref, *, lt):
    L = CHUNK
    n_chunks = lt // L
    t = pl.program_id(1)

    @pl.when(t == 0)
    def _():
        s_ref[...] = s0_ref[...]
        xbuf[0:SUBLANES, :] = cb_ref[0]

    @pl.when(t > 0)
    def _():
        xbuf[0:SUBLANES, :] = xbuf[lt:lt + SUBLANES, :]

    xbuf[SUBLANES:SUBLANES + lt, :] = z_ref[0, :, 0:GDN_CONV_CH]
    cbo_ref[0] = xbuf[lt:lt + SUBLANES, :]

    base = SUBLANES - (CONV_W - 1)
    conv = xbuf[base:base + lt, :] * cw_ref[0:1, :]
    for j in range(1, CONV_W):
        conv = conv + xbuf[base + j:base + j + lt, :] * cw_ref[j:j + 1, :]
    act = _silu(conv)
    for h in range(GDN_HEADS):
        for part, scale in ((0, GDN_DK ** -0.5), (1, 1.0)):
            sl = slice(part * GDN_QK + GDN_DK * h, part * GDN_QK + GDN_DK * (h + 1))
            xh = act[:, sl]
            ss = jnp.sum(xh * xh, axis=-1, keepdims=True)
            qkv[:, sl] = xh * (lax.rsqrt(ss + EPS) * scale)
    qkv[:, 2 * GDN_QK:] = act[:, 2 * GDN_QK:]

    row = lax.broadcasted_iota(jnp.int32, (L, L), 0)
    col = lax.broadcasted_iota(jnp.int32, (L, L), 1)
    causal = row >= col
    strict = row > col
    eye = (row == col).astype(F32)
    diag16 = (row >> 4) == (col >> 4)
    diag32 = (row >> 5) == (col >> 5)
    off32 = diag32 & jnp.logical_not(diag16)
    off64 = jnp.logical_not(diag32)
    sel_r = lax.broadcasted_iota(jnp.int32, (SUBLANES, LANES), 0)
    sel_c = lax.broadcasted_iota(jnp.int32, (SUBLANES, LANES), 1)
    sel = (sel_c == sel_r + SMALL_A0).astype(BF16)
    brow = lax.broadcasted_iota(jnp.int32, (lt, lt), 0)
    bcol = lax.broadcasted_iota(jnp.int32, (lt, lt), 1)
    shift = L.bit_length() - 1
    chunk_tril = ((brow >= bcol) & ((brow >> shift) == (bcol >> shift))).astype(BF16)
    small_all = z_ref[0, :, 2048:2176]
    gs_all = _mm_sel(chunk_tril, small_all)
    grow_all = _mm_nt_sel(sel, gs_all)
    e_r = lax.broadcasted_iota(jnp.int32, (LANES, GDN_HEADS * LANES), 0)
    e_h = lax.broadcasted_iota(jnp.int32, (LANES, GDN_HEADS * LANES), 1) >> (LANES.bit_length() - 1)
    bfull_ref[...] = _mm_sel_r(small_all, (e_r == e_h + SMALL_BETA0).astype(BF16))
    gfull = _mm_sel_r(gs_all, (e_r == e_h + SMALL_A0).astype(BF16))
    gfull_ref[...] = gfull
    for c in range(n_chunks):
        grow_ref[c] = grow_all[:, L * c:L * (c + 1)]
        glast_ref[c:c + 1, :] = gfull[L * (c + 1) - 1:L * (c + 1), :]
    nw = nw_ref[...]

    probs = [(c, h) for c in range(n_chunks) for h in range(GDN_HEADS)]
    rows_of = lambda c: slice(L * c, L * (c + 1))
    head_of = lambda h: slice(LANES * h, LANES * (h + 1))
    q_ = {p: qkv[rows_of(p[0]), GDN_DK * p[1]:GDN_DK * (p[1] + 1)] for p in probs}
    k_ = {p: qkv[rows_of(p[0]), GDN_QK + GDN_DK * p[1]:GDN_QK + GDN_DK * (p[1] + 1)] for p in probs}
    beta_ = {p: bfull_ref[rows_of(p[0]), head_of(p[1])] for p in probs}
    g_ = {p: gfull_ref[rows_of(p[0]), head_of(p[1])] for p in probs}
    dec_ = {p: jnp.exp(jnp.where(causal, g_[p][:, 0:L] - grow_ref[p[0]][p[1]:p[1] + 1, :], -jnp.inf))
            for p in probs}
    kk_ = {p: _mm_nt(k_[p], k_[p]) for p in probs}
    qk_ = {p: _mm_nt(q_[p], k_[p]) for p in probs}
    a_ = {p: jnp.where(strict, beta_[p][:, 0:L] * kk_[p] * dec_[p], 0.0) for p in probs}
    qk_ = {p: jnp.where(causal, qk_[p] * dec_[p], 0.0) for p in probs}
    pw_ = {p: jnp.where(diag16, -a_[p], 0.0) for p in probs}
    tinv_ = {p: eye + pw_[p] for p in probs}
    for _ in range(3):
        pw_ = {p: _mm(pw_[p], pw_[p]) for p in probs}
        tinv_ = {p: tinv_[p] + _mm(tinv_[p], pw_[p]) for p in probs}
    for off in (off32, off64):
        te_ = {p: _mm(tinv_[p], jnp.where(off, a_[p], 0.0)) for p in probs}
        tinv_ = {p: tinv_[p] - _mm(te_[p], tinv_[p]) for p in probs}
    eg_ = {p: jnp.exp(g_[p]) for p in probs}
    sol_ = {}
    for p in probs:
        v_p = qkv[rows_of(p[0]), 2 * GDN_QK + GDN_DV * p[1]:2 * GDN_QK + GDN_DV * (p[1] + 1)]
        rhs = jnp.concatenate([beta_[p] * v_p, (beta_[p] * eg_[p]) * k_[p]], axis=-1)
        sol_[p] = _mm(tinv_[p], rhs)

    st_ = [s_ref[0, h] for h in range(GDN_HEADS)]
    for c in range(n_chunks):
        heads = [(c, h) for h in range(GDN_HEADS)]
        g_last = glast_ref[c:c + 1, :]
        u_ = {p: sol_[p][:, 0:GDN_DV] - _mm(sol_[p][:, GDN_DV:], st_[p[1]]) for p in heads}
        o_ = {p: _mm(qk_[p], u_[p]) + _mm(q_[p] * eg_[p], st_[p[1]]) for p in heads}
        st_ = [st_[h] * jnp.exp(g_last[:, head_of(h)])
               + _mm_tn(k_[(c, h)] * jnp.exp(g_last[:, head_of(h)] - g_[(c, h)]), u_[(c, h)])
               for h in range(GDN_HEADS)]
        for h in range(GDN_HEADS):
            gate = z_ref[0, rows_of(c), GDN_CONV_CH + GDN_DV * h:GDN_CONV_CH + GDN_DV * (h + 1)]
            o_ref[0, rows_of(c), GDN_DV * h:GDN_DV * (h + 1)] = _rms_rows(o_[(c, h)], nw) * _silu(gate)
    for h in range(GDN_HEADS):
        s_ref[0, h] = st_[h]


def _gdn_call(zd, cb8, s0, conv_w, norm_w, lt):
    B, T, _ = zd.shape
    st_spec = pl.BlockSpec((1, GDN_HEADS, GDN_DK, GDN_DV), lambda b, t: (b, 0, 0, 0))
    cb_spec = pl.BlockSpec((1, SUBLANES, GDN_CONV_CH), lambda b, t: (b, 0, 0))
    return pl.pallas_call(
        functools.partial(_gdn_kernel, lt=lt),
        grid=(B, T // lt),
        in_specs=[pl.BlockSpec((1, lt, 2176), lambda b, t: (b, t, 0)), cb_spec, st_spec,
                  _const_spec((CONV_W, GDN_CONV_CH)), _const_spec((1, GDN_DV))],
        out_specs=[pl.BlockSpec((1, lt, GDN_V), lambda b, t: (b, t, 0)), st_spec, cb_spec],
        out_shape=[jax.ShapeDtypeStruct((B, T, GDN_V), F32),
                   jax.ShapeDtypeStruct((B, GDN_HEADS, GDN_DK, GDN_DV), F32),
                   jax.ShapeDtypeStruct((B, SUBLANES, GDN_CONV_CH), F32)],
        scratch_shapes=[pltpu.VMEM((lt + SUBLANES, GDN_CONV_CH), F32), pltpu.VMEM((lt, GDN_CONV_CH), F32),
                        pltpu.VMEM((lt, GDN_HEADS * LANES), F32), pltpu.VMEM((lt, GDN_HEADS * LANES), F32),
                        pltpu.VMEM((lt // CHUNK, SUBLANES, CHUNK), F32),
                        pltpu.VMEM((SUBLANES, GDN_HEADS * LANES), F32)],
        compiler_params=_params(("parallel", "arbitrary")),
        name="gdn",
    )(zd, cb8, s0, conv_w, norm_w)


S5_LANE_CHUNK = 512


def _s5_kernel(u_ref, h0_ref, bb_ref, cc_ref, lam_ref, d_ref, y_ref, h_ref, bu_ref, *, lt):
    nb = SUBLANES

    @pl.when(pl.program_id(1) == 0)
    def _():
        h_ref[...] = h0_ref[...]

    u = u_ref[...].reshape(lt * nb, BRANCH_W)
    ch, sh = BRANCH_W // 2, S5_HALF // 2
    for part in range(2):
        for hb in range(2):
            bu_ref[:, part * S5_HALF + hb * sh:part * S5_HALF + (hb + 1) * sh] = _mm(
                u[:, hb * ch:(hb + 1) * ch], bb_ref[2 * part + hb])
    for j in range(S5_HALF // S5_LANE_CHUNK):
        re = slice(S5_LANE_CHUNK * j, S5_LANE_CHUNK * (j + 1))
        im = slice(S5_HALF + S5_LANE_CHUNK * j, S5_HALF + S5_LANE_CHUNK * (j + 1))
        lr = jnp.broadcast_to(lam_ref[0:1, re], (nb, S5_LANE_CHUNK))
        li = jnp.broadcast_to(lam_ref[1:2, re], (nb, S5_LANE_CHUNK))

        def step(tt, carry, re=re, im=im, lr=lr, li=li):
            hr, hi = carry
            rows = pl.ds(pl.multiple_of(tt * nb, nb), nb)
            nr = lr * hr - li * hi + bu_ref[rows, re]
            ni = lr * hi + li * hr + bu_ref[rows, im]
            bu_ref[rows, re] = nr
            bu_ref[rows, im] = ni
            return nr, ni

        hr, hi = lax.fori_loop(0, lt, step, (h_ref[:, re], h_ref[:, im]))
        h_ref[:, re] = hr
        h_ref[:, im] = hi
    ys = [_mm(bu_ref[:, hb * sh:(hb + 1) * sh], cc_ref[hb])
          + _mm(bu_ref[:, S5_HALF + hb * sh:S5_HALF + (hb + 1) * sh], cc_ref[2 + hb]) for hb in range(2)]
    y = jnp.concatenate(ys, axis=-1) + d_ref[...] * u
    y_ref[...] = y.reshape(lt, nb, BRANCH_W)


def _s5_call(zs, h0, p, lt):
    T = zs.shape[0]
    B = h0.shape[0]
    nb = SUBLANES
    h_spec = pl.BlockSpec((nb, 2 * S5_HALF), lambda b, t: (b, 0))
    tok_spec = pl.BlockSpec((lt, None, nb, BRANCH_W), lambda b, t: (t, b, 0, 0))
    y, h = pl.pallas_call(
        functools.partial(_s5_kernel, lt=lt),
        grid=(B // nb, T // lt),
        in_specs=[tok_spec, h_spec,
                  _const_spec((4, BRANCH_W // 2, S5_HALF // 2)), _const_spec((4, S5_HALF // 2, BRANCH_W // 2)),
                  _const_spec((2, S5_HALF)), _const_spec((1, BRANCH_W))],
        out_specs=[tok_spec, h_spec],
        out_shape=[jax.ShapeDtypeStruct((T, B // nb, nb, BRANCH_W), F32),
                   jax.ShapeDtypeStruct((B, 2 * S5_HALF), F32)],
        scratch_shapes=[pltpu.VMEM((lt * nb, 2 * S5_HALF), F32)],
        compiler_params=_params(("parallel", "arbitrary")),
        name="s5",
    )(zs.reshape(T, B // nb, nb, BRANCH_W), h0, p["s5_bblk"], p["s5_cblk"], p["s5_lam"], p["s5_d"])
    return y.reshape(T, B * BRANCH_W), h


def _merge_kernel(x_ref, mod_ref, nw_ref, oa_ref, ob_ref, ys_ref, wgate_ref, bgate_ref, wa_ref, wb_ref, wc_ref,
                  wglu_ref, bglu_ref, wout_ref, o_ref):
    nb, lt, d = x_ref.shape
    n = nb * lt
    x = x_ref[...]
    h = _rms_rows(x, nw_ref[...]) * (1.0 + mod_ref[:, 1:2, :]) + mod_ref[:, 0:1, :]
    gate = _sigmoid(_mm(h.reshape(n, d), wgate_ref[...]) + bgate_ref[...])
    y_a = _mm(oa_ref[...].reshape(n, BRANCH_W), wa_ref[...])
    y_b = _mm(ob_ref[...].reshape(n, BRANCH_W), wb_ref[...])
    ys = [ys_ref[:, BRANCH_W * i:BRANCH_W * (i + 1)] for i in range(nb)]
    yc = _gelu_tanh(ys[0] if nb == 1 else jnp.concatenate(ys, axis=0))
    yc = yc * _sigmoid(_mm(yc, wglu_ref[...]) + bglu_ref[...])
    y_c = _mm(yc, wc_ref[...])
    mix = gate[:, 0:d] * y_a + gate[:, d:2 * d] * y_b + gate[:, 2 * d:3 * d] * y_c
    m = _mm(mix, wout_ref[...])
    o_ref[...] = x + mod_ref[:, 2:3, :] * m.reshape(nb, lt, d)


def _merge_call(x, mod, oa, ob, ys, p, nb, lt):
    B, T, D = x.shape
    tok = lambda w: pl.BlockSpec((nb, lt, w), lambda b, t: (b, t, 0))
    return pl.pallas_call(
        _merge_kernel,
        grid=(B // nb, T // lt),
        in_specs=[tok(D), pl.BlockSpec((nb, 6, D), lambda b, t: (b, 0, 0)), _const_spec((1, D)),
                  tok(BRANCH_W), tok(BRANCH_W), pl.BlockSpec((lt, nb * BRANCH_W), lambda b, t: (t, b)),
                  _const_spec((D, 3 * D)), _const_spec((1, 3 * D)),
                  _const_spec((BRANCH_W, D)), _const_spec((BRANCH_W, D)), _const_spec((BRANCH_W, D)),
                  _const_spec((BRANCH_W, BRANCH_W)), _const_spec((1, BRANCH_W)), _const_spec((D, D))],
        out_specs=tok(D),
        out_shape=jax.ShapeDtypeStruct((B, T, D), F32),
        compiler_params=_params(("parallel", "parallel")),
        name="merge",
    )(x, mod, p["norm1_w"], oa, ob, ys, p["w_gate"], p["b_gate"], p["w_br_a"], p["w_br_b"], p["w_br_c"],
      p["w_glu"], p["b_glu"], p["w_out"])


def _moe_kernel(x_ref, mod_ref, nw_ref, wr_ref, br_ref, ex_ref, wg_ref, wu_ref, wd_ref, fw_ref, o_ref,
                hn_ref, cexp_ref, acc_ref, *, final_norm):
    nb, lt, d = x_ref.shape
    n = nb * lt
    j = pl.program_id(2)
    gw = EXPERTS_PER_GROUP * D_FF

    @pl.when(j == 0)
    def _():
        x = x_ref[...]
        hn = (_rms_rows(x, nw_ref[...]) * (1.0 + mod_ref[:, 4:5, :]) + mod_ref[:, 3:4, :]).reshape(n, d)
        hn_ref[...] = hn.astype(BF16)
        logits = _mm_f32(hn, wr_ref[...]) + br_ref[...]
        lane = lax.broadcasted_iota(jnp.int32, (n, LANES), 1)
        neg = -jnp.inf
        gl = jnp.where(lane < MOE_GROUPS, logits, neg)
        gmax = jnp.max(gl, axis=-1, keepdims=True)
        g_sel = jnp.min(jnp.where(gl == gmax, lane, LANES), axis=-1, keepdims=True)
        p_sel = 1.0 / jnp.sum(jnp.exp(gl - gmax), axis=-1, keepdims=True)
        e_id = lane - ROUTE_E0
        in_group = (e_id >= 0) & (e_id < N_EXPERTS) & ((e_id >> 2) == g_sel)
        el = jnp.where(in_group, logits, neg)
        m1 = jnp.max(el, axis=-1, keepdims=True)
        i1 = jnp.min(jnp.where(el == m1, lane, LANES), axis=-1, keepdims=True)
        el2 = jnp.where(lane == i1, neg, el)
        m2 = jnp.max(el2, axis=-1, keepdims=True)
        i2 = jnp.min(jnp.where(el2 == m2, lane, LANES), axis=-1, keepdims=True)
        e2 = jnp.exp(m2 - m1)
        w1 = p_sel / (1.0 + e2)
        w2 = p_sel * e2 / (1.0 + e2)
        comb = jnp.where(lane == i1, w1, jnp.where(lane == i2, w2, 0.0))
        for g in range(MOE_GROUPS):
            cexp_ref[g] = _mm(comb, ex_ref[:, gw * g:gw * (g + 1)])
        acc_ref[...] = jnp.zeros_like(acc_ref)

    hn = hn_ref[...]
    he = _silu(_mm(hn, wg_ref[0])) * _mm(hn, wu_ref[0])
    acc_ref[...] += _mm(he * cexp_ref[j], wd_ref[0])

    @pl.when(j == MOE_GROUPS - 1)
    def _():
        out = x_ref[...] + mod_ref[:, 5:6, :] * acc_ref[...].reshape(nb, lt, d)
        if final_norm:
            out = _rms_rows(out, fw_ref[...])
        o_ref[...] = out


def _moe_call(x, mod, p, final_w, nb, lt, final_norm):
    B, T, D = x.shape
    n = nb * lt
    gw = EXPERTS_PER_GROUP * D_FF
    tok = pl.BlockSpec((nb, lt, D), lambda b, t, j: (b, t, 0))
    return pl.pallas_call(
        functools.partial(_moe_kernel, final_norm=final_norm),
        grid=(B // nb, T // lt, MOE_GROUPS),
        in_specs=[tok, pl.BlockSpec((nb, 6, D), lambda b, t, j: (b, 0, 0)), _const_spec((1, D)),
                  _const_spec((D, LANES)), _const_spec((1, LANES)), _const_spec((LANES, N_EXPERTS * D_FF)),
                  pl.BlockSpec((1, D, gw), lambda b, t, j: (j, 0, 0)),
                  pl.BlockSpec((1, D, gw), lambda b, t, j: (j, 0, 0)),
                  pl.BlockSpec((1, gw, D), lambda b, t, j: (j, 0, 0)),
                  _const_spec((1, D))],
        out_specs=tok,
        out_shape=jax.ShapeDtypeStruct((B, T, D), F32),
        scratch_shapes=[pltpu.VMEM((n, D), BF16), pltpu.VMEM((MOE_GROUPS, n, gw), F32), pltpu.VMEM((n, D), F32)],
        compiler_params=_params(("parallel", "parallel", "arbitrary")),
        name="moe",
    )(x, mod, p["norm2_w"], p["w_route"], p["b_route"], p["route_expand"], p["w_eg"], p["w_eu"], p["w_ed"],
      final_w)


def _prep_layer(l, norm1_w, norm2_w, w_in, w_alpha_up, b_alpha, gla_norm_w, conv_w, gdn_a_log, gdn_dt_bias,
                gdn_norm_w, s5_a_re, s5_a_im, s5_log_dt, s5_b_re, s5_b_im, s5_c_re, s5_c_im, s5_d, w_glu, b_glu,
                w_br_a, w_br_b, w_br_c, w_gate, b_gate, w_out, w_route_group, b_route_group, w_route_expert,
                b_route_expert, w_exp_gate, w_exp_up, w_exp_down):
    D = D_MODEL
    w = w_in[l]
    o = 0
    cols = {}
    for name, size in (("qa", GLA_QK), ("ka", GLA_QK), ("va", GLA_V), ("ga", GLA_V), ("aa", GLA_LOWRANK),
                       ("qkv", GDN_CONV_CH), ("beta", GDN_HEADS), ("a", GDN_HEADS), ("gb", GDN_V),
                       ("u", BRANCH_W)):
        cols[name] = w[:, o:o + size]
        o += size
    p = {}
    p["norm1_w"] = norm1_w[l].reshape(1, D)
    p["norm2_w"] = norm2_w[l].reshape(1, D)
    p["w_gla"] = jnp.concatenate([cols["qa"] * (GLA_DK ** -0.5), cols["ka"], cols["va"], cols["ga"]],
                                 axis=1).astype(BF16)
    p["w_gdn"] = jnp.concatenate([cols["qkv"], cols["gb"]], axis=1).astype(BF16)
    p["w_s5"] = cols["u"].astype(BF16)
    small = jnp.zeros((D, LANES), F32)
    small = small.at[:, 0:GLA_LOWRANK].set(cols["aa"])
    small = small.at[:, SMALL_BETA0:SMALL_BETA0 + GDN_HEADS].set(cols["beta"])
    small = small.at[:, SMALL_A0:SMALL_A0 + GDN_HEADS].set(cols["a"])
    p["w_small"] = small.astype(BF16)
    p["w_up"] = jnp.zeros((LANES, GLA_QK), F32).at[0:GLA_LOWRANK].set(w_alpha_up[l]).astype(BF16)
    p["b_alpha"] = b_alpha[l].reshape(1, GLA_QK)
    gs = jnp.zeros((2, LANES), F32)
    gs = gs.at[0, SMALL_A0:SMALL_A0 + GDN_HEADS].set(jnp.exp(gdn_a_log[l]))
    gs = gs.at[1, SMALL_A0:SMALL_A0 + GDN_HEADS].set(gdn_dt_bias[l])
    p["gdn_small"] = gs
    p["gla_norm_w"] = gla_norm_w[l].reshape(1, GLA_DV)
    p["gdn_norm_w"] = gdn_norm_w[l].reshape(1, GDN_DV)
    p["conv_w"] = conv_w[l]

    a_re, a_im = s5_a_re[l], s5_a_im[l]
    dt = jnp.exp(s5_log_dt[l])[:, None]
    mag = jnp.exp(a_re * dt)
    lr, li = mag * jnp.cos(a_im * dt), mag * jnp.sin(a_im * dt)
    den = a_re * a_re + a_im * a_im
    f_re = ((lr - 1.0) * a_re + li * a_im) / den
    f_im = (li * a_re - (lr - 1.0) * a_im) / den
    bb_re = f_re[..., None] * s5_b_re[l] - f_im[..., None] * s5_b_im[l]
    bb_im = f_re[..., None] * s5_b_im[l] + f_im[..., None] * s5_b_re[l]
    gh = S5_GROUPS // 2
    eye_h = jnp.eye(gh, dtype=F32)
    blk = lambda m: jnp.einsum("gpc,gh->gchp", m, eye_h).reshape(gh * S5_GROUP, gh * S5_STATE)
    cblk = lambda m: jnp.einsum("gcp,gh->hpgc", m, eye_h).reshape(gh * S5_STATE, gh * S5_GROUP)
    p["s5_bblk"] = jnp.stack([blk(m[hb * gh:(hb + 1) * gh]) for m in (bb_re, bb_im) for hb in range(2)]
                             ).astype(BF16)
    p["s5_cblk"] = jnp.stack([cblk(m[hb * gh:(hb + 1) * gh]) for m in (s5_c_re[l], -s5_c_im[l])
                              for hb in range(2)]).astype(BF16)
    p["s5_lam"] = jnp.stack([lr.reshape(S5_HALF), li.reshape(S5_HALF)])
    p["s5_d"] = s5_d[l].reshape(1, BRANCH_W)

    p["w_gate"] = w_gate[l].astype(BF16)
    p["b_gate"] = b_gate[l].reshape(1, 3 * D)
    p["w_br_a"] = w_br_a[l].astype(BF16)
    p["w_br_b"] = w_br_b[l].astype(BF16)
    p["w_br_c"] = w_br_c[l].astype(BF16)
    p["w_glu"] = w_glu[l].astype(BF16)
    p["b_glu"] = b_glu[l].reshape(1, BRANCH_W)
    p["w_out"] = w_out[l].astype(BF16)

    wr = jnp.zeros((D, LANES), F32)
    wr = wr.at[:, 0:MOE_GROUPS].set(w_route_group[l]).at[:, ROUTE_E0:ROUTE_E0 + N_EXPERTS].set(w_route_expert[l])
    br = jnp.zeros((1, LANES), F32)
    br = br.at[0, 0:MOE_GROUPS].set(b_route_group[l]).at[0, ROUTE_E0:ROUTE_E0 + N_EXPERTS].set(b_route_expert[l])
    p["w_route"], p["b_route"] = wr, br
    e_of_col = jnp.arange(N_EXPERTS * D_FF) // D_FF
    p["route_expand"] = (jnp.arange(LANES)[:, None] == (e_of_col[None, :] + ROUTE_E0)).astype(F32)
    gw = EXPERTS_PER_GROUP * D_FF
    p["w_eg"] = w_exp_gate[l].reshape(MOE_GROUPS, EXPERTS_PER_GROUP, D, D_FF).transpose(0, 2, 1, 3) \
        .reshape(MOE_GROUPS, D, gw).astype(BF16)
    p["w_eu"] = w_exp_up[l].reshape(MOE_GROUPS, EXPERTS_PER_GROUP, D, D_FF).transpose(0, 2, 1, 3) \
        .reshape(MOE_GROUPS, D, gw).astype(BF16)
    p["w_ed"] = w_exp_down[l].reshape(MOE_GROUPS, gw, D).astype(BF16)
    return p


def _gla_state_in(s):
    st = jnp.swapaxes(s, -1, -2)
    z = jnp.zeros_like(st)
    halves = [jnp.concatenate([st[:, h], z[:, h]] if h % 2 == 0 else [z[:, h], st[:, h]], axis=-1)
              for h in range(GLA_HEADS)]
    return jnp.stack(halves, axis=1)


def _gla_state_out(st):
    parts = [st[:, h, :, GLA_DK * (h % 2):GLA_DK * (h % 2 + 1)] for h in range(GLA_HEADS)]
    return jnp.swapaxes(jnp.stack(parts, axis=1), -1, -2)


def _trunk(x, mods, params, states, final_w, nb, lt, lt_rec, lt_s5):
    new = []
    for l, p in enumerate(params):
        s_gla, s_gdn, cbuf, s5_re, s5_im = states[l]
        B = x.shape[0]
        zg, zd, zs = _inproj_call(x, mods[l], p, nb, lt)
        oa, gla_t = _gla_call(zg, _gla_state_in(s_gla), p["gla_norm_w"], lt_rec)
        cb8 = jnp.concatenate([jnp.zeros((B, SUBLANES - (CONV_W - 1), GDN_CONV_CH), F32), cbuf], axis=1)
        ob, gdn_new, cb_new = _gdn_call(zd, cb8, s_gdn, p["conv_w"], p["gdn_norm_w"], lt_rec)
        h0 = jnp.concatenate([s5_re.reshape(B, S5_HALF), s5_im.reshape(B, S5_HALF)], axis=1)
        ys, h_new = _s5_call(zs, h0, p, lt_s5)
        x = _merge_call(x, mods[l], oa, ob, ys, p, nb, lt)
        x = _moe_call(x, mods[l], p, final_w, nb, lt, final_norm=(l == len(params) - 1))
        new.append((_gla_state_out(gla_t), gdn_new, cb_new[:, SUBLANES - (CONV_W - 1):],
                    h_new[:, :S5_HALF].reshape(B, S5_GROUPS, S5_STATE),
                    h_new[:, S5_HALF:].reshape(B, S5_GROUPS, S5_STATE)))
    return x, [jnp.stack([new[l][i] for l in range(len(params))]) for i in range(5)]


def kernel(x_prompt, x_sample, c_prompt, c_sample, state_gla, state_gdn, cache_gdn_conv, state_s5_re, state_s5_im, norm1_w, norm2_w, final_norm_w, w_ada, b_ada, w_in, w_alpha_up, b_alpha, gla_norm_w, conv_w, gdn_a_log, gdn_dt_bias, gdn_norm_w, s5_a_re, s5_a_im, s5_log_dt, s5_b_re, s5_b_im, s5_c_re, s5_c_im, s5_d, w_glu, b_glu, w_br_a, w_br_b, w_br_c, w_gate, b_gate, w_out, w_route_group, b_route_group, w_route_expert, b_route_expert, w_exp_gate, w_exp_up, w_exp_down):
    depth = w_in.shape[0]
    bp, tp, _ = x_prompt.shape
    bs, ts, _ = x_sample.shape
    params = [_prep_layer(l, norm1_w, norm2_w, w_in, w_alpha_up, b_alpha, gla_norm_w, conv_w, gdn_a_log,
                          gdn_dt_bias, gdn_norm_w, s5_a_re, s5_a_im, s5_log_dt, s5_b_re, s5_b_im, s5_c_re,
                          s5_c_im, s5_d, w_glu, b_glu, w_br_a, w_br_b, w_br_c, w_gate, b_gate, w_out,
                          w_route_group, b_route_group, w_route_expert, b_route_expert, w_exp_gate, w_exp_up,
                          w_exp_down) for l in range(depth)]
    final_w = final_norm_w.reshape(1, D_MODEL)

    mod = _ada_call(jnp.concatenate([c_prompt, c_sample], axis=0), w_ada, b_ada)
    mod = mod.reshape(depth, bp + bs, 6, D_MODEL)
    mods_p = [mod[l, :bp] for l in range(depth)]
    mods_s = [mod[l, bp:] for l in range(depth)]

    zeros_p = [(jnp.zeros((bp, GLA_HEADS, GLA_DK, GLA_DV), F32), jnp.zeros((bp, GDN_HEADS, GDN_DK, GDN_DV), F32),
                jnp.zeros((bp, CONV_W - 1, GDN_CONV_CH), F32), jnp.zeros((bp, S5_GROUPS, S5_STATE), F32),
                jnp.zeros((bp, S5_GROUPS, S5_STATE), F32)) for _ in range(depth)]
    lt_p = min(256, tp)
    y_p, new_p = _trunk(x_prompt, mods_p, params, zeros_p, final_w, nb=1, lt=lt_p, lt_rec=lt_p,
                        lt_s5=min(CHUNK, tp))

    states_s = [(state_gla[l], state_gdn[l], cache_gdn_conv[l], state_s5_re[l], state_s5_im[l])
                for l in range(depth)]
    nb_s = max(1, min(bs, 256 // ts))
    y_s, new_s = _trunk(x_sample, mods_s, params, states_s, final_w, nb=nb_s, lt=ts, lt_rec=ts, lt_s5=ts)
    return (y_p, y_s, *new_p, *new_s)
```

```python
import functools
import math

import jax
import jax.numpy as jnp
from jax import lax
from jax.experimental import pallas as pl
from jax.experimental.pallas import tpu as pltpu

F32 = jnp.float32
BF16 = jnp.bfloat16
HIGHEST = lax.Precision.HIGHEST

D_MODEL = 1024
EPS = 1e-6
CHUNK = 64
BRANCH_W = 512
GLA_HEADS, GLA_DK, GLA_DV, GLA_LOWRANK, GLA_TAU = 4, 64, 128, 16, 16.0
GDN_HEADS, GDN_DK, GDN_DV, CONV_W = 4, 128, 128, 4
S5_GROUP, S5_GROUPS, S5_STATE = 16, 32, 64
GLA_QK, GLA_V, GDN_QK, GDN_V = 256, 512, 512, 512
GDN_CONV_CH = 1536
MOE_GROUPS, EXPERTS_PER_GROUP, N_EXPERTS, D_FF = 4, 4, 16, 256

LANES = 128
SUBLANES = 8
VMEM_LIMIT = 56 * 1024 * 1024
S5_HALF = S5_GROUPS * S5_STATE
SMALL_BETA0, SMALL_A0 = 16, 20
ROUTE_E0 = 4


def _mm(a, b):
    return jnp.dot(a.astype(BF16), b.astype(BF16), preferred_element_type=F32)


def _mm_nt(a, b):
    return lax.dot_general(a.astype(BF16), b.astype(BF16), (((1,), (1,)), ((), ())), preferred_element_type=F32)


def _mm_tn(a, b):
    return lax.dot_general(a.astype(BF16), b.astype(BF16), (((0,), (0,)), ((), ())), preferred_element_type=F32)


def _mm_f32(a, b):
    return jnp.dot(a, b, preferred_element_type=F32, precision=HIGHEST)


def _mm_nt_f32(a, b):
    return lax.dot_general(a, b, (((1,), (1,)), ((), ())), preferred_element_type=F32, precision=HIGHEST)


def _split3(x):
    a = x.astype(BF16)
    r = x - a.astype(F32)
    b = r.astype(BF16)
    c = (r - b.astype(F32)).astype(BF16)
    return a, b, c


def _mm_sel(m01, x):
    a, b, c = _split3(x)
    return _mm(m01, a) + _mm(m01, b) + _mm(m01, c)


def _mm_sel_r(x, m01):
    a, b, c = _split3(x)
    return _mm(a, m01) + _mm(b, m01) + _mm(c, m01)


def _mm_nt_sel(m01, x):
    a, b, c = _split3(x)
    return _mm_nt(m01, a) + _mm_nt(m01, b) + _mm_nt(m01, c)


def _sigmoid(x):
    return 1.0 / (1.0 + jnp.exp(-x))


def _silu(x):
    return x * _sigmoid(x)


def _softplus(x):
    return jnp.maximum(x, 0.0) + jnp.log1p(jnp.exp(-jnp.abs(x)))


def _rms_rows(x, w):
    ms = jnp.mean(x * x, axis=-1, keepdims=True)
    return x * lax.rsqrt(ms + EPS) * w


def _gelu_tanh(x):
    c = math.sqrt(2.0 / math.pi)
    return x * (0.5 * (1.0 + jnp.tanh(c * (x + 0.044715 * (x * x * x)))))


def _const_spec(shape):
    nd = len(shape)
    return pl.BlockSpec(shape, lambda *_: (0,) * nd)


def _params(sem):
    return pltpu.CompilerParams(dimension_semantics=sem, vmem_limit_bytes=VMEM_LIMIT)


def _ada_kernel(c_ref, w_ref, b_ref, o_ref):
    c = c_ref[...]
    o_ref[0] = _mm(_silu(c), w_ref[0]) + b_ref[0]


def _ada_call(c_all, w_ada, b_ada):
    depth = w_ada.shape[0]
    nrow = c_all.shape[0]
    return pl.pallas_call(
        _ada_kernel,
        grid=(depth, 6),
        in_specs=[pl.BlockSpec((nrow, D_MODEL), lambda l, j: (0, 0)),
                  pl.BlockSpec((1, D_MODEL, D_MODEL), lambda l, j: (l, 0, j)),
                  pl.BlockSpec((1, 1, D_MODEL), lambda l, j: (l, 0, j))],
        out_specs=pl.BlockSpec((1, nrow, D_MODEL), lambda l, j: (l, 0, j)),
        out_shape=jax.ShapeDtypeStruct((depth, nrow, 6 * D_MODEL), F32),
        compiler_params=_params(("parallel", "parallel")),
        name="adaln",
    )(c_all, w_ada, b_ada.reshape(depth, 1, 6 * D_MODEL))


def _inproj_kernel(x_ref, mod_ref, nw_ref, wg_ref, wd_ref, ws_ref, wsm_ref, wup_ref, bal_ref, sp_ref,
                   zg_ref, zd_ref, zs_ref):
    nb, lt, d = x_ref.shape
    x = x_ref[...]
    h = _rms_rows(x, nw_ref[...]) * (1.0 + mod_ref[:, 1:2, :]) + mod_ref[:, 0:1, :]
    h2 = h.reshape(nb * lt, d).astype(BF16)

    zg = _mm(h2, wg_ref[...])
    zsm = _mm(h2, wsm_ref[...])
    la_pre = _mm(zsm, wup_ref[...]) + bal_ref[...]
    log_a = (jnp.minimum(la_pre, 0.0) - jnp.log1p(jnp.exp(-jnp.abs(la_pre)))) * (1.0 / GLA_TAU)
    zg_ref[:, :, 0:512] = zg[:, 0:512].reshape(nb, lt, 512)
    zg_ref[:, :, 512:768] = log_a.reshape(nb, lt, 256)
    zg_ref[:, :, 768:1792] = zg[:, 512:1536].reshape(nb, lt, 1024)

    zd_ref[:, :, 0:2048] = _mm(h2, wd_ref[...]).reshape(nb, lt, 2048)
    lane = lax.broadcasted_iota(jnp.int32, zsm.shape, 1)
    beta = _sigmoid(zsm)
    log_g = -sp_ref[0:1, :] * _softplus(zsm + sp_ref[1:2, :])
    is_beta = (lane >= SMALL_BETA0) & (lane < SMALL_A0)
    is_a = (lane >= SMALL_A0) & (lane < SMALL_A0 + GDN_HEADS)
    small = jnp.where(is_beta, beta, jnp.where(is_a, log_g, 0.0))
    zd_ref[:, :, 2048:2176] = small.reshape(nb, lt, LANES)

    zs = _mm(h2, ws_ref[...])
    for i in range(nb):
        zs_ref[:, BRANCH_W * i:BRANCH_W * (i + 1)] = zs[lt * i:lt * (i + 1), :]


def _inproj_call(x, mod, p, nb, lt):
    B, T, D = x.shape
    grid = (B // nb, T // lt)
    tok = lambda w: pl.BlockSpec((nb, lt, w), lambda b, t: (b, t, 0))
    return pl.pallas_call(
        _inproj_kernel,
        grid=grid,
        in_specs=[tok(D),
                  pl.BlockSpec((nb, 6, D), lambda b, t: (b, 0, 0)),
                  _const_spec((1, D)),
                  _const_spec(p["w_gla"].shape), _const_spec(p["w_gdn"].shape), _const_spec(p["w_s5"].shape),
                  _const_spec(p["w_small"].shape), _const_spec(p["w_up"].shape), _const_spec((1, GLA_QK)),
                  _const_spec((2, LANES))],
        out_specs=[tok(1792), tok(2176), pl.BlockSpec((lt, nb * BRANCH_W), lambda b, t: (t, b))],
        out_shape=[jax.ShapeDtypeStruct((B, T, 1792), F32),
                   jax.ShapeDtypeStruct((B, T, 2176), F32),
                   jax.ShapeDtypeStruct((T, B * BRANCH_W), F32)],
        compiler_params=_params(("parallel", "parallel")),
        name="inproj",
    )(x, mod, p["norm1_w"], p["w_gla"], p["w_gdn"], p["w_s5"], p["w_small"], p["w_up"], p["b_alpha"],
      p["gdn_small"])


def _gla_kernel(z_ref, s0_ref, nw_ref, o_ref, s_ref, *, n_chunks):
    L = CHUNK

    @pl.when(pl.program_id(1) == 0)
    def _():
        s_ref[...] = s0_ref[...]

    row = lax.broadcasted_iota(jnp.int32, (L, L), 0)
    col = lax.broadcasted_iota(jnp.int32, (L, L), 1)
    causal = row >= col
    lane = lax.broadcasted_iota(jnp.int32, (L, LANES), 1)
    own_half = (lane < GLA_DK, lane >= GLA_DK)
    nw = nw_ref[...]
    lt = n_chunks * L
    brow = lax.broadcasted_iota(jnp.int32, (lt, lt), 0)
    bcol = lax.broadcasted_iota(jnp.int32, (lt, lt), 1)
    shift = L.bit_length() - 1
    chunk_tril = ((brow >= bcol) & ((brow >> shift) == (bcol >> shift))).astype(BF16)
    b_all = _mm_sel(chunk_tril, z_ref[0, :, 512:768])

    probs = [(c, h) for c in range(n_chunks) for h in range(GLA_HEADS)]
    rows_of = lambda c: slice(L * c, L * (c + 1))
    pair_of = lambda h: slice(LANES * (h // 2), LANES * (h // 2) + LANES)
    qi_, ki_, qs_, kl_, dl_ = {}, {}, {}, {}, {}
    for c in range(n_chunks):
        q = z_ref[0, rows_of(c), 0:256]
        k = z_ref[0, rows_of(c), 256:512]
        b = b_all[rows_of(c), :]
        b_mid = b[L // 2 - 1:L // 2, :]
        b_last = b[L - 1:L, :]
        qi_[c] = q * jnp.exp(b - b_mid)
        ki_[c] = k * jnp.exp(b_mid - b)
        qs_[c] = q * jnp.exp(b)
        kl_[c] = k * jnp.exp(b_last - b)
        dl_[c] = jnp.exp(b_last)
    v_ = {p: z_ref[0, rows_of(p[0]), 768 + GLA_DV * p[1]:768 + GLA_DV * (p[1] + 1)] for p in probs}
    att_ = {p: jnp.where(causal, _mm_nt(qi_[p[0]][:, pair_of(p[1])],
                                        jnp.where(own_half[p[1] % 2], ki_[p[0]][:, pair_of(p[1])], 0.0)), 0.0)
            for p in probs}
    kv_ = {p: _mm_tn(v_[p], jnp.where(own_half[p[1] % 2], kl_[p[0]][:, pair_of(p[1])], 0.0)) for p in probs}
    av_ = {p: _mm(att_[p], v_[p]) for p in probs}
    st_ = {(0, h): s_ref[0, h] for h in range(GLA_HEADS)}
    for c in range(n_chunks):
        for h in range(GLA_HEADS):
            st_[(c + 1, h)] = st_[(c, h)] * dl_[c][:, pair_of(h)] + kv_[(c, h)]
    for p in probs:
        c, h = p
        o_h = av_[p] + _mm_nt(qs_[c][:, pair_of(h)], st_[p])
        gate = z_ref[0, rows_of(c), 1280 + GLA_DV * h:1280 + GLA_DV * (h + 1)]
        o_ref[0, rows_of(c), GLA_DV * h:GLA_DV * (h + 1)] = _rms_rows(o_h, nw) * _silu(gate)
    for h in range(GLA_HEADS):
        s_ref[0, h] = st_[(n_chunks, h)]


def _gla_call(zg, s0t, norm_w, lt):
    B, T, _ = zg.shape
    st_spec = pl.BlockSpec((1, GLA_HEADS, GLA_DV, LANES), lambda b, t: (b, 0, 0, 0))
    return pl.pallas_call(
        functools.partial(_gla_kernel, n_chunks=lt // CHUNK),
        grid=(B, T // lt),
        in_specs=[pl.BlockSpec((1, lt, 1792), lambda b, t: (b, t, 0)), st_spec, _const_spec((1, GLA_DV))],
        out_specs=[pl.BlockSpec((1, lt, GLA_V), lambda b, t: (b, t, 0)), st_spec],
        out_shape=[jax.ShapeDtypeStruct((B, T, GLA_V), F32),
                   jax.ShapeDtypeStruct((B, GLA_HEADS, GLA_DV, LANES), F32)],
        compiler_params=_params(("parallel", "arbitrary")),
        name="gla",
    )(zg, s0t, norm_w)


def _gdn_kernel(z_ref, cb_ref, s0_ref, cw_ref, nw_ref, o_ref, s_ref, cbo_ref, xbuf, qkv, bfull_ref, gfull_ref,
                grow_ref, glast_ref, *, lt):
    L = CHUNK
    n_chunks = lt // L
    t = pl.program_id(1)

    @pl.when(t == 0)
    def _():
        s_ref[...] = s0_ref[...]
        xbuf[0:SUBLANES, :] = cb_ref[0]

    @pl.when(t > 0)
    def _():
        xbuf[0:SUBLANES, :] = xbuf[lt:lt + SUBLANES, :]

    xbuf[SUBLANES:SUBLANES + lt, :] = z_ref[0, :, 0:GDN_CONV_CH]
    cbo_ref[0] = xbuf[lt:lt + SUBLANES, :]

    base = SUBLANES - (CONV_W - 1)
    conv = xbuf[base:base + lt, :] * cw_ref[0:1, :]
    for j in range(1, CONV_W):
        conv = conv + xbuf[base + j:base + j + lt, :] * cw_ref[j:j + 1, :]
    act = _silu(conv)
    for h in range(GDN_HEADS):
        for part, scale in ((0, GDN_DK ** -0.5), (1, 1.0)):
            sl = slice(part * GDN_QK + GDN_DK * h, part * GDN_QK + GDN_DK * (h + 1))
            xh = act[:, sl]
            ss = jnp.sum(xh * xh, axis=-1, keepdims=True)
            qkv[:, sl] = xh * (lax.rsqrt(ss + EPS) * scale)
    qkv[:, 2 * GDN_QK:] = act[:, 2 * GDN_QK:]

    row = lax.broadcasted_iota(jnp.int32, (L, L), 0)
    col = lax.broadcasted_iota(jnp.int32, (L, L), 1)
    causal = row >= col
    strict = row > col
    eye = (row == col).astype(F32)
    diag16 = (row >> 4) == (col >> 4)
    diag32 = (row >> 5) == (col >> 5)
    off32 = diag32 & jnp.logical_not(diag16)
    off64 = jnp.logical_not(diag32)
    sel_r = lax.broadcasted_iota(jnp.int32, (SUBLANES, LANES), 0)
    sel_c = lax.broadcasted_iota(jnp.int32, (SUBLANES, LANES), 1)
    sel = (sel_c == sel_r + SMALL_A0).astype(BF16)
    brow = lax.broadcasted_iota(jnp.int32, (lt, lt), 0)
    bcol = lax.broadcasted_iota(jnp.int32, (lt, lt), 1)
    shift = L.bit_length() - 1
    chunk_tril = ((brow >= bcol) & ((brow >> shift) == (bcol >> shift))).astype(BF16)
    small_all = z_ref[0, :, 2048:2176]
    gs_all = _mm_sel(chunk_tril, small_all)
    grow_all = _mm_nt_sel(sel, gs_all)
    e_r = lax.broadcasted_iota(jnp.int32, (LANES, GDN_HEADS * LANES), 0)
    e_h = lax.broadcasted_iota(jnp.int32, (LANES, GDN_HEADS * LANES), 1) >> (LANES.bit_length() - 1)
    bfull_ref[...] = _mm_sel_r(small_all, (e_r == e_h + SMALL_BETA0).astype(BF16))
    gfull = _mm_sel_r(gs_all, (e_r == e_h + SMALL_A0).astype(BF16))
    gfull_ref[...] = gfull
    for c in range(n_chunks):
        grow_ref[c] = grow_all[:, L * c:L * (c + 1)]
        glast_ref[c:c + 1, :] = gfull[L * (c + 1) - 1:L * (c + 1), :]
    nw = nw_ref[...]

    probs = [(c, h) for c in range(n_chunks) for h in range(GDN_HEADS)]
    rows_of = lambda c: slice(L * c, L * (c + 1))
    head_of = lambda h: slice(LANES * h, LANES * (h + 1))
    q_ = {p: qkv[rows_of(p[0]), GDN_DK * p[1]:GDN_DK * (p[1] + 1)] for p in probs}
    k_ = {p: qkv[rows_of(p[0]), GDN_QK + GDN_DK * p[1]:GDN_QK + GDN_DK * (p[1] + 1)] for p in probs}
    beta_ = {p: bfull_ref[rows_of(p[0]), head_of(p[1])] for p in probs}
    g_ = {p: gfull_ref[rows_of(p[0]), head_of(p[1])] for p in probs}
    dec_ = {p: jnp.exp(jnp.where(causal, g_[p][:, 0:L] - grow_ref[p[0]][p[1]:p[1] + 1, :], -jnp.inf))
            for p in probs}
    kk_ = {p: _mm_nt(k_[p], k_[p]) for p in probs}
    qk_ = {p: _mm_nt(q_[p], k_[p]) for p in probs}
    a_ = {p: jnp.where(strict, beta_[p][:, 0:L] * kk_[p] * dec_[p], 0.0) for p in probs}
    qk_ = {p: jnp.where(causal, qk_[p] * dec_[p], 0.0) for p in probs}
    pw_ = {p: jnp.where(diag16, -a_[p], 0.0) for p in probs}
    tinv_ = {p: eye + pw_[p] for p in probs}
    for _ in range(3):
        pw_ = {p: _mm(pw_[p], pw_[p]) for p in probs}
        tinv_ = {p: tinv_[p] + _mm(tinv_[p], pw_[p]) for p in probs}
    for off in (off32, off64):
        te_ = {p: _mm(tinv_[p], jnp.where(off, a_[p], 0.0)) for p in probs}
        tinv_ = {p: tinv_[p] - _mm(te_[p], tinv_[p]) for p in probs}
    eg_ = {p: jnp.exp(g_[p]) for p in probs}
    sol_ = {}
    for p in probs:
        v_p = qkv[rows_of(p[0]), 2 * GDN_QK + GDN_DV * p[1]:2 * GDN_QK + GDN_DV * (p[1] + 1)]
        rhs = jnp.concatenate([beta_[p] * v_p, (beta_[p] * eg_[p]) * k_[p]], axis=-1)
        sol_[p] = _mm(tinv_[p], rhs)

    st_ = [s_ref[0, h] for h in range(GDN_HEADS)]
    for c in range(n_chunks):
        heads = [(c, h) for h in range(GDN_HEADS)]
        g_last = glast_ref[c:c + 1, :]
        u_ = {p: sol_[p][:, 0:GDN_DV] - _mm(sol_[p][:, GDN_DV:], st_[p[1]]) for p in heads}
        o_ = {p: _mm(qk_[p], u_[p]) + _mm(q_[p] * eg_[p], st_[p[1]]) for p in heads}
        st_ = [st_[h] * jnp.exp(g_last[:, head_of(h)])
               + _mm_tn(k_[(c, h)] * jnp.exp(g_last[:, head_of(h)] - g_[(c, h)]), u_[(c, h)])
               for h in range(GDN_HEADS)]
        for h in range(GDN_HEADS):
            gate = z_ref[0, rows_of(c), GDN_CONV_CH + GDN_DV * h:GDN_CONV_CH + GDN_DV * (h + 1)]
            o_ref[0, rows_of(c), GDN_DV * h:GDN_DV * (h + 1)] = _rms_rows(o_[(c, h)], nw) * _silu(gate)
    for h in range(GDN_HEADS):
        s_ref[0, h] = st_[h]


def _gdn_call(zd, cb8, s0, conv_w, norm_w, lt):
    B, T, _ = zd.shape
    st_spec = pl.BlockSpec((1, GDN_HEADS, GDN_DK, GDN_DV), lambda b, t: (b, 0, 0, 0))
    cb_spec = pl.BlockSpec((1, SUBLANES, GDN_CONV_CH), lambda b, t: (b, 0, 0))
    return pl.pallas_call(
        functools.partial(_gdn_kernel, lt=lt),
        grid=(B, T // lt),
        in_specs=[pl.BlockSpec((1, lt, 2176), lambda b, t: (b, t, 0)), cb_spec, st_spec,
                  _const_spec((CONV_W, GDN_CONV_CH)), _const_spec((1, GDN_DV))],
        out_specs=[pl.BlockSpec((1, lt, GDN_V), lambda b, t: (b, t, 0)), st_spec, cb_spec],
        out_shape=[jax.ShapeDtypeStruct((B, T, GDN_V), F32),
                   jax.ShapeDtypeStruct((B, GDN_HEADS, GDN_DK, GDN_DV), F32),
                   jax.ShapeDtypeStruct((B, SUBLANES, GDN_CONV_CH), F32)],
        scratch_shapes=[pltpu.VMEM((lt + SUBLANES, GDN_CONV_CH), F32), pltpu.VMEM((lt, GDN_CONV_CH), F32),
                        pltpu.VMEM((lt, GDN_HEADS * LANES), F32), pltpu.VMEM((lt, GDN_HEADS * LANES), F32),
                        pltpu.VMEM((lt // CHUNK, SUBLANES, CHUNK), F32),
                        pltpu.VMEM((SUBLANES, GDN_HEADS * LANES), F32)],
        compiler_params=_params(("parallel", "arbitrary")),
        name="gdn",
    )(zd, cb8, s0, conv_w, norm_w)


S5_LANE_CHUNK = 512


def _s5_kernel(u_ref, h0_ref, bb_ref, cc_ref, lam_ref, d_ref, y_ref, h_ref, bu_ref, *, lt):
    nb = SUBLANES

    @pl.when(pl.program_id(1) == 0)
    def _():
        h_ref[...] = h0_ref[...]

    u = u_ref[...].reshape(lt * nb, BRANCH_W)
    ch, sh = BRANCH_W // 2, S5_HALF // 2
    for part in range(2):
        for hb in range(2):
            bu_ref[:, part * S5_HALF + hb * sh:part * S5_HALF + (hb + 1) * sh] = _mm(
                u[:, hb * ch:(hb + 1) * ch], bb_ref[2 * part + hb])
    for j in range(S5_HALF // S5_LANE_CHUNK):
        re = slice(S5_LANE_CHUNK * j, S5_LANE_CHUNK * (j + 1))
        im = slice(S5_HALF + S5_LANE_CHUNK * j, S5_HALF + S5_LANE_CHUNK * (j + 1))
        lr = jnp.broadcast_to(lam_ref[0:1, re], (nb, S5_LANE_CHUNK))
        li = jnp.broadcast_to(lam_ref[1:2, re], (nb, S5_LANE_CHUNK))

        def step(tt, carry, re=re, im=im, lr=lr, li=li):
            hr, hi = carry
            rows = pl.ds(pl.multiple_of(tt * nb, nb), nb)
            nr = lr * hr - li * hi + bu_ref[rows, re]
            ni = lr * hi + li * hr + bu_ref[rows, im]
            bu_ref[rows, re] = nr
            bu_ref[rows, im] = ni
            return nr, ni

        hr, hi = lax.fori_loop(0, lt, step, (h_ref[:, re], h_ref[:, im]))
        h_ref[:, re] = hr
        h_ref[:, im] = hi
    ys = [_mm(bu_ref[:, hb * sh:(hb + 1) * sh], cc_ref[hb])
          + _mm(bu_ref[:, S5_HALF + hb * sh:S5_HALF + (hb + 1) * sh], cc_ref[2 + hb]) for hb in range(2)]
    y = jnp.concatenate(ys, axis=-1) + d_ref[...] * u
    y_ref[...] = y.reshape(lt, nb, BRANCH_W)


def _s5_call(zs, h0, p, lt):
    T = zs.shape[0]
    B = h0.shape[0]
    nb = SUBLANES
    h_spec = pl.BlockSpec((nb, 2 * S5_HALF), lambda b, t: (b, 0))
    tok_spec = pl.BlockSpec((lt, None, nb, BRANCH_W), lambda b, t: (t, b, 0, 0))
    y, h = pl.pallas_call(
        functools.partial(_s5_kernel, lt=lt),
        grid=(B // nb, T // lt),
        in_specs=[tok_spec, h_spec,
                  _const_spec((4, BRANCH_W // 2, S5_HALF // 2)), _const_spec((4, S5_HALF // 2, BRANCH_W // 2)),
                  _const_spec((2, S5_HALF)), _const_spec((1, BRANCH_W))],
        out_specs=[tok_spec, h_spec],
        out_shape=[jax.ShapeDtypeStruct((T, B // nb, nb, BRANCH_W), F32),
                   jax.ShapeDtypeStruct((B, 2 * S5_HALF), F32)],
        scratch_shapes=[pltpu.VMEM((lt * nb, 2 * S5_HALF), F32)],
        compiler_params=_params(("parallel", "arbitrary")),
        name="s5",
    )(zs.reshape(T, B // nb, nb, BRANCH_W), h0, p["s5_bblk"], p["s5_cblk"], p["s5_lam"], p["s5_d"])
    return y.reshape(T, B * BRANCH_W), h


def _merge_kernel(x_ref, mod_ref, nw_ref, oa_ref, ob_ref, ys_ref, wgate_ref, bgate_ref, wa_ref, wb_ref, wc_ref,
                  wglu_ref, bglu_ref, wout_ref, o_ref):
    nb, lt, d = x_ref.shape
    n = nb * lt
    x = x_ref[...]
    h = _rms_rows(x, nw_ref[...]) * (1.0 + mod_ref[:, 1:2, :]) + mod_ref[:, 0:1, :]
    gate = _sigmoid(_mm(h.reshape(n, d), wgate_ref[...]) + bgate_ref[...])
    y_a = _mm(oa_ref[...].reshape(n, BRANCH_W), wa_ref[...])
    y_b = _mm(ob_ref[...].reshape(n, BRANCH_W), wb_ref[...])
    ys = [ys_ref[:, BRANCH_W * i:BRANCH_W * (i + 1)] for i in range(nb)]
    yc = _gelu_tanh(ys[0] if nb == 1 else jnp.concatenate(ys, axis=0))
    yc = yc * _sigmoid(_mm(yc, wglu_ref[...]) + bglu_ref[...])
    y_c = _mm(yc, wc_ref[...])
    mix = gate[:, 0:d] * y_a + gate[:, d:2 * d] * y_b + gate[:, 2 * d:3 * d] * y_c
    m = _mm(mix, wout_ref[...])
    o_ref[...] = x + mod_ref[:, 2:3, :] * m.reshape(nb, lt, d)


def _merge_call(x, mod, oa, ob, ys, p, nb, lt):
    B, T, D = x.shape
    tok = lambda w: pl.BlockSpec((nb, lt, w), lambda b, t: (b, t, 0))
    return pl.pallas_call(
        _merge_kernel,
        grid=(B // nb, T // lt),
        in_specs=[tok(D), pl.BlockSpec((nb, 6, D), lambda b, t: (b, 0, 0)), _const_spec((1, D)),
                  tok(BRANCH_W), tok(BRANCH_W), pl.BlockSpec((lt, nb * BRANCH_W), lambda b, t: (t, b)),
                  _const_spec((D, 3 * D)), _const_spec((1, 3 * D)),
                  _const_spec((BRANCH_W, D)), _const_spec((BRANCH_W, D)), _const_spec((BRANCH_W, D)),
                  _const_spec((BRANCH_W, BRANCH_W)), _const_spec((1, BRANCH_W)), _const_spec((D, D))],
        out_specs=tok(D),
        out_shape=jax.ShapeDtypeStruct((B, T, D), F32),
        compiler_params=_params(("parallel", "parallel")),
        name="merge",
    )(x, mod, p["norm1_w"], oa, ob, ys, p["w_gate"], p["b_gate"], p["w_br_a"], p["w_br_b"], p["w_br_c"],
      p["w_glu"], p["b_glu"], p["w_out"])


MOE_ROWS = 192
MOE_ROWS_PAD = 256


def _moe_kernel(x_ref, mod_ref, nw_ref, wr_ref, br_ref, ex_ref, wg_ref, wu_ref, wd_ref, fw_ref, o_ref,
                hn_ref, comb_ref, krow_ref, acc_ref, *, final_norm):
    nb, lt, d = x_ref.shape
    n = nb * lt
    R, RP = MOE_ROWS, MOE_ROWS_PAD
    gw = EXPERTS_PER_GROUP * D_FF

    x = x_ref[...]
    hn = (_rms_rows(x, nw_ref[...]) * (1.0 + mod_ref[:, 4:5, :]) + mod_ref[:, 3:4, :]).reshape(n, d)
    hn_hi = hn.astype(BF16)
    hn_ref[...] = hn_hi
    hn_lo = (hn - hn_hi.astype(F32)).astype(BF16)
    logits = _mm(hn_hi, wr_ref[0]) + (_mm(hn_hi, wr_ref[1]) + _mm(hn_lo, wr_ref[0])) + br_ref[...]
    lane = lax.broadcasted_iota(jnp.int32, (n, LANES), 1)
    neg = -jnp.inf
    gl = jnp.where(lane < MOE_GROUPS, logits, neg)
    gmax = jnp.max(gl, axis=-1, keepdims=True)
    g_sel = jnp.min(jnp.where(gl == gmax, lane, LANES), axis=-1, keepdims=True)
    p_sel = 1.0 / jnp.sum(jnp.exp(gl - gmax), axis=-1, keepdims=True)
    e_id = lane - ROUTE_E0
    in_group = (e_id >= 0) & (e_id < N_EXPERTS) & ((e_id >> 2) == g_sel)
    el = jnp.where(in_group, logits, neg)
    m1 = jnp.max(el, axis=-1, keepdims=True)
    i1 = jnp.min(jnp.where(el == m1, lane, LANES), axis=-1, keepdims=True)
    el2 = jnp.where(lane == i1, neg, el)
    m2 = jnp.max(el2, axis=-1, keepdims=True)
    i2 = jnp.min(jnp.where(el2 == m2, lane, LANES), axis=-1, keepdims=True)
    e2 = jnp.exp(m2 - m1)
    w1 = p_sel / (1.0 + e2)
    w2 = p_sel * e2 / (1.0 + e2)
    comb_ref[...] = jnp.where(lane == i1, w1, jnp.where(lane == i2, w2, 0.0)).astype(BF16)

    member = lane == g_sel
    trow = lax.broadcasted_iota(jnp.int32, (n, n), 0)
    tcol = lax.broadcasted_iota(jnp.int32, (n, n), 1)
    rank = _mm((trow > tcol).astype(BF16), member.astype(BF16))
    key = jnp.where(member, rank, -1.0)
    sel_r = lax.broadcasted_iota(jnp.int32, (SUBLANES, LANES), 0)
    sel_c = lax.broadcasted_iota(jnp.int32, (SUBLANES, LANES), 1)
    krow_ref[...] = _mm_nt_sel((sel_r == sel_c).astype(BF16), key)
    acc_ref[...] = jnp.zeros_like(acc_ref)

    row_pos = lax.broadcasted_iota(jnp.int32, (R, n), 0).astype(F32)
    col_pos_i = lax.broadcasted_iota(jnp.int32, (n, RP), 1)
    col_pos = col_pos_i.astype(F32)
    col_ok = col_pos_i < R
    e_r = lax.broadcasted_iota(jnp.int32, (LANES, RP), 0)
    for g in range(MOE_GROUPS):
        last = jnp.max(key[:, g:g + 1])
        n_pass = (last.astype(jnp.int32) + R) // R
        key_b = _mm_sel_r(key, (e_r == g).astype(BF16))

        def one_pass(s, carry, g=g, key_b=key_b):
            base = (s * R).astype(F32)
            gather = (krow_ref[g:g + 1, :] - base == row_pos).astype(BF16)
            xg = _mm(gather, hn_ref[...])
            cg = _mm(_mm(gather, comb_ref[...]), ex_ref[:, gw * g:gw * (g + 1)])
            he = _silu(_mm(xg, wg_ref[g])) * _mm(xg, wu_ref[g])
            yg = _mm(he * cg, wd_ref[g])
            yg = jnp.concatenate([yg, jnp.zeros((RP - R, d), F32)], axis=0)
            scatter = ((key_b - base == col_pos) & col_ok).astype(BF16)
            acc_ref[...] += _mm(scatter, yg)
            return carry

        lax.fori_loop(0, n_pass, one_pass, 0)

    out = x + mod_ref[:, 5:6, :] * acc_ref[...].reshape(nb, lt, d)
    if final_norm:
        out = _rms_rows(out, fw_ref[...])
    o_ref[...] = out


def _moe_call(x, mod, p, final_w, nb, lt, final_norm):
    B, T, D = x.shape
    n = nb * lt
    gw = EXPERTS_PER_GROUP * D_FF
    tok = pl.BlockSpec((nb, lt, D), lambda b, t: (b, t, 0))
    resident = lambda shape: pl.BlockSpec(shape, lambda *_: (0,) * len(shape), pipeline_mode=pl.Buffered(1))
    return pl.pallas_call(
        functools.partial(_moe_kernel, final_norm=final_norm),
        grid=(B // nb, T // lt),
        in_specs=[tok, pl.BlockSpec((nb, 6, D), lambda b, t: (b, 0, 0)), _const_spec((1, D)),
                  _const_spec((2, D, LANES)), _const_spec((1, LANES)), _const_spec((LANES, N_EXPERTS * D_FF)),
                  resident((MOE_GROUPS, D, gw)), resident((MOE_GROUPS, D, gw)), resident((MOE_GROUPS, gw, D)),
                  _const_spec((1, D))],
        out_specs=tok,
        out_shape=jax.ShapeDtypeStruct((B, T, D), F32),
        scratch_shapes=[pltpu.VMEM((n, D), BF16), pltpu.VMEM((n, LANES), BF16), pltpu.VMEM((SUBLANES, n), F32),
                        pltpu.VMEM((n, D), F32)],
        compiler_params=_params(("parallel", "parallel")),
        name="moe",
    )(x, mod, p["norm2_w"], p["w_route"], p["b_route"], p["route_expand"], p["w_eg"], p["w_eu"], p["w_ed"],
      final_w)


def _prep_layer(l, norm1_w, norm2_w, w_in, w_alpha_up, b_alpha, gla_norm_w, conv_w, gdn_a_log, gdn_dt_bias,
                gdn_norm_w, s5_a_re, s5_a_im, s5_log_dt, s5_b_re, s5_b_im, s5_c_re, s5_c_im, s5_d, w_glu, b_glu,
                w_br_a, w_br_b, w_br_c, w_gate, b_gate, w_out, w_route_group, b_route_group, w_route_expert,
                b_route_expert, w_exp_gate, w_exp_up, w_exp_down):
    D = D_MODEL
    w = w_in[l]
    o = 0
    cols = {}
    for name, size in (("qa", GLA_QK), ("ka", GLA_QK), ("va", GLA_V), ("ga", GLA_V), ("aa", GLA_LOWRANK),
                       ("qkv", GDN_CONV_CH), ("beta", GDN_HEADS), ("a", GDN_HEADS), ("gb", GDN_V),
                       ("u", BRANCH_W)):
        cols[name] = w[:, o:o + size]
        o += size
    p = {}
    p["norm1_w"] = norm1_w[l].reshape(1, D)
    p["norm2_w"] = norm2_w[l].reshape(1, D)
    p["w_gla"] = jnp.concatenate([cols["qa"] * (GLA_DK ** -0.5), cols["ka"], cols["va"], cols["ga"]],
                                 axis=1).astype(BF16)
    p["w_gdn"] = jnp.concatenate([cols["qkv"], cols["gb"]], axis=1).astype(BF16)
    p["w_s5"] = cols["u"].astype(BF16)
    small = jnp.zeros((D, LANES), F32)
    small = small.at[:, 0:GLA_LOWRANK].set(cols["aa"])
    small = small.at[:, SMALL_BETA0:SMALL_BETA0 + GDN_HEADS].set(cols["beta"])
    small = small.at[:, SMALL_A0:SMALL_A0 + GDN_HEADS].set(cols["a"])
    p["w_small"] = small.astype(BF16)
    p["w_up"] = jnp.zeros((LANES, GLA_QK), F32).at[0:GLA_LOWRANK].set(w_alpha_up[l]).astype(BF16)
    p["b_alpha"] = b_alpha[l].reshape(1, GLA_QK)
    gs = jnp.zeros((2, LANES), F32)
    gs = gs.at[0, SMALL_A0:SMALL_A0 + GDN_HEADS].set(jnp.exp(gdn_a_log[l]))
    gs = gs.at[1, SMALL_A0:SMALL_A0 + GDN_HEADS].set(gdn_dt_bias[l])
    p["gdn_small"] = gs
    p["gla_norm_w"] = gla_norm_w[l].reshape(1, GLA_DV)
    p["gdn_norm_w"] = gdn_norm_w[l].reshape(1, GDN_DV)
    p["conv_w"] = conv_w[l]

    a_re, a_im = s5_a_re[l], s5_a_im[l]
    dt = jnp.exp(s5_log_dt[l])[:, None]
    mag = jnp.exp(a_re * dt)
    lr, li = mag * jnp.cos(a_im * dt), mag * jnp.sin(a_im * dt)
    den = a_re * a_re + a_im * a_im
    f_re = ((lr - 1.0) * a_re + li * a_im) / den
    f_im = (li * a_re - (lr - 1.0) * a_im) / den
    bb_re = f_re[..., None] * s5_b_re[l] - f_im[..., None] * s5_b_im[l]
    bb_im = f_re[..., None] * s5_b_im[l] + f_im[..., None] * s5_b_re[l]
    gh = S5_GROUPS // 2
    eye_h = jnp.eye(gh, dtype=F32)
    blk = lambda m: jnp.einsum("gpc,gh->gchp", m, eye_h).reshape(gh * S5_GROUP, gh * S5_STATE)
    cblk = lambda m: jnp.einsum("gcp,gh->hpgc", m, eye_h).reshape(gh * S5_STATE, gh * S5_GROUP)
    p["s5_bblk"] = jnp.stack([blk(m[hb * gh:(hb + 1) * gh]) for m in (bb_re, bb_im) for hb in range(2)]
                             ).astype(BF16)
    p["s5_cblk"] = jnp.stack([cblk(m[hb * gh:(hb + 1) * gh]) for m in (s5_c_re[l], -s5_c_im[l])
                              for hb in range(2)]).astype(BF16)
    p["s5_lam"] = jnp.stack([lr.reshape(S5_HALF), li.reshape(S5_HALF)])
    p["s5_d"] = s5_d[l].reshape(1, BRANCH_W)

    p["w_gate"] = w_gate[l].astype(BF16)
    p["b_gate"] = b_gate[l].reshape(1, 3 * D)
    p["w_br_a"] = w_br_a[l].astype(BF16)
    p["w_br_b"] = w_br_b[l].astype(BF16)
    p["w_br_c"] = w_br_c[l].astype(BF16)
    p["w_glu"] = w_glu[l].astype(BF16)
    p["b_glu"] = b_glu[l].reshape(1, BRANCH_W)
    p["w_out"] = w_out[l].astype(BF16)

    wr = jnp.zeros((D, LANES), F32)
    wr = wr.at[:, 0:MOE_GROUPS].set(w_route_group[l]).at[:, ROUTE_E0:ROUTE_E0 + N_EXPERTS].set(w_route_expert[l])
    br = jnp.zeros((1, LANES), F32)
    br = br.at[0, 0:MOE_GROUPS].set(b_route_group[l]).at[0, ROUTE_E0:ROUTE_E0 + N_EXPERTS].set(b_route_expert[l])
    wr_hi = wr.astype(BF16)
    p["w_route"] = jnp.stack([wr_hi, (wr - wr_hi.astype(F32)).astype(BF16)])
    p["b_route"] = br
    e_of_col = jnp.arange(N_EXPERTS * D_FF) // D_FF
    p["route_expand"] = (jnp.arange(LANES)[:, None] == (e_of_col[None, :] + ROUTE_E0)).astype(BF16)
    gw = EXPERTS_PER_GROUP * D_FF
    p["w_eg"] = w_exp_gate[l].reshape(MOE_GROUPS, EXPERTS_PER_GROUP, D, D_FF).transpose(0, 2, 1, 3) \
        .reshape(MOE_GROUPS, D, gw).astype(BF16)
    p["w_eu"] = w_exp_up[l].reshape(MOE_GROUPS, EXPERTS_PER_GROUP, D, D_FF).transpose(0, 2, 1, 3) \
        .reshape(MOE_GROUPS, D, gw).astype(BF16)
    p["w_ed"] = w_exp_down[l].reshape(MOE_GROUPS, gw, D).astype(BF16)
    return p


def _gla_state_in(s):
    st = jnp.swapaxes(s, -1, -2)
    z = jnp.zeros_like(st)
    halves = [jnp.concatenate([st[:, h], z[:, h]] if h % 2 == 0 else [z[:, h], st[:, h]], axis=-1)
              for h in range(GLA_HEADS)]
    return jnp.stack(halves, axis=1)


def _gla_state_out(st):
    parts = [st[:, h, :, GLA_DK * (h % 2):GLA_DK * (h % 2 + 1)] for h in range(GLA_HEADS)]
    return jnp.swapaxes(jnp.stack(parts, axis=1), -1, -2)


TOKEN_TILE = 256
MOE_TILE = 512


def _tiles(B, T):
    lt = min(TOKEN_TILE, T)
    nb = max(1, min(B, TOKEN_TILE // lt))
    lt_moe = min(MOE_TILE, T)
    nb_moe = max(1, min(B, MOE_TILE // lt_moe))
    return dict(nb=nb, lt=lt, lt_rec=lt, lt_s5=min(CHUNK, T), nb_moe=nb_moe, lt_moe=lt_moe)


def _trunk(x, mods, params, states, final_w):
    tiles = _tiles(x.shape[0], x.shape[1])
    nb, lt, lt_rec, lt_s5 = tiles["nb"], tiles["lt"], tiles["lt_rec"], tiles["lt_s5"]
    new = []
    for l, p in enumerate(params):
        s_gla, s_gdn, cbuf, s5_re, s5_im = states[l]
        B = x.shape[0]
        zg, zd, zs = _inproj_call(x, mods[l], p, nb, lt)
        oa, gla_t = _gla_call(zg, _gla_state_in(s_gla), p["gla_norm_w"], lt_rec)
        cb8 = jnp.concatenate([jnp.zeros((B, SUBLANES - (CONV_W - 1), GDN_CONV_CH), F32), cbuf], axis=1)
        ob, gdn_new, cb_new = _gdn_call(zd, cb8, s_gdn, p["conv_w"], p["gdn_norm_w"], lt_rec)
        h0 = jnp.concatenate([s5_re.reshape(B, S5_HALF), s5_im.reshape(B, S5_HALF)], axis=1)
        ys, h_new = _s5_call(zs, h0, p, lt_s5)
        x = _merge_call(x, mods[l], oa, ob, ys, p, nb, lt)
        x = _moe_call(x, mods[l], p, final_w, tiles["nb_moe"], tiles["lt_moe"],
                      final_norm=(l == len(params) - 1))
        new.append((_gla_state_out(gla_t), gdn_new, cb_new[:, SUBLANES - (CONV_W - 1):],
                    h_new[:, :S5_HALF].reshape(B, S5_GROUPS, S5_STATE),
                    h_new[:, S5_HALF:].reshape(B, S5_GROUPS, S5_STATE)))
    return x, [jnp.stack([new[l][i] for l in range(len(params))]) for i in range(5)]


def kernel(x_prompt, x_sample, c_prompt, c_sample, state_gla, state_gdn, cache_gdn_conv, state_s5_re, state_s5_im, norm1_w, norm2_w, final_norm_w, w_ada, b_ada, w_in, w_alpha_up, b_alpha, gla_norm_w, conv_w, gdn_a_log, gdn_dt_bias, gdn_norm_w, s5_a_re, s5_a_im, s5_log_dt, s5_b_re, s5_b_im, s5_c_re, s5_c_im, s5_d, w_glu, b_glu, w_br_a, w_br_b, w_br_c, w_gate, b_gate, w_out, w_route_group, b_route_group, w_route_expert, b_route_expert, w_exp_gate, w_exp_up, w_exp_down):
    depth = w_in.shape[0]
    bp, tp, _ = x_prompt.shape
    bs, ts, _ = x_sample.shape
    params = [_prep_layer(l, norm1_w, norm2_w, w_in, w_alpha_up, b_alpha, gla_norm_w, conv_w, gdn_a_log,
                          gdn_dt_bias, gdn_norm_w, s5_a_re, s5_a_im, s5_log_dt, s5_b_re, s5_b_im, s5_c_re,
                          s5_c_im, s5_d, w_glu, b_glu, w_br_a, w_br_b, w_br_c, w_gate, b_gate, w_out,
                          w_route_group, b_route_group, w_route_expert, b_route_expert, w_exp_gate, w_exp_up,
                          w_exp_down) for l in range(depth)]
    final_w = final_norm_w.reshape(1, D_MODEL)

    mod = _ada_call(jnp.concatenate([c_prompt, c_sample], axis=0), w_ada, b_ada)
    mod = mod.reshape(depth, bp + bs, 6, D_MODEL)
    mods_p = [mod[l, :bp] for l in range(depth)]
    mods_s = [mod[l, bp:] for l in range(depth)]

    zeros_p = [(jnp.zeros((bp, GLA_HEADS, GLA_DK, GLA_DV), F32), jnp.zeros((bp, GDN_HEADS, GDN_DK, GDN_DV), F32),
                jnp.zeros((bp, CONV_W - 1, GDN_CONV_CH), F32), jnp.zeros((bp, S5_GROUPS, S5_STATE), F32),
                jnp.zeros((bp, S5_GROUPS, S5_STATE), F32)) for _ in range(depth)]
    y_p, new_p = _trunk(x_prompt, mods_p, params, zeros_p, final_w)

    states_s = [(state_gla[l], state_gdn[l], cache_gdn_conv[l], state_s5_re[l], state_s5_im[l])
                for l in range(depth)]
    y_s, new_s = _trunk(x_sample, mods_s, params, states_s, final_w)
    return (y_p, y_s, *new_p, *new_s)
```

```python
import functools
import math

import jax
import jax.numpy as jnp
from jax import lax
from jax.experimental import pallas as pl
from jax.experimental.pallas import tpu as pltpu

F32 = jnp.float32
BF16 = jnp.bfloat16
HIGHEST = lax.Precision.HIGHEST

D_MODEL = 1024
EPS = 1e-6
CHUNK = 64
BRANCH_W = 512
GLA_HEADS, GLA_DK, GLA_DV, GLA_LOWRANK, GLA_TAU = 4, 64, 128, 16, 16.0
GDN_HEADS, GDN_DK, GDN_DV, CONV_W = 4, 128, 128, 4
S5_GROUP, S5_GROUPS, S5_STATE = 16, 32, 64
GLA_QK, GLA_V, GDN_QK, GDN_V = 256, 512, 512, 512
GDN_CONV_CH = 1536
MOE_GROUPS, EXPERTS_PER_GROUP, N_EXPERTS, D_FF = 4, 4, 16, 256

LANES = 128
SUBLANES = 8
VMEM_LIMIT = 56 * 1024 * 1024
S5_HALF = S5_GROUPS * S5_STATE
SMALL_BETA0, SMALL_A0 = 16, 20
ROUTE_E0 = 4


def _mm(a, b):
    return jnp.dot(a.astype(BF16), b.astype(BF16), preferred_element_type=F32)


def _mm_nt(a, b):
    return lax.dot_general(a.astype(BF16), b.astype(BF16), (((1,), (1,)), ((), ())), preferred_element_type=F32)


def _mm_tn(a, b):
    return lax.dot_general(a.astype(BF16), b.astype(BF16), (((0,), (0,)), ((), ())), preferred_element_type=F32)


def _mm_f32(a, b):
    return jnp.dot(a, b, preferred_element_type=F32, precision=HIGHEST)


def _mm_nt_f32(a, b):
    return lax.dot_general(a, b, (((1,), (1,)), ((), ())), preferred_element_type=F32, precision=HIGHEST)


def _split3(x):
    a = x.astype(BF16)
    r = x - a.astype(F32)
    b = r.astype(BF16)
    c = (r - b.astype(F32)).astype(BF16)
    return a, b, c


def _mm_sel(m01, x):
    a, b, c = _split3(x)
    return _mm(m01, a) + _mm(m01, b) + _mm(m01, c)


def _mm_sel_r(x, m01):
    a, b, c = _split3(x)
    return _mm(a, m01) + _mm(b, m01) + _mm(c, m01)


def _mm_nt_sel(m01, x):
    a, b, c = _split3(x)
    return _mm_nt(m01, a) + _mm_nt(m01, b) + _mm_nt(m01, c)


def _sigmoid(x):
    return 1.0 / (1.0 + jnp.exp(-x))


def _silu(x):
    return x * _sigmoid(x)


def _softplus(x):
    return jnp.maximum(x, 0.0) + jnp.log1p(jnp.exp(-jnp.abs(x)))


def _rms_rows(x, w):
    ms = jnp.mean(x * x, axis=-1, keepdims=True)
    return x * lax.rsqrt(ms + EPS) * w


def _gelu_tanh(x):
    c = math.sqrt(2.0 / math.pi)
    return x * (0.5 * (1.0 + jnp.tanh(c * (x + 0.044715 * (x * x * x)))))


def _const_spec(shape):
    nd = len(shape)
    return pl.BlockSpec(shape, lambda *_: (0,) * nd)


def _time_major_shape(B, T, nb):
    return (T, B * BRANCH_W)


def _time_major_spec(nb, lt):
    return pl.BlockSpec((lt, nb * BRANCH_W), lambda b, t: (t, b))


def _params(sem):
    return pltpu.CompilerParams(dimension_semantics=sem, vmem_limit_bytes=VMEM_LIMIT)


def _ada_kernel(c_ref, w_ref, b_ref, o_ref):
    c = c_ref[...]
    o_ref[0] = _mm(_silu(c), w_ref[0]) + b_ref[0]


def _ada_call(c_all, w_ada, b_ada):
    depth = w_ada.shape[0]
    nrow = c_all.shape[0]
    return pl.pallas_call(
        _ada_kernel,
        grid=(depth, 6),
        in_specs=[pl.BlockSpec((nrow, D_MODEL), lambda l, j: (0, 0)),
                  pl.BlockSpec((1, D_MODEL, D_MODEL), lambda l, j: (l, 0, j)),
                  pl.BlockSpec((1, 1, D_MODEL), lambda l, j: (l, 0, j))],
        out_specs=pl.BlockSpec((1, nrow, D_MODEL), lambda l, j: (l, 0, j)),
        out_shape=jax.ShapeDtypeStruct((depth, nrow, 6 * D_MODEL), F32),
        compiler_params=_params(("parallel", "parallel")),
        name="adaln",
    )(c_all, w_ada, b_ada.reshape(depth, 1, 6 * D_MODEL))


def _inproj_kernel(x_ref, mod_ref, nw_ref, wg_ref, wd_ref, ws_ref, wsm_ref, wup_ref, bal_ref, sp_ref,
                   zg_ref, zd_ref, zs_ref):
    nb, lt, d = x_ref.shape
    x = x_ref[...]
    h = _rms_rows(x, nw_ref[...]) * (1.0 + mod_ref[:, 1:2, :]) + mod_ref[:, 0:1, :]
    h2 = h.reshape(nb * lt, d).astype(BF16)

    zg = _mm(h2, wg_ref[...])
    zsm = _mm(h2, wsm_ref[...])
    la_pre = _mm(zsm, wup_ref[...]) + bal_ref[...]
    log_a = (jnp.minimum(la_pre, 0.0) - jnp.log1p(jnp.exp(-jnp.abs(la_pre)))) * (1.0 / GLA_TAU)
    zg_ref[:, :, 0:512] = zg[:, 0:512].reshape(nb, lt, 512)
    zg_ref[:, :, 512:768] = log_a.reshape(nb, lt, 256)
    zg_ref[:, :, 768:1792] = zg[:, 512:1536].reshape(nb, lt, 1024)

    zd_ref[:, :, 0:2048] = _mm(h2, wd_ref[...]).reshape(nb, lt, 2048)
    lane = lax.broadcasted_iota(jnp.int32, zsm.shape, 1)
    beta = _sigmoid(zsm)
    log_g = -sp_ref[0:1, :] * _softplus(zsm + sp_ref[1:2, :])
    is_beta = (lane >= SMALL_BETA0) & (lane < SMALL_A0)
    is_a = (lane >= SMALL_A0) & (lane < SMALL_A0 + GDN_HEADS)
    small = jnp.where(is_beta, beta, jnp.where(is_a, log_g, 0.0))
    zd_ref[:, :, 2048:2176] = small.reshape(nb, lt, LANES)

    zs = _mm(h2, ws_ref[...])
    for i in range(nb):
        zs_ref[:, BRANCH_W * i:BRANCH_W * (i + 1)] = zs[lt * i:lt * (i + 1), :]


def _inproj_call(x, mod, p, nb, lt):
    B, T, D = x.shape
    grid = (B // nb, T // lt)
    tok = lambda w: pl.BlockSpec((nb, lt, w), lambda b, t: (b, t, 0))
    return pl.pallas_call(
        _inproj_kernel,
        grid=grid,
        in_specs=[tok(D),
                  pl.BlockSpec((nb, 6, D), lambda b, t: (b, 0, 0)),
                  _const_spec((1, D)),
                  _const_spec(p["w_gla"].shape), _const_spec(p["w_gdn"].shape), _const_spec(p["w_s5"].shape),
                  _const_spec(p["w_small"].shape), _const_spec(p["w_up"].shape), _const_spec((1, GLA_QK)),
                  _const_spec((2, LANES))],
        out_specs=[tok(1792), tok(2176), _time_major_spec(nb, lt)],
        out_shape=[jax.ShapeDtypeStruct((B, T, 1792), F32),
                   jax.ShapeDtypeStruct((B, T, 2176), F32),
                   jax.ShapeDtypeStruct(_time_major_shape(B, T, nb), F32)],
        compiler_params=_params(("parallel", "parallel")),
        name="inproj",
    )(x, mod, p["norm1_w"], p["w_gla"], p["w_gdn"], p["w_s5"], p["w_small"], p["w_up"], p["b_alpha"],
      p["gdn_small"])


def _gla_kernel(z_ref, s0_ref, nw_ref, o_ref, s_ref, *, n_chunks):
    L = CHUNK

    @pl.when(pl.program_id(1) == 0)
    def _():
        s_ref[...] = s0_ref[...]

    row = lax.broadcasted_iota(jnp.int32, (L, L), 0)
    col = lax.broadcasted_iota(jnp.int32, (L, L), 1)
    causal = row >= col
    lane = lax.broadcasted_iota(jnp.int32, (L, LANES), 1)
    own_half = (lane < GLA_DK, lane >= GLA_DK)
    nw = nw_ref[...]
    lt = n_chunks * L
    brow = lax.broadcasted_iota(jnp.int32, (lt, lt), 0)
    bcol = lax.broadcasted_iota(jnp.int32, (lt, lt), 1)
    shift = L.bit_length() - 1
    chunk_tril = ((brow >= bcol) & ((brow >> shift) == (bcol >> shift))).astype(BF16)
    b_all = _mm_sel(chunk_tril, z_ref[0, :, 512:768])

    probs = [(c, h) for c in range(n_chunks) for h in range(GLA_HEADS)]
    rows_of = lambda c: slice(L * c, L * (c + 1))
    pair_of = lambda h: slice(LANES * (h // 2), LANES * (h // 2) + LANES)
    qi_, ki_, qs_, kl_, dl_ = {}, {}, {}, {}, {}
    for c in range(n_chunks):
        q = z_ref[0, rows_of(c), 0:256]
        k = z_ref[0, rows_of(c), 256:512]
        b = b_all[rows_of(c), :]
        b_mid = b[L // 2 - 1:L // 2, :]
        b_last = b[L - 1:L, :]
        qi_[c] = q * jnp.exp(b - b_mid)
        ki_[c] = k * jnp.exp(b_mid - b)
        qs_[c] = q * jnp.exp(b)
        kl_[c] = k * jnp.exp(b_last - b)
        dl_[c] = jnp.exp(b_last)
    v_ = {p: z_ref[0, rows_of(p[0]), 768 + GLA_DV * p[1]:768 + GLA_DV * (p[1] + 1)] for p in probs}
    att_ = {p: jnp.where(causal, _mm_nt(qi_[p[0]][:, pair_of(p[1])],
                                        jnp.where(own_half[p[1] % 2], ki_[p[0]][:, pair_of(p[1])], 0.0)), 0.0)
            for p in probs}
    kv_ = {p: _mm_tn(v_[p], jnp.where(own_half[p[1] % 2], kl_[p[0]][:, pair_of(p[1])], 0.0)) for p in probs}
    av_ = {p: _mm(att_[p], v_[p]) for p in probs}
    st_ = {(0, h): s_ref[0, h] for h in range(GLA_HEADS)}
    for c in range(n_chunks):
        for h in range(GLA_HEADS):
            st_[(c + 1, h)] = st_[(c, h)] * dl_[c][:, pair_of(h)] + kv_[(c, h)]
    for p in probs:
        c, h = p
        o_h = av_[p] + _mm_nt(qs_[c][:, pair_of(h)], st_[p])
        gate = z_ref[0, rows_of(c), 1280 + GLA_DV * h:1280 + GLA_DV * (h + 1)]
        o_ref[0, rows_of(c), GLA_DV * h:GLA_DV * (h + 1)] = _rms_rows(o_h, nw) * _silu(gate)
    for h in range(GLA_HEADS):
        s_ref[0, h] = st_[(n_chunks, h)]


def _gla_call(zg, s0t, norm_w, lt):
    B, T, _ = zg.shape
    st_spec = pl.BlockSpec((1, GLA_HEADS, GLA_DV, LANES), lambda b, t: (b, 0, 0, 0))
    return pl.pallas_call(
        functools.partial(_gla_kernel, n_chunks=lt // CHUNK),
        grid=(B, T // lt),
        in_specs=[pl.BlockSpec((1, lt, 1792), lambda b, t: (b, t, 0)), st_spec, _const_spec((1, GLA_DV))],
        out_specs=[pl.BlockSpec((1, lt, GLA_V), lambda b, t: (b, t, 0)), st_spec],
        out_shape=[jax.ShapeDtypeStruct((B, T, GLA_V), F32),
                   jax.ShapeDtypeStruct((B, GLA_HEADS, GLA_DV, LANES), F32)],
        compiler_params=_params(("parallel", "arbitrary")),
        name="gla",
    )(zg, s0t, norm_w)


def _gdn_kernel(z_ref, cb_ref, s0_ref, cw_ref, nw_ref, o_ref, s_ref, cbo_ref, xbuf, qkv, bfull_ref, gfull_ref,
                grow_ref, glast_ref, *, lt):
    L = CHUNK
    n_chunks = lt // L
    t = pl.program_id(1)

    @pl.when(t == 0)
    def _():
        s_ref[...] = s0_ref[...]
        xbuf[0:SUBLANES, :] = cb_ref[0]

    @pl.when(t > 0)
    def _():
        xbuf[0:SUBLANES, :] = xbuf[lt:lt + SUBLANES, :]

    xbuf[SUBLANES:SUBLANES + lt, :] = z_ref[0, :, 0:GDN_CONV_CH]
    cbo_ref[0] = xbuf[lt:lt + SUBLANES, :]

    base = SUBLANES - (CONV_W - 1)
    conv = xbuf[base:base + lt, :] * cw_ref[0:1, :]
    for j in range(1, CONV_W):
        conv = conv + xbuf[base + j:base + j + lt, :] * cw_ref[j:j + 1, :]
    act = _silu(conv)
    for h in range(GDN_HEADS):
        for part, scale in ((0, GDN_DK ** -0.5), (1, 1.0)):
            sl = slice(part * GDN_QK + GDN_DK * h, part * GDN_QK + GDN_DK * (h + 1))
            xh = act[:, sl]
            ss = jnp.sum(xh * xh, axis=-1, keepdims=True)
            qkv[:, sl] = xh * (lax.rsqrt(ss + EPS) * scale)
    qkv[:, 2 * GDN_QK:] = act[:, 2 * GDN_QK:]

    row = lax.broadcasted_iota(jnp.int32, (L, L), 0)
    col = lax.broadcasted_iota(jnp.int32, (L, L), 1)
    causal = row >= col
    strict = row > col
    eye = (row == col).astype(F32)
    diag16 = (row >> 4) == (col >> 4)
    diag32 = (row >> 5) == (col >> 5)
    off32 = diag32 & jnp.logical_not(diag16)
    off64 = jnp.logical_not(diag32)
    sel_r = lax.broadcasted_iota(jnp.int32, (SUBLANES, LANES), 0)
    sel_c = lax.broadcasted_iota(jnp.int32, (SUBLANES, LANES), 1)
    sel = (sel_c == sel_r + SMALL_A0).astype(BF16)
    brow = lax.broadcasted_iota(jnp.int32, (lt, lt), 0)
    bcol = lax.broadcasted_iota(jnp.int32, (lt, lt), 1)
    shift = L.bit_length() - 1
    chunk_tril = ((brow >= bcol) & ((brow >> shift) == (bcol >> shift))).astype(BF16)
    small_all = z_ref[0, :, 2048:2176]
    gs_all = _mm_sel(chunk_tril, small_all)
    grow_all = _mm_nt_sel(sel, gs_all)
    e_r = lax.broadcasted_iota(jnp.int32, (LANES, GDN_HEADS * LANES), 0)
    e_h = lax.broadcasted_iota(jnp.int32, (LANES, GDN_HEADS * LANES), 1) >> (LANES.bit_length() - 1)
    bfull_ref[...] = _mm_sel_r(small_all, (e_r == e_h + SMALL_BETA0).astype(BF16))
    gfull = _mm_sel_r(gs_all, (e_r == e_h + SMALL_A0).astype(BF16))
    gfull_ref[...] = gfull
    for c in range(n_chunks):
        grow_ref[c] = grow_all[:, L * c:L * (c + 1)]
        glast_ref[c:c + 1, :] = gfull[L * (c + 1) - 1:L * (c + 1), :]
    nw = nw_ref[...]

    probs = [(c, h) for c in range(n_chunks) for h in range(GDN_HEADS)]
    rows_of = lambda c: slice(L * c, L * (c + 1))
    head_of = lambda h: slice(LANES * h, LANES * (h + 1))
    q_ = {p: qkv[rows_of(p[0]), GDN_DK * p[1]:GDN_DK * (p[1] + 1)] for p in probs}
    k_ = {p: qkv[rows_of(p[0]), GDN_QK + GDN_DK * p[1]:GDN_QK + GDN_DK * (p[1] + 1)] for p in probs}
    beta_ = {p: bfull_ref[rows_of(p[0]), head_of(p[1])] for p in probs}
    g_ = {p: gfull_ref[rows_of(p[0]), head_of(p[1])] for p in probs}
    dec_ = {p: jnp.exp(jnp.where(causal, g_[p][:, 0:L] - grow_ref[p[0]][p[1]:p[1] + 1, :], -jnp.inf))
            for p in probs}
    kk_ = {p: _mm_nt(k_[p], k_[p]) for p in probs}
    qk_ = {p: _mm_nt(q_[p], k_[p]) for p in probs}
    a_ = {p: jnp.where(strict, beta_[p][:, 0:L] * kk_[p] * dec_[p], 0.0) for p in probs}
    qk_ = {p: jnp.where(causal, qk_[p] * dec_[p], 0.0) for p in probs}
    pw_ = {p: jnp.where(diag16, -a_[p], 0.0) for p in probs}
    tinv_ = {p: eye + pw_[p] for p in probs}
    for _ in range(3):
        pw_ = {p: _mm(pw_[p], pw_[p]) for p in probs}
        tinv_ = {p: tinv_[p] + _mm(tinv_[p], pw_[p]) for p in probs}
    for off in (off32, off64):
        te_ = {p: _mm(tinv_[p], jnp.where(off, a_[p], 0.0)) for p in probs}
        tinv_ = {p: tinv_[p] - _mm(te_[p], tinv_[p]) for p in probs}
    eg_ = {p: jnp.exp(g_[p]) for p in probs}
    sol_ = {}
    for p in probs:
        v_p = qkv[rows_of(p[0]), 2 * GDN_QK + GDN_DV * p[1]:2 * GDN_QK + GDN_DV * (p[1] + 1)]
        rhs = jnp.concatenate([beta_[p] * v_p, (beta_[p] * eg_[p]) * k_[p]], axis=-1)
        sol_[p] = _mm(tinv_[p], rhs)

    st_ = [s_ref[0, h] for h in range(GDN_HEADS)]
    for c in range(n_chunks):
        heads = [(c, h) for h in range(GDN_HEADS)]
        g_last = glast_ref[c:c + 1, :]
        u_ = {p: sol_[p][:, 0:GDN_DV] - _mm(sol_[p][:, GDN_DV:], st_[p[1]]) for p in heads}
        o_ = {p: _mm(qk_[p], u_[p]) + _mm(q_[p] * eg_[p], st_[p[1]]) for p in heads}
        st_ = [st_[h] * jnp.exp(g_last[:, head_of(h)])
               + _mm_tn(k_[(c, h)] * jnp.exp(g_last[:, head_of(h)] - g_[(c, h)]), u_[(c, h)])
               for h in range(GDN_HEADS)]
        for h in range(GDN_HEADS):
            gate = z_ref[0, rows_of(c), GDN_CONV_CH + GDN_DV * h:GDN_CONV_CH + GDN_DV * (h + 1)]
            o_ref[0, rows_of(c), GDN_DV * h:GDN_DV * (h + 1)] = _rms_rows(o_[(c, h)], nw) * _silu(gate)
    for h in range(GDN_HEADS):
        s_ref[0, h] = st_[h]


def _gdn_call(zd, cb8, s0, conv_w, norm_w, lt):
    B, T, _ = zd.shape
    st_spec = pl.BlockSpec((1, GDN_HEADS, GDN_DK, GDN_DV), lambda b, t: (b, 0, 0, 0))
    cb_spec = pl.BlockSpec((1, SUBLANES, GDN_CONV_CH), lambda b, t: (b, 0, 0))
    return pl.pallas_call(
        functools.partial(_gdn_kernel, lt=lt),
        grid=(B, T // lt),
        in_specs=[pl.BlockSpec((1, lt, 2176), lambda b, t: (b, t, 0)), cb_spec, st_spec,
                  _const_spec((CONV_W, GDN_CONV_CH)), _const_spec((1, GDN_DV))],
        out_specs=[pl.BlockSpec((1, lt, GDN_V), lambda b, t: (b, t, 0)), st_spec, cb_spec],
        out_shape=[jax.ShapeDtypeStruct((B, T, GDN_V), F32),
                   jax.ShapeDtypeStruct((B, GDN_HEADS, GDN_DK, GDN_DV), F32),
                   jax.ShapeDtypeStruct((B, SUBLANES, GDN_CONV_CH), F32)],
        scratch_shapes=[pltpu.VMEM((lt + SUBLANES, GDN_CONV_CH), F32), pltpu.VMEM((lt, GDN_CONV_CH), F32),
                        pltpu.VMEM((lt, GDN_HEADS * LANES), F32), pltpu.VMEM((lt, GDN_HEADS * LANES), F32),
                        pltpu.VMEM((lt // CHUNK, SUBLANES, CHUNK), F32),
                        pltpu.VMEM((SUBLANES, GDN_HEADS * LANES), F32)],
        compiler_params=_params(("parallel", "arbitrary")),
        name="gdn",
    )(zd, cb8, s0, conv_w, norm_w)


S5_LANE_CHUNK = 512


def _s5_kernel(u_ref, h0_ref, bb_ref, cc_ref, lam_ref, d_ref, y_ref, h_ref, bu_ref, *, lt):
    nb = SUBLANES

    @pl.when(pl.program_id(1) == 0)
    def _():
        h_ref[...] = h0_ref[...]

    u = u_ref[...].reshape(lt * nb, BRANCH_W)
    ch, sh = BRANCH_W // 2, S5_HALF // 2
    for part in range(2):
        for hb in range(2):
            bu_ref[:, part * S5_HALF + hb * sh:part * S5_HALF + (hb + 1) * sh] = _mm(
                u[:, hb * ch:(hb + 1) * ch], bb_ref[2 * part + hb])
    for j in range(S5_HALF // S5_LANE_CHUNK):
        re = slice(S5_LANE_CHUNK * j, S5_LANE_CHUNK * (j + 1))
        im = slice(S5_HALF + S5_LANE_CHUNK * j, S5_HALF + S5_LANE_CHUNK * (j + 1))
        lr = jnp.broadcast_to(lam_ref[0:1, re], (nb, S5_LANE_CHUNK))
        li = jnp.broadcast_to(lam_ref[1:2, re], (nb, S5_LANE_CHUNK))

        def step(tt, carry, re=re, im=im, lr=lr, li=li):
            hr, hi = carry
            rows = pl.ds(pl.multiple_of(tt * nb, nb), nb)
            nr = lr * hr - li * hi + bu_ref[rows, re]
            ni = lr * hi + li * hr + bu_ref[rows, im]
            bu_ref[rows, re] = nr
            bu_ref[rows, im] = ni
            return nr, ni

        hr, hi = lax.fori_loop(0, lt, step, (h_ref[:, re], h_ref[:, im]))
        h_ref[:, re] = hr
        h_ref[:, im] = hi
    ys = [_mm(bu_ref[:, hb * sh:(hb + 1) * sh], cc_ref[hb])
          + _mm(bu_ref[:, S5_HALF + hb * sh:S5_HALF + (hb + 1) * sh], cc_ref[2 + hb]) for hb in range(2)]
    y = jnp.concatenate(ys, axis=-1) + d_ref[...] * u
    y_ref[...] = y.reshape(lt, nb, BRANCH_W)


def _s5_call(zs, h0, p, lt):
    T = zs.shape[0]
    B = h0.shape[0]
    nb = SUBLANES
    h_spec = pl.BlockSpec((nb, 2 * S5_HALF), lambda b, t: (b, 0))
    tok_spec = pl.BlockSpec((lt, None, nb, BRANCH_W), lambda b, t: (t, b, 0, 0))
    y, h = pl.pallas_call(
        functools.partial(_s5_kernel, lt=lt),
        grid=(B // nb, T // lt),
        in_specs=[tok_spec, h_spec,
                  _const_spec((4, BRANCH_W // 2, S5_HALF // 2)), _const_spec((4, S5_HALF // 2, BRANCH_W // 2)),
                  _const_spec((2, S5_HALF)), _const_spec((1, BRANCH_W))],
        out_specs=[tok_spec, h_spec],
        out_shape=[jax.ShapeDtypeStruct((T, B // nb, nb, BRANCH_W), F32),
                   jax.ShapeDtypeStruct((B, 2 * S5_HALF), F32)],
        scratch_shapes=[pltpu.VMEM((lt * nb, 2 * S5_HALF), F32)],
        compiler_params=_params(("parallel", "arbitrary")),
        name="s5",
    )(zs.reshape(T, B // nb, nb, BRANCH_W), h0, p["s5_bblk"], p["s5_cblk"], p["s5_lam"], p["s5_d"])
    return y.reshape(zs.shape), h


def _merge_kernel(x_ref, mod_ref, nw_ref, oa_ref, ob_ref, ys_ref, wgate_ref, bgate_ref, wa_ref, wb_ref, wc_ref,
                  wglu_ref, bglu_ref, wout_ref, o_ref):
    nb, lt, d = x_ref.shape
    n = nb * lt
    x = x_ref[...]
    h = _rms_rows(x, nw_ref[...]) * (1.0 + mod_ref[:, 1:2, :]) + mod_ref[:, 0:1, :]
    gate = _sigmoid(_mm(h.reshape(n, d), wgate_ref[...]) + bgate_ref[...])
    y_a = _mm(oa_ref[...].reshape(n, BRANCH_W), wa_ref[...])
    y_b = _mm(ob_ref[...].reshape(n, BRANCH_W), wb_ref[...])
    ys = [ys_ref[:, BRANCH_W * i:BRANCH_W * (i + 1)] for i in range(nb)]
    yc = _gelu_tanh(ys[0] if nb == 1 else jnp.concatenate(ys, axis=0))
    yc = yc * _sigmoid(_mm(yc, wglu_ref[...]) + bglu_ref[...])
    y_c = _mm(yc, wc_ref[...])
    mix = gate[:, 0:d] * y_a + gate[:, d:2 * d] * y_b + gate[:, 2 * d:3 * d] * y_c
    m = _mm(mix, wout_ref[...])
    o_ref[...] = x + mod_ref[:, 2:3, :] * m.reshape(nb, lt, d)


def _merge_call(x, mod, oa, ob, ys, p, nb, lt):
    B, T, D = x.shape
    tok = lambda w: pl.BlockSpec((nb, lt, w), lambda b, t: (b, t, 0))
    return pl.pallas_call(
        _merge_kernel,
        grid=(B // nb, T // lt),
        in_specs=[tok(D), pl.BlockSpec((nb, 6, D), lambda b, t: (b, 0, 0)), _const_spec((1, D)),
                  tok(BRANCH_W), tok(BRANCH_W), _time_major_spec(nb, lt),
                  _const_spec((D, 3 * D)), _const_spec((1, 3 * D)),
                  _const_spec((BRANCH_W, D)), _const_spec((BRANCH_W, D)), _const_spec((BRANCH_W, D)),
                  _const_spec((BRANCH_W, BRANCH_W)), _const_spec((1, BRANCH_W)), _const_spec((D, D))],
        out_specs=tok(D),
        out_shape=jax.ShapeDtypeStruct((B, T, D), F32),
        compiler_params=_params(("parallel", "parallel")),
        name="merge",
    )(x, mod, p["norm1_w"], oa, ob, ys, p["w_gate"], p["b_gate"], p["w_br_a"], p["w_br_b"], p["w_br_c"],
      p["w_glu"], p["b_glu"], p["w_out"])


MOE_ROWS = 160


def _moe_kernel(x_ref, mod_ref, nw_ref, wr_ref, br_ref, ex_ref, wg_ref, wu_ref, wd_ref, fw_ref, o_ref,
                hn_ref, comb_ref, krow_ref, acc_ref, *, final_norm):
    nb, lt, d = x_ref.shape
    n = nb * lt
    R = MOE_ROWS
    gw = EXPERTS_PER_GROUP * D_FF

    x = x_ref[...]
    hn = (_rms_rows(x, nw_ref[...]) * (1.0 + mod_ref[:, 4:5, :]) + mod_ref[:, 3:4, :]).reshape(n, d)
    hn_hi = hn.astype(BF16)
    hn_ref[...] = hn_hi
    hn_lo = (hn - hn_hi.astype(F32)).astype(BF16)
    logits = _mm(hn_hi, wr_ref[0]) + (_mm(hn_hi, wr_ref[1]) + _mm(hn_lo, wr_ref[0])) + br_ref[...]
    lane = lax.broadcasted_iota(jnp.int32, (n, LANES), 1)
    neg = -jnp.inf
    gl = jnp.where(lane < MOE_GROUPS, logits, neg)
    gmax = jnp.max(gl, axis=-1, keepdims=True)
    g_sel = jnp.min(jnp.where(gl == gmax, lane, LANES), axis=-1, keepdims=True)
    p_sel = 1.0 / jnp.sum(jnp.exp(gl - gmax), axis=-1, keepdims=True)
    e_id = lane - ROUTE_E0
    in_group = (e_id >= 0) & (e_id < N_EXPERTS) & ((e_id >> 2) == g_sel)
    el = jnp.where(in_group, logits, neg)
    m1 = jnp.max(el, axis=-1, keepdims=True)
    i1 = jnp.min(jnp.where(el == m1, lane, LANES), axis=-1, keepdims=True)
    el2 = jnp.where(lane == i1, neg, el)
    m2 = jnp.max(el2, axis=-1, keepdims=True)
    i2 = jnp.min(jnp.where(el2 == m2, lane, LANES), axis=-1, keepdims=True)
    e2 = jnp.exp(m2 - m1)
    w1 = p_sel / (1.0 + e2)
    w2 = p_sel * e2 / (1.0 + e2)
    comb_ref[...] = jnp.where(lane == i1, w1, jnp.where(lane == i2, w2, 0.0)).astype(BF16)

    member = lane == g_sel
    trow = lax.broadcasted_iota(jnp.int32, (n, n), 0)
    tcol = lax.broadcasted_iota(jnp.int32, (n, n), 1)
    rank = _mm((trow > tcol).astype(BF16), member.astype(BF16))
    key = jnp.where(member, rank, -1.0)
    sel_r = lax.broadcasted_iota(jnp.int32, (SUBLANES, LANES), 0)
    sel_c = lax.broadcasted_iota(jnp.int32, (SUBLANES, LANES), 1)
    krow_ref[...] = _mm_nt_sel((sel_r == sel_c).astype(BF16), key)
    acc_ref[...] = jnp.zeros_like(acc_ref)

    row_pos = lax.broadcasted_iota(jnp.int32, (R, n), 0).astype(F32)
    for g in range(MOE_GROUPS):
        last = jnp.max(key[:, g:g + 1])
        n_pass = (last.astype(jnp.int32) + R) // R

        def one_pass(s, carry, g=g):
            base = (s * R).astype(F32)
            gather = (krow_ref[g:g + 1, :] - base == row_pos).astype(BF16)
            xg = _mm(gather, hn_ref[...])
            cg = _mm(_mm(gather, comb_ref[...]), ex_ref[:, gw * g:gw * (g + 1)])
            experts = range(EXPERTS_PER_GROUP * g, EXPERTS_PER_GROUP * (g + 1))
            he = jnp.concatenate([_silu(_mm(xg, wg_ref[e])) * _mm(xg, wu_ref[e]) for e in experts], axis=-1)
            yg = _mm(he * cg, wd_ref[g])
            acc_ref[...] += _mm_tn(gather, yg)
            return carry

        lax.fori_loop(0, n_pass, one_pass, 0)

    out = x + mod_ref[:, 5:6, :] * acc_ref[...].reshape(nb, lt, d)
    if final_norm:
        out = _rms_rows(out, fw_ref[...])
    o_ref[...] = out


def _moe_call(x, mod, p, final_w, nb, lt, final_norm):
    B, T, D = x.shape
    n = nb * lt
    gw = EXPERTS_PER_GROUP * D_FF
    tok = pl.BlockSpec((nb, lt, D), lambda b, t: (b, t, 0))
    resident = lambda shape: pl.BlockSpec(shape, lambda *_: (0,) * len(shape), pipeline_mode=pl.Buffered(1))
    return pl.pallas_call(
        functools.partial(_moe_kernel, final_norm=final_norm),
        grid=(B // nb, T // lt),
        in_specs=[tok, pl.BlockSpec((nb, 6, D), lambda b, t: (b, 0, 0)), _const_spec((1, D)),
                  _const_spec((2, D, LANES)), _const_spec((1, LANES)), _const_spec((LANES, N_EXPERTS * D_FF)),
                  resident((N_EXPERTS, D, D_FF)), resident((N_EXPERTS, D, D_FF)), resident((MOE_GROUPS, gw, D)),
                  _const_spec((1, D))],
        out_specs=tok,
        out_shape=jax.ShapeDtypeStruct((B, T, D), F32),
        scratch_shapes=[pltpu.VMEM((n, D), BF16), pltpu.VMEM((n, LANES), BF16), pltpu.VMEM((SUBLANES, n), F32),
                        pltpu.VMEM((n, D), F32)],
        compiler_params=_params(("parallel", "parallel")),
        name="moe",
    )(x, mod, p["norm2_w"], p["w_route"], p["b_route"], p["route_expand"], p["w_eg"], p["w_eu"], p["w_ed"],
      final_w)


def _prep_layer(l, norm1_w, norm2_w, w_in, w_alpha_up, b_alpha, gla_norm_w, conv_w, gdn_a_log, gdn_dt_bias,
                gdn_norm_w, s5_a_re, s5_a_im, s5_log_dt, s5_b_re, s5_b_im, s5_c_re, s5_c_im, s5_d, w_glu, b_glu,
                w_br_a, w_br_b, w_br_c, w_gate, b_gate, w_out, w_route_group, b_route_group, w_route_expert,
                b_route_expert, w_exp_gate, w_exp_up, w_exp_down):
    D = D_MODEL
    w = w_in[l]
    o = 0
    cols = {}
    for name, size in (("qa", GLA_QK), ("ka", GLA_QK), ("va", GLA_V), ("ga", GLA_V), ("aa", GLA_LOWRANK),
                       ("qkv", GDN_CONV_CH), ("beta", GDN_HEADS), ("a", GDN_HEADS), ("gb", GDN_V),
                       ("u", BRANCH_W)):
        cols[name] = w[:, o:o + size]
        o += size
    p = {}
    p["norm1_w"] = norm1_w[l].reshape(1, D)
    p["norm2_w"] = norm2_w[l].reshape(1, D)
    p["w_gla"] = jnp.concatenate([cols["qa"] * (GLA_DK ** -0.5), cols["ka"], cols["va"], cols["ga"]],
                                 axis=1).astype(BF16)
    p["w_gdn"] = jnp.concatenate([cols["qkv"], cols["gb"]], axis=1).astype(BF16)
    p["w_s5"] = cols["u"].astype(BF16)
    assert SMALL_BETA0 == GLA_LOWRANK and SMALL_A0 == SMALL_BETA0 + GDN_HEADS
    used = SMALL_A0 + GDN_HEADS
    p["w_small"] = jnp.concatenate([cols["aa"], cols["beta"], cols["a"], jnp.zeros((D, LANES - used), F32)],
                                   axis=1).astype(BF16)
    p["w_up"] = jnp.pad(w_alpha_up[l], ((0, LANES - GLA_LOWRANK), (0, 0))).astype(BF16)
    p["b_alpha"] = b_alpha[l].reshape(1, GLA_QK)
    p["gdn_small"] = jnp.pad(jnp.stack([jnp.exp(gdn_a_log[l]), gdn_dt_bias[l]]),
                             ((0, 0), (SMALL_A0, LANES - used)))
    p["gla_norm_w"] = gla_norm_w[l].reshape(1, GLA_DV)
    p["gdn_norm_w"] = gdn_norm_w[l].reshape(1, GDN_DV)
    p["conv_w"] = conv_w[l]

    a_re, a_im = s5_a_re[l], s5_a_im[l]
    dt = jnp.exp(s5_log_dt[l])[:, None]
    mag = jnp.exp(a_re * dt)
    lr, li = mag * jnp.cos(a_im * dt), mag * jnp.sin(a_im * dt)
    den = a_re * a_re + a_im * a_im
    f_re = ((lr - 1.0) * a_re + li * a_im) / den
    f_im = (li * a_re - (lr - 1.0) * a_im) / den
    bb_re = f_re[..., None] * s5_b_re[l] - f_im[..., None] * s5_b_im[l]
    bb_im = f_re[..., None] * s5_b_im[l] + f_im[..., None] * s5_b_re[l]
    gh = S5_GROUPS // 2
    eye_h = jnp.eye(gh, dtype=F32)
    blk = lambda m: (jnp.transpose(m, (0, 2, 1))[:, :, None, :] * eye_h[:, None, :, None]
                     ).reshape(gh * S5_GROUP, gh * S5_STATE)
    cblk = lambda m: (jnp.transpose(m, (2, 0, 1))[None, :, :, :] * eye_h[:, None, :, None]
                      ).reshape(gh * S5_STATE, gh * S5_GROUP)
    p["s5_bblk"] = jnp.stack([blk(m[hb * gh:(hb + 1) * gh]) for m in (bb_re, bb_im) for hb in range(2)]
                             ).astype(BF16)
    p["s5_cblk"] = jnp.stack([cblk(m[hb * gh:(hb + 1) * gh]) for m in (s5_c_re[l], -s5_c_im[l])
                              for hb in range(2)]).astype(BF16)
    p["s5_lam"] = jnp.stack([lr.reshape(S5_HALF), li.reshape(S5_HALF)])
    p["s5_d"] = s5_d[l].reshape(1, BRANCH_W)

    p["w_gate"] = w_gate[l].astype(BF16)
    p["b_gate"] = b_gate[l].reshape(1, 3 * D)
    p["w_br_a"] = w_br_a[l].astype(BF16)
    p["w_br_b"] = w_br_b[l].astype(BF16)
    p["w_br_c"] = w_br_c[l].astype(BF16)
    p["w_glu"] = w_glu[l].astype(BF16)
    p["b_glu"] = b_glu[l].reshape(1, BRANCH_W)
    p["w_out"] = w_out[l].astype(BF16)

    assert ROUTE_E0 == MOE_GROUPS
    n_route = MOE_GROUPS + N_EXPERTS
    wr = jnp.concatenate([w_route_group[l], w_route_expert[l], jnp.zeros((D, LANES - n_route), F32)], axis=1)
    br = jnp.pad(jnp.concatenate([b_route_group[l], b_route_expert[l]]), (0, LANES - n_route)).reshape(1, LANES)
    wr_hi = wr.astype(BF16)
    p["w_route"] = jnp.stack([wr_hi, (wr - wr_hi.astype(F32)).astype(BF16)])
    p["b_route"] = br
    e_of_col = jnp.arange(N_EXPERTS * D_FF) // D_FF
    p["route_expand"] = (jnp.arange(LANES)[:, None] == (e_of_col[None, :] + ROUTE_E0)).astype(BF16)
    gw = EXPERTS_PER_GROUP * D_FF
    p["w_eg"] = w_exp_gate[l].astype(BF16)
    p["w_eu"] = w_exp_up[l].astype(BF16)
    p["w_ed"] = w_exp_down[l].reshape(MOE_GROUPS, gw, D).astype(BF16)
    return p


def _gla_state_in(s):
    st = jnp.swapaxes(s, -1, -2)
    z = jnp.zeros_like(st)
    halves = [jnp.concatenate([st[:, h], z[:, h]] if h % 2 == 0 else [z[:, h], st[:, h]], axis=-1)
              for h in range(GLA_HEADS)]
    return jnp.stack(halves, axis=1)


def _gla_state_out(st):
    parts = [st[:, h, :, GLA_DK * (h % 2):GLA_DK * (h % 2 + 1)] for h in range(GLA_HEADS)]
    return jnp.swapaxes(jnp.stack(parts, axis=1), -1, -2)


TOKEN_TILE = 256
MOE_TILE = 512


def _tiles(B, T):
    lt = min(TOKEN_TILE, T)
    nb = max(1, min(B, TOKEN_TILE // lt))
    lt_moe = min(MOE_TILE, T)
    nb_moe = max(1, min(B, MOE_TILE // lt_moe))
    return dict(nb=nb, lt=lt, lt_rec=lt, lt_s5=min(CHUNK, T), nb_moe=nb_moe, lt_moe=lt_moe)


def _trunk(x, mods, params, states, final_w):
    tiles = _tiles(x.shape[0], x.shape[1])
    nb, lt, lt_rec, lt_s5 = tiles["nb"], tiles["lt"], tiles["lt_rec"], tiles["lt_s5"]
    new = []
    for l, p in enumerate(params):
        s_gla, s_gdn, cbuf, s5_re, s5_im = states[l]
        B = x.shape[0]
        zg, zd, zs = _inproj_call(x, mods[l], p, nb, lt)
        oa, gla_t = _gla_call(zg, _gla_state_in(s_gla), p["gla_norm_w"], lt_rec)
        cb8 = jnp.concatenate([jnp.zeros((B, SUBLANES - (CONV_W - 1), GDN_CONV_CH), F32), cbuf], axis=1)
        ob, gdn_new, cb_new = _gdn_call(zd, cb8, s_gdn, p["conv_w"], p["gdn_norm_w"], lt_rec)
        h0 = jnp.concatenate([s5_re.reshape(B, S5_HALF), s5_im.reshape(B, S5_HALF)], axis=1)
        ys, h_new = _s5_call(zs, h0, p, lt_s5)
        x = _merge_call(x, mods[l], oa, ob, ys, p, nb, lt)
        x = _moe_call(x, mods[l], p, final_w, tiles["nb_moe"], tiles["lt_moe"],
                      final_norm=(l == len(params) - 1))
        new.append((_gla_state_out(gla_t), gdn_new, cb_new[:, SUBLANES - (CONV_W - 1):],
                    h_new[:, :S5_HALF].reshape(B, S5_GROUPS, S5_STATE),
                    h_new[:, S5_HALF:].reshape(B, S5_GROUPS, S5_STATE)))
    return x, [jnp.stack([new[l][i] for l in range(len(params))]) for i in range(5)]


def kernel(x_prompt, x_sample, c_prompt, c_sample, state_gla, state_gdn, cache_gdn_conv, state_s5_re, state_s5_im, norm1_w, norm2_w, final_norm_w, w_ada, b_ada, w_in, w_alpha_up, b_alpha, gla_norm_w, conv_w, gdn_a_log, gdn_dt_bias, gdn_norm_w, s5_a_re, s5_a_im, s5_log_dt, s5_b_re, s5_b_im, s5_c_re, s5_c_im, s5_d, w_glu, b_glu, w_br_a, w_br_b, w_br_c, w_gate, b_gate, w_out, w_route_group, b_route_group, w_route_expert, b_route_expert, w_exp_gate, w_exp_up, w_exp_down):
    depth = w_in.shape[0]
    bp, tp, _ = x_prompt.shape
    bs, ts, _ = x_sample.shape
    params = [_prep_layer(l, norm1_w, norm2_w, w_in, w_alpha_up, b_alpha, gla_norm_w, conv_w, gdn_a_log,
                          gdn_dt_bias, gdn_norm_w, s5_a_re, s5_a_im, s5_log_dt, s5_b_re, s5_b_im, s5_c_re,
                          s5_c_im, s5_d, w_glu, b_glu, w_br_a, w_br_b, w_br_c, w_gate, b_gate, w_out,
                          w_route_group, b_route_group, w_route_expert, b_route_expert, w_exp_gate, w_exp_up,
                          w_exp_down) for l in range(depth)]
    final_w = final_norm_w.reshape(1, D_MODEL)

    mod = _ada_call(jnp.concatenate([c_prompt, c_sample], axis=0), w_ada, b_ada)
    mod = mod.reshape(depth, bp + bs, 6, D_MODEL)
    mods_p = [mod[l, :bp] for l in range(depth)]
    mods_s = [mod[l, bp:] for l in range(depth)]

    zeros_p = [(jnp.zeros((bp, GLA_HEADS, GLA_DK, GLA_DV), F32), jnp.zeros((bp, GDN_HEADS, GDN_DK, GDN_DV), F32),
                jnp.zeros((bp, CONV_W - 1, GDN_CONV_CH), F32), jnp.zeros((bp, S5_GROUPS, S5_STATE), F32),
                jnp.zeros((bp, S5_GROUPS, S5_STATE), F32)) for _ in range(depth)]
    y_p, new_p = _trunk(x_prompt, mods_p, params, zeros_p, final_w)

    states_s = [(state_gla[l], state_gdn[l], cache_gdn_conv[l], state_s5_re[l], state_s5_im[l])
                for l in range(depth)]
    y_s, new_s = _trunk(x_sample, mods_s, params, states_s, final_w)
    return (y_p, y_s, *new_p, *new_s)
```

```python
import functools
import math

import jax
import jax.numpy as jnp
from jax import lax
from jax.experimental import pallas as pl
from jax.experimental.pallas import tpu as pltpu

F32 = jnp.float32
BF16 = jnp.bfloat16
HIGHEST = lax.Precision.HIGHEST

D_MODEL = 1024
EPS = 1e-6
CHUNK = 64
BRANCH_W = 512
GLA_HEADS, GLA_DK, GLA_DV, GLA_LOWRANK, GLA_TAU = 4, 64, 128, 16, 16.0
GDN_HEADS, GDN_DK, GDN_DV, CONV_W = 4, 128, 128, 4
S5_GROUP, S5_GROUPS, S5_STATE = 16, 32, 64
GLA_QK, GLA_V, GDN_QK, GDN_V = 256, 512, 512, 512
GDN_CONV_CH = 1536
MOE_GROUPS, EXPERTS_PER_GROUP, N_EXPERTS, D_FF = 4, 4, 16, 256

LANES = 128
SUBLANES = 8
VMEM_LIMIT = 56 * 1024 * 1024
S5_HALF = S5_GROUPS * S5_STATE
SMALL_BETA0, SMALL_A0 = 16, 20
ROUTE_E0 = 4


def _mm(a, b):
    return jnp.dot(a.astype(BF16), b.astype(BF16), preferred_element_type=F32)


def _mm_nt(a, b):
    return lax.dot_general(a.astype(BF16), b.astype(BF16), (((1,), (1,)), ((), ())), preferred_element_type=F32)


def _mm_tn(a, b):
    return lax.dot_general(a.astype(BF16), b.astype(BF16), (((0,), (0,)), ((), ())), preferred_element_type=F32)


def _mm_f32(a, b):
    return jnp.dot(a, b, preferred_element_type=F32, precision=HIGHEST)


def _mm_nt_f32(a, b):
    return lax.dot_general(a, b, (((1,), (1,)), ((), ())), preferred_element_type=F32, precision=HIGHEST)


def _split3(x):
    a = x.astype(BF16)
    r = x - a.astype(F32)
    b = r.astype(BF16)
    c = (r - b.astype(F32)).astype(BF16)
    return a, b, c


def _mm_sel(m01, x):
    a, b, c = _split3(x)
    return _mm(m01, a) + _mm(m01, b) + _mm(m01, c)


def _mm_sel_r(x, m01):
    a, b, c = _split3(x)
    return _mm(a, m01) + _mm(b, m01) + _mm(c, m01)


def _mm_nt_sel(m01, x):
    a, b, c = _split3(x)
    return _mm_nt(m01, a) + _mm_nt(m01, b) + _mm_nt(m01, c)


def _sigmoid(x):
    return 1.0 / (1.0 + jnp.exp(-x))


def _silu(x):
    return x * _sigmoid(x)


def _softplus(x):
    return jnp.maximum(x, 0.0) + jnp.log1p(jnp.exp(-jnp.abs(x)))


def _rms_rows(x, w):
    ms = jnp.mean(x * x, axis=-1, keepdims=True)
    return x * lax.rsqrt(ms + EPS) * w


def _gelu_tanh(x):
    c = math.sqrt(2.0 / math.pi)
    return x * (0.5 * (1.0 + jnp.tanh(c * (x + 0.044715 * (x * x * x)))))


def _const_spec(shape):
    nd = len(shape)
    return pl.BlockSpec(shape, lambda *_: (0,) * nd)


def _time_major_shape(B, T, nb):
    return (T, B * BRANCH_W)


def _time_major_spec(nb, lt):
    return pl.BlockSpec((lt, nb * BRANCH_W), lambda b, t: (t, b))


def _params(sem):
    return pltpu.CompilerParams(dimension_semantics=sem, vmem_limit_bytes=VMEM_LIMIT)


def _ada_kernel(c_ref, w_ref, b_ref, o_ref):
    c = c_ref[...]
    o_ref[0] = _mm(_silu(c), w_ref[0]) + b_ref[0]


def _ada_call(c_all, w_ada, b_ada):
    depth = w_ada.shape[0]
    nrow = c_all.shape[0]
    return pl.pallas_call(
        _ada_kernel,
        grid=(depth, 6),
        in_specs=[pl.BlockSpec((nrow, D_MODEL), lambda l, j: (0, 0)),
                  pl.BlockSpec((1, D_MODEL, D_MODEL), lambda l, j: (l, 0, j)),
                  pl.BlockSpec((1, 1, D_MODEL), lambda l, j: (l, 0, j))],
        out_specs=pl.BlockSpec((1, nrow, D_MODEL), lambda l, j: (l, 0, j)),
        out_shape=jax.ShapeDtypeStruct((depth, nrow, 6 * D_MODEL), F32),
        compiler_params=_params(("parallel", "parallel")),
        name="adaln",
    )(c_all, w_ada, b_ada.reshape(depth, 1, 6 * D_MODEL))


MXU_COLS = 256


def _inproj_kernel(x_ref, mod_ref, nw_ref, wg_ref, wd_ref, ws_ref, wsm_ref, wup_ref, bal_ref, sp_ref, cb_ref,
                   cw_ref, zg_ref, zd_ref, zs_ref, cbo_ref, *xbufs):
    nb, lt, d = x_ref.shape
    span = lt + SUBLANES

    @pl.when(pl.program_id(1) == 0)
    def _():
        for i in range(nb):
            for k, xb in enumerate(xbufs):
                xb[span * i + lt:span * (i + 1), :] = cb_ref[i, :, MXU_COLS * k:MXU_COLS * (k + 1)]

    x = x_ref[...]
    h = _rms_rows(x, nw_ref[...]) * (1.0 + mod_ref[:, 1:2, :]) + mod_ref[:, 0:1, :]
    h2 = h.reshape(nb * lt, d).astype(BF16)

    base = SUBLANES - (CONV_W - 1)
    for i in range(nb):
        for xb in xbufs:
            xb[span * i:span * i + SUBLANES, :] = xb[span * i + lt:span * (i + 1), :]
    def gla_block(j):
        dst = MXU_COLS * j if j < 2 else MXU_COLS * (j + 1)
        zg_ref[:, :, dst:dst + MXU_COLS] = _mm(h2, wg_ref[:, MXU_COLS * j:MXU_COLS * (j + 1)]).reshape(
            nb, lt, MXU_COLS)

    def gate_block(j):
        cs = slice(GDN_CONV_CH + MXU_COLS * j, GDN_CONV_CH + MXU_COLS * (j + 1))
        zd_ref[:, :, cs] = _mm(h2, wd_ref[:, cs]).reshape(nb, lt, MXU_COLS)

    def s5_block(j):
        zs = _mm(h2, ws_ref[:, MXU_COLS * j:MXU_COLS * (j + 1)])
        for i in range(nb):
            zs_ref[:, BRANCH_W * i + MXU_COLS * j:BRANCH_W * i + MXU_COLS * (j + 1)] = zs[lt * i:lt * (i + 1), :]

    def small_block(_):
        zsm = _mm(h2, wsm_ref[...])
        la_pre = _mm(zsm, wup_ref[...]) + bal_ref[...]
        log_a = (jnp.minimum(la_pre, 0.0) - jnp.log1p(jnp.exp(-jnp.abs(la_pre)))) * (1.0 / GLA_TAU)
        zg_ref[:, :, 512:768] = log_a.reshape(nb, lt, 256)
        lane = lax.broadcasted_iota(jnp.int32, zsm.shape, 1)
        beta = _sigmoid(zsm)
        log_g = -sp_ref[0:1, :] * _softplus(zsm + sp_ref[1:2, :])
        is_beta = (lane >= SMALL_BETA0) & (lane < SMALL_A0)
        is_a = (lane >= SMALL_A0) & (lane < SMALL_A0 + GDN_HEADS)
        small = jnp.where(is_beta, beta, jnp.where(is_a, log_g, 0.0))
        zd_ref[:, :, 2048:2176] = small.reshape(nb, lt, LANES)

    fillers = [[(gla_block, 0), (gate_block, 0)], [(gla_block, 1), (gate_block, 1)],
               [(gla_block, 2), (s5_block, 0)], [(gla_block, 3), (s5_block, 1)],
               [(gla_block, 4), (small_block, 0)], [(gla_block, 5)]]
    def conv_matmul(mblk):
        zd = _mm(h2, wd_ref[:, MXU_COLS * mblk:MXU_COLS * (mblk + 1)])
        for i in range(nb):
            xbufs[mblk][span * i + SUBLANES:span * (i + 1), :] = zd[lt * i:lt * (i + 1), :]
        for fn, arg in fillers[mblk]:
            fn(arg)

    n_blk = GDN_CONV_CH // MXU_COLS
    conv_matmul(0)
    for mblk in range(n_blk):
        if mblk + 1 < n_blk:
            conv_matmul(mblk + 1)
        xb = xbufs[mblk]
        for i in range(nb):
            r0 = span * i
            for cblk in range(MXU_COLS * mblk // LANES, MXU_COLS * (mblk + 1) // LANES):
                cs = slice(LANES * cblk, LANES * (cblk + 1))
                ls = slice(LANES * cblk - MXU_COLS * mblk, LANES * (cblk + 1) - MXU_COLS * mblk)
                conv = xb[r0 + base:r0 + base + lt, ls] * cw_ref[0:1, cs]
                for j in range(1, CONV_W):
                    conv = conv + xb[r0 + base + j:r0 + base + j + lt, ls] * cw_ref[j:j + 1, cs]
                act = _silu(conv)
                if cblk < 2 * GDN_HEADS:
                    scale = GDN_DK ** -0.5 if cblk < GDN_HEADS else 1.0
                    ss = jnp.sum(act * act, axis=-1, keepdims=True)
                    act = act * (lax.rsqrt(ss + EPS) * scale)
                zd_ref[i, :, cs] = act
    for i in range(nb):
        for k, xb in enumerate(xbufs):
            cbo_ref[i, :, MXU_COLS * k:MXU_COLS * (k + 1)] = xb[span * i + lt:span * (i + 1), :]


def _inproj_call(x, mod, cb8, p, nb, lt):
    B, T, D = x.shape
    assert nb == 1 or T == lt, "several sequences per step only when each is a single time block"
    grid = (B // nb, T // lt)
    tok = lambda w: pl.BlockSpec((nb, lt, w), lambda b, t: (b, t, 0))
    cb_spec = pl.BlockSpec((nb, SUBLANES, GDN_CONV_CH), lambda b, t: (b, 0, 0))
    return pl.pallas_call(
        _inproj_kernel,
        grid=grid,
        in_specs=[tok(D),
                  pl.BlockSpec((nb, 6, D), lambda b, t: (b, 0, 0)),
                  _const_spec((1, D)),
                  _const_spec(p["w_gla"].shape), _const_spec(p["w_gdn"].shape), _const_spec(p["w_s5"].shape),
                  _const_spec(p["w_small"].shape), _const_spec(p["w_up"].shape), _const_spec((1, GLA_QK)),
                  _const_spec((2, LANES)), cb_spec, _const_spec((CONV_W, GDN_CONV_CH))],
        out_specs=[tok(1792), tok(2176), _time_major_spec(nb, lt), cb_spec],
        out_shape=[jax.ShapeDtypeStruct((B, T, 1792), F32),
                   jax.ShapeDtypeStruct((B, T, 2176), F32),
                   jax.ShapeDtypeStruct(_time_major_shape(B, T, nb), F32),
                   jax.ShapeDtypeStruct((B, SUBLANES, GDN_CONV_CH), F32)],
        scratch_shapes=[pltpu.VMEM((nb * (lt + SUBLANES), MXU_COLS), F32)] * (GDN_CONV_CH // MXU_COLS),
        compiler_params=_params(("parallel", "arbitrary")),
        name="inproj",
    )(x, mod, p["norm1_w"], p["w_gla"], p["w_gdn"], p["w_s5"], p["w_small"], p["w_up"], p["b_alpha"],
      p["gdn_small"], cb8, p["conv_w"])


def _gla_kernel(z_ref, s0_ref, nw_ref, o_ref, s_ref, *, n_chunks):
    L = CHUNK

    @pl.when(pl.program_id(1) == 0)
    def _():
        s_ref[...] = s0_ref[...]

    row = lax.broadcasted_iota(jnp.int32, (L, L), 0)
    col = lax.broadcasted_iota(jnp.int32, (L, L), 1)
    causal = row >= col
    lane = lax.broadcasted_iota(jnp.int32, (L, LANES), 1)
    own_half = (lane < GLA_DK, lane >= GLA_DK)
    nw = nw_ref[...]
    lt = n_chunks * L
    brow = lax.broadcasted_iota(jnp.int32, (lt, lt), 0)
    bcol = lax.broadcasted_iota(jnp.int32, (lt, lt), 1)
    shift = L.bit_length() - 1
    chunk_tril = ((brow >= bcol) & ((brow >> shift) == (bcol >> shift))).astype(BF16)
    b_all = _mm_sel(chunk_tril, z_ref[0, :, 512:768])

    probs = [(c, h) for c in range(n_chunks) for h in range(GLA_HEADS)]
    rows_of = lambda c: slice(L * c, L * (c + 1))
    pair_of = lambda h: slice(LANES * (h // 2), LANES * (h // 2) + LANES)
    qi_, ki_, qs_, kl_, dl_ = {}, {}, {}, {}, {}
    for c in range(n_chunks):
        q = z_ref[0, rows_of(c), 0:256]
        k = z_ref[0, rows_of(c), 256:512]
        b = b_all[rows_of(c), :]
        b_mid = b[L // 2 - 1:L // 2, :]
        b_last = b[L - 1:L, :]
        qi_[c] = q * jnp.exp(b - b_mid)
        ki_[c] = k * jnp.exp(b_mid - b)
        qs_[c] = q * jnp.exp(b)
        kl_[c] = k * jnp.exp(b_last - b)
        dl_[c] = jnp.exp(b_last)
    v_ = {p: z_ref[0, rows_of(p[0]), 768 + GLA_DV * p[1]:768 + GLA_DV * (p[1] + 1)] for p in probs}
    att_ = {p: jnp.where(causal, _mm_nt(qi_[p[0]][:, pair_of(p[1])],
                                        jnp.where(own_half[p[1] % 2], ki_[p[0]][:, pair_of(p[1])], 0.0)), 0.0)
            for p in probs}
    kv_ = {p: _mm_tn(v_[p], jnp.where(own_half[p[1] % 2], kl_[p[0]][:, pair_of(p[1])], 0.0)) for p in probs}
    av_ = {p: _mm(att_[p], v_[p]) for p in probs}
    st_ = {(0, h): s_ref[0, h] for h in range(GLA_HEADS)}
    for c in range(n_chunks):
        for h in range(GLA_HEADS):
            st_[(c + 1, h)] = st_[(c, h)] * dl_[c][:, pair_of(h)] + kv_[(c, h)]
    for p in probs:
        c, h = p
        o_h = av_[p] + _mm_nt(qs_[c][:, pair_of(h)], st_[p])
        gate = z_ref[0, rows_of(c), 1280 + GLA_DV * h:1280 + GLA_DV * (h + 1)]
        o_ref[0, rows_of(c), GLA_DV * h:GLA_DV * (h + 1)] = _rms_rows(o_h, nw) * _silu(gate)
    for h in range(GLA_HEADS):
        s_ref[0, h] = st_[(n_chunks, h)]


def _gla_call(zg, s0t, norm_w, lt):
    B, T, _ = zg.shape
    st_spec = pl.BlockSpec((1, GLA_HEADS, GLA_DV, LANES), lambda b, t: (b, 0, 0, 0))
    return pl.pallas_call(
        functools.partial(_gla_kernel, n_chunks=lt // CHUNK),
        grid=(B, T // lt),
        in_specs=[pl.BlockSpec((1, lt, 1792), lambda b, t: (b, t, 0)), st_spec, _const_spec((1, GLA_DV))],
        out_specs=[pl.BlockSpec((1, lt, GLA_V), lambda b, t: (b, t, 0)), st_spec],
        out_shape=[jax.ShapeDtypeStruct((B, T, GLA_V), F32),
                   jax.ShapeDtypeStruct((B, GLA_HEADS, GLA_DV, LANES), F32)],
        compiler_params=_params(("parallel", "arbitrary")),
        name="gla",
    )(zg, s0t, norm_w)


def _gdn_kernel(z_ref, s0_ref, nw_ref, o_ref, s_ref, bfull_ref, gfull_ref, grow_ref, glast_ref, *, lt):
    L = CHUNK
    n_chunks = lt // L

    @pl.when(pl.program_id(1) == 0)
    def _():
        s_ref[...] = s0_ref[...]

    qkv = z_ref.at[0]
    row = lax.broadcasted_iota(jnp.int32, (L, L), 0)
    col = lax.broadcasted_iota(jnp.int32, (L, L), 1)
    causal = row >= col
    strict = row > col
    eye = (row == col).astype(F32)
    diag16 = (row >> 4) == (col >> 4)
    diag32 = (row >> 5) == (col >> 5)
    off32 = diag32 & jnp.logical_not(diag16)
    off64 = jnp.logical_not(diag32)
    sel_r = lax.broadcasted_iota(jnp.int32, (SUBLANES, LANES), 0)
    sel_c = lax.broadcasted_iota(jnp.int32, (SUBLANES, LANES), 1)
    sel = (sel_c == sel_r + SMALL_A0).astype(BF16)
    brow = lax.broadcasted_iota(jnp.int32, (lt, lt), 0)
    bcol = lax.broadcasted_iota(jnp.int32, (lt, lt), 1)
    shift = L.bit_length() - 1
    chunk_tril = ((brow >= bcol) & ((brow >> shift) == (bcol >> shift))).astype(BF16)
    small_all = z_ref[0, :, 2048:2176]
    gs_all = _mm_sel(chunk_tril, small_all)
    grow_all = _mm_nt_sel(sel, gs_all)
    e_r = lax.broadcasted_iota(jnp.int32, (LANES, GDN_HEADS * LANES), 0)
    e_h = lax.broadcasted_iota(jnp.int32, (LANES, GDN_HEADS * LANES), 1) >> (LANES.bit_length() - 1)
    bfull_ref[...] = _mm_sel_r(small_all, (e_r == e_h + SMALL_BETA0).astype(BF16))
    gfull = _mm_sel_r(gs_all, (e_r == e_h + SMALL_A0).astype(BF16))
    gfull_ref[...] = gfull
    for c in range(n_chunks):
        grow_ref[c] = grow_all[:, L * c:L * (c + 1)]
        glast_ref[c:c + 1, :] = gfull[L * (c + 1) - 1:L * (c + 1), :]
    nw = nw_ref[...]

    probs = [(c, h) for c in range(n_chunks) for h in range(GDN_HEADS)]
    rows_of = lambda c: slice(L * c, L * (c + 1))
    head_of = lambda h: slice(LANES * h, LANES * (h + 1))
    q_ = {p: qkv[rows_of(p[0]), GDN_DK * p[1]:GDN_DK * (p[1] + 1)] for p in probs}
    k_ = {p: qkv[rows_of(p[0]), GDN_QK + GDN_DK * p[1]:GDN_QK + GDN_DK * (p[1] + 1)] for p in probs}
    beta_ = {p: bfull_ref[rows_of(p[0]), head_of(p[1])] for p in probs}
    g_ = {p: gfull_ref[rows_of(p[0]), head_of(p[1])] for p in probs}
    dec_ = {p: jnp.exp(jnp.where(causal, g_[p][:, 0:L] - grow_ref[p[0]][p[1]:p[1] + 1, :], -jnp.inf))
            for p in probs}
    kk_ = {p: _mm_nt(k_[p], k_[p]) for p in probs}
    qk_ = {p: _mm_nt(q_[p], k_[p]) for p in probs}
    a_ = {p: jnp.where(strict, beta_[p][:, 0:L] * kk_[p] * dec_[p], 0.0) for p in probs}
    qk_ = {p: jnp.where(causal, qk_[p] * dec_[p], 0.0) for p in probs}
    pw_ = {p: jnp.where(diag16, -a_[p], 0.0) for p in probs}
    tinv_ = {p: eye + pw_[p] for p in probs}
    for _ in range(3):
        pw_ = {p: _mm(pw_[p], pw_[p]) for p in probs}
        tinv_ = {p: tinv_[p] + _mm(tinv_[p], pw_[p]) for p in probs}
    for off in (off32, off64):
        te_ = {p: _mm(tinv_[p], jnp.where(off, a_[p], 0.0)) for p in probs}
        tinv_ = {p: tinv_[p] - _mm(te_[p], tinv_[p]) for p in probs}
    eg_ = {p: jnp.exp(g_[p]) for p in probs}
    sol_ = {}
    for p in probs:
        v_p = qkv[rows_of(p[0]), 2 * GDN_QK + GDN_DV * p[1]:2 * GDN_QK + GDN_DV * (p[1] + 1)]
        rhs = jnp.concatenate([beta_[p] * v_p, (beta_[p] * eg_[p]) * k_[p]], axis=-1)
        sol_[p] = _mm(tinv_[p], rhs)

    st_ = [s_ref[0, h] for h in range(GDN_HEADS)]
    for c in range(n_chunks):
        heads = [(c, h) for h in range(GDN_HEADS)]
        g_last = glast_ref[c:c + 1, :]
        u_ = {p: sol_[p][:, 0:GDN_DV] - _mm(sol_[p][:, GDN_DV:], st_[p[1]]) for p in heads}
        o_ = {p: _mm(qk_[p], u_[p]) + _mm(q_[p] * eg_[p], st_[p[1]]) for p in heads}
        st_ = [st_[h] * jnp.exp(g_last[:, head_of(h)])
               + _mm_tn(k_[(c, h)] * jnp.exp(g_last[:, head_of(h)] - g_[(c, h)]), u_[(c, h)])
               for h in range(GDN_HEADS)]
        for h in range(GDN_HEADS):
            gate = z_ref[0, rows_of(c), GDN_CONV_CH + GDN_DV * h:GDN_CONV_CH + GDN_DV * (h + 1)]
            o_ref[0, rows_of(c), GDN_DV * h:GDN_DV * (h + 1)] = _rms_rows(o_[(c, h)], nw) * _silu(gate)
    for h in range(GDN_HEADS):
        s_ref[0, h] = st_[h]


def _gdn_call(zd, s0, norm_w, lt):
    B, T, _ = zd.shape
    assert lt // CHUNK <= SUBLANES
    st_spec = pl.BlockSpec((1, GDN_HEADS, GDN_DK, GDN_DV), lambda b, t: (b, 0, 0, 0))
    return pl.pallas_call(
        functools.partial(_gdn_kernel, lt=lt),
        grid=(B, T // lt),
        in_specs=[pl.BlockSpec((1, lt, 2176), lambda b, t: (b, t, 0)), st_spec, _const_spec((1, GDN_DV))],
        out_specs=[pl.BlockSpec((1, lt, GDN_V), lambda b, t: (b, t, 0)), st_spec],
        out_shape=[jax.ShapeDtypeStruct((B, T, GDN_V), F32),
                   jax.ShapeDtypeStruct((B, GDN_HEADS, GDN_DK, GDN_DV), F32)],
        scratch_shapes=[pltpu.VMEM((lt, GDN_HEADS * LANES), F32), pltpu.VMEM((lt, GDN_HEADS * LANES), F32),
                        pltpu.VMEM((lt // CHUNK, SUBLANES, CHUNK), F32),
                        pltpu.VMEM((SUBLANES, GDN_HEADS * LANES), F32)],
        compiler_params=_params(("parallel", "arbitrary")),
        name="gdn",
    )(zd, s0, norm_w)


S5_LANE_CHUNK = 512


def _s5_kernel(u_ref, h0_ref, bb_ref, cc_ref, lam_ref, d_ref, y_ref, h_ref, bu_ref, *, lt):
    nb = SUBLANES

    @pl.when(pl.program_id(1) == 0)
    def _():
        h_ref[...] = h0_ref[...]

    u = u_ref[...].reshape(lt * nb, BRANCH_W)
    ch, sh = BRANCH_W // 2, S5_HALF // 2
    for part in range(2):
        for hb in range(2):
            bu_ref[:, part * S5_HALF + hb * sh:part * S5_HALF + (hb + 1) * sh] = _mm(
                u[:, hb * ch:(hb + 1) * ch], bb_ref[2 * part + hb])
    for j in range(S5_HALF // S5_LANE_CHUNK):
        re = slice(S5_LANE_CHUNK * j, S5_LANE_CHUNK * (j + 1))
        im = slice(S5_HALF + S5_LANE_CHUNK * j, S5_HALF + S5_LANE_CHUNK * (j + 1))
        lr = jnp.broadcast_to(lam_ref[0:1, re], (nb, S5_LANE_CHUNK))
        li = jnp.broadcast_to(lam_ref[1:2, re], (nb, S5_LANE_CHUNK))

        def step(tt, carry, re=re, im=im, lr=lr, li=li):
            hr, hi = carry
            rows = pl.ds(pl.multiple_of(tt * nb, nb), nb)
            nr = lr * hr - li * hi + bu_ref[rows, re]
            ni = lr * hi + li * hr + bu_ref[rows, im]
            bu_ref[rows, re] = nr
            bu_ref[rows, im] = ni
            return nr, ni

        hr, hi = lax.fori_loop(0, lt, step, (h_ref[:, re], h_ref[:, im]))
        h_ref[:, re] = hr
        h_ref[:, im] = hi
    ys = [_mm(bu_ref[:, hb * sh:(hb + 1) * sh], cc_ref[hb])
          + _mm(bu_ref[:, S5_HALF + hb * sh:S5_HALF + (hb + 1) * sh], cc_ref[2 + hb]) for hb in range(2)]
    y = jnp.concatenate(ys, axis=-1) + d_ref[...] * u
    y_ref[...] = y.reshape(lt, nb, BRANCH_W)


def _s5_call(zs, h0, p, lt):
    T = zs.shape[0]
    B = h0.shape[0]
    nb = SUBLANES
    h_spec = pl.BlockSpec((nb, 2 * S5_HALF), lambda b, t: (b, 0))
    tok_spec = pl.BlockSpec((lt, None, nb, BRANCH_W), lambda b, t: (t, b, 0, 0))
    y, h = pl.pallas_call(
        functools.partial(_s5_kernel, lt=lt),
        grid=(B // nb, T // lt),
        in_specs=[tok_spec, h_spec,
                  _const_spec((4, BRANCH_W // 2, S5_HALF // 2)), _const_spec((4, S5_HALF // 2, BRANCH_W // 2)),
                  _const_spec((2, S5_HALF)), _const_spec((1, BRANCH_W))],
        out_specs=[tok_spec, h_spec],
        out_shape=[jax.ShapeDtypeStruct((T, B // nb, nb, BRANCH_W), F32),
                   jax.ShapeDtypeStruct((B, 2 * S5_HALF), F32)],
        scratch_shapes=[pltpu.VMEM((lt * nb, 2 * S5_HALF), F32)],
        compiler_params=_params(("parallel", "arbitrary")),
        name="s5",
    )(zs.reshape(T, B // nb, nb, BRANCH_W), h0, p["s5_bblk"], p["s5_cblk"], p["s5_lam"], p["s5_d"])
    return y.reshape(zs.shape), h


def _merge_kernel(x_ref, mod_ref, nw_ref, oa_ref, ob_ref, ys_ref, wgate_ref, bgate_ref, wa_ref, wb_ref, wc_ref,
                  wglu_ref, bglu_ref, wout_ref, o_ref):
    nb, lt, d = x_ref.shape
    n = nb * lt
    x = x_ref[...]
    h = _rms_rows(x, nw_ref[...]) * (1.0 + mod_ref[:, 1:2, :]) + mod_ref[:, 0:1, :]
    gate = _sigmoid(_mm(h.reshape(n, d), wgate_ref[...]) + bgate_ref[...])
    y_a = _mm(oa_ref[...].reshape(n, BRANCH_W), wa_ref[...])
    y_b = _mm(ob_ref[...].reshape(n, BRANCH_W), wb_ref[...])
    ys = [ys_ref[:, BRANCH_W * i:BRANCH_W * (i + 1)] for i in range(nb)]
    yc = _gelu_tanh(ys[0] if nb == 1 else jnp.concatenate(ys, axis=0))
    yc = yc * _sigmoid(_mm(yc, wglu_ref[...]) + bglu_ref[...])
    y_c = _mm(yc, wc_ref[...])
    mix = gate[:, 0:d] * y_a + gate[:, d:2 * d] * y_b + gate[:, 2 * d:3 * d] * y_c
    m = _mm(mix, wout_ref[...])
    o_ref[...] = x + mod_ref[:, 2:3, :] * m.reshape(nb, lt, d)


def _merge_call(x, mod, oa, ob, ys, p, nb, lt):
    B, T, D = x.shape
    tok = lambda w: pl.BlockSpec((nb, lt, w), lambda b, t: (b, t, 0))
    return pl.pallas_call(
        _merge_kernel,
        grid=(B // nb, T // lt),
        in_specs=[tok(D), pl.BlockSpec((nb, 6, D), lambda b, t: (b, 0, 0)), _const_spec((1, D)),
                  tok(BRANCH_W), tok(BRANCH_W), _time_major_spec(nb, lt),
                  _const_spec((D, 3 * D)), _const_spec((1, 3 * D)),
                  _const_spec((BRANCH_W, D)), _const_spec((BRANCH_W, D)), _const_spec((BRANCH_W, D)),
                  _const_spec((BRANCH_W, BRANCH_W)), _const_spec((1, BRANCH_W)), _const_spec((D, D))],
        out_specs=tok(D),
        out_shape=jax.ShapeDtypeStruct((B, T, D), F32),
        compiler_params=_params(("parallel", "parallel")),
        name="merge",
    )(x, mod, p["norm1_w"], oa, ob, ys, p["w_gate"], p["b_gate"], p["w_br_a"], p["w_br_b"], p["w_br_c"],
      p["w_glu"], p["b_glu"], p["w_out"])


MOE_ROWS = 160


def _moe_kernel(x_ref, mod_ref, nw_ref, wr_ref, br_ref, ex_ref, wg_ref, wu_ref, wd_ref, fw_ref, o_ref,
                hn_ref, comb_ref, krow_ref, acc_ref, *, final_norm):
    nb, lt, d = x_ref.shape
    n = nb * lt
    R = MOE_ROWS
    gw = EXPERTS_PER_GROUP * D_FF

    x = x_ref[...]
    hn = (_rms_rows(x, nw_ref[...]) * (1.0 + mod_ref[:, 4:5, :]) + mod_ref[:, 3:4, :]).reshape(n, d)
    hn_hi = hn.astype(BF16)
    hn_ref[...] = hn_hi
    hn_lo = (hn - hn_hi.astype(F32)).astype(BF16)
    logits = _mm(hn_hi, wr_ref[0]) + (_mm(hn_hi, wr_ref[1]) + _mm(hn_lo, wr_ref[0])) + br_ref[...]
    lane = lax.broadcasted_iota(jnp.int32, (n, LANES), 1)
    neg = -jnp.inf
    gl = jnp.where(lane < MOE_GROUPS, logits, neg)
    gmax = jnp.max(gl, axis=-1, keepdims=True)
    g_sel = jnp.min(jnp.where(gl == gmax, lane, LANES), axis=-1, keepdims=True)
    p_sel = 1.0 / jnp.sum(jnp.exp(gl - gmax), axis=-1, keepdims=True)
    e_id = lane - ROUTE_E0
    in_group = (e_id >= 0) & (e_id < N_EXPERTS) & ((e_id >> 2) == g_sel)
    el = jnp.where(in_group, logits, neg)
    m1 = jnp.max(el, axis=-1, keepdims=True)
    i1 = jnp.min(jnp.where(el == m1, lane, LANES), axis=-1, keepdims=True)
    el2 = jnp.where(lane == i1, neg, el)
    m2 = jnp.max(el2, axis=-1, keepdims=True)
    i2 = jnp.min(jnp.where(el2 == m2, lane, LANES), axis=-1, keepdims=True)
    e2 = jnp.exp(m2 - m1)
    w1 = p_sel / (1.0 + e2)
    w2 = p_sel * e2 / (1.0 + e2)
    comb_ref[...] = jnp.where(lane == i1, w1, jnp.where(lane == i2, w2, 0.0)).astype(BF16)

    member = lane == g_sel
    trow = lax.broadcasted_iota(jnp.int32, (n, n), 0)
    tcol = lax.broadcasted_iota(jnp.int32, (n, n), 1)
    rank = _mm((trow > tcol).astype(BF16), member.astype(BF16))
    key = jnp.where(member, rank, -1.0)
    sel_r = lax.broadcasted_iota(jnp.int32, (SUBLANES, LANES), 0)
    sel_c = lax.broadcasted_iota(jnp.int32, (SUBLANES, LANES), 1)
    krow_ref[...] = _mm_nt_sel((sel_r == sel_c).astype(BF16), key)
    acc_ref[...] = jnp.zeros_like(acc_ref)

    row_pos = lax.broadcasted_iota(jnp.int32, (R, n), 0).astype(F32)
    for g in range(MOE_GROUPS):
        last = jnp.max(key[:, g:g + 1])
        n_pass = (last.astype(jnp.int32) + R) // R

        def one_pass(s, carry, g=g):
            base = (s * R).astype(F32)
            gather = (krow_ref[g:g + 1, :] - base == row_pos).astype(BF16)
            xg = _mm(gather, hn_ref[...])
            cg = _mm(_mm(gather, comb_ref[...]), ex_ref[:, gw * g:gw * (g + 1)])
            experts = range(EXPERTS_PER_GROUP * g, EXPERTS_PER_GROUP * (g + 1))
            he = jnp.concatenate([_silu(_mm(xg, wg_ref[e])) * _mm(xg, wu_ref[e]) for e in experts], axis=-1)
            yg = _mm(he * cg, wd_ref[g])
            acc_ref[...] += _mm_tn(gather, yg)
            return carry

        lax.fori_loop(0, n_pass, one_pass, 0)

    out = x + mod_ref[:, 5:6, :] * acc_ref[...].reshape(nb, lt, d)
    if final_norm:
        out = _rms_rows(out, fw_ref[...])
    o_ref[...] = out


def _moe_call(x, mod, p, final_w, nb, lt, final_norm):
    B, T, D = x.shape
    n = nb * lt
    gw = EXPERTS_PER_GROUP * D_FF
    tok = pl.BlockSpec((nb, lt, D), lambda b, t: (b, t, 0))
    resident = lambda shape: pl.BlockSpec(shape, lambda *_: (0,) * len(shape), pipeline_mode=pl.Buffered(1))
    return pl.pallas_call(
        functools.partial(_moe_kernel, final_norm=final_norm),
        grid=(B // nb, T // lt),
        in_specs=[tok, pl.BlockSpec((nb, 6, D), lambda b, t: (b, 0, 0)), _const_spec((1, D)),
                  _const_spec((2, D, LANES)), _const_spec((1, LANES)), _const_spec((LANES, N_EXPERTS * D_FF)),
                  resident((N_EXPERTS, D, D_FF)), resident((N_EXPERTS, D, D_FF)), resident((MOE_GROUPS, gw, D)),
                  _const_spec((1, D))],
        out_specs=tok,
        out_shape=jax.ShapeDtypeStruct((B, T, D), F32),
        scratch_shapes=[pltpu.VMEM((n, D), BF16), pltpu.VMEM((n, LANES), BF16), pltpu.VMEM((SUBLANES, n), F32),
                        pltpu.VMEM((n, D), F32)],
        compiler_params=_params(("parallel", "parallel")),
        name="moe",
    )(x, mod, p["norm2_w"], p["w_route"], p["b_route"], p["route_expand"], p["w_eg"], p["w_eu"], p["w_ed"],
      final_w)


def _prep_layer(l, norm1_w, norm2_w, w_in, w_alpha_up, b_alpha, gla_norm_w, conv_w, gdn_a_log, gdn_dt_bias,
                gdn_norm_w, s5_a_re, s5_a_im, s5_log_dt, s5_b_re, s5_b_im, s5_c_re, s5_c_im, s5_d, w_glu, b_glu,
                w_br_a, w_br_b, w_br_c, w_gate, b_gate, w_out, w_route_group, b_route_group, w_route_expert,
                b_route_expert, w_exp_gate, w_exp_up, w_exp_down):
    D = D_MODEL
    w = w_in[l]
    o = 0
    cols = {}
    for name, size in (("qa", GLA_QK), ("ka", GLA_QK), ("va", GLA_V), ("ga", GLA_V), ("aa", GLA_LOWRANK),
                       ("qkv", GDN_CONV_CH), ("beta", GDN_HEADS), ("a", GDN_HEADS), ("gb", GDN_V),
                       ("u", BRANCH_W)):
        cols[name] = w[:, o:o + size]
        o += size
    p = {}
    p["norm1_w"] = norm1_w[l].reshape(1, D)
    p["norm2_w"] = norm2_w[l].reshape(1, D)
    p["w_gla"] = jnp.concatenate([cols["qa"] * (GLA_DK ** -0.5), cols["ka"], cols["va"], cols["ga"]],
                                 axis=1).astype(BF16)
    p["w_gdn"] = jnp.concatenate([cols["qkv"], cols["gb"]], axis=1).astype(BF16)
    p["w_s5"] = cols["u"].astype(BF16)
    assert SMALL_BETA0 == GLA_LOWRANK and SMALL_A0 == SMALL_BETA0 + GDN_HEADS
    used = SMALL_A0 + GDN_HEADS
    p["w_small"] = jnp.concatenate([cols["aa"], cols["beta"], cols["a"], jnp.zeros((D, LANES - used), F32)],
                                   axis=1).astype(BF16)
    p["w_up"] = jnp.pad(w_alpha_up[l], ((0, LANES - GLA_LOWRANK), (0, 0))).astype(BF16)
    p["b_alpha"] = b_alpha[l].reshape(1, GLA_QK)
    p["gdn_small"] = jnp.pad(jnp.stack([jnp.exp(gdn_a_log[l]), gdn_dt_bias[l]]),
                             ((0, 0), (SMALL_A0, LANES - used)))
    p["gla_norm_w"] = gla_norm_w[l].reshape(1, GLA_DV)
    p["gdn_norm_w"] = gdn_norm_w[l].reshape(1, GDN_DV)
    p["conv_w"] = conv_w[l]

    a_re, a_im = s5_a_re[l], s5_a_im[l]
    dt = jnp.exp(s5_log_dt[l])[:, None]
    mag = jnp.exp(a_re * dt)
    lr, li = mag * jnp.cos(a_im * dt), mag * jnp.sin(a_im * dt)
    den = a_re * a_re + a_im * a_im
    f_re = ((lr - 1.0) * a_re + li * a_im) / den
    f_im = (li * a_re - (lr - 1.0) * a_im) / den
    bb_re = f_re[..., None] * s5_b_re[l] - f_im[..., None] * s5_b_im[l]
    bb_im = f_re[..., None] * s5_b_im[l] + f_im[..., None] * s5_b_re[l]
    gh = S5_GROUPS // 2
    eye_h = jnp.eye(gh, dtype=F32)
    blk = lambda m: (jnp.transpose(m, (0, 2, 1))[:, :, None, :] * eye_h[:, None, :, None]
                     ).reshape(gh * S5_GROUP, gh * S5_STATE)
    cblk = lambda m: (jnp.transpose(m, (2, 0, 1))[None, :, :, :] * eye_h[:, None, :, None]
                      ).reshape(gh * S5_STATE, gh * S5_GROUP)
    p["s5_bblk"] = jnp.stack([blk(m[hb * gh:(hb + 1) * gh]) for m in (bb_re, bb_im) for hb in range(2)]
                             ).astype(BF16)
    p["s5_cblk"] = jnp.stack([cblk(m[hb * gh:(hb + 1) * gh]) for m in (s5_c_re[l], -s5_c_im[l])
                              for hb in range(2)]).astype(BF16)
    p["s5_lam"] = jnp.stack([lr.reshape(S5_HALF), li.reshape(S5_HALF)])
    p["s5_d"] = s5_d[l].reshape(1, BRANCH_W)

    p["w_gate"] = w_gate[l].astype(BF16)
    p["b_gate"] = b_gate[l].reshape(1, 3 * D)
    p["w_br_a"] = w_br_a[l].astype(BF16)
    p["w_br_b"] = w_br_b[l].astype(BF16)
    p["w_br_c"] = w_br_c[l].astype(BF16)
    p["w_glu"] = w_glu[l].astype(BF16)
    p["b_glu"] = b_glu[l].reshape(1, BRANCH_W)
    p["w_out"] = w_out[l].astype(BF16)

    assert ROUTE_E0 == MOE_GROUPS
    n_route = MOE_GROUPS + N_EXPERTS
    wr = jnp.concatenate([w_route_group[l], w_route_expert[l], jnp.zeros((D, LANES - n_route), F32)], axis=1)
    br = jnp.pad(jnp.concatenate([b_route_group[l], b_route_expert[l]]), (0, LANES - n_route)).reshape(1, LANES)
    wr_hi = wr.astype(BF16)
    p["w_route"] = jnp.stack([wr_hi, (wr - wr_hi.astype(F32)).astype(BF16)])
    p["b_route"] = br
    e_of_col = jnp.arange(N_EXPERTS * D_FF) // D_FF
    p["route_expand"] = (jnp.arange(LANES)[:, None] == (e_of_col[None, :] + ROUTE_E0)).astype(BF16)
    gw = EXPERTS_PER_GROUP * D_FF
    p["w_eg"] = w_exp_gate[l].astype(BF16)
    p["w_eu"] = w_exp_up[l].astype(BF16)
    p["w_ed"] = w_exp_down[l].reshape(MOE_GROUPS, gw, D).astype(BF16)
    return p


def _gla_state_in(s):
    st = jnp.swapaxes(s, -1, -2)
    z = jnp.zeros_like(st)
    halves = [jnp.concatenate([st[:, h], z[:, h]] if h % 2 == 0 else [z[:, h], st[:, h]], axis=-1)
              for h in range(GLA_HEADS)]
    return jnp.stack(halves, axis=1)


def _gla_state_out(st):
    parts = [st[:, h, :, GLA_DK * (h % 2):GLA_DK * (h % 2 + 1)] for h in range(GLA_HEADS)]
    return jnp.swapaxes(jnp.stack(parts, axis=1), -1, -2)


TOKEN_TILE = 512
MIXER_TILE = 256
MOE_TILE = 512


def _tiles(B, T):
    lt = min(TOKEN_TILE, T)
    nb = max(1, min(B, TOKEN_TILE // lt))
    lt_moe = min(MOE_TILE, T)
    nb_moe = max(1, min(B, MOE_TILE // lt_moe))
    return dict(nb=nb, lt=lt, lt_rec=min(MIXER_TILE, T), lt_s5=min(CHUNK, T), nb_moe=nb_moe, lt_moe=lt_moe)


def _trunk(x, mods, params, states, final_w):
    tiles = _tiles(x.shape[0], x.shape[1])
    nb, lt, lt_rec, lt_s5 = tiles["nb"], tiles["lt"], tiles["lt_rec"], tiles["lt_s5"]
    new = []
    for l, p in enumerate(params):
        s_gla, s_gdn, cbuf, s5_re, s5_im = states[l]
        B = x.shape[0]
        cb8 = jnp.concatenate([jnp.zeros((B, SUBLANES - (CONV_W - 1), GDN_CONV_CH), F32), cbuf], axis=1)
        zg, zd, zs, cb_new = _inproj_call(x, mods[l], cb8, p, nb, lt)
        oa, gla_t = _gla_call(zg, _gla_state_in(s_gla), p["gla_norm_w"], lt_rec)
        ob, gdn_new = _gdn_call(zd, s_gdn, p["gdn_norm_w"], lt_rec)
        h0 = jnp.concatenate([s5_re.reshape(B, S5_HALF), s5_im.reshape(B, S5_HALF)], axis=1)
        ys, h_new = _s5_call(zs, h0, p, lt_s5)
        x = _merge_call(x, mods[l], oa, ob, ys, p, nb, lt)
        x = _moe_call(x, mods[l], p, final_w, tiles["nb_moe"], tiles["lt_moe"],
                      final_norm=(l == len(params) - 1))
        new.append((_gla_state_out(gla_t), gdn_new, cb_new[:, SUBLANES - (CONV_W - 1):],
                    h_new[:, :S5_HALF].reshape(B, S5_GROUPS, S5_STATE),
                    h_new[:, S5_HALF:].reshape(B, S5_GROUPS, S5_STATE)))
    return x, [jnp.stack([new[l][i] for l in range(len(params))]) for i in range(5)]


def kernel(x_prompt, x_sample, c_prompt, c_sample, state_gla, state_gdn, cache_gdn_conv, state_s5_re, state_s5_im, norm1_w, norm2_w, final_norm_w, w_ada, b_ada, w_in, w_alpha_up, b_alpha, gla_norm_w, conv_w, gdn_a_log, gdn_dt_bias, gdn_norm_w, s5_a_re, s5_a_im, s5_log_dt, s5_b_re, s5_b_im, s5_c_re, s5_c_im, s5_d, w_glu, b_glu, w_br_a, w_br_b, w_br_c, w_gate, b_gate, w_out, w_route_group, b_route_group, w_route_expert, b_route_expert, w_exp_gate, w_exp_up, w_exp_down):
    depth = w_in.shape[0]
    bp, tp, _ = x_prompt.shape
    bs, ts, _ = x_sample.shape
    params = [_prep_layer(l, norm1_w, norm2_w, w_in, w_alpha_up, b_alpha, gla_norm_w, conv_w, gdn_a_log,
                          gdn_dt_bias, gdn_norm_w, s5_a_re, s5_a_im, s5_log_dt, s5_b_re, s5_b_im, s5_c_re,
                          s5_c_im, s5_d, w_glu, b_glu, w_br_a, w_br_b, w_br_c, w_gate, b_gate, w_out,
                          w_route_group, b_route_group, w_route_expert, b_route_expert, w_exp_gate, w_exp_up,
                          w_exp_down) for l in range(depth)]
    final_w = final_norm_w.reshape(1, D_MODEL)

    mod = _ada_call(jnp.concatenate([c_prompt, c_sample], axis=0), w_ada, b_ada)
    mod = mod.reshape(depth, bp + bs, 6, D_MODEL)
    mods_p = [mod[l, :bp] for l in range(depth)]
    mods_s = [mod[l, bp:] for l in range(depth)]

    zeros_p = [(jnp.zeros((bp, GLA_HEADS, GLA_DK, GLA_DV), F32), jnp.zeros((bp, GDN_HEADS, GDN_DK, GDN_DV), F32),
                jnp.zeros((bp, CONV_W - 1, GDN_CONV_CH), F32), jnp.zeros((bp, S5_GROUPS, S5_STATE), F32),
                jnp.zeros((bp, S5_GROUPS, S5_STATE), F32)) for _ in range(depth)]
    y_p, new_p = _trunk(x_prompt, mods_p, params, zeros_p, final_w)

    states_s = [(state_gla[l], state_gdn[l], cache_gdn_conv[l], state_s5_re[l], state_s5_im[l])
                for l in range(depth)]
    y_s, new_s = _trunk(x_sample, mods_s, params, states_s, final_w)
    return (y_p, y_s, *new_p, *new_s)
```

```python
import functools
import math

import jax
import jax.numpy as jnp
from jax import lax
from jax.experimental import pallas as pl
from jax.experimental.pallas import tpu as pltpu

F32 = jnp.float32
BF16 = jnp.bfloat16
HIGHEST = lax.Precision.HIGHEST

D_MODEL = 1024
EPS = 1e-6
CHUNK = 64
BRANCH_W = 512
GLA_HEADS, GLA_DK, GLA_DV, GLA_LOWRANK, GLA_TAU = 4, 64, 128, 16, 16.0
GDN_HEADS, GDN_DK, GDN_DV, CONV_W = 4, 128, 128, 4
S5_GROUP, S5_GROUPS, S5_STATE = 16, 32, 64
GLA_QK, GLA_V, GDN_QK, GDN_V = 256, 512, 512, 512
GDN_CONV_CH = 1536
MOE_GROUPS, EXPERTS_PER_GROUP, N_EXPERTS, D_FF = 4, 4, 16, 256

LANES = 128
SUBLANES = 8
VMEM_LIMIT = 56 * 1024 * 1024
S5_HALF = S5_GROUPS * S5_STATE
SMALL_BETA0, SMALL_A0 = 16, 20
ROUTE_E0 = 4


def _mm(a, b):
    return jnp.dot(a.astype(BF16), b.astype(BF16), preferred_element_type=F32)


def _mm_nt(a, b):
    return lax.dot_general(a.astype(BF16), b.astype(BF16), (((1,), (1,)), ((), ())), preferred_element_type=F32)


def _mm_tn(a, b):
    return lax.dot_general(a.astype(BF16), b.astype(BF16), (((0,), (0,)), ((), ())), preferred_element_type=F32)


def _mm_f32(a, b):
    return jnp.dot(a, b, preferred_element_type=F32, precision=HIGHEST)


def _mm_nt_f32(a, b):
    return lax.dot_general(a, b, (((1,), (1,)), ((), ())), preferred_element_type=F32, precision=HIGHEST)


def _split3(x):
    a = x.astype(BF16)
    r = x - a.astype(F32)
    b = r.astype(BF16)
    c = (r - b.astype(F32)).astype(BF16)
    return a, b, c


def _mm_sel(m01, x):
    a, b, c = _split3(x)
    return _mm(m01, a) + _mm(m01, b) + _mm(m01, c)


def _mm_sel_r(x, m01):
    a, b, c = _split3(x)
    return _mm(a, m01) + _mm(b, m01) + _mm(c, m01)


def _mm_nt_sel(m01, x):
    a, b, c = _split3(x)
    return _mm_nt(m01, a) + _mm_nt(m01, b) + _mm_nt(m01, c)


def _sigmoid(x):
    return 1.0 / (1.0 + jnp.exp(-x))


def _silu(x):
    return x * _sigmoid(x)


def _softplus(x):
    return jnp.maximum(x, 0.0) + jnp.log1p(jnp.exp(-jnp.abs(x)))


def _rms_rows(x, w):
    ms = jnp.mean(x * x, axis=-1, keepdims=True)
    return x * lax.rsqrt(ms + EPS) * w


def _gelu_tanh(x):
    c = math.sqrt(2.0 / math.pi)
    return x * (0.5 * (1.0 + jnp.tanh(c * (x + 0.044715 * (x * x * x)))))


def _const_spec(shape):
    nd = len(shape)
    return pl.BlockSpec(shape, lambda *_: (0,) * nd)


def _time_major_shape(B, T, nb):
    return (T, B * BRANCH_W)


def _time_major_spec(nb, lt):
    return pl.BlockSpec((lt, nb * BRANCH_W), lambda b, t: (t, b))


def _params(sem):
    return pltpu.CompilerParams(dimension_semantics=sem, vmem_limit_bytes=VMEM_LIMIT)


def _ada_kernel(c_ref, w_ref, b_ref, o_ref):
    c = c_ref[...]
    o_ref[0] = _mm(_silu(c), w_ref[0]) + b_ref[0]


def _ada_call(c_all, w_ada, b_ada):
    depth = w_ada.shape[0]
    nrow = c_all.shape[0]
    return pl.pallas_call(
        _ada_kernel,
        grid=(depth, 6),
        in_specs=[pl.BlockSpec((nrow, D_MODEL), lambda l, j: (0, 0)),
                  pl.BlockSpec((1, D_MODEL, D_MODEL), lambda l, j: (l, 0, j)),
                  pl.BlockSpec((1, 1, D_MODEL), lambda l, j: (l, 0, j))],
        out_specs=pl.BlockSpec((1, nrow, D_MODEL), lambda l, j: (l, 0, j)),
        out_shape=jax.ShapeDtypeStruct((depth, nrow, 6 * D_MODEL), F32),
        compiler_params=_params(("parallel", "parallel")),
        name="adaln",
    )(c_all, w_ada, b_ada.reshape(depth, 1, 6 * D_MODEL))


MXU_COLS = 256


def _inproj_kernel(x_ref, mod_ref, nw_ref, wg_ref, wd_ref, ws_ref, wsm_ref, wup_ref, bal_ref, sp_ref, cb_ref,
                   cw_ref, zg_ref, zd_ref, zs_ref, cbo_ref, *xbufs):
    nb, lt, d = x_ref.shape
    span = lt + SUBLANES

    @pl.when(pl.program_id(1) == 0)
    def _():
        for i in range(nb):
            for k, xb in enumerate(xbufs):
                xb[span * i + lt:span * (i + 1), :] = cb_ref[i, :, MXU_COLS * k:MXU_COLS * (k + 1)]

    x = x_ref[...]
    h = _rms_rows(x, nw_ref[...]) * (1.0 + mod_ref[:, 1:2, :]) + mod_ref[:, 0:1, :]
    h2 = h.reshape(nb * lt, d).astype(BF16)

    base = SUBLANES - (CONV_W - 1)
    for i in range(nb):
        for xb in xbufs:
            xb[span * i:span * i + SUBLANES, :] = xb[span * i + lt:span * (i + 1), :]
    def gla_block(j):
        dst = MXU_COLS * j if j < 2 else MXU_COLS * (j + 1)
        zg_ref[:, :, dst:dst + MXU_COLS] = _mm(h2, wg_ref[:, MXU_COLS * j:MXU_COLS * (j + 1)]).reshape(
            nb, lt, MXU_COLS)

    def gate_block(j):
        cs = slice(GDN_CONV_CH + MXU_COLS * j, GDN_CONV_CH + MXU_COLS * (j + 1))
        zd_ref[:, :, cs] = _mm(h2, wd_ref[:, cs]).reshape(nb, lt, MXU_COLS)

    def s5_block(j):
        zs = _mm(h2, ws_ref[:, MXU_COLS * j:MXU_COLS * (j + 1)])
        for i in range(nb):
            zs_ref[:, BRANCH_W * i + MXU_COLS * j:BRANCH_W * i + MXU_COLS * (j + 1)] = zs[lt * i:lt * (i + 1), :]

    def small_block(_):
        zsm = _mm(h2, wsm_ref[...])
        la_pre = _mm(zsm, wup_ref[...]) + bal_ref[...]
        log_a = (jnp.minimum(la_pre, 0.0) - jnp.log1p(jnp.exp(-jnp.abs(la_pre)))) * (1.0 / GLA_TAU)
        zg_ref[:, :, 512:768] = log_a.reshape(nb, lt, 256)
        lane = lax.broadcasted_iota(jnp.int32, zsm.shape, 1)
        beta = _sigmoid(zsm)
        log_g = -sp_ref[0:1, :] * _softplus(zsm + sp_ref[1:2, :])
        is_beta = (lane >= SMALL_BETA0) & (lane < SMALL_A0)
        is_a = (lane >= SMALL_A0) & (lane < SMALL_A0 + GDN_HEADS)
        small = jnp.where(is_beta, beta, jnp.where(is_a, log_g, 0.0))
        zd_ref[:, :, 2048:2176] = small.reshape(nb, lt, LANES)

    fillers = [[(gla_block, 0), (gate_block, 0)], [(gla_block, 1), (gate_block, 1)],
               [(gla_block, 2), (s5_block, 0)], [(gla_block, 3), (s5_block, 1)],
               [(gla_block, 4), (small_block, 0)], [(gla_block, 5)]]
    def conv_matmul(mblk):
        zd = _mm(h2, wd_ref[:, MXU_COLS * mblk:MXU_COLS * (mblk + 1)])
        for i in range(nb):
            xbufs[mblk][span * i + SUBLANES:span * (i + 1), :] = zd[lt * i:lt * (i + 1), :]
        for fn, arg in fillers[mblk]:
            fn(arg)

    n_blk = GDN_CONV_CH // MXU_COLS
    conv_matmul(0)
    for mblk in range(n_blk):
        if mblk + 1 < n_blk:
            conv_matmul(mblk + 1)
        xb = xbufs[mblk]
        for i in range(nb):
            r0 = span * i
            for cblk in range(MXU_COLS * mblk // LANES, MXU_COLS * (mblk + 1) // LANES):
                cs = slice(LANES * cblk, LANES * (cblk + 1))
                ls = slice(LANES * cblk - MXU_COLS * mblk, LANES * (cblk + 1) - MXU_COLS * mblk)
                conv = xb[r0 + base:r0 + base + lt, ls] * cw_ref[0:1, cs]
                for j in range(1, CONV_W):
                    conv = conv + xb[r0 + base + j:r0 + base + j + lt, ls] * cw_ref[j:j + 1, cs]
                act = _silu(conv)
                if cblk < 2 * GDN_HEADS:
                    scale = GDN_DK ** -0.5 if cblk < GDN_HEADS else 1.0
                    ss = jnp.sum(act * act, axis=-1, keepdims=True)
                    act = act * (lax.rsqrt(ss + EPS) * scale)
                zd_ref[i, :, cs] = act
    for i in range(nb):
        for k, xb in enumerate(xbufs):
            cbo_ref[i, :, MXU_COLS * k:MXU_COLS * (k + 1)] = xb[span * i + lt:span * (i + 1), :]


def _inproj_call(x, mod, cb8, p, nb, lt):
    B, T, D = x.shape
    assert nb == 1 or T == lt, "several sequences per step only when each is a single time block"
    grid = (B // nb, T // lt)
    tok = lambda w: pl.BlockSpec((nb, lt, w), lambda b, t: (b, t, 0))
    cb_spec = pl.BlockSpec((nb, SUBLANES, GDN_CONV_CH), lambda b, t: (b, 0, 0))
    return pl.pallas_call(
        _inproj_kernel,
        grid=grid,
        in_specs=[tok(D),
                  pl.BlockSpec((nb, 6, D), lambda b, t: (b, 0, 0)),
                  _const_spec((1, D)),
                  _const_spec(p["w_gla"].shape), _const_spec(p["w_gdn"].shape), _const_spec(p["w_s5"].shape),
                  _const_spec(p["w_small"].shape), _const_spec(p["w_up"].shape), _const_spec((1, GLA_QK)),
                  _const_spec((2, LANES)), cb_spec, _const_spec((CONV_W, GDN_CONV_CH))],
        out_specs=[tok(1792), tok(2176), _time_major_spec(nb, lt), cb_spec],
        out_shape=[jax.ShapeDtypeStruct((B, T, 1792), F32),
                   jax.ShapeDtypeStruct((B, T, 2176), F32),
                   jax.ShapeDtypeStruct(_time_major_shape(B, T, nb), F32),
                   jax.ShapeDtypeStruct((B, SUBLANES, GDN_CONV_CH), F32)],
        scratch_shapes=[pltpu.VMEM((nb * (lt + SUBLANES), MXU_COLS), F32)] * (GDN_CONV_CH // MXU_COLS),
        compiler_params=_params(("parallel", "arbitrary")),
        name="inproj",
    )(x, mod, p["norm1_w"], p["w_gla"], p["w_gdn"], p["w_s5"], p["w_small"], p["w_up"], p["b_alpha"],
      p["gdn_small"], cb8, p["conv_w"])


def _gla_kernel(z_ref, s0_ref, nw_ref, o_ref, s_ref):
    L = CHUNK
    nb, lt, _ = z_ref.shape
    n_chunks = lt // L
    n_rows = nb * lt

    @pl.when(pl.program_id(1) == 0)
    def _():
        s_ref[...] = s0_ref[...]

    row = lax.broadcasted_iota(jnp.int32, (L, L), 0)
    col = lax.broadcasted_iota(jnp.int32, (L, L), 1)
    causal = row >= col
    lane = lax.broadcasted_iota(jnp.int32, (L, LANES), 1)
    own_half = (lane < GLA_DK, lane >= GLA_DK)
    nw = nw_ref[...]
    brow = lax.broadcasted_iota(jnp.int32, (n_rows, n_rows), 0)
    bcol = lax.broadcasted_iota(jnp.int32, (n_rows, n_rows), 1)
    shift = L.bit_length() - 1
    chunk_tril = ((brow >= bcol) & ((brow >> shift) == (bcol >> shift))).astype(BF16)
    b_all = _mm_sel(chunk_tril, z_ref[:, :, 512:768].reshape(n_rows, GLA_QK))

    def zcols(c, lo, width):
        i, cc = divmod(c, n_chunks)
        return z_ref[i, L * cc:L * (cc + 1), lo:lo + width]

    all_chunks = range(nb * n_chunks)
    probs = [(c, h) for c in all_chunks for h in range(GLA_HEADS)]
    rows_of = lambda c: slice(L * c, L * (c + 1))
    pair_of = lambda h: slice(LANES * (h // 2), LANES * (h // 2) + LANES)
    qi_, ki_, qs_, kl_, dl_ = {}, {}, {}, {}, {}
    for c in all_chunks:
        q = zcols(c, 0, GLA_QK)
        k = zcols(c, GLA_QK, GLA_QK)
        b = b_all[rows_of(c), :]
        b_mid = b[L // 2 - 1:L // 2, :]
        b_last = b[L - 1:L, :]
        qi_[c] = q * jnp.exp(b - b_mid)
        ki_[c] = k * jnp.exp(b_mid - b)
        qs_[c] = q * jnp.exp(b)
        kl_[c] = k * jnp.exp(b_last - b)
        dl_[c] = jnp.exp(b_last)
    v_ = {p: zcols(p[0], 768 + GLA_DV * p[1], GLA_DV) for p in probs}
    att_ = {p: jnp.where(causal, _mm_nt(qi_[p[0]][:, pair_of(p[1])],
                                        jnp.where(own_half[p[1] % 2], ki_[p[0]][:, pair_of(p[1])], 0.0)), 0.0)
            for p in probs}
    kv_ = {p: _mm_tn(v_[p], jnp.where(own_half[p[1] % 2], kl_[p[0]][:, pair_of(p[1])], 0.0)) for p in probs}
    av_ = {p: _mm(att_[p], v_[p]) for p in probs}
    st_ = {}
    for i in range(nb):
        for h in range(GLA_HEADS):
            st = s_ref[i, h]
            for cc in range(n_chunks):
                c = n_chunks * i + cc
                st_[(c, h)] = st
                st = st * dl_[c][:, pair_of(h)] + kv_[(c, h)]
            s_ref[i, h] = st
    for p in probs:
        c, h = p
        i, cc = divmod(c, n_chunks)
        o_h = av_[p] + _mm_nt(qs_[c][:, pair_of(h)], st_[p])
        gate = zcols(c, 1280 + GLA_DV * h, GLA_DV)
        o_ref[i, L * cc:L * (cc + 1), GLA_DV * h:GLA_DV * (h + 1)] = _rms_rows(o_h, nw) * _silu(gate)


def _gla_call(zg, s0t, norm_w, nb, lt):
    B, T, _ = zg.shape
    st_spec = pl.BlockSpec((nb, GLA_HEADS, GLA_DV, LANES), lambda b, t: (b, 0, 0, 0))
    return pl.pallas_call(
        _gla_kernel,
        grid=(B // nb, T // lt),
        in_specs=[pl.BlockSpec((nb, lt, 1792), lambda b, t: (b, t, 0)), st_spec, _const_spec((1, GLA_DV))],
        out_specs=[pl.BlockSpec((nb, lt, GLA_V), lambda b, t: (b, t, 0)), st_spec],
        out_shape=[jax.ShapeDtypeStruct((B, T, GLA_V), F32),
                   jax.ShapeDtypeStruct((B, GLA_HEADS, GLA_DV, LANES), F32)],
        compiler_params=_params(("parallel", "arbitrary")),
        name="gla",
    )(zg, s0t, norm_w)


def _gdn_kernel(z_ref, s0_ref, nw_ref, o_ref, s_ref, bfull_ref, gfull_ref, grow_ref, glast_ref):
    L = CHUNK
    nb, lt, _ = z_ref.shape
    n_chunks = lt // L
    n_rows = nb * lt

    @pl.when(pl.program_id(1) == 0)
    def _():
        s_ref[...] = s0_ref[...]

    row = lax.broadcasted_iota(jnp.int32, (L, L), 0)
    col = lax.broadcasted_iota(jnp.int32, (L, L), 1)
    causal = row >= col
    strict = row > col
    eye = (row == col).astype(F32)
    diag16 = (row >> 4) == (col >> 4)
    diag32 = (row >> 5) == (col >> 5)
    off32 = diag32 & jnp.logical_not(diag16)
    off64 = jnp.logical_not(diag32)
    sel_r = lax.broadcasted_iota(jnp.int32, (SUBLANES, LANES), 0)
    sel_c = lax.broadcasted_iota(jnp.int32, (SUBLANES, LANES), 1)
    sel = (sel_c == sel_r + SMALL_A0).astype(BF16)
    brow = lax.broadcasted_iota(jnp.int32, (n_rows, n_rows), 0)
    bcol = lax.broadcasted_iota(jnp.int32, (n_rows, n_rows), 1)
    shift = L.bit_length() - 1
    chunk_tril = ((brow >= bcol) & ((brow >> shift) == (bcol >> shift))).astype(BF16)
    small_all = z_ref[:, :, 2048:2176].reshape(n_rows, LANES)
    gs_all = _mm_sel(chunk_tril, small_all)
    grow_all = _mm_nt_sel(sel, gs_all)
    e_r = lax.broadcasted_iota(jnp.int32, (LANES, GDN_HEADS * LANES), 0)
    e_h = lax.broadcasted_iota(jnp.int32, (LANES, GDN_HEADS * LANES), 1) >> (LANES.bit_length() - 1)
    bfull_ref[...] = _mm_sel_r(small_all, (e_r == e_h + SMALL_BETA0).astype(BF16))
    gfull = _mm_sel_r(gs_all, (e_r == e_h + SMALL_A0).astype(BF16))
    gfull_ref[...] = gfull
    for c in range(nb * n_chunks):
        grow_ref[c] = grow_all[:, L * c:L * (c + 1)]
        glast_ref[c:c + 1, :] = gfull[L * (c + 1) - 1:L * (c + 1), :]
    nw = nw_ref[...]

    probs = [(c, h) for c in range(nb * n_chunks) for h in range(GDN_HEADS)]
    rows_of = lambda c: slice(L * c, L * (c + 1))
    head_of = lambda h: slice(LANES * h, LANES * (h + 1))

    def zcols(c, lo, width):
        i, cc = divmod(c, n_chunks)
        return z_ref[i, L * cc:L * (cc + 1), lo:lo + width]

    q_ = {p: zcols(p[0], GDN_DK * p[1], GDN_DK) for p in probs}
    k_ = {p: zcols(p[0], GDN_QK + GDN_DK * p[1], GDN_DK) for p in probs}
    beta_ = {p: bfull_ref[rows_of(p[0]), head_of(p[1])] for p in probs}
    g_ = {p: gfull_ref[rows_of(p[0]), head_of(p[1])] for p in probs}
    dec_ = {p: jnp.exp(jnp.where(causal, g_[p][:, 0:L] - grow_ref[p[0]][p[1]:p[1] + 1, :], -jnp.inf))
            for p in probs}
    kk_ = {p: _mm_nt(k_[p], k_[p]) for p in probs}
    qk_ = {p: _mm_nt(q_[p], k_[p]) for p in probs}
    a_ = {p: jnp.where(strict, beta_[p][:, 0:L] * kk_[p] * dec_[p], 0.0) for p in probs}
    qk_ = {p: jnp.where(causal, qk_[p] * dec_[p], 0.0) for p in probs}
    pw_ = {p: jnp.where(diag16, -a_[p], 0.0) for p in probs}
    tinv_ = {p: eye + pw_[p] for p in probs}
    for _ in range(3):
        pw_ = {p: _mm(pw_[p], pw_[p]) for p in probs}
        tinv_ = {p: tinv_[p] + _mm(tinv_[p], pw_[p]) for p in probs}
    for off in (off32, off64):
        te_ = {p: _mm(tinv_[p], jnp.where(off, a_[p], 0.0)) for p in probs}
        tinv_ = {p: tinv_[p] - _mm(te_[p], tinv_[p]) for p in probs}
    eg_ = {p: jnp.exp(g_[p]) for p in probs}
    sol_ = {}
    for p in probs:
        v_p = zcols(p[0], 2 * GDN_QK + GDN_DV * p[1], GDN_DV)
        rhs = jnp.concatenate([beta_[p] * v_p, (beta_[p] * eg_[p]) * k_[p]], axis=-1)
        sol_[p] = _mm(tinv_[p], rhs)

    st_ = {(i, h): s_ref[i, h] for i in range(nb) for h in range(GDN_HEADS)}
    for cc in range(n_chunks):
        heads = [(i, n_chunks * i + cc, h) for i in range(nb) for h in range(GDN_HEADS)]
        u_, o_, new_ = {}, {}, {}
        for i, c, h in heads:
            u_[(c, h)] = sol_[(c, h)][:, 0:GDN_DV] - _mm(sol_[(c, h)][:, GDN_DV:], st_[(i, h)])
        for i, c, h in heads:
            o_[(c, h)] = _mm(qk_[(c, h)], u_[(c, h)]) + _mm(q_[(c, h)] * eg_[(c, h)], st_[(i, h)])
        for i, c, h in heads:
            g_last = glast_ref[c:c + 1, head_of(h)]
            new_[(i, h)] = (st_[(i, h)] * jnp.exp(g_last)
                            + _mm_tn(k_[(c, h)] * jnp.exp(g_last - g_[(c, h)]), u_[(c, h)]))
        st_ = new_
        for i, c, h in heads:
            gate = z_ref[i, L * cc:L * (cc + 1), GDN_CONV_CH + GDN_DV * h:GDN_CONV_CH + GDN_DV * (h + 1)]
            o_ref[i, L * cc:L * (cc + 1), GDN_DV * h:GDN_DV * (h + 1)] = _rms_rows(o_[(c, h)], nw) * _silu(gate)
    for i in range(nb):
        for h in range(GDN_HEADS):
            s_ref[i, h] = st_[(i, h)]


def _gdn_call(zd, s0, norm_w, nb, lt):
    B, T, _ = zd.shape
    n_rows = nb * lt
    assert n_rows // CHUNK <= SUBLANES
    st_spec = pl.BlockSpec((nb, GDN_HEADS, GDN_DK, GDN_DV), lambda b, t: (b, 0, 0, 0))
    return pl.pallas_call(
        _gdn_kernel,
        grid=(B // nb, T // lt),
        in_specs=[pl.BlockSpec((nb, lt, 2176), lambda b, t: (b, t, 0)), st_spec, _const_spec((1, GDN_DV))],
        out_specs=[pl.BlockSpec((nb, lt, GDN_V), lambda b, t: (b, t, 0)), st_spec],
        out_shape=[jax.ShapeDtypeStruct((B, T, GDN_V), F32),
                   jax.ShapeDtypeStruct((B, GDN_HEADS, GDN_DK, GDN_DV), F32)],
        scratch_shapes=[pltpu.VMEM((n_rows, GDN_HEADS * LANES), F32), pltpu.VMEM((n_rows, GDN_HEADS * LANES), F32),
                        pltpu.VMEM((n_rows // CHUNK, SUBLANES, CHUNK), F32),
                        pltpu.VMEM((SUBLANES, GDN_HEADS * LANES), F32)],
        compiler_params=_params(("parallel", "arbitrary")),
        name="gdn",
    )(zd, s0, norm_w)


S5_LANE_CHUNK = 512


def _s5_kernel(u_ref, h0_ref, bb_ref, cc_ref, lam_ref, d_ref, y_ref, h_ref, bu_ref, *, lt):
    nb = SUBLANES

    @pl.when(pl.program_id(1) == 0)
    def _():
        h_ref[...] = h0_ref[...]

    u = u_ref[...].reshape(lt * nb, BRANCH_W)
    ch, sh = BRANCH_W // 2, S5_HALF // 2
    for part in range(2):
        for hb in range(2):
            bu_ref[:, part * S5_HALF + hb * sh:part * S5_HALF + (hb + 1) * sh] = _mm(
                u[:, hb * ch:(hb + 1) * ch], bb_ref[2 * part + hb])
    for j in range(S5_HALF // S5_LANE_CHUNK):
        re = slice(S5_LANE_CHUNK * j, S5_LANE_CHUNK * (j + 1))
        im = slice(S5_HALF + S5_LANE_CHUNK * j, S5_HALF + S5_LANE_CHUNK * (j + 1))
        lr = jnp.broadcast_to(lam_ref[0:1, re], (nb, S5_LANE_CHUNK))
        li = jnp.broadcast_to(lam_ref[1:2, re], (nb, S5_LANE_CHUNK))

        def step(tt, carry, re=re, im=im, lr=lr, li=li):
            hr, hi = carry
            rows = pl.ds(pl.multiple_of(tt * nb, nb), nb)
            nr = lr * hr - li * hi + bu_ref[rows, re]
            ni = lr * hi + li * hr + bu_ref[rows, im]
            bu_ref[rows, re] = nr
            bu_ref[rows, im] = ni
            return nr, ni

        hr, hi = lax.fori_loop(0, lt, step, (h_ref[:, re], h_ref[:, im]))
        h_ref[:, re] = hr
        h_ref[:, im] = hi
    ys = [_mm(bu_ref[:, hb * sh:(hb + 1) * sh], cc_ref[hb])
          + _mm(bu_ref[:, S5_HALF + hb * sh:S5_HALF + (hb + 1) * sh], cc_ref[2 + hb]) for hb in range(2)]
    y = jnp.concatenate(ys, axis=-1) + d_ref[...] * u
    y_ref[...] = y.reshape(lt, nb, BRANCH_W)


def _s5_call(zs, h0, p, lt):
    T = zs.shape[0]
    B = h0.shape[0]
    nb = SUBLANES
    h_spec = pl.BlockSpec((nb, 2 * S5_HALF), lambda b, t: (b, 0))
    tok_spec = pl.BlockSpec((lt, None, nb, BRANCH_W), lambda b, t: (t, b, 0, 0))
    y, h = pl.pallas_call(
        functools.partial(_s5_kernel, lt=lt),
        grid=(B // nb, T // lt),
        in_specs=[tok_spec, h_spec,
                  _const_spec((4, BRANCH_W // 2, S5_HALF // 2)), _const_spec((4, S5_HALF // 2, BRANCH_W // 2)),
                  _const_spec((2, S5_HALF)), _const_spec((1, BRANCH_W))],
        out_specs=[tok_spec, h_spec],
        out_shape=[jax.ShapeDtypeStruct((T, B // nb, nb, BRANCH_W), F32),
                   jax.ShapeDtypeStruct((B, 2 * S5_HALF), F32)],
        scratch_shapes=[pltpu.VMEM((lt * nb, 2 * S5_HALF), F32)],
        compiler_params=_params(("parallel", "arbitrary")),
        name="s5",
    )(zs.reshape(T, B // nb, nb, BRANCH_W), h0, p["s5_bblk"], p["s5_cblk"], p["s5_lam"], p["s5_d"])
    return y.reshape(zs.shape), h


def _merge_kernel(x_ref, mod_ref, nw_ref, oa_ref, ob_ref, ys_ref, wgate_ref, bgate_ref, wa_ref, wb_ref, wc_ref,
                  wglu_ref, bglu_ref, wout_ref, o_ref):
    nb, lt, d = x_ref.shape
    n = nb * lt
    x = x_ref[...]
    h = _rms_rows(x, nw_ref[...]) * (1.0 + mod_ref[:, 1:2, :]) + mod_ref[:, 0:1, :]
    gate = _sigmoid(_mm(h.reshape(n, d), wgate_ref[...]) + bgate_ref[...])
    y_a = _mm(oa_ref[...].reshape(n, BRANCH_W), wa_ref[...])
    y_b = _mm(ob_ref[...].reshape(n, BRANCH_W), wb_ref[...])
    ys = [ys_ref[:, BRANCH_W * i:BRANCH_W * (i + 1)] for i in range(nb)]
    yc = _gelu_tanh(ys[0] if nb == 1 else jnp.concatenate(ys, axis=0))
    yc = yc * _sigmoid(_mm(yc, wglu_ref[...]) + bglu_ref[...])
    y_c = _mm(yc, wc_ref[...])
    mix = gate[:, 0:d] * y_a + gate[:, d:2 * d] * y_b + gate[:, 2 * d:3 * d] * y_c
    m = _mm(mix, wout_ref[...])
    o_ref[...] = x + mod_ref[:, 2:3, :] * m.reshape(nb, lt, d)


def _merge_call(x, mod, oa, ob, ys, p, nb, lt):
    B, T, D = x.shape
    tok = lambda w: pl.BlockSpec((nb, lt, w), lambda b, t: (b, t, 0))
    return pl.pallas_call(
        _merge_kernel,
        grid=(B // nb, T // lt),
        in_specs=[tok(D), pl.BlockSpec((nb, 6, D), lambda b, t: (b, 0, 0)), _const_spec((1, D)),
                  tok(BRANCH_W), tok(BRANCH_W), _time_major_spec(nb, lt),
                  _const_spec((D, 3 * D)), _const_spec((1, 3 * D)),
                  _const_spec((BRANCH_W, D)), _const_spec((BRANCH_W, D)), _const_spec((BRANCH_W, D)),
                  _const_spec((BRANCH_W, BRANCH_W)), _const_spec((1, BRANCH_W)), _const_spec((D, D))],
        out_specs=tok(D),
        out_shape=jax.ShapeDtypeStruct((B, T, D), F32),
        compiler_params=_params(("parallel", "parallel")),
        name="merge",
    )(x, mod, p["norm1_w"], oa, ob, ys, p["w_gate"], p["b_gate"], p["w_br_a"], p["w_br_b"], p["w_br_c"],
      p["w_glu"], p["b_glu"], p["w_out"])


MOE_ROWS = 160


def _moe_kernel(x_ref, mod_ref, nw_ref, wr_ref, br_ref, ex_ref, wg_ref, wu_ref, wd_ref, fw_ref, o_ref,
                hn_ref, comb_ref, krow_ref, acc_ref, *, final_norm):
    nb, lt, d = x_ref.shape
    n = nb * lt
    R = MOE_ROWS
    gw = EXPERTS_PER_GROUP * D_FF

    x = x_ref[...]
    hn = (_rms_rows(x, nw_ref[...]) * (1.0 + mod_ref[:, 4:5, :]) + mod_ref[:, 3:4, :]).reshape(n, d)
    hn_hi = hn.astype(BF16)
    hn_ref[...] = hn_hi
    hn_lo = (hn - hn_hi.astype(F32)).astype(BF16)
    logits = _mm(hn_hi, wr_ref[0]) + (_mm(hn_hi, wr_ref[1]) + _mm(hn_lo, wr_ref[0])) + br_ref[...]
    lane = lax.broadcasted_iota(jnp.int32, (n, LANES), 1)
    neg = -jnp.inf
    gl = jnp.where(lane < MOE_GROUPS, logits, neg)
    gmax = jnp.max(gl, axis=-1, keepdims=True)
    g_sel = jnp.min(jnp.where(gl == gmax, lane, LANES), axis=-1, keepdims=True)
    p_sel = 1.0 / jnp.sum(jnp.exp(gl - gmax), axis=-1, keepdims=True)
    e_id = lane - ROUTE_E0
    in_group = (e_id >= 0) & (e_id < N_EXPERTS) & ((e_id >> 2) == g_sel)
    el = jnp.where(in_group, logits, neg)
    m1 = jnp.max(el, axis=-1, keepdims=True)
    i1 = jnp.min(jnp.where(el == m1, lane, LANES), axis=-1, keepdims=True)
    el2 = jnp.where(lane == i1, neg, el)
    m2 = jnp.max(el2, axis=-1, keepdims=True)
    i2 = jnp.min(jnp.where(el2 == m2, lane, LANES), axis=-1, keepdims=True)
    e2 = jnp.exp(m2 - m1)
    w1 = p_sel / (1.0 + e2)
    w2 = p_sel * e2 / (1.0 + e2)
    comb_ref[...] = jnp.where(lane == i1, w1, jnp.where(lane == i2, w2, 0.0)).astype(BF16)

    member = lane == g_sel
    trow = lax.broadcasted_iota(jnp.int32, (n, n), 0)
    tcol = lax.broadcasted_iota(jnp.int32, (n, n), 1)
    rank = _mm((trow > tcol).astype(BF16), member.astype(BF16))
    key = jnp.where(member, rank, -1.0)
    sel_r = lax.broadcasted_iota(jnp.int32, (SUBLANES, LANES), 0)
    sel_c = lax.broadcasted_iota(jnp.int32, (SUBLANES, LANES), 1)
    krow_ref[...] = _mm_nt_sel((sel_r == sel_c).astype(BF16), key)
    acc_ref[...] = jnp.zeros_like(acc_ref)

    row_pos = lax.broadcasted_iota(jnp.int32, (R, n), 0).astype(F32)
    for g in range(MOE_GROUPS):
        last = jnp.max(key[:, g:g + 1])
        n_pass = (last.astype(jnp.int32) + R) // R

        def one_pass(s, carry, g=g):
            base = (s * R).astype(F32)
            gather = (krow_ref[g:g + 1, :] - base == row_pos).astype(BF16)
            xg = _mm(gather, hn_ref[...])
            cg = _mm(_mm(gather, comb_ref[...]), ex_ref[:, gw * g:gw * (g + 1)])
            experts = range(EXPERTS_PER_GROUP * g, EXPERTS_PER_GROUP * (g + 1))
            he = jnp.concatenate([_silu(_mm(xg, wg_ref[e])) * _mm(xg, wu_ref[e]) for e in experts], axis=-1)
            yg = _mm(he * cg, wd_ref[g])
            acc_ref[...] += _mm_tn(gather, yg)
            return carry

        lax.fori_loop(0, n_pass, one_pass, 0)

    out = x + mod_ref[:, 5:6, :] * acc_ref[...].reshape(nb, lt, d)
    if final_norm:
        out = _rms_rows(out, fw_ref[...])
    o_ref[...] = out


def _moe_call(x, mod, p, final_w, nb, lt, final_norm):
    B, T, D = x.shape
    n = nb * lt
    gw = EXPERTS_PER_GROUP * D_FF
    tok = pl.BlockSpec((nb, lt, D), lambda b, t: (b, t, 0))
    resident = lambda shape: pl.BlockSpec(shape, lambda *_: (0,) * len(shape), pipeline_mode=pl.Buffered(1))
    return pl.pallas_call(
        functools.partial(_moe_kernel, final_norm=final_norm),
        grid=(B // nb, T // lt),
        in_specs=[tok, pl.BlockSpec((nb, 6, D), lambda b, t: (b, 0, 0)), _const_spec((1, D)),
                  _const_spec((2, D, LANES)), _const_spec((1, LANES)), _const_spec((LANES, N_EXPERTS * D_FF)),
                  resident((N_EXPERTS, D, D_FF)), resident((N_EXPERTS, D, D_FF)), resident((MOE_GROUPS, gw, D)),
                  _const_spec((1, D))],
        out_specs=tok,
        out_shape=jax.ShapeDtypeStruct((B, T, D), F32),
        scratch_shapes=[pltpu.VMEM((n, D), BF16), pltpu.VMEM((n, LANES), BF16), pltpu.VMEM((SUBLANES, n), F32),
                        pltpu.VMEM((n, D), F32)],
        compiler_params=_params(("parallel", "parallel")),
        name="moe",
    )(x, mod, p["norm2_w"], p["w_route"], p["b_route"], p["route_expand"], p["w_eg"], p["w_eu"], p["w_ed"],
      final_w)


def _prep_layer(l, norm1_w, norm2_w, w_in, w_alpha_up, b_alpha, gla_norm_w, conv_w, gdn_a_log, gdn_dt_bias,
                gdn_norm_w, s5_a_re, s5_a_im, s5_log_dt, s5_b_re, s5_b_im, s5_c_re, s5_c_im, s5_d, w_glu, b_glu,
                w_br_a, w_br_b, w_br_c, w_gate, b_gate, w_out, w_route_group, b_route_group, w_route_expert,
                b_route_expert, w_exp_gate, w_exp_up, w_exp_down):
    D = D_MODEL
    w = w_in[l]
    o = 0
    cols = {}
    for name, size in (("qa", GLA_QK), ("ka", GLA_QK), ("va", GLA_V), ("ga", GLA_V), ("aa", GLA_LOWRANK),
                       ("qkv", GDN_CONV_CH), ("beta", GDN_HEADS), ("a", GDN_HEADS), ("gb", GDN_V),
                       ("u", BRANCH_W)):
        cols[name] = w[:, o:o + size]
        o += size
    p = {}
    p["norm1_w"] = norm1_w[l].reshape(1, D)
    p["norm2_w"] = norm2_w[l].reshape(1, D)
    p["w_gla"] = jnp.concatenate([cols["qa"] * (GLA_DK ** -0.5), cols["ka"], cols["va"], cols["ga"]],
                                 axis=1).astype(BF16)
    p["w_gdn"] = jnp.concatenate([cols["qkv"], cols["gb"]], axis=1).astype(BF16)
    p["w_s5"] = cols["u"].astype(BF16)
    assert SMALL_BETA0 == GLA_LOWRANK and SMALL_A0 == SMALL_BETA0 + GDN_HEADS
    used = SMALL_A0 + GDN_HEADS
    p["w_small"] = jnp.concatenate([cols["aa"], cols["beta"], cols["a"], jnp.zeros((D, LANES - used), F32)],
                                   axis=1).astype(BF16)
    p["w_up"] = jnp.pad(w_alpha_up[l], ((0, LANES - GLA_LOWRANK), (0, 0))).astype(BF16)
    p["b_alpha"] = b_alpha[l].reshape(1, GLA_QK)
    p["gdn_small"] = jnp.pad(jnp.stack([jnp.exp(gdn_a_log[l]), gdn_dt_bias[l]]),
                             ((0, 0), (SMALL_A0, LANES - used)))
    p["gla_norm_w"] = gla_norm_w[l].reshape(1, GLA_DV)
    p["gdn_norm_w"] = gdn_norm_w[l].reshape(1, GDN_DV)
    p["conv_w"] = conv_w[l]

    a_re, a_im = s5_a_re[l], s5_a_im[l]
    dt = jnp.exp(s5_log_dt[l])[:, None]
    mag = jnp.exp(a_re * dt)
    lr, li = mag * jnp.cos(a_im * dt), mag * jnp.sin(a_im * dt)
    den = a_re * a_re + a_im * a_im
    f_re = ((lr - 1.0) * a_re + li * a_im) / den
    f_im = (li * a_re - (lr - 1.0) * a_im) / den
    bb_re = f_re[..., None] * s5_b_re[l] - f_im[..., None] * s5_b_im[l]
    bb_im = f_re[..., None] * s5_b_im[l] + f_im[..., None] * s5_b_re[l]
    gh = S5_GROUPS // 2
    eye_h = jnp.eye(gh, dtype=F32)
    blk = lambda m: (jnp.transpose(m, (0, 2, 1))[:, :, None, :] * eye_h[:, None, :, None]
                     ).reshape(gh * S5_GROUP, gh * S5_STATE)
    cblk = lambda m: (jnp.transpose(m, (2, 0, 1))[None, :, :, :] * eye_h[:, None, :, None]
                      ).reshape(gh * S5_STATE, gh * S5_GROUP)
    p["s5_bblk"] = jnp.stack([blk(m[hb * gh:(hb + 1) * gh]) for m in (bb_re, bb_im) for hb in range(2)]
                             ).astype(BF16)
    p["s5_cblk"] = jnp.stack([cblk(m[hb * gh:(hb + 1) * gh]) for m in (s5_c_re[l], -s5_c_im[l])
                              for hb in range(2)]).astype(BF16)
    p["s5_lam"] = jnp.stack([lr.reshape(S5_HALF), li.reshape(S5_HALF)])
    p["s5_d"] = s5_d[l].reshape(1, BRANCH_W)

    p["w_gate"] = w_gate[l].astype(BF16)
    p["b_gate"] = b_gate[l].reshape(1, 3 * D)
    p["w_br_a"] = w_br_a[l].astype(BF16)
    p["w_br_b"] = w_br_b[l].astype(BF16)
    p["w_br_c"] = w_br_c[l].astype(BF16)
    p["w_glu"] = w_glu[l].astype(BF16)
    p["b_glu"] = b_glu[l].reshape(1, BRANCH_W)
    p["w_out"] = w_out[l].astype(BF16)

    assert ROUTE_E0 == MOE_GROUPS
    n_route = MOE_GROUPS + N_EXPERTS
    wr = jnp.concatenate([w_route_group[l], w_route_expert[l], jnp.zeros((D, LANES - n_route), F32)], axis=1)
    br = jnp.pad(jnp.concatenate([b_route_group[l], b_route_expert[l]]), (0, LANES - n_route)).reshape(1, LANES)
    wr_hi = wr.astype(BF16)
    p["w_route"] = jnp.stack([wr_hi, (wr - wr_hi.astype(F32)).astype(BF16)])
    p["b_route"] = br
    e_of_col = jnp.arange(N_EXPERTS * D_FF) // D_FF
    p["route_expand"] = (jnp.arange(LANES)[:, None] == (e_of_col[None, :] + ROUTE_E0)).astype(BF16)
    gw = EXPERTS_PER_GROUP * D_FF
    p["w_eg"] = w_exp_gate[l].astype(BF16)
    p["w_eu"] = w_exp_up[l].astype(BF16)
    p["w_ed"] = w_exp_down[l].reshape(MOE_GROUPS, gw, D).astype(BF16)
    return p


def _gla_state_in(s):
    st = jnp.swapaxes(s, -1, -2)
    z = jnp.zeros_like(st)
    halves = [jnp.concatenate([st[:, h], z[:, h]] if h % 2 == 0 else [z[:, h], st[:, h]], axis=-1)
              for h in range(GLA_HEADS)]
    return jnp.stack(halves, axis=1)


def _gla_state_out(st):
    parts = [st[:, h, :, GLA_DK * (h % 2):GLA_DK * (h % 2 + 1)] for h in range(GLA_HEADS)]
    return jnp.swapaxes(jnp.stack(parts, axis=1), -1, -2)


TOKEN_TILE = 512
MIXER_TILE = 256
GDN_CHUNKS = 8
GLA_CHUNKS = 4
MOE_TILE = 512


def _tiles(B, T):
    lt = min(TOKEN_TILE, T)
    nb = max(1, min(B, TOKEN_TILE // lt))
    lt_moe = min(MOE_TILE, T)
    nb_moe = max(1, min(B, MOE_TILE // lt_moe))
    lt_rec = min(MIXER_TILE, T)
    nb_gdn = max(1, min(B, GDN_CHUNKS // (lt_rec // CHUNK)))
    nb_gla = max(1, min(B, GLA_CHUNKS // (lt_rec // CHUNK)))
    return dict(nb=nb, lt=lt, nb_gdn=nb_gdn, nb_gla=nb_gla, lt_rec=lt_rec, lt_s5=min(CHUNK, T), nb_moe=nb_moe,
                lt_moe=lt_moe)


def _trunk(x, mods, params, states, final_w):
    tiles = _tiles(x.shape[0], x.shape[1])
    nb, lt, lt_rec, lt_s5 = tiles["nb"], tiles["lt"], tiles["lt_rec"], tiles["lt_s5"]
    new = []
    for l, p in enumerate(params):
        s_gla, s_gdn, cbuf, s5_re, s5_im = states[l]
        B = x.shape[0]
        cb8 = jnp.concatenate([jnp.zeros((B, SUBLANES - (CONV_W - 1), GDN_CONV_CH), F32), cbuf], axis=1)
        zg, zd, zs, cb_new = _inproj_call(x, mods[l], cb8, p, nb, lt)
        oa, gla_t = _gla_call(zg, _gla_state_in(s_gla), p["gla_norm_w"], tiles["nb_gla"], lt_rec)
        ob, gdn_new = _gdn_call(zd, s_gdn, p["gdn_norm_w"], tiles["nb_gdn"], lt_rec)
        h0 = jnp.concatenate([s5_re.reshape(B, S5_HALF), s5_im.reshape(B, S5_HALF)], axis=1)
        ys, h_new = _s5_call(zs, h0, p, lt_s5)
        x = _merge_call(x, mods[l], oa, ob, ys, p, nb, lt)
        x = _moe_call(x, mods[l], p, final_w, tiles["nb_moe"], tiles["lt_moe"],
                      final_norm=(l == len(params) - 1))
        new.append((_gla_state_out(gla_t), gdn_new, cb_new[:, SUBLANES - (CONV_W - 1):],
                    h_new[:, :S5_HALF].reshape(B, S5_GROUPS, S5_STATE),
                    h_new[:, S5_HALF:].reshape(B, S5_GROUPS, S5_STATE)))
    return x, [jnp.stack([new[l][i] for l in range(len(params))]) for i in range(5)]


def kernel(x_prompt, x_sample, c_prompt, c_sample, state_gla, state_gdn, cache_gdn_conv, state_s5_re, state_s5_im, norm1_w, norm2_w, final_norm_w, w_ada, b_ada, w_in, w_alpha_up, b_alpha, gla_norm_w, conv_w, gdn_a_log, gdn_dt_bias, gdn_norm_w, s5_a_re, s5_a_im, s5_log_dt, s5_b_re, s5_b_im, s5_c_re, s5_c_im, s5_d, w_glu, b_glu, w_br_a, w_br_b, w_br_c, w_gate, b_gate, w_out, w_route_group, b_route_group, w_route_expert, b_route_expert, w_exp_gate, w_exp_up, w_exp_down):
    depth = w_in.shape[0]
    bp, tp, _ = x_prompt.shape
    bs, ts, _ = x_sample.shape
    params = [_prep_layer(l, norm1_w, norm2_w, w_in, w_alpha_up, b_alpha, gla_norm_w, conv_w, gdn_a_log,
                          gdn_dt_bias, gdn_norm_w, s5_a_re, s5_a_im, s5_log_dt, s5_b_re, s5_b_im, s5_c_re,
                          s5_c_im, s5_d, w_glu, b_glu, w_br_a, w_br_b, w_br_c, w_gate, b_gate, w_out,
                          w_route_group, b_route_group, w_route_expert, b_route_expert, w_exp_gate, w_exp_up,
                          w_exp_down) for l in range(depth)]
    final_w = final_norm_w.reshape(1, D_MODEL)

    mod = _ada_call(jnp.concatenate([c_prompt, c_sample], axis=0), w_ada, b_ada)
    mod = mod.reshape(depth, bp + bs, 6, D_MODEL)
    mods_p = [mod[l, :bp] for l in range(depth)]
    mods_s = [mod[l, bp:] for l in range(depth)]

    zeros_p = [(jnp.zeros((bp, GLA_HEADS, GLA_DK, GLA_DV), F32), jnp.zeros((bp, GDN_HEADS, GDN_DK, GDN_DV), F32),
                jnp.zeros((bp, CONV_W - 1, GDN_CONV_CH), F32), jnp.zeros((bp, S5_GROUPS, S5_STATE), F32),
                jnp.zeros((bp, S5_GROUPS, S5_STATE), F32)) for _ in range(depth)]
    y_p, new_p = _trunk(x_prompt, mods_p, params, zeros_p, final_w)

    states_s = [(state_gla[l], state_gdn[l], cache_gdn_conv[l], state_s5_re[l], state_s5_im[l])
                for l in range(depth)]
    y_s, new_s = _trunk(x_sample, mods_s, params, states_s, final_w)
    return (y_p, y_s, *new_p, *new_s)
```

```python
import functools
import math

import jax
import jax.numpy as jnp
from jax import lax
from jax.experimental import pallas as pl
from jax.experimental.pallas import tpu as pltpu

F32 = jnp.float32
BF16 = jnp.bfloat16
HIGHEST = lax.Precision.HIGHEST

D_MODEL = 1024
EPS = 1e-6
CHUNK = 64
BRANCH_W = 512
GLA_HEADS, GLA_DK, GLA_DV, GLA_LOWRANK, GLA_TAU = 4, 64, 128, 16, 16.0
GDN_HEADS, GDN_DK, GDN_DV, CONV_W = 4, 128, 128, 4
S5_GROUP, S5_GROUPS, S5_STATE = 16, 32, 64
GLA_QK, GLA_V, GDN_QK, GDN_V = 256, 512, 512, 512
GDN_CONV_CH = 1536
MOE_GROUPS, EXPERTS_PER_GROUP, N_EXPERTS, D_FF = 4, 4, 16, 256

LANES = 128
SUBLANES = 8
VMEM_LIMIT = 56 * 1024 * 1024
S5_HALF = S5_GROUPS * S5_STATE
SMALL_BETA0, SMALL_A0 = 16, 20
ROUTE_E0 = 4


def _mm(a, b):
    return jnp.dot(a.astype(BF16), b.astype(BF16), preferred_element_type=F32)


def _mm_nt(a, b):
    return lax.dot_general(a.astype(BF16), b.astype(BF16), (((1,), (1,)), ((), ())), preferred_element_type=F32)


def _mm_tn(a, b):
    return lax.dot_general(a.astype(BF16), b.astype(BF16), (((0,), (0,)), ((), ())), preferred_element_type=F32)


def _mm_f32(a, b):
    return jnp.dot(a, b, preferred_element_type=F32, precision=HIGHEST)


def _mm_nt_f32(a, b):
    return lax.dot_general(a, b, (((1,), (1,)), ((), ())), preferred_element_type=F32, precision=HIGHEST)


def _split3(x):
    a = x.astype(BF16)
    r = x - a.astype(F32)
    b = r.astype(BF16)
    c = (r - b.astype(F32)).astype(BF16)
    return a, b, c


def _mm_sel(m01, x):
    a, b, c = _split3(x)
    return _mm(m01, a) + _mm(m01, b) + _mm(m01, c)


def _mm_sel_r(x, m01):
    a, b, c = _split3(x)
    return _mm(a, m01) + _mm(b, m01) + _mm(c, m01)


def _mm_nt_sel(m01, x):
    a, b, c = _split3(x)
    return _mm_nt(m01, a) + _mm_nt(m01, b) + _mm_nt(m01, c)


def _sigmoid(x):
    return 1.0 / (1.0 + jnp.exp(-x))


def _silu(x):
    return x * _sigmoid(x)


def _softplus(x):
    return jnp.maximum(x, 0.0) + jnp.log1p(jnp.exp(-jnp.abs(x)))


def _rms_rows(x, w):
    ms = jnp.mean(x * x, axis=-1, keepdims=True)
    return x * lax.rsqrt(ms + EPS) * w


def _gelu_tanh(x):
    c = math.sqrt(2.0 / math.pi)
    return x * (0.5 * (1.0 + jnp.tanh(c * (x + 0.044715 * (x * x * x)))))


def _const_spec(shape):
    nd = len(shape)
    return pl.BlockSpec(shape, lambda *_: (0,) * nd)


def _time_major_shape(B, T, nb):
    return (T, B * BRANCH_W)


def _time_major_spec(nb, lt):
    return pl.BlockSpec((lt, nb * BRANCH_W), lambda b, t: (t, b))


def _params(sem):
    return pltpu.CompilerParams(dimension_semantics=sem, vmem_limit_bytes=VMEM_LIMIT)


def _ada_kernel(c_ref, w_ref, b_ref, o_ref):
    c = c_ref[...]
    o_ref[0] = _mm(_silu(c), w_ref[0]) + b_ref[0]


def _ada_call(c_all, w_ada, b_ada):
    depth = w_ada.shape[0]
    nrow = c_all.shape[0]
    return pl.pallas_call(
        _ada_kernel,
        grid=(depth, 6),
        in_specs=[pl.BlockSpec((nrow, D_MODEL), lambda l, j: (0, 0)),
                  pl.BlockSpec((1, D_MODEL, D_MODEL), lambda l, j: (l, 0, j)),
                  pl.BlockSpec((1, 1, D_MODEL), lambda l, j: (l, 0, j))],
        out_specs=pl.BlockSpec((1, nrow, D_MODEL), lambda l, j: (l, 0, j)),
        out_shape=jax.ShapeDtypeStruct((depth, nrow, 6 * D_MODEL), F32),
        compiler_params=_params(("parallel", "parallel")),
        name="adaln",
    )(c_all, w_ada, b_ada.reshape(depth, 1, 6 * D_MODEL))


MXU_COLS = 256


def _inproj_kernel(x_ref, mod_ref, nw_ref, wg_ref, wd_ref, ws_ref, wsm_ref, wup_ref, bal_ref, sp_ref, cb_ref,
                   cw_ref, zg_ref, zd_ref, zs_ref, cbo_ref, *xbufs):
    nb, lt, d = x_ref.shape
    span = lt + SUBLANES

    @pl.when(pl.program_id(1) == 0)
    def _():
        for i in range(nb):
            for k, xb in enumerate(xbufs):
                xb[span * i + lt:span * (i + 1), :] = cb_ref[i, :, MXU_COLS * k:MXU_COLS * (k + 1)]

    x = x_ref[...]
    h = _rms_rows(x, nw_ref[...]) * (1.0 + mod_ref[:, 1:2, :]) + mod_ref[:, 0:1, :]
    h2 = h.reshape(nb * lt, d).astype(BF16)

    base = SUBLANES - (CONV_W - 1)
    for i in range(nb):
        for xb in xbufs:
            xb[span * i:span * i + SUBLANES, :] = xb[span * i + lt:span * (i + 1), :]
    def gla_block(j):
        dst = MXU_COLS * j if j < 2 else MXU_COLS * (j + 1)
        zg_ref[:, :, dst:dst + MXU_COLS] = _mm(h2, wg_ref[:, MXU_COLS * j:MXU_COLS * (j + 1)]).reshape(
            nb, lt, MXU_COLS)

    def gate_block(j):
        cs = slice(GDN_CONV_CH + MXU_COLS * j, GDN_CONV_CH + MXU_COLS * (j + 1))
        zd_ref[:, :, cs] = _mm(h2, wd_ref[:, cs]).reshape(nb, lt, MXU_COLS)

    def s5_block(j):
        zs = _mm(h2, ws_ref[:, MXU_COLS * j:MXU_COLS * (j + 1)])
        for i in range(nb):
            zs_ref[:, BRANCH_W * i + MXU_COLS * j:BRANCH_W * i + MXU_COLS * (j + 1)] = zs[lt * i:lt * (i + 1), :]

    def small_block(_):
        zsm = _mm(h2, wsm_ref[...])
        la_pre = _mm(zsm, wup_ref[...]) + bal_ref[...]
        log_a = (jnp.minimum(la_pre, 0.0) - jnp.log1p(jnp.exp(-jnp.abs(la_pre)))) * (1.0 / GLA_TAU)
        zg_ref[:, :, 512:768] = log_a.reshape(nb, lt, 256)
        lane = lax.broadcasted_iota(jnp.int32, zsm.shape, 1)
        beta = _sigmoid(zsm)
        log_g = -sp_ref[0:1, :] * _softplus(zsm + sp_ref[1:2, :])
        is_beta = (lane >= SMALL_BETA0) & (lane < SMALL_A0)
        is_a = (lane >= SMALL_A0) & (lane < SMALL_A0 + GDN_HEADS)
        small = jnp.where(is_beta, beta, jnp.where(is_a, log_g, 0.0))
        zd_ref[:, :, 2048:2176] = small.reshape(nb, lt, LANES)

    fillers = [[(gla_block, 0), (gate_block, 0)], [(gla_block, 1), (gate_block, 1)],
               [(gla_block, 2), (s5_block, 0)], [(gla_block, 3), (s5_block, 1)],
               [(gla_block, 4), (small_block, 0)], [(gla_block, 5)]]
    def conv_matmul(mblk):
        zd = _mm(h2, wd_ref[:, MXU_COLS * mblk:MXU_COLS * (mblk + 1)])
        for i in range(nb):
            xbufs[mblk][span * i + SUBLANES:span * (i + 1), :] = zd[lt * i:lt * (i + 1), :]
        for fn, arg in fillers[mblk]:
            fn(arg)

    n_blk = GDN_CONV_CH // MXU_COLS
    conv_matmul(0)
    for mblk in range(n_blk):
        if mblk + 1 < n_blk:
            conv_matmul(mblk + 1)
        xb = xbufs[mblk]
        for i in range(nb):
            r0 = span * i
            for cblk in range(MXU_COLS * mblk // LANES, MXU_COLS * (mblk + 1) // LANES):
                cs = slice(LANES * cblk, LANES * (cblk + 1))
                ls = slice(LANES * cblk - MXU_COLS * mblk, LANES * (cblk + 1) - MXU_COLS * mblk)
                conv = xb[r0 + base:r0 + base + lt, ls] * cw_ref[0:1, cs]
                for j in range(1, CONV_W):
                    conv = conv + xb[r0 + base + j:r0 + base + j + lt, ls] * cw_ref[j:j + 1, cs]
                act = _silu(conv)
                if cblk < 2 * GDN_HEADS:
                    scale = GDN_DK ** -0.5 if cblk < GDN_HEADS else 1.0
                    ss = jnp.sum(act * act, axis=-1, keepdims=True)
                    act = act * (lax.rsqrt(ss + EPS) * scale)
                zd_ref[i, :, cs] = act
    for i in range(nb):
        for k, xb in enumerate(xbufs):
            cbo_ref[i, :, MXU_COLS * k:MXU_COLS * (k + 1)] = xb[span * i + lt:span * (i + 1), :]


def _inproj_call(x, mod, cb8, p, nb, lt):
    B, T, D = x.shape
    assert nb == 1 or T == lt, "several sequences per step only when each is a single time block"
    grid = (B // nb, T // lt)
    tok = lambda w: pl.BlockSpec((nb, lt, w), lambda b, t: (b, t, 0))
    cb_spec = pl.BlockSpec((nb, SUBLANES, GDN_CONV_CH), lambda b, t: (b, 0, 0))
    return pl.pallas_call(
        _inproj_kernel,
        grid=grid,
        in_specs=[tok(D),
                  pl.BlockSpec((nb, 6, D), lambda b, t: (b, 0, 0)),
                  _const_spec((1, D)),
                  _const_spec(p["w_gla"].shape), _const_spec(p["w_gdn"].shape), _const_spec(p["w_s5"].shape),
                  _const_spec(p["w_small"].shape), _const_spec(p["w_up"].shape), _const_spec((1, GLA_QK)),
                  _const_spec((2, LANES)), cb_spec, _const_spec((CONV_W, GDN_CONV_CH))],
        out_specs=[tok(1792), tok(2176), _time_major_spec(nb, lt), cb_spec],
        out_shape=[jax.ShapeDtypeStruct((B, T, 1792), F32),
                   jax.ShapeDtypeStruct((B, T, 2176), F32),
                   jax.ShapeDtypeStruct(_time_major_shape(B, T, nb), F32),
                   jax.ShapeDtypeStruct((B, SUBLANES, GDN_CONV_CH), F32)],
        scratch_shapes=[pltpu.VMEM((nb * (lt + SUBLANES), MXU_COLS), F32)] * (GDN_CONV_CH // MXU_COLS),
        compiler_params=_params(("parallel", "arbitrary")),
        name="inproj",
    )(x, mod, p["norm1_w"], p["w_gla"], p["w_gdn"], p["w_s5"], p["w_small"], p["w_up"], p["b_alpha"],
      p["gdn_small"], cb8, p["conv_w"])


def _gla_kernel(z_ref, s0_ref, nw_ref, o_ref, s_ref):
    L = CHUNK
    nb, lt, _ = z_ref.shape
    n_chunks = lt // L
    n_rows = nb * lt

    @pl.when(pl.program_id(1) == 0)
    def _():
        s_ref[...] = s0_ref[...]

    row = lax.broadcasted_iota(jnp.int32, (L, L), 0)
    col = lax.broadcasted_iota(jnp.int32, (L, L), 1)
    causal = row >= col
    lane = lax.broadcasted_iota(jnp.int32, (L, LANES), 1)
    own_half = (lane < GLA_DK, lane >= GLA_DK)
    nw = nw_ref[...]
    brow = lax.broadcasted_iota(jnp.int32, (n_rows, n_rows), 0)
    bcol = lax.broadcasted_iota(jnp.int32, (n_rows, n_rows), 1)
    shift = L.bit_length() - 1
    chunk_tril = ((brow >= bcol) & ((brow >> shift) == (bcol >> shift))).astype(BF16)
    b_all = _mm_sel(chunk_tril, z_ref[:, :, 512:768].reshape(n_rows, GLA_QK))

    def zcols(c, lo, width):
        i, cc = divmod(c, n_chunks)
        return z_ref[i, L * cc:L * (cc + 1), lo:lo + width]

    all_chunks = range(nb * n_chunks)
    probs = [(c, h) for c in all_chunks for h in range(GLA_HEADS)]
    rows_of = lambda c: slice(L * c, L * (c + 1))
    pair_of = lambda h: slice(LANES * (h // 2), LANES * (h // 2) + LANES)
    qi_, ki_, qs_, kl_, dl_ = {}, {}, {}, {}, {}
    for c in all_chunks:
        q = zcols(c, 0, GLA_QK)
        k = zcols(c, GLA_QK, GLA_QK)
        b = b_all[rows_of(c), :]
        b_mid = b[L // 2 - 1:L // 2, :]
        b_last = b[L - 1:L, :]
        qi_[c] = q * jnp.exp(b - b_mid)
        ki_[c] = k * jnp.exp(b_mid - b)
        qs_[c] = q * jnp.exp(b)
        kl_[c] = k * jnp.exp(b_last - b)
        dl_[c] = jnp.exp(b_last)
    v_ = {p: zcols(p[0], 768 + GLA_DV * p[1], GLA_DV) for p in probs}
    att_ = {p: jnp.where(causal, _mm_nt(qi_[p[0]][:, pair_of(p[1])],
                                        jnp.where(own_half[p[1] % 2], ki_[p[0]][:, pair_of(p[1])], 0.0)), 0.0)
            for p in probs}
    kv_ = {p: _mm_tn(v_[p], jnp.where(own_half[p[1] % 2], kl_[p[0]][:, pair_of(p[1])], 0.0)) for p in probs}
    av_ = {p: _mm(att_[p], v_[p]) for p in probs}
    st_ = {}
    for i in range(nb):
        for h in range(GLA_HEADS):
            st = s_ref[i, h]
            for cc in range(n_chunks):
                c = n_chunks * i + cc
                st_[(c, h)] = st
                st = st * dl_[c][:, pair_of(h)] + kv_[(c, h)]
            s_ref[i, h] = st
    for p in probs:
        c, h = p
        i, cc = divmod(c, n_chunks)
        o_h = av_[p] + _mm_nt(qs_[c][:, pair_of(h)], st_[p])
        gate = zcols(c, 1280 + GLA_DV * h, GLA_DV)
        o_ref[i, L * cc:L * (cc + 1), GLA_DV * h:GLA_DV * (h + 1)] = _rms_rows(o_h, nw) * _silu(gate)


def _gla_call(zg, s0t, norm_w, nb, lt):
    B, T, _ = zg.shape
    st_spec = pl.BlockSpec((nb, GLA_HEADS, GLA_DV, LANES), lambda b, t: (b, 0, 0, 0))
    return pl.pallas_call(
        _gla_kernel,
        grid=(B // nb, T // lt),
        in_specs=[pl.BlockSpec((nb, lt, 1792), lambda b, t: (b, t, 0)), st_spec, _const_spec((1, GLA_DV))],
        out_specs=[pl.BlockSpec((nb, lt, GLA_V), lambda b, t: (b, t, 0)), st_spec],
        out_shape=[jax.ShapeDtypeStruct((B, T, GLA_V), F32),
                   jax.ShapeDtypeStruct((B, GLA_HEADS, GLA_DV, LANES), F32)],
        compiler_params=_params(("parallel", "arbitrary")),
        name="gla",
    )(zg, s0t, norm_w)


def _gdn_kernel(z_ref, s0_ref, nw_ref, o_ref, s_ref, bfull_ref, gfull_ref, grow_ref, glast_ref):
    L = CHUNK
    nb, lt, _ = z_ref.shape
    n_chunks = lt // L
    n_rows = nb * lt

    @pl.when(pl.program_id(1) == 0)
    def _():
        s_ref[...] = s0_ref[...]

    row = lax.broadcasted_iota(jnp.int32, (L, L), 0)
    col = lax.broadcasted_iota(jnp.int32, (L, L), 1)
    causal = row >= col
    strict = row > col
    eye = (row == col).astype(F32)
    diag16 = (row >> 4) == (col >> 4)
    diag32 = (row >> 5) == (col >> 5)
    off32 = diag32 & jnp.logical_not(diag16)
    off64 = jnp.logical_not(diag32)
    sel_r = lax.broadcasted_iota(jnp.int32, (SUBLANES, LANES), 0)
    sel_c = lax.broadcasted_iota(jnp.int32, (SUBLANES, LANES), 1)
    sel = (sel_c == sel_r + SMALL_A0).astype(BF16)
    brow = lax.broadcasted_iota(jnp.int32, (n_rows, n_rows), 0)
    bcol = lax.broadcasted_iota(jnp.int32, (n_rows, n_rows), 1)
    shift = L.bit_length() - 1
    chunk_tril = ((brow >= bcol) & ((brow >> shift) == (bcol >> shift))).astype(BF16)
    small_all = z_ref[:, :, 2048:2176].reshape(n_rows, LANES)
    gs_all = _mm_sel(chunk_tril, small_all)
    grow_all = _mm_nt_sel(sel, gs_all)
    e_r = lax.broadcasted_iota(jnp.int32, (LANES, GDN_HEADS * LANES), 0)
    e_h = lax.broadcasted_iota(jnp.int32, (LANES, GDN_HEADS * LANES), 1) >> (LANES.bit_length() - 1)
    bfull_ref[...] = _mm_sel_r(small_all, (e_r == e_h + SMALL_BETA0).astype(BF16))
    gfull = _mm_sel_r(gs_all, (e_r == e_h + SMALL_A0).astype(BF16))
    gfull_ref[...] = gfull
    for c in range(nb * n_chunks):
        grow_ref[c] = grow_all[:, L * c:L * (c + 1)]
        glast_ref[c:c + 1, :] = gfull[L * (c + 1) - 1:L * (c + 1), :]
    nw = nw_ref[...]

    probs = [(c, h) for c in range(nb * n_chunks) for h in range(GDN_HEADS)]
    rows_of = lambda c: slice(L * c, L * (c + 1))
    head_of = lambda h: slice(LANES * h, LANES * (h + 1))

    def zcols(c, lo, width):
        i, cc = divmod(c, n_chunks)
        return z_ref[i, L * cc:L * (cc + 1), lo:lo + width]

    q_ = {p: zcols(p[0], GDN_DK * p[1], GDN_DK) for p in probs}
    k_ = {p: zcols(p[0], GDN_QK + GDN_DK * p[1], GDN_DK) for p in probs}
    beta_ = {p: bfull_ref[rows_of(p[0]), head_of(p[1])] for p in probs}
    g_ = {p: gfull_ref[rows_of(p[0]), head_of(p[1])] for p in probs}
    dec_ = {p: jnp.exp(jnp.where(causal, g_[p][:, 0:L] - grow_ref[p[0]][p[1]:p[1] + 1, :], -jnp.inf))
            for p in probs}
    kk_ = {p: _mm_nt(k_[p], k_[p]) for p in probs}
    qk_ = {p: _mm_nt(q_[p], k_[p]) for p in probs}
    a_ = {p: jnp.where(strict, beta_[p][:, 0:L] * kk_[p] * dec_[p], 0.0) for p in probs}
    qk_ = {p: jnp.where(causal, qk_[p] * dec_[p], 0.0) for p in probs}
    pw_ = {p: jnp.where(diag16, -a_[p], 0.0) for p in probs}
    tinv_ = {p: eye + pw_[p] for p in probs}
    for _ in range(3):
        pw_ = {p: _mm(pw_[p], pw_[p]) for p in probs}
        tinv_ = {p: tinv_[p] + _mm(tinv_[p], pw_[p]) for p in probs}
    for off in (off32, off64):
        te_ = {p: _mm(tinv_[p], jnp.where(off, a_[p], 0.0)) for p in probs}
        tinv_ = {p: tinv_[p] - _mm(te_[p], tinv_[p]) for p in probs}
    eg_ = {p: jnp.exp(g_[p]) for p in probs}
    sol_ = {}
    for p in probs:
        v_p = zcols(p[0], 2 * GDN_QK + GDN_DV * p[1], GDN_DV)
        rhs = jnp.concatenate([beta_[p] * v_p, (beta_[p] * eg_[p]) * k_[p]], axis=-1)
        sol_[p] = _mm(tinv_[p], rhs)

    st_ = {(i, h): s_ref[i, h] for i in range(nb) for h in range(GDN_HEADS)}
    for cc in range(n_chunks):
        heads = [(i, n_chunks * i + cc, h) for i in range(nb) for h in range(GDN_HEADS)]
        u_, o_, new_ = {}, {}, {}
        for i, c, h in heads:
            u_[(c, h)] = sol_[(c, h)][:, 0:GDN_DV] - _mm(sol_[(c, h)][:, GDN_DV:], st_[(i, h)])
        for i, c, h in heads:
            o_[(c, h)] = _mm(qk_[(c, h)], u_[(c, h)]) + _mm(q_[(c, h)] * eg_[(c, h)], st_[(i, h)])
        for i, c, h in heads:
            g_last = glast_ref[c:c + 1, head_of(h)]
            new_[(i, h)] = (st_[(i, h)] * jnp.exp(g_last)
                            + _mm_tn(k_[(c, h)] * jnp.exp(g_last - g_[(c, h)]), u_[(c, h)]))
        st_ = new_
        for i, c, h in heads:
            gate = z_ref[i, L * cc:L * (cc + 1), GDN_CONV_CH + GDN_DV * h:GDN_CONV_CH + GDN_DV * (h + 1)]
            o_ref[i, L * cc:L * (cc + 1), GDN_DV * h:GDN_DV * (h + 1)] = _rms_rows(o_[(c, h)], nw) * _silu(gate)
    for i in range(nb):
        for h in range(GDN_HEADS):
            s_ref[i, h] = st_[(i, h)]


def _gdn_call(zd, s0, norm_w, nb, lt):
    B, T, _ = zd.shape
    n_rows = nb * lt
    last_rows = -(-(n_rows // CHUNK) // SUBLANES) * SUBLANES
    st_spec = pl.BlockSpec((nb, GDN_HEADS, GDN_DK, GDN_DV), lambda b, t: (b, 0, 0, 0))
    return pl.pallas_call(
        _gdn_kernel,
        grid=(B // nb, T // lt),
        in_specs=[pl.BlockSpec((nb, lt, 2176), lambda b, t: (b, t, 0)), st_spec, _const_spec((1, GDN_DV))],
        out_specs=[pl.BlockSpec((nb, lt, GDN_V), lambda b, t: (b, t, 0)), st_spec],
        out_shape=[jax.ShapeDtypeStruct((B, T, GDN_V), F32),
                   jax.ShapeDtypeStruct((B, GDN_HEADS, GDN_DK, GDN_DV), F32)],
        scratch_shapes=[pltpu.VMEM((n_rows, GDN_HEADS * LANES), F32), pltpu.VMEM((n_rows, GDN_HEADS * LANES), F32),
                        pltpu.VMEM((n_rows // CHUNK, SUBLANES, CHUNK), F32),
                        pltpu.VMEM((last_rows, GDN_HEADS * LANES), F32)],
        compiler_params=_params(("parallel", "arbitrary")),
        name="gdn",
    )(zd, s0, norm_w)


S5_LANE_CHUNK = 512


def _s5_kernel(u_ref, h0_ref, bb_ref, cc_ref, lam_ref, d_ref, y_ref, h_ref, bu_ref, *, lt):
    nb = SUBLANES

    @pl.when(pl.program_id(1) == 0)
    def _():
        h_ref[...] = h0_ref[...]

    u = u_ref[...].reshape(lt * nb, BRANCH_W)
    ch, sh = BRANCH_W // 2, S5_HALF // 2
    for part in range(2):
        for hb in range(2):
            bu_ref[:, part * S5_HALF + hb * sh:part * S5_HALF + (hb + 1) * sh] = _mm(
                u[:, hb * ch:(hb + 1) * ch], bb_ref[2 * part + hb])
    for j in range(S5_HALF // S5_LANE_CHUNK):
        re = slice(S5_LANE_CHUNK * j, S5_LANE_CHUNK * (j + 1))
        im = slice(S5_HALF + S5_LANE_CHUNK * j, S5_HALF + S5_LANE_CHUNK * (j + 1))
        lr = jnp.broadcast_to(lam_ref[0:1, re], (nb, S5_LANE_CHUNK))
        li = jnp.broadcast_to(lam_ref[1:2, re], (nb, S5_LANE_CHUNK))

        hr, hi = h_ref[:, re], h_ref[:, im]
        for tt in range(lt):
            rows = slice(tt * nb, (tt + 1) * nb)
            hr, hi = lr * hr - li * hi + bu_ref[rows, re], lr * hi + li * hr + bu_ref[rows, im]
            bu_ref[rows, re] = hr
            bu_ref[rows, im] = hi
        h_ref[:, re] = hr
        h_ref[:, im] = hi
    ys = [_mm(bu_ref[:, hb * sh:(hb + 1) * sh], cc_ref[hb])
          + _mm(bu_ref[:, S5_HALF + hb * sh:S5_HALF + (hb + 1) * sh], cc_ref[2 + hb]) for hb in range(2)]
    y = jnp.concatenate(ys, axis=-1) + d_ref[...] * u
    y_ref[...] = y.reshape(lt, nb, BRANCH_W)


def _s5_call(zs, h0, p, lt):
    T = zs.shape[0]
    B = h0.shape[0]
    nb = SUBLANES
    h_spec = pl.BlockSpec((nb, 2 * S5_HALF), lambda b, t: (b, 0))
    tok_spec = pl.BlockSpec((lt, None, nb, BRANCH_W), lambda b, t: (t, b, 0, 0))
    y, h = pl.pallas_call(
        functools.partial(_s5_kernel, lt=lt),
        grid=(B // nb, T // lt),
        in_specs=[tok_spec, h_spec,
                  _const_spec((4, BRANCH_W // 2, S5_HALF // 2)), _const_spec((4, S5_HALF // 2, BRANCH_W // 2)),
                  _const_spec((2, S5_HALF)), _const_spec((1, BRANCH_W))],
        out_specs=[tok_spec, h_spec],
        out_shape=[jax.ShapeDtypeStruct((T, B // nb, nb, BRANCH_W), F32),
                   jax.ShapeDtypeStruct((B, 2 * S5_HALF), F32)],
        scratch_shapes=[pltpu.VMEM((lt * nb, 2 * S5_HALF), F32)],
        compiler_params=_params(("parallel", "arbitrary")),
        name="s5",
    )(zs.reshape(T, B // nb, nb, BRANCH_W), h0, p["s5_bblk"], p["s5_cblk"], p["s5_lam"], p["s5_d"])
    return y.reshape(zs.shape), h


def _merge_kernel(x_ref, mod_ref, nw_ref, oa_ref, ob_ref, ys_ref, wgate_ref, bgate_ref, wa_ref, wb_ref, wc_ref,
                  wglu_ref, bglu_ref, wout_ref, o_ref):
    nb, lt, d = x_ref.shape
    n = nb * lt
    x = x_ref[...]
    h = _rms_rows(x, nw_ref[...]) * (1.0 + mod_ref[:, 1:2, :]) + mod_ref[:, 0:1, :]
    gate = _sigmoid(_mm(h.reshape(n, d), wgate_ref[...]) + bgate_ref[...])
    y_a = _mm(oa_ref[...].reshape(n, BRANCH_W), wa_ref[...])
    y_b = _mm(ob_ref[...].reshape(n, BRANCH_W), wb_ref[...])
    ys = [ys_ref[:, BRANCH_W * i:BRANCH_W * (i + 1)] for i in range(nb)]
    yc = _gelu_tanh(ys[0] if nb == 1 else jnp.concatenate(ys, axis=0))
    yc = yc * _sigmoid(_mm(yc, wglu_ref[...]) + bglu_ref[...])
    y_c = _mm(yc, wc_ref[...])
    mix = gate[:, 0:d] * y_a + gate[:, d:2 * d] * y_b + gate[:, 2 * d:3 * d] * y_c
    m = _mm(mix, wout_ref[...])
    o_ref[...] = x + mod_ref[:, 2:3, :] * m.reshape(nb, lt, d)


def _merge_call(x, mod, oa, ob, ys, p, nb, lt):
    B, T, D = x.shape
    tok = lambda w: pl.BlockSpec((nb, lt, w), lambda b, t: (b, t, 0))
    return pl.pallas_call(
        _merge_kernel,
        grid=(B // nb, T // lt),
        in_specs=[tok(D), pl.BlockSpec((nb, 6, D), lambda b, t: (b, 0, 0)), _const_spec((1, D)),
                  tok(BRANCH_W), tok(BRANCH_W), _time_major_spec(nb, lt),
                  _const_spec((D, 3 * D)), _const_spec((1, 3 * D)),
                  _const_spec((BRANCH_W, D)), _const_spec((BRANCH_W, D)), _const_spec((BRANCH_W, D)),
                  _const_spec((BRANCH_W, BRANCH_W)), _const_spec((1, BRANCH_W)), _const_spec((D, D))],
        out_specs=tok(D),
        out_shape=jax.ShapeDtypeStruct((B, T, D), F32),
        compiler_params=_params(("parallel", "parallel")),
        name="merge",
    )(x, mod, p["norm1_w"], oa, ob, ys, p["w_gate"], p["b_gate"], p["w_br_a"], p["w_br_b"], p["w_br_c"],
      p["w_glu"], p["b_glu"], p["w_out"])


MOE_ROWS = 160


def _moe_kernel(x_ref, mod_ref, nw_ref, wr_ref, br_ref, ex_ref, wg_ref, wu_ref, wd_ref, fw_ref, o_ref,
                hn_ref, comb_ref, krow_ref, acc_ref, *, final_norm):
    nb, lt, d = x_ref.shape
    n = nb * lt
    R = MOE_ROWS
    gw = EXPERTS_PER_GROUP * D_FF

    x = x_ref[...]
    hn = (_rms_rows(x, nw_ref[...]) * (1.0 + mod_ref[:, 4:5, :]) + mod_ref[:, 3:4, :]).reshape(n, d)
    hn_hi = hn.astype(BF16)
    hn_ref[...] = hn_hi
    hn_lo = (hn - hn_hi.astype(F32)).astype(BF16)
    logits = _mm(hn_hi, wr_ref[0]) + (_mm(hn_hi, wr_ref[1]) + _mm(hn_lo, wr_ref[0])) + br_ref[...]
    lane = lax.broadcasted_iota(jnp.int32, (n, LANES), 1)
    neg = -jnp.inf
    gl = jnp.where(lane < MOE_GROUPS, logits, neg)
    gmax = jnp.max(gl, axis=-1, keepdims=True)
    g_sel = jnp.min(jnp.where(gl == gmax, lane, LANES), axis=-1, keepdims=True)
    p_sel = 1.0 / jnp.sum(jnp.exp(gl - gmax), axis=-1, keepdims=True)
    e_id = lane - ROUTE_E0
    in_group = (e_id >= 0) & (e_id < N_EXPERTS) & ((e_id >> 2) == g_sel)
    el = jnp.where(in_group, logits, neg)
    m1 = jnp.max(el, axis=-1, keepdims=True)
    i1 = jnp.min(jnp.where(el == m1, lane, LANES), axis=-1, keepdims=True)
    el2 = jnp.where(lane == i1, neg, el)
    m2 = jnp.max(el2, axis=-1, keepdims=True)
    i2 = jnp.min(jnp.where(el2 == m2, lane, LANES), axis=-1, keepdims=True)
    e2 = jnp.exp(m2 - m1)
    w1 = p_sel / (1.0 + e2)
    w2 = p_sel * e2 / (1.0 + e2)
    comb_ref[...] = jnp.where(lane == i1, w1, jnp.where(lane == i2, w2, 0.0)).astype(BF16)

    member = lane == g_sel
    trow = lax.broadcasted_iota(jnp.int32, (n, n), 0)
    tcol = lax.broadcasted_iota(jnp.int32, (n, n), 1)
    rank = _mm((trow > tcol).astype(BF16), member.astype(BF16))
    key = jnp.where(member, rank, -1.0)
    sel_r = lax.broadcasted_iota(jnp.int32, (SUBLANES, LANES), 0)
    sel_c = lax.broadcasted_iota(jnp.int32, (SUBLANES, LANES), 1)
    krow_ref[...] = _mm_nt_sel((sel_r == sel_c).astype(BF16), key)
    acc_ref[...] = jnp.zeros_like(acc_ref)

    row_pos = lax.broadcasted_iota(jnp.int32, (R, n), 0).astype(F32)
    for g in range(MOE_GROUPS):
        last = jnp.max(key[:, g:g + 1])
        n_pass = (last.astype(jnp.int32) + R) // R

        def one_pass(s, carry, g=g):
            base = (s * R).astype(F32)
            gather = (krow_ref[g:g + 1, :] - base == row_pos).astype(BF16)
            xg = _mm(gather, hn_ref[...])
            cg = _mm(_mm(gather, comb_ref[...]), ex_ref[:, gw * g:gw * (g + 1)])
            experts = range(EXPERTS_PER_GROUP * g, EXPERTS_PER_GROUP * (g + 1))
            he = jnp.concatenate([_silu(_mm(xg, wg_ref[e])) * _mm(xg, wu_ref[e]) for e in experts], axis=-1)
            yg = _mm(he * cg, wd_ref[g])
            acc_ref[...] += _mm_tn(gather, yg)
            return carry

        lax.fori_loop(0, n_pass, one_pass, 0)

    out = x + mod_ref[:, 5:6, :] * acc_ref[...].reshape(nb, lt, d)
    if final_norm:
        out = _rms_rows(out, fw_ref[...])
    o_ref[...] = out


def _moe_call(x, mod, p, final_w, nb, lt, final_norm):
    B, T, D = x.shape
    n = nb * lt
    gw = EXPERTS_PER_GROUP * D_FF
    tok = pl.BlockSpec((nb, lt, D), lambda b, t: (b, t, 0))
    resident = lambda shape: pl.BlockSpec(shape, lambda *_: (0,) * len(shape), pipeline_mode=pl.Buffered(1))
    return pl.pallas_call(
        functools.partial(_moe_kernel, final_norm=final_norm),
        grid=(B // nb, T // lt),
        in_specs=[tok, pl.BlockSpec((nb, 6, D), lambda b, t: (b, 0, 0)), _const_spec((1, D)),
                  _const_spec((2, D, LANES)), _const_spec((1, LANES)), _const_spec((LANES, N_EXPERTS * D_FF)),
                  resident((N_EXPERTS, D, D_FF)), resident((N_EXPERTS, D, D_FF)), resident((MOE_GROUPS, gw, D)),
                  _const_spec((1, D))],
        out_specs=tok,
        out_shape=jax.ShapeDtypeStruct((B, T, D), F32),
        scratch_shapes=[pltpu.VMEM((n, D), BF16), pltpu.VMEM((n, LANES), BF16), pltpu.VMEM((SUBLANES, n), F32),
                        pltpu.VMEM((n, D), F32)],
        compiler_params=_params(("parallel", "parallel")),
        name="moe",
    )(x, mod, p["norm2_w"], p["w_route"], p["b_route"], p["route_expand"], p["w_eg"], p["w_eu"], p["w_ed"],
      final_w)


def _prep_layer(l, norm1_w, norm2_w, w_in, w_alpha_up, b_alpha, gla_norm_w, conv_w, gdn_a_log, gdn_dt_bias,
                gdn_norm_w, s5_a_re, s5_a_im, s5_log_dt, s5_b_re, s5_b_im, s5_c_re, s5_c_im, s5_d, w_glu, b_glu,
                w_br_a, w_br_b, w_br_c, w_gate, b_gate, w_out, w_route_group, b_route_group, w_route_expert,
                b_route_expert, w_exp_gate, w_exp_up, w_exp_down):
    D = D_MODEL
    w = w_in[l]
    o = 0
    cols = {}
    for name, size in (("qa", GLA_QK), ("ka", GLA_QK), ("va", GLA_V), ("ga", GLA_V), ("aa", GLA_LOWRANK),
                       ("qkv", GDN_CONV_CH), ("beta", GDN_HEADS), ("a", GDN_HEADS), ("gb", GDN_V),
                       ("u", BRANCH_W)):
        cols[name] = w[:, o:o + size]
        o += size
    p = {}
    p["norm1_w"] = norm1_w[l].reshape(1, D)
    p["norm2_w"] = norm2_w[l].reshape(1, D)
    p["w_gla"] = jnp.concatenate([cols["qa"] * (GLA_DK ** -0.5), cols["ka"], cols["va"], cols["ga"]],
                                 axis=1).astype(BF16)
    p["w_gdn"] = jnp.concatenate([cols["qkv"], cols["gb"]], axis=1).astype(BF16)
    p["w_s5"] = cols["u"].astype(BF16)
    assert SMALL_BETA0 == GLA_LOWRANK and SMALL_A0 == SMALL_BETA0 + GDN_HEADS
    used = SMALL_A0 + GDN_HEADS
    p["w_small"] = jnp.concatenate([cols["aa"], cols["beta"], cols["a"], jnp.zeros((D, LANES - used), F32)],
                                   axis=1).astype(BF16)
    p["w_up"] = jnp.pad(w_alpha_up[l], ((0, LANES - GLA_LOWRANK), (0, 0))).astype(BF16)
    p["b_alpha"] = b_alpha[l].reshape(1, GLA_QK)
    p["gdn_small"] = jnp.pad(jnp.stack([jnp.exp(gdn_a_log[l]), gdn_dt_bias[l]]),
                             ((0, 0), (SMALL_A0, LANES - used)))
    p["gla_norm_w"] = gla_norm_w[l].reshape(1, GLA_DV)
    p["gdn_norm_w"] = gdn_norm_w[l].reshape(1, GDN_DV)
    p["conv_w"] = conv_w[l]

    a_re, a_im = s5_a_re[l], s5_a_im[l]
    dt = jnp.exp(s5_log_dt[l])[:, None]
    mag = jnp.exp(a_re * dt)
    lr, li = mag * jnp.cos(a_im * dt), mag * jnp.sin(a_im * dt)
    den = a_re * a_re + a_im * a_im
    f_re = ((lr - 1.0) * a_re + li * a_im) / den
    f_im = (li * a_re - (lr - 1.0) * a_im) / den
    bb_re = f_re[..., None] * s5_b_re[l] - f_im[..., None] * s5_b_im[l]
    bb_im = f_re[..., None] * s5_b_im[l] + f_im[..., None] * s5_b_re[l]
    gh = S5_GROUPS // 2
    eye_h = jnp.eye(gh, dtype=F32)
    blk = lambda m: (jnp.transpose(m, (0, 2, 1))[:, :, None, :] * eye_h[:, None, :, None]
                     ).reshape(gh * S5_GROUP, gh * S5_STATE)
    cblk = lambda m: (jnp.transpose(m, (2, 0, 1))[None, :, :, :] * eye_h[:, None, :, None]
                      ).reshape(gh * S5_STATE, gh * S5_GROUP)
    p["s5_bblk"] = jnp.stack([blk(m[hb * gh:(hb + 1) * gh]) for m in (bb_re, bb_im) for hb in range(2)]
                             ).astype(BF16)
    p["s5_cblk"] = jnp.stack([cblk(m[hb * gh:(hb + 1) * gh]) for m in (s5_c_re[l], -s5_c_im[l])
                              for hb in range(2)]).astype(BF16)
    p["s5_lam"] = jnp.stack([lr.reshape(S5_HALF), li.reshape(S5_HALF)])
    p["s5_d"] = s5_d[l].reshape(1, BRANCH_W)

    p["w_gate"] = w_gate[l].astype(BF16)
    p["b_gate"] = b_gate[l].reshape(1, 3 * D)
    p["w_br_a"] = w_br_a[l].astype(BF16)
    p["w_br_b"] = w_br_b[l].astype(BF16)
    p["w_br_c"] = w_br_c[l].astype(BF16)
    p["w_glu"] = w_glu[l].astype(BF16)
    p["b_glu"] = b_glu[l].reshape(1, BRANCH_W)
    p["w_out"] = w_out[l].astype(BF16)

    assert ROUTE_E0 == MOE_GROUPS
    n_route = MOE_GROUPS + N_EXPERTS
    wr = jnp.concatenate([w_route_group[l], w_route_expert[l], jnp.zeros((D, LANES - n_route), F32)], axis=1)
    br = jnp.pad(jnp.concatenate([b_route_group[l], b_route_expert[l]]), (0, LANES - n_route)).reshape(1, LANES)
    wr_hi = wr.astype(BF16)
    p["w_route"] = jnp.stack([wr_hi, (wr - wr_hi.astype(F32)).astype(BF16)])
    p["b_route"] = br
    e_of_col = jnp.arange(N_EXPERTS * D_FF) // D_FF
    p["route_expand"] = (jnp.arange(LANES)[:, None] == (e_of_col[None, :] + ROUTE_E0)).astype(BF16)
    gw = EXPERTS_PER_GROUP * D_FF
    p["w_eg"] = w_exp_gate[l].astype(BF16)
    p["w_eu"] = w_exp_up[l].astype(BF16)
    p["w_ed"] = w_exp_down[l].reshape(MOE_GROUPS, gw, D).astype(BF16)
    return p


def _gla_state_in(s):
    st = jnp.swapaxes(s, -1, -2)
    z = jnp.zeros_like(st)
    halves = [jnp.concatenate([st[:, h], z[:, h]] if h % 2 == 0 else [z[:, h], st[:, h]], axis=-1)
              for h in range(GLA_HEADS)]
    return jnp.stack(halves, axis=1)


def _gla_state_out(st):
    parts = [st[:, h, :, GLA_DK * (h % 2):GLA_DK * (h % 2 + 1)] for h in range(GLA_HEADS)]
    return jnp.swapaxes(jnp.stack(parts, axis=1), -1, -2)


TOKEN_TILE = 512
MIXER_TILE = 256
GDN_CHUNKS = 8
GLA_CHUNKS = 4
MOE_TILE = 512


def _tiles(B, T):
    lt = min(TOKEN_TILE, T)
    nb = max(1, min(B, TOKEN_TILE // lt))
    lt_moe = min(MOE_TILE, T)
    nb_moe = max(1, min(B, MOE_TILE // lt_moe))
    lt_rec = min(MIXER_TILE, T)
    nb_gdn = max(1, min(B, GDN_CHUNKS // (lt_rec // CHUNK)))
    nb_gla = max(1, min(B, GLA_CHUNKS // (lt_rec // CHUNK)))
    return dict(nb=nb, lt=lt, nb_gdn=nb_gdn, nb_gla=nb_gla, lt_rec=lt_rec, lt_s5=min(CHUNK, T), nb_moe=nb_moe,
                lt_moe=lt_moe)


def _trunk(x, mods, params, states, final_w):
    tiles = _tiles(x.shape[0], x.shape[1])
    nb, lt, lt_rec, lt_s5 = tiles["nb"], tiles["lt"], tiles["lt_rec"], tiles["lt_s5"]
    new = []
    for l, p in enumerate(params):
        s_gla, s_gdn, cbuf, s5_re, s5_im = states[l]
        B = x.shape[0]
        cb8 = jnp.concatenate([jnp.zeros((B, SUBLANES - (CONV_W - 1), GDN_CONV_CH), F32), cbuf], axis=1)
        zg, zd, zs, cb_new = _inproj_call(x, mods[l], cb8, p, nb, lt)
        oa, gla_t = _gla_call(zg, _gla_state_in(s_gla), p["gla_norm_w"], tiles["nb_gla"], lt_rec)
        ob, gdn_new = _gdn_call(zd, s_gdn, p["gdn_norm_w"], tiles["nb_gdn"], lt_rec)
        h0 = jnp.concatenate([s5_re.reshape(B, S5_HALF), s5_im.reshape(B, S5_HALF)], axis=1)
        ys, h_new = _s5_call(zs, h0, p, lt_s5)
        x = _merge_call(x, mods[l], oa, ob, ys, p, nb, lt)
        x = _moe_call(x, mods[l], p, final_w, tiles["nb_moe"], tiles["lt_moe"],
                      final_norm=(l == len(params) - 1))
        new.append((_gla_state_out(gla_t), gdn_new, cb_new[:, SUBLANES - (CONV_W - 1):],
                    h_new[:, :S5_HALF].reshape(B, S5_GROUPS, S5_STATE),
                    h_new[:, S5_HALF:].reshape(B, S5_GROUPS, S5_STATE)))
    return x, [jnp.stack([new[l][i] for l in range(len(params))]) for i in range(5)]


def kernel(x_prompt, x_sample, c_prompt, c_sample, state_gla, state_gdn, cache_gdn_conv, state_s5_re, state_s5_im, norm1_w, norm2_w, final_norm_w, w_ada, b_ada, w_in, w_alpha_up, b_alpha, gla_norm_w, conv_w, gdn_a_log, gdn_dt_bias, gdn_norm_w, s5_a_re, s5_a_im, s5_log_dt, s5_b_re, s5_b_im, s5_c_re, s5_c_im, s5_d, w_glu, b_glu, w_br_a, w_br_b, w_br_c, w_gate, b_gate, w_out, w_route_group, b_route_group, w_route_expert, b_route_expert, w_exp_gate, w_exp_up, w_exp_down):
    depth = w_in.shape[0]
    bp, tp, _ = x_prompt.shape
    bs, ts, _ = x_sample.shape
    params = [_prep_layer(l, norm1_w, norm2_w, w_in, w_alpha_up, b_alpha, gla_norm_w, conv_w, gdn_a_log,
                          gdn_dt_bias, gdn_norm_w, s5_a_re, s5_a_im, s5_log_dt, s5_b_re, s5_b_im, s5_c_re,
                          s5_c_im, s5_d, w_glu, b_glu, w_br_a, w_br_b, w_br_c, w_gate, b_gate, w_out,
                          w_route_group, b_route_group, w_route_expert, b_route_expert, w_exp_gate, w_exp_up,
                          w_exp_down) for l in range(depth)]
    final_w = final_norm_w.reshape(1, D_MODEL)

    mod = _ada_call(jnp.concatenate([c_prompt, c_sample], axis=0), w_ada, b_ada)
    mod = mod.reshape(depth, bp + bs, 6, D_MODEL)
    mods_p = [mod[l, :bp] for l in range(depth)]
    mods_s = [mod[l, bp:] for l in range(depth)]

    zeros_p = [(jnp.zeros((bp, GLA_HEADS, GLA_DK, GLA_DV), F32), jnp.zeros((bp, GDN_HEADS, GDN_DK, GDN_DV), F32),
                jnp.zeros((bp, CONV_W - 1, GDN_CONV_CH), F32), jnp.zeros((bp, S5_GROUPS, S5_STATE), F32),
                jnp.zeros((bp, S5_GROUPS, S5_STATE), F32)) for _ in range(depth)]
    y_p, new_p = _trunk(x_prompt, mods_p, params, zeros_p, final_w)

    states_s = [(state_gla[l], state_gdn[l], cache_gdn_conv[l], state_s5_re[l], state_s5_im[l])
                for l in range(depth)]
    y_s, new_s = _trunk(x_sample, mods_s, params, states_s, final_w)
    return (y_p, y_s, *new_p, *new_s)
```

```python
import functools
import math

import jax
import jax.numpy as jnp
from jax import lax
from jax.experimental import pallas as pl
from jax.experimental.pallas import tpu as pltpu

F32 = jnp.float32
BF16 = jnp.bfloat16
HIGHEST = lax.Precision.HIGHEST

D_MODEL = 1024
EPS = 1e-6
CHUNK = 64
BRANCH_W = 512
GLA_HEADS, GLA_DK, GLA_DV, GLA_LOWRANK, GLA_TAU = 4, 64, 128, 16, 16.0
GDN_HEADS, GDN_DK, GDN_DV, CONV_W = 4, 128, 128, 4
S5_GROUP, S5_GROUPS, S5_STATE = 16, 32, 64
GLA_QK, GLA_V, GDN_QK, GDN_V = 256, 512, 512, 512
GDN_CONV_CH = 1536
MOE_GROUPS, EXPERTS_PER_GROUP, N_EXPERTS, D_FF = 4, 4, 16, 256

LANES = 128
SUBLANES = 8
VMEM_LIMIT = 56 * 1024 * 1024
S5_HALF = S5_GROUPS * S5_STATE
SMALL_BETA0, SMALL_A0 = 16, 20
ROUTE_E0 = 4


def _mm(a, b):
    return jnp.dot(a.astype(BF16), b.astype(BF16), preferred_element_type=F32)


def _mm_nt(a, b):
    return lax.dot_general(a.astype(BF16), b.astype(BF16), (((1,), (1,)), ((), ())), preferred_element_type=F32)


def _mm_tn(a, b):
    return lax.dot_general(a.astype(BF16), b.astype(BF16), (((0,), (0,)), ((), ())), preferred_element_type=F32)


def _mm_f32(a, b):
    return jnp.dot(a, b, preferred_element_type=F32, precision=HIGHEST)


def _mm_nt_f32(a, b):
    return lax.dot_general(a, b, (((1,), (1,)), ((), ())), preferred_element_type=F32, precision=HIGHEST)


def _split3(x):
    a = x.astype(BF16)
    r = x - a.astype(F32)
    b = r.astype(BF16)
    c = (r - b.astype(F32)).astype(BF16)
    return a, b, c


def _mm_sel(m01, x):
    a, b, c = _split3(x)
    return _mm(m01, a) + _mm(m01, b) + _mm(m01, c)


def _mm_sel_r(x, m01):
    a, b, c = _split3(x)
    return _mm(a, m01) + _mm(b, m01) + _mm(c, m01)


def _mm_nt_sel(m01, x):
    a, b, c = _split3(x)
    return _mm_nt(m01, a) + _mm_nt(m01, b) + _mm_nt(m01, c)


def _sigmoid(x):
    return 1.0 / (1.0 + jnp.exp(-x))


def _silu(x):
    return x * _sigmoid(x)


def _softplus(x):
    return jnp.maximum(x, 0.0) + jnp.log1p(jnp.exp(-jnp.abs(x)))


def _rms_rows(x, w):
    ms = jnp.mean(x * x, axis=-1, keepdims=True)
    return x * lax.rsqrt(ms + EPS) * w


def _gelu_tanh(x):
    c = math.sqrt(2.0 / math.pi)
    return x * (0.5 * (1.0 + jnp.tanh(c * (x + 0.044715 * (x * x * x)))))


def _const_spec(shape):
    nd = len(shape)
    return pl.BlockSpec(shape, lambda *_: (0,) * nd)


def _time_major_shape(B, T, nb):
    return (T, B * BRANCH_W)


def _time_major_spec(nb, lt):
    return pl.BlockSpec((lt, nb * BRANCH_W), lambda b, t: (t, b))


def _params(sem):
    return pltpu.CompilerParams(dimension_semantics=sem, vmem_limit_bytes=VMEM_LIMIT)


def _ada_kernel(c_ref, w_ref, b_ref, o_ref):
    c = c_ref[...]
    o_ref[0] = _mm(_silu(c), w_ref[0]) + b_ref[0]


def _ada_call(c_all, w_ada, b_ada):
    depth = w_ada.shape[0]
    nrow = c_all.shape[0]
    return pl.pallas_call(
        _ada_kernel,
        grid=(depth, 6),
        in_specs=[pl.BlockSpec((nrow, D_MODEL), lambda l, j: (0, 0)),
                  pl.BlockSpec((1, D_MODEL, D_MODEL), lambda l, j: (l, 0, j)),
                  pl.BlockSpec((1, 1, D_MODEL), lambda l, j: (l, 0, j))],
        out_specs=pl.BlockSpec((1, nrow, D_MODEL), lambda l, j: (l, 0, j)),
        out_shape=jax.ShapeDtypeStruct((depth, nrow, 6 * D_MODEL), F32),
        compiler_params=_params(("parallel", "parallel")),
        name="adaln",
    )(c_all, w_ada, b_ada.reshape(depth, 1, 6 * D_MODEL))


MXU_COLS = 256


def _inproj_kernel(x_ref, mod_ref, nw_ref, wg_ref, wd_ref, ws_ref, wsm_ref, wup_ref, bal_ref, sp_ref, cb_ref,
                   cw_ref, zg_ref, zd_ref, zs_ref, cbo_ref, *xbufs):
    nb, lt, d = x_ref.shape
    span = lt + SUBLANES

    @pl.when(pl.program_id(1) == 0)
    def _():
        for i in range(nb):
            for k, xb in enumerate(xbufs):
                xb[span * i + lt:span * (i + 1), :] = cb_ref[i, :, MXU_COLS * k:MXU_COLS * (k + 1)]

    x = x_ref[...]
    h = _rms_rows(x, nw_ref[...]) * (1.0 + mod_ref[:, 1:2, :]) + mod_ref[:, 0:1, :]
    h2 = h.reshape(nb * lt, d).astype(BF16)

    base = SUBLANES - (CONV_W - 1)
    for i in range(nb):
        for xb in xbufs:
            xb[span * i:span * i + SUBLANES, :] = xb[span * i + lt:span * (i + 1), :]
    def gla_block(j):
        dst = MXU_COLS * j if j < 2 else MXU_COLS * (j + 1)
        zg_ref[:, :, dst:dst + MXU_COLS] = _mm(h2, wg_ref[:, MXU_COLS * j:MXU_COLS * (j + 1)]).reshape(
            nb, lt, MXU_COLS)

    def gate_block(j):
        cs = slice(GDN_CONV_CH + MXU_COLS * j, GDN_CONV_CH + MXU_COLS * (j + 1))
        zd_ref[:, :, cs] = _mm(h2, wd_ref[:, cs]).reshape(nb, lt, MXU_COLS)

    def s5_block(j):
        zs = _mm(h2, ws_ref[:, MXU_COLS * j:MXU_COLS * (j + 1)])
        for i in range(nb):
            zs_ref[:, BRANCH_W * i + MXU_COLS * j:BRANCH_W * i + MXU_COLS * (j + 1)] = zs[lt * i:lt * (i + 1), :]

    def small_block(_):
        zsm = _mm(h2, wsm_ref[...])
        la_pre = _mm(zsm, wup_ref[...]) + bal_ref[...]
        log_a = (jnp.minimum(la_pre, 0.0) - jnp.log1p(jnp.exp(-jnp.abs(la_pre)))) * (1.0 / GLA_TAU)
        zg_ref[:, :, 512:768] = log_a.reshape(nb, lt, 256)
        lane = lax.broadcasted_iota(jnp.int32, zsm.shape, 1)
        beta = _sigmoid(zsm)
        log_g = -sp_ref[0:1, :] * _softplus(zsm + sp_ref[1:2, :])
        is_beta = (lane >= SMALL_BETA0) & (lane < SMALL_A0)
        is_a = (lane >= SMALL_A0) & (lane < SMALL_A0 + GDN_HEADS)
        small = jnp.where(is_beta, beta, jnp.where(is_a, log_g, 0.0))
        zd_ref[:, :, 2048:2176] = small.reshape(nb, lt, LANES)

    fillers = [[(gla_block, 0), (gate_block, 0)], [(gla_block, 1), (gate_block, 1)],
               [(gla_block, 2), (s5_block, 0)], [(gla_block, 3), (s5_block, 1)],
               [(gla_block, 4), (small_block, 0)], [(gla_block, 5)]]
    def conv_matmul(mblk):
        zd = _mm(h2, wd_ref[:, MXU_COLS * mblk:MXU_COLS * (mblk + 1)])
        for i in range(nb):
            xbufs[mblk][span * i + SUBLANES:span * (i + 1), :] = zd[lt * i:lt * (i + 1), :]
        for fn, arg in fillers[mblk]:
            fn(arg)

    n_blk = GDN_CONV_CH // MXU_COLS
    conv_matmul(0)
    for mblk in range(n_blk):
        if mblk + 1 < n_blk:
            conv_matmul(mblk + 1)
        xb = xbufs[mblk]
        for i in range(nb):
            r0 = span * i
            for cblk in range(MXU_COLS * mblk // LANES, MXU_COLS * (mblk + 1) // LANES):
                cs = slice(LANES * cblk, LANES * (cblk + 1))
                ls = slice(LANES * cblk - MXU_COLS * mblk, LANES * (cblk + 1) - MXU_COLS * mblk)
                conv = xb[r0 + base:r0 + base + lt, ls] * cw_ref[0:1, cs]
                for j in range(1, CONV_W):
                    conv = conv + xb[r0 + base + j:r0 + base + j + lt, ls] * cw_ref[j:j + 1, cs]
                act = _silu(conv)
                if cblk < 2 * GDN_HEADS:
                    scale = GDN_DK ** -0.5 if cblk < GDN_HEADS else 1.0
                    ss = jnp.sum(act * act, axis=-1, keepdims=True)
                    act = act * (lax.rsqrt(ss + EPS) * scale)
                zd_ref[i, :, cs] = act
    for i in range(nb):
        for k, xb in enumerate(xbufs):
            cbo_ref[i, :, MXU_COLS * k:MXU_COLS * (k + 1)] = xb[span * i + lt:span * (i + 1), :]


def _inproj_call(x, mod, cb8, p, nb, lt):
    B, T, D = x.shape
    assert nb == 1 or T == lt, "several sequences per step only when each is a single time block"
    grid = (B // nb, T // lt)
    tok = lambda w: pl.BlockSpec((nb, lt, w), lambda b, t: (b, t, 0))
    cb_spec = pl.BlockSpec((nb, SUBLANES, GDN_CONV_CH), lambda b, t: (b, 0, 0))
    return pl.pallas_call(
        _inproj_kernel,
        grid=grid,
        in_specs=[tok(D),
                  pl.BlockSpec((nb, 6, D), lambda b, t: (b, 0, 0)),
                  _const_spec((1, D)),
                  _const_spec(p["w_gla"].shape), _const_spec(p["w_gdn"].shape), _const_spec(p["w_s5"].shape),
                  _const_spec(p["w_small"].shape), _const_spec(p["w_up"].shape), _const_spec((1, GLA_QK)),
                  _const_spec((2, LANES)), cb_spec, _const_spec((CONV_W, GDN_CONV_CH))],
        out_specs=[tok(1792), tok(2176), _time_major_spec(nb, lt), cb_spec],
        out_shape=[jax.ShapeDtypeStruct((B, T, 1792), F32),
                   jax.ShapeDtypeStruct((B, T, 2176), F32),
                   jax.ShapeDtypeStruct(_time_major_shape(B, T, nb), F32),
                   jax.ShapeDtypeStruct((B, SUBLANES, GDN_CONV_CH), F32)],
        scratch_shapes=[pltpu.VMEM((nb * (lt + SUBLANES), MXU_COLS), F32)] * (GDN_CONV_CH // MXU_COLS),
        compiler_params=_params(("parallel", "arbitrary")),
        name="inproj",
    )(x, mod, p["norm1_w"], p["w_gla"], p["w_gdn"], p["w_s5"], p["w_small"], p["w_up"], p["b_alpha"],
      p["gdn_small"], cb8, p["conv_w"])


def _gla_kernel(z_ref, s0_ref, nw_ref, o_ref, s_ref):
    L = CHUNK
    nb, lt, _ = z_ref.shape
    n_chunks = lt // L
    n_rows = nb * lt

    @pl.when(pl.program_id(1) == 0)
    def _():
        s_ref[...] = s0_ref[...]

    row = lax.broadcasted_iota(jnp.int32, (L, L), 0)
    col = lax.broadcasted_iota(jnp.int32, (L, L), 1)
    causal = row >= col
    lane = lax.broadcasted_iota(jnp.int32, (L, LANES), 1)
    own_half = (lane < GLA_DK, lane >= GLA_DK)
    nw = nw_ref[...]
    brow = lax.broadcasted_iota(jnp.int32, (n_rows, n_rows), 0)
    bcol = lax.broadcasted_iota(jnp.int32, (n_rows, n_rows), 1)
    shift = L.bit_length() - 1
    chunk_tril = ((brow >= bcol) & ((brow >> shift) == (bcol >> shift))).astype(BF16)
    b_all = _mm_sel(chunk_tril, z_ref[:, :, 512:768].reshape(n_rows, GLA_QK))

    def zcols(c, lo, width):
        i, cc = divmod(c, n_chunks)
        return z_ref[i, L * cc:L * (cc + 1), lo:lo + width]

    all_chunks = range(nb * n_chunks)
    probs = [(c, h) for c in all_chunks for h in range(GLA_HEADS)]
    rows_of = lambda c: slice(L * c, L * (c + 1))
    pair_of = lambda h: slice(LANES * (h // 2), LANES * (h // 2) + LANES)
    qi_, ki_, qs_, kl_, dl_ = {}, {}, {}, {}, {}
    for c in all_chunks:
        q = zcols(c, 0, GLA_QK)
        k = zcols(c, GLA_QK, GLA_QK)
        b = b_all[rows_of(c), :]
        b_mid = b[L // 2 - 1:L // 2, :]
        b_last = b[L - 1:L, :]
        qi_[c] = q * jnp.exp(b - b_mid)
        ki_[c] = k * jnp.exp(b_mid - b)
        qs_[c] = q * jnp.exp(b)
        kl_[c] = k * jnp.exp(b_last - b)
        dl_[c] = jnp.exp(b_last)
    v_ = {p: zcols(p[0], 768 + GLA_DV * p[1], GLA_DV) for p in probs}
    att_ = {p: jnp.where(causal, _mm_nt(qi_[p[0]][:, pair_of(p[1])],
                                        jnp.where(own_half[p[1] % 2], ki_[p[0]][:, pair_of(p[1])], 0.0)), 0.0)
            for p in probs}
    kv_ = {p: _mm_tn(v_[p], jnp.where(own_half[p[1] % 2], kl_[p[0]][:, pair_of(p[1])], 0.0)) for p in probs}
    av_ = {p: _mm(att_[p], v_[p]) for p in probs}
    st_ = {}
    for i in range(nb):
        for h in range(GLA_HEADS):
            st = s_ref[i, h]
            for cc in range(n_chunks):
                c = n_chunks * i + cc
                st_[(c, h)] = st
                st = st * dl_[c][:, pair_of(h)] + kv_[(c, h)]
            s_ref[i, h] = st
    for p in probs:
        c, h = p
        i, cc = divmod(c, n_chunks)
        o_h = av_[p] + _mm_nt(qs_[c][:, pair_of(h)], st_[p])
        gate = zcols(c, 1280 + GLA_DV * h, GLA_DV)
        o_ref[i, L * cc:L * (cc + 1), GLA_DV * h:GLA_DV * (h + 1)] = _rms_rows(o_h, nw) * _silu(gate)


def _gla_call(zg, s0t, norm_w, nb, lt):
    B, T, _ = zg.shape
    st_spec = pl.BlockSpec((nb, GLA_HEADS, GLA_DV, LANES), lambda b, t: (b, 0, 0, 0))
    return pl.pallas_call(
        _gla_kernel,
        grid=(B // nb, T // lt),
        in_specs=[pl.BlockSpec((nb, lt, 1792), lambda b, t: (b, t, 0)), st_spec, _const_spec((1, GLA_DV))],
        out_specs=[pl.BlockSpec((nb, lt, GLA_V), lambda b, t: (b, t, 0)), st_spec],
        out_shape=[jax.ShapeDtypeStruct((B, T, GLA_V), F32),
                   jax.ShapeDtypeStruct((B, GLA_HEADS, GLA_DV, LANES), F32)],
        compiler_params=_params(("parallel", "arbitrary")),
        name="gla",
    )(zg, s0t, norm_w)


def _gdn_kernel(z_ref, s0_ref, nw_ref, o_ref, s_ref, bfull_ref, gfull_ref, grow_ref, glast_ref):
    L = CHUNK
    nb, lt, _ = z_ref.shape
    n_chunks = lt // L
    n_rows = nb * lt

    @pl.when(pl.program_id(1) == 0)
    def _():
        s_ref[...] = s0_ref[...]

    row = lax.broadcasted_iota(jnp.int32, (L, L), 0)
    col = lax.broadcasted_iota(jnp.int32, (L, L), 1)
    causal = row >= col
    strict = row > col
    eye = (row == col).astype(F32)
    diag16 = (row >> 4) == (col >> 4)
    diag32 = (row >> 5) == (col >> 5)
    off32 = diag32 & jnp.logical_not(diag16)
    off64 = jnp.logical_not(diag32)
    sel_r = lax.broadcasted_iota(jnp.int32, (SUBLANES, LANES), 0)
    sel_c = lax.broadcasted_iota(jnp.int32, (SUBLANES, LANES), 1)
    sel = (sel_c == sel_r + SMALL_A0).astype(BF16)
    brow = lax.broadcasted_iota(jnp.int32, (n_rows, n_rows), 0)
    bcol = lax.broadcasted_iota(jnp.int32, (n_rows, n_rows), 1)
    shift = L.bit_length() - 1
    chunk_tril = ((brow >= bcol) & ((brow >> shift) == (bcol >> shift))).astype(BF16)
    small_all = z_ref[:, :, 2048:2176].reshape(n_rows, LANES)
    gs_all = _mm_sel(chunk_tril, small_all)
    grow_all = _mm_nt_sel(sel, gs_all)
    e_r = lax.broadcasted_iota(jnp.int32, (LANES, GDN_HEADS * LANES), 0)
    e_h = lax.broadcasted_iota(jnp.int32, (LANES, GDN_HEADS * LANES), 1) >> (LANES.bit_length() - 1)
    bfull_ref[...] = _mm_sel_r(small_all, (e_r == e_h + SMALL_BETA0).astype(BF16))
    gfull = _mm_sel_r(gs_all, (e_r == e_h + SMALL_A0).astype(BF16))
    gfull_ref[...] = gfull
    for c in range(nb * n_chunks):
        grow_ref[c] = grow_all[:, L * c:L * (c + 1)]
        glast_ref[c:c + 1, :] = gfull[L * (c + 1) - 1:L * (c + 1), :]
    nw = nw_ref[...]

    probs = [(c, h) for c in range(nb * n_chunks) for h in range(GDN_HEADS)]
    rows_of = lambda c: slice(L * c, L * (c + 1))
    head_of = lambda h: slice(LANES * h, LANES * (h + 1))

    def zcols(c, lo, width):
        i, cc = divmod(c, n_chunks)
        return z_ref[i, L * cc:L * (cc + 1), lo:lo + width]

    q_ = {p: zcols(p[0], GDN_DK * p[1], GDN_DK) for p in probs}
    k_ = {p: zcols(p[0], GDN_QK + GDN_DK * p[1], GDN_DK) for p in probs}
    beta_ = {p: bfull_ref[rows_of(p[0]), head_of(p[1])] for p in probs}
    g_ = {p: gfull_ref[rows_of(p[0]), head_of(p[1])] for p in probs}
    dec_ = {p: jnp.exp(jnp.where(causal, g_[p][:, 0:L] - grow_ref[p[0]][p[1]:p[1] + 1, :], -jnp.inf))
            for p in probs}
    kk_ = {p: _mm_nt(k_[p], k_[p]) for p in probs}
    qk_ = {p: _mm_nt(q_[p], k_[p]) for p in probs}
    a_ = {p: jnp.where(strict, beta_[p][:, 0:L] * kk_[p] * dec_[p], 0.0) for p in probs}
    qk_ = {p: jnp.where(causal, qk_[p] * dec_[p], 0.0) for p in probs}
    pw_ = {p: jnp.where(diag16, -a_[p], 0.0) for p in probs}
    tinv_ = {p: eye + pw_[p] for p in probs}
    for _ in range(3):
        pw_ = {p: _mm(pw_[p], pw_[p]) for p in probs}
        tinv_ = {p: tinv_[p] + _mm(tinv_[p], pw_[p]) for p in probs}
    for off in (off32, off64):
        te_ = {p: _mm(tinv_[p], jnp.where(off, a_[p], 0.0)) for p in probs}
        tinv_ = {p: tinv_[p] - _mm(te_[p], tinv_[p]) for p in probs}
    eg_ = {p: jnp.exp(g_[p]) for p in probs}
    sol_ = {}
    for p in probs:
        v_p = zcols(p[0], 2 * GDN_QK + GDN_DV * p[1], GDN_DV)
        rhs = jnp.concatenate([beta_[p] * v_p, (beta_[p] * eg_[p]) * k_[p]], axis=-1)
        sol_[p] = _mm(tinv_[p], rhs)

    st_ = {(i, h): s_ref[i, h] for i in range(nb) for h in range(GDN_HEADS)}
    for cc in range(n_chunks):
        heads = [(i, n_chunks * i + cc, h) for i in range(nb) for h in range(GDN_HEADS)]
        u_, o_, new_ = {}, {}, {}
        for i, c, h in heads:
            u_[(c, h)] = sol_[(c, h)][:, 0:GDN_DV] - _mm(sol_[(c, h)][:, GDN_DV:], st_[(i, h)])
        for i, c, h in heads:
            o_[(c, h)] = _mm(qk_[(c, h)], u_[(c, h)]) + _mm(q_[(c, h)] * eg_[(c, h)], st_[(i, h)])
        for i, c, h in heads:
            g_last = glast_ref[c:c + 1, head_of(h)]
            new_[(i, h)] = (st_[(i, h)] * jnp.exp(g_last)
                            + _mm_tn(k_[(c, h)] * jnp.exp(g_last - g_[(c, h)]), u_[(c, h)]))
        st_ = new_
        for i, c, h in heads:
            gate = z_ref[i, L * cc:L * (cc + 1), GDN_CONV_CH + GDN_DV * h:GDN_CONV_CH + GDN_DV * (h + 1)]
            o_ref[i, L * cc:L * (cc + 1), GDN_DV * h:GDN_DV * (h + 1)] = _rms_rows(o_[(c, h)], nw) * _silu(gate)
    for i in range(nb):
        for h in range(GDN_HEADS):
            s_ref[i, h] = st_[(i, h)]


def _gdn_call(zd, s0, norm_w, nb, lt):
    B, T, _ = zd.shape
    n_rows = nb * lt
    last_rows = -(-(n_rows // CHUNK) // SUBLANES) * SUBLANES
    st_spec = pl.BlockSpec((nb, GDN_HEADS, GDN_DK, GDN_DV), lambda b, t: (b, 0, 0, 0))
    return pl.pallas_call(
        _gdn_kernel,
        grid=(B // nb, T // lt),
        in_specs=[pl.BlockSpec((nb, lt, 2176), lambda b, t: (b, t, 0)), st_spec, _const_spec((1, GDN_DV))],
        out_specs=[pl.BlockSpec((nb, lt, GDN_V), lambda b, t: (b, t, 0)), st_spec],
        out_shape=[jax.ShapeDtypeStruct((B, T, GDN_V), F32),
                   jax.ShapeDtypeStruct((B, GDN_HEADS, GDN_DK, GDN_DV), F32)],
        scratch_shapes=[pltpu.VMEM((n_rows, GDN_HEADS * LANES), F32), pltpu.VMEM((n_rows, GDN_HEADS * LANES), F32),
                        pltpu.VMEM((n_rows // CHUNK, SUBLANES, CHUNK), F32),
                        pltpu.VMEM((last_rows, GDN_HEADS * LANES), F32)],
        compiler_params=_params(("parallel", "arbitrary")),
        name="gdn",
    )(zd, s0, norm_w)


S5_LANE_CHUNK = 512


def _s5_kernel(u_ref, h0_ref, bb_ref, cc_ref, lam_ref, d_ref, y_ref, h_ref, bu_ref, *, lt):
    nb = SUBLANES

    @pl.when(pl.program_id(1) == 0)
    def _():
        h_ref[...] = h0_ref[...]

    u = u_ref[...].reshape(lt * nb, BRANCH_W)
    ch, sh = BRANCH_W // 2, S5_HALF // 2
    for part in range(2):
        for hb in range(2):
            bu_ref[:, part * S5_HALF + hb * sh:part * S5_HALF + (hb + 1) * sh] = _mm(
                u[:, hb * ch:(hb + 1) * ch], bb_ref[2 * part + hb])
    for j in range(S5_HALF // S5_LANE_CHUNK):
        re = slice(S5_LANE_CHUNK * j, S5_LANE_CHUNK * (j + 1))
        im = slice(S5_HALF + S5_LANE_CHUNK * j, S5_HALF + S5_LANE_CHUNK * (j + 1))
        lr = jnp.broadcast_to(lam_ref[0:1, re], (nb, S5_LANE_CHUNK))
        li = jnp.broadcast_to(lam_ref[1:2, re], (nb, S5_LANE_CHUNK))

        hr, hi = h_ref[:, re], h_ref[:, im]
        for tt in range(lt):
            rows = slice(tt * nb, (tt + 1) * nb)
            hr, hi = lr * hr - li * hi + bu_ref[rows, re], lr * hi + li * hr + bu_ref[rows, im]
            bu_ref[rows, re] = hr
            bu_ref[rows, im] = hi
        h_ref[:, re] = hr
        h_ref[:, im] = hi
    ys = [_mm(bu_ref[:, hb * sh:(hb + 1) * sh], cc_ref[hb])
          + _mm(bu_ref[:, S5_HALF + hb * sh:S5_HALF + (hb + 1) * sh], cc_ref[2 + hb]) for hb in range(2)]
    y = jnp.concatenate(ys, axis=-1) + d_ref[...] * u
    y_ref[...] = y.reshape(lt, nb, BRANCH_W)


def _s5_call(zs, h0, p, lt):
    T = zs.shape[0]
    B = h0.shape[0]
    nb = SUBLANES
    h_spec = pl.BlockSpec((nb, 2 * S5_HALF), lambda b, t: (b, 0))
    tok_spec = pl.BlockSpec((lt, None, nb, BRANCH_W), lambda b, t: (t, b, 0, 0))
    y, h = pl.pallas_call(
        functools.partial(_s5_kernel, lt=lt),
        grid=(B // nb, T // lt),
        in_specs=[tok_spec, h_spec,
                  _const_spec((4, BRANCH_W // 2, S5_HALF // 2)), _const_spec((4, S5_HALF // 2, BRANCH_W // 2)),
                  _const_spec((2, S5_HALF)), _const_spec((1, BRANCH_W))],
        out_specs=[tok_spec, h_spec],
        out_shape=[jax.ShapeDtypeStruct((T, B // nb, nb, BRANCH_W), F32),
                   jax.ShapeDtypeStruct((B, 2 * S5_HALF), F32)],
        scratch_shapes=[pltpu.VMEM((lt * nb, 2 * S5_HALF), F32)],
        compiler_params=_params(("parallel", "arbitrary")),
        name="s5",
    )(zs.reshape(T, B // nb, nb, BRANCH_W), h0, p["s5_bblk"], p["s5_cblk"], p["s5_lam"], p["s5_d"])
    return y.reshape(zs.shape), h


def _merge_kernel(x_ref, mod_ref, nw_ref, oa_ref, ob_ref, ys_ref, wgate_ref, bgate_ref, wa_ref, wb_ref, wc_ref,
                  wglu_ref, bglu_ref, wout_ref, o_ref):
    nb, lt, d = x_ref.shape
    n = nb * lt
    x = x_ref[...]
    h = _rms_rows(x, nw_ref[...]) * (1.0 + mod_ref[:, 1:2, :]) + mod_ref[:, 0:1, :]
    gate = _sigmoid(_mm(h.reshape(n, d), wgate_ref[...]) + bgate_ref[...])
    y_a = _mm(oa_ref[...].reshape(n, BRANCH_W), wa_ref[...])
    y_b = _mm(ob_ref[...].reshape(n, BRANCH_W), wb_ref[...])
    ys = [ys_ref[:, BRANCH_W * i:BRANCH_W * (i + 1)] for i in range(nb)]
    yc = _gelu_tanh(ys[0] if nb == 1 else jnp.concatenate(ys, axis=0))
    yc = yc * _sigmoid(_mm(yc, wglu_ref[...]) + bglu_ref[...])
    y_c = _mm(yc, wc_ref[...])
    mix = gate[:, 0:d] * y_a + gate[:, d:2 * d] * y_b + gate[:, 2 * d:3 * d] * y_c
    m = _mm(mix, wout_ref[...])
    o_ref[...] = x + mod_ref[:, 2:3, :] * m.reshape(nb, lt, d)


def _merge_call(x, mod, oa, ob, ys, p, nb, lt):
    B, T, D = x.shape
    tok = lambda w: pl.BlockSpec((nb, lt, w), lambda b, t: (b, t, 0))
    return pl.pallas_call(
        _merge_kernel,
        grid=(B // nb, T // lt),
        in_specs=[tok(D), pl.BlockSpec((nb, 6, D), lambda b, t: (b, 0, 0)), _const_spec((1, D)),
                  tok(BRANCH_W), tok(BRANCH_W), _time_major_spec(nb, lt),
                  _const_spec((D, 3 * D)), _const_spec((1, 3 * D)),
                  _const_spec((BRANCH_W, D)), _const_spec((BRANCH_W, D)), _const_spec((BRANCH_W, D)),
                  _const_spec((BRANCH_W, BRANCH_W)), _const_spec((1, BRANCH_W)), _const_spec((D, D))],
        out_specs=tok(D),
        out_shape=jax.ShapeDtypeStruct((B, T, D), F32),
        compiler_params=_params(("parallel", "parallel")),
        name="merge",
    )(x, mod, p["norm1_w"], oa, ob, ys, p["w_gate"], p["b_gate"], p["w_br_a"], p["w_br_b"], p["w_br_c"],
      p["w_glu"], p["b_glu"], p["w_out"])


MOE_ROWS = 160


def _moe_kernel(x_ref, mod_ref, nw_ref, wr_ref, br_ref, ex_ref, wg_ref, wu_ref, wd_ref, fw_ref, o_ref,
                hn_ref, comb_ref, krow_ref, acc_ref, *, final_norm):
    nb, lt, d = x_ref.shape
    R = MOE_ROWS
    gw = EXPERTS_PER_GROUP * D_FF
    halves = hn_ref.shape[0]
    if nb >= halves:
        hb, hl = nb // halves, lt
    else:
        hb, hl = nb, lt // halves
    n = hb * hl
    row_pos = lax.broadcasted_iota(jnp.int32, (R, n), 0).astype(F32)
    lane = lax.broadcasted_iota(jnp.int32, (n, LANES), 1)
    trow = lax.broadcasted_iota(jnp.int32, (n, n), 0)
    tcol = lax.broadcasted_iota(jnp.int32, (n, n), 1)
    before = (trow > tcol).astype(BF16)
    sel_r = lax.broadcasted_iota(jnp.int32, (SUBLANES, LANES), 0)
    sel_c = lax.broadcasted_iota(jnp.int32, (SUBLANES, LANES), 1)
    pick = (sel_r == sel_c).astype(BF16)
    neg = -jnp.inf

    def xs(k):
        return x_ref[hb * k:hb * (k + 1)] if nb >= halves else x_ref[:, hl * k:hl * (k + 1), :]

    def mods(k, row):
        m = mod_ref[hb * k:hb * (k + 1)] if nb >= halves else mod_ref[...]
        return m[:, row:row + 1, :]

    def route(k):
        hn = (_rms_rows(xs(k), nw_ref[...]) * (1.0 + mods(k, 4)) + mods(k, 3)).reshape(n, d)
        hn_hi = hn.astype(BF16)
        hn_ref[k] = hn_hi
        hn_lo = (hn - hn_hi.astype(F32)).astype(BF16)
        logits = _mm(hn_hi, wr_ref[0]) + (_mm(hn_hi, wr_ref[1]) + _mm(hn_lo, wr_ref[0])) + br_ref[...]
        gl = jnp.where(lane < MOE_GROUPS, logits, neg)
        gmax = jnp.max(gl, axis=-1, keepdims=True)
        g_sel = jnp.min(jnp.where(gl == gmax, lane, LANES), axis=-1, keepdims=True)
        p_sel = 1.0 / jnp.sum(jnp.exp(gl - gmax), axis=-1, keepdims=True)
        e_id = lane - ROUTE_E0
        in_group = (e_id >= 0) & (e_id < N_EXPERTS) & ((e_id >> 2) == g_sel)
        el = jnp.where(in_group, logits, neg)
        m1 = jnp.max(el, axis=-1, keepdims=True)
        i1 = jnp.min(jnp.where(el == m1, lane, LANES), axis=-1, keepdims=True)
        el2 = jnp.where(lane == i1, neg, el)
        m2 = jnp.max(el2, axis=-1, keepdims=True)
        i2 = jnp.min(jnp.where(el2 == m2, lane, LANES), axis=-1, keepdims=True)
        e2 = jnp.exp(m2 - m1)
        w1 = p_sel / (1.0 + e2)
        w2 = p_sel * e2 / (1.0 + e2)
        comb_ref[k] = jnp.where(lane == i1, w1, jnp.where(lane == i2, w2, 0.0)).astype(BF16)
        member = lane == g_sel
        rank = _mm(before, member.astype(BF16))
        key = jnp.where(member, rank, -1.0)
        krow_ref[k] = _mm_nt_sel(pick, key)
        return [(jnp.max(key[:, g:g + 1]).astype(jnp.int32) + R) // R for g in range(MOE_GROUPS)]

    def one_pass(k, g, s):
        base = float(s * R) if isinstance(s, int) else (s * R).astype(F32)
        gather = (krow_ref[k, g:g + 1, :] - base == row_pos).astype(BF16)
        xg = _mm(gather, hn_ref[k])
        cg = _mm(_mm(gather, comb_ref[k]), ex_ref[:, gw * g:gw * (g + 1)])
        experts = range(EXPERTS_PER_GROUP * g, EXPERTS_PER_GROUP * (g + 1))
        he = jnp.concatenate([_silu(_mm(xg, wg_ref[e])) * _mm(xg, wu_ref[e]) for e in experts], axis=-1)
        yg = _mm(he * cg, wd_ref[g])
        return _mm_tn(gather, yg)

    n_pass = [route(k) for k in range(halves)]
    for k in range(halves):
        acc = one_pass(k, 0, 0)
        for g in range(1, MOE_GROUPS):
            acc = acc + one_pass(k, g, 0)
        acc_ref[k] = acc
    for k in range(halves):
        for g in range(MOE_GROUPS):
            def extra(s, carry, k=k, g=g):
                acc_ref[k] += one_pass(k, g, s)
                return carry

            lax.fori_loop(1, n_pass[k][g], extra, 0)
    for k in range(halves):
        out = xs(k) + mods(k, 5) * acc_ref[k].reshape(hb, hl, d)
        if final_norm:
            out = _rms_rows(out, fw_ref[...])
        if nb >= halves:
            o_ref[hb * k:hb * (k + 1)] = out
        else:
            o_ref[:, hl * k:hl * (k + 1), :] = out


def _moe_call(x, mod, p, final_w, nb, lt, final_norm):
    B, T, D = x.shape
    halves = MOE_HALVES if (nb * lt) % (MOE_HALVES * SUBLANES * 2) == 0 and max(nb, lt) % MOE_HALVES == 0 else 1
    n = nb * lt // halves
    gw = EXPERTS_PER_GROUP * D_FF
    tok = pl.BlockSpec((nb, lt, D), lambda b, t: (b, t, 0))
    resident = lambda shape: pl.BlockSpec(shape, lambda *_: (0,) * len(shape), pipeline_mode=pl.Buffered(1))
    return pl.pallas_call(
        functools.partial(_moe_kernel, final_norm=final_norm),
        grid=(B // nb, T // lt),
        in_specs=[tok, pl.BlockSpec((nb, 6, D), lambda b, t: (b, 0, 0)), _const_spec((1, D)),
                  _const_spec((2, D, LANES)), _const_spec((1, LANES)), _const_spec((LANES, N_EXPERTS * D_FF)),
                  resident((N_EXPERTS, D, D_FF)), resident((N_EXPERTS, D, D_FF)), resident((MOE_GROUPS, gw, D)),
                  _const_spec((1, D))],
        out_specs=tok,
        out_shape=jax.ShapeDtypeStruct((B, T, D), F32),
        scratch_shapes=[pltpu.VMEM((halves, n, D), BF16), pltpu.VMEM((halves, n, LANES), BF16),
                        pltpu.VMEM((halves, SUBLANES, n), F32), pltpu.VMEM((halves, n, D), F32)],
        compiler_params=_params(("parallel", "parallel")),
        name="moe",
    )(x, mod, p["norm2_w"], p["w_route"], p["b_route"], p["route_expand"], p["w_eg"], p["w_eu"], p["w_ed"],
      final_w)


def _prep_layer(l, norm1_w, norm2_w, w_in, w_alpha_up, b_alpha, gla_norm_w, conv_w, gdn_a_log, gdn_dt_bias,
                gdn_norm_w, s5_a_re, s5_a_im, s5_log_dt, s5_b_re, s5_b_im, s5_c_re, s5_c_im, s5_d, w_glu, b_glu,
                w_br_a, w_br_b, w_br_c, w_gate, b_gate, w_out, w_route_group, b_route_group, w_route_expert,
                b_route_expert, w_exp_gate, w_exp_up, w_exp_down):
    D = D_MODEL
    w = w_in[l]
    o = 0
    cols = {}
    for name, size in (("qa", GLA_QK), ("ka", GLA_QK), ("va", GLA_V), ("ga", GLA_V), ("aa", GLA_LOWRANK),
                       ("qkv", GDN_CONV_CH), ("beta", GDN_HEADS), ("a", GDN_HEADS), ("gb", GDN_V),
                       ("u", BRANCH_W)):
        cols[name] = w[:, o:o + size]
        o += size
    p = {}
    p["norm1_w"] = norm1_w[l].reshape(1, D)
    p["norm2_w"] = norm2_w[l].reshape(1, D)
    p["w_gla"] = jnp.concatenate([cols["qa"] * (GLA_DK ** -0.5), cols["ka"], cols["va"], cols["ga"]],
                                 axis=1).astype(BF16)
    p["w_gdn"] = jnp.concatenate([cols["qkv"], cols["gb"]], axis=1).astype(BF16)
    p["w_s5"] = cols["u"].astype(BF16)
    assert SMALL_BETA0 == GLA_LOWRANK and SMALL_A0 == SMALL_BETA0 + GDN_HEADS
    used = SMALL_A0 + GDN_HEADS
    p["w_small"] = jnp.concatenate([cols["aa"], cols["beta"], cols["a"], jnp.zeros((D, LANES - used), F32)],
                                   axis=1).astype(BF16)
    p["w_up"] = jnp.pad(w_alpha_up[l], ((0, LANES - GLA_LOWRANK), (0, 0))).astype(BF16)
    p["b_alpha"] = b_alpha[l].reshape(1, GLA_QK)
    p["gdn_small"] = jnp.pad(jnp.stack([jnp.exp(gdn_a_log[l]), gdn_dt_bias[l]]),
                             ((0, 0), (SMALL_A0, LANES - used)))
    p["gla_norm_w"] = gla_norm_w[l].reshape(1, GLA_DV)
    p["gdn_norm_w"] = gdn_norm_w[l].reshape(1, GDN_DV)
    p["conv_w"] = conv_w[l]

    a_re, a_im = s5_a_re[l], s5_a_im[l]
    dt = jnp.exp(s5_log_dt[l])[:, None]
    mag = jnp.exp(a_re * dt)
    lr, li = mag * jnp.cos(a_im * dt), mag * jnp.sin(a_im * dt)
    den = a_re * a_re + a_im * a_im
    f_re = ((lr - 1.0) * a_re + li * a_im) / den
    f_im = (li * a_re - (lr - 1.0) * a_im) / den
    bb_re = f_re[..., None] * s5_b_re[l] - f_im[..., None] * s5_b_im[l]
    bb_im = f_re[..., None] * s5_b_im[l] + f_im[..., None] * s5_b_re[l]
    gh = S5_GROUPS // 2
    eye_h = jnp.eye(gh, dtype=F32)
    blk = lambda m: (jnp.transpose(m, (0, 2, 1))[:, :, None, :] * eye_h[:, None, :, None]
                     ).reshape(gh * S5_GROUP, gh * S5_STATE)
    cblk = lambda m: (jnp.transpose(m, (2, 0, 1))[None, :, :, :] * eye_h[:, None, :, None]
                      ).reshape(gh * S5_STATE, gh * S5_GROUP)
    p["s5_bblk"] = jnp.stack([blk(m[hb * gh:(hb + 1) * gh]) for m in (bb_re, bb_im) for hb in range(2)]
                             ).astype(BF16)
    p["s5_cblk"] = jnp.stack([cblk(m[hb * gh:(hb + 1) * gh]) for m in (s5_c_re[l], -s5_c_im[l])
                              for hb in range(2)]).astype(BF16)
    p["s5_lam"] = jnp.stack([lr.reshape(S5_HALF), li.reshape(S5_HALF)])
    p["s5_d"] = s5_d[l].reshape(1, BRANCH_W)

    p["w_gate"] = w_gate[l].astype(BF16)
    p["b_gate"] = b_gate[l].reshape(1, 3 * D)
    p["w_br_a"] = w_br_a[l].astype(BF16)
    p["w_br_b"] = w_br_b[l].astype(BF16)
    p["w_br_c"] = w_br_c[l].astype(BF16)
    p["w_glu"] = w_glu[l].astype(BF16)
    p["b_glu"] = b_glu[l].reshape(1, BRANCH_W)
    p["w_out"] = w_out[l].astype(BF16)

    assert ROUTE_E0 == MOE_GROUPS
    n_route = MOE_GROUPS + N_EXPERTS
    wr = jnp.concatenate([w_route_group[l], w_route_expert[l], jnp.zeros((D, LANES - n_route), F32)], axis=1)
    br = jnp.pad(jnp.concatenate([b_route_group[l], b_route_expert[l]]), (0, LANES - n_route)).reshape(1, LANES)
    wr_hi = wr.astype(BF16)
    p["w_route"] = jnp.stack([wr_hi, (wr - wr_hi.astype(F32)).astype(BF16)])
    p["b_route"] = br
    e_of_col = jnp.arange(N_EXPERTS * D_FF) // D_FF
    p["route_expand"] = (jnp.arange(LANES)[:, None] == (e_of_col[None, :] + ROUTE_E0)).astype(BF16)
    gw = EXPERTS_PER_GROUP * D_FF
    p["w_eg"] = w_exp_gate[l].astype(BF16)
    p["w_eu"] = w_exp_up[l].astype(BF16)
    p["w_ed"] = w_exp_down[l].reshape(MOE_GROUPS, gw, D).astype(BF16)
    return p


def _gla_state_in(s):
    st = jnp.swapaxes(s, -1, -2)
    z = jnp.zeros_like(st)
    halves = [jnp.concatenate([st[:, h], z[:, h]] if h % 2 == 0 else [z[:, h], st[:, h]], axis=-1)
              for h in range(GLA_HEADS)]
    return jnp.stack(halves, axis=1)


def _gla_state_out(st):
    parts = [st[:, h, :, GLA_DK * (h % 2):GLA_DK * (h % 2 + 1)] for h in range(GLA_HEADS)]
    return jnp.swapaxes(jnp.stack(parts, axis=1), -1, -2)


TOKEN_TILE = 512
MIXER_TILE = 256
GDN_CHUNKS = 8
GLA_CHUNKS = 4
MOE_HALVES = 2
MOE_TILE = 512 * MOE_HALVES


def _tiles(B, T):
    lt = min(TOKEN_TILE, T)
    nb = max(1, min(B, TOKEN_TILE // lt))
    lt_moe = min(MOE_TILE, T)
    nb_moe = max(1, min(B, MOE_TILE // lt_moe))
    lt_rec = min(MIXER_TILE, T)
    nb_gdn = max(1, min(B, GDN_CHUNKS // (lt_rec // CHUNK)))
    nb_gla = max(1, min(B, GLA_CHUNKS // (lt_rec // CHUNK)))
    return dict(nb=nb, lt=lt, nb_gdn=nb_gdn, nb_gla=nb_gla, lt_rec=lt_rec, lt_s5=min(CHUNK, T), nb_moe=nb_moe,
                lt_moe=lt_moe)


def _trunk(x, mods, params, states, final_w):
    tiles = _tiles(x.shape[0], x.shape[1])
    nb, lt, lt_rec, lt_s5 = tiles["nb"], tiles["lt"], tiles["lt_rec"], tiles["lt_s5"]
    new = []
    for l, p in enumerate(params):
        s_gla, s_gdn, cbuf, s5_re, s5_im = states[l]
        B = x.shape[0]
        cb8 = jnp.concatenate([jnp.zeros((B, SUBLANES - (CONV_W - 1), GDN_CONV_CH), F32), cbuf], axis=1)
        zg, zd, zs, cb_new = _inproj_call(x, mods[l], cb8, p, nb, lt)
        oa, gla_t = _gla_call(zg, _gla_state_in(s_gla), p["gla_norm_w"], tiles["nb_gla"], lt_rec)
        ob, gdn_new = _gdn_call(zd, s_gdn, p["gdn_norm_w"], tiles["nb_gdn"], lt_rec)
        h0 = jnp.concatenate([s5_re.reshape(B, S5_HALF), s5_im.reshape(B, S5_HALF)], axis=1)
        ys, h_new = _s5_call(zs, h0, p, lt_s5)
        x = _merge_call(x, mods[l], oa, ob, ys, p, nb, lt)
        x = _moe_call(x, mods[l], p, final_w, tiles["nb_moe"], tiles["lt_moe"],
                      final_norm=(l == len(params) - 1))
        new.append((_gla_state_out(gla_t), gdn_new, cb_new[:, SUBLANES - (CONV_W - 1):],
                    h_new[:, :S5_HALF].reshape(B, S5_GROUPS, S5_STATE),
                    h_new[:, S5_HALF:].reshape(B, S5_GROUPS, S5_STATE)))
    return x, [jnp.stack([new[l][i] for l in range(len(params))]) for i in range(5)]


def kernel(x_prompt, x_sample, c_prompt, c_sample, state_gla, state_gdn, cache_gdn_conv, state_s5_re, state_s5_im, norm1_w, norm2_w, final_norm_w, w_ada, b_ada, w_in, w_alpha_up, b_alpha, gla_norm_w, conv_w, gdn_a_log, gdn_dt_bias, gdn_norm_w, s5_a_re, s5_a_im, s5_log_dt, s5_b_re, s5_b_im, s5_c_re, s5_c_im, s5_d, w_glu, b_glu, w_br_a, w_br_b, w_br_c, w_gate, b_gate, w_out, w_route_group, b_route_group, w_route_expert, b_route_expert, w_exp_gate, w_exp_up, w_exp_down):
    depth = w_in.shape[0]
    bp, tp, _ = x_prompt.shape
    bs, ts, _ = x_sample.shape
    params = [_prep_layer(l, norm1_w, norm2_w, w_in, w_alpha_up, b_alpha, gla_norm_w, conv_w, gdn_a_log,
                          gdn_dt_bias, gdn_norm_w, s5_a_re, s5_a_im, s5_log_dt, s5_b_re, s5_b_im, s5_c_re,
                          s5_c_im, s5_d, w_glu, b_glu, w_br_a, w_br_b, w_br_c, w_gate, b_gate, w_out,
                          w_route_group, b_route_group, w_route_expert, b_route_expert, w_exp_gate, w_exp_up,
                          w_exp_down) for l in range(depth)]
    final_w = final_norm_w.reshape(1, D_MODEL)

    mod = _ada_call(jnp.concatenate([c_prompt, c_sample], axis=0), w_ada, b_ada)
    mod = mod.reshape(depth, bp + bs, 6, D_MODEL)
    mods_p = [mod[l, :bp] for l in range(depth)]
    mods_s = [mod[l, bp:] for l in range(depth)]

    zeros_p = [(jnp.zeros((bp, GLA_HEADS, GLA_DK, GLA_DV), F32), jnp.zeros((bp, GDN_HEADS, GDN_DK, GDN_DV), F32),
                jnp.zeros((bp, CONV_W - 1, GDN_CONV_CH), F32), jnp.zeros((bp, S5_GROUPS, S5_STATE), F32),
                jnp.zeros((bp, S5_GROUPS, S5_STATE), F32)) for _ in range(depth)]
    y_p, new_p = _trunk(x_prompt, mods_p, params, zeros_p, final_w)

    states_s = [(state_gla[l], state_gdn[l], cache_gdn_conv[l], state_s5_re[l], state_s5_im[l])
                for l in range(depth)]
    y_s, new_s = _trunk(x_sample, mods_s, params, states_s, final_w)
    return (y_p, y_s, *new_p, *new_s)
```

```python
import functools
import math

import jax
import jax.numpy as jnp
from jax import lax
from jax.experimental import pallas as pl
from jax.experimental.pallas import tpu as pltpu

F32 = jnp.float32
BF16 = jnp.bfloat16
HIGHEST = lax.Precision.HIGHEST

D_MODEL = 1024
EPS = 1e-6
CHUNK = 64
BRANCH_W = 512
GLA_HEADS, GLA_DK, GLA_DV, GLA_LOWRANK, GLA_TAU = 4, 64, 128, 16, 16.0
GDN_HEADS, GDN_DK, GDN_DV, CONV_W = 4, 128, 128, 4
S5_GROUP, S5_GROUPS, S5_STATE = 16, 32, 64
GLA_QK, GLA_V, GDN_QK, GDN_V = 256, 512, 512, 512
GDN_CONV_CH = 1536
MOE_GROUPS, EXPERTS_PER_GROUP, N_EXPERTS, D_FF = 4, 4, 16, 256

LANES = 128
SUBLANES = 8
VMEM_LIMIT = 56 * 1024 * 1024
S5_HALF = S5_GROUPS * S5_STATE
SMALL_BETA0, SMALL_A0 = 16, 20
ROUTE_E0 = 4


def _mm(a, b):
    return jnp.dot(a.astype(BF16), b.astype(BF16), preferred_element_type=F32)


def _mm_nt(a, b):
    return lax.dot_general(a.astype(BF16), b.astype(BF16), (((1,), (1,)), ((), ())), preferred_element_type=F32)


def _mm_tn(a, b):
    return lax.dot_general(a.astype(BF16), b.astype(BF16), (((0,), (0,)), ((), ())), preferred_element_type=F32)


def _mm_f32(a, b):
    return jnp.dot(a, b, preferred_element_type=F32, precision=HIGHEST)


def _mm_nt_f32(a, b):
    return lax.dot_general(a, b, (((1,), (1,)), ((), ())), preferred_element_type=F32, precision=HIGHEST)


def _split3(x):
    a = x.astype(BF16)
    r = x - a.astype(F32)
    b = r.astype(BF16)
    c = (r - b.astype(F32)).astype(BF16)
    return a, b, c


def _mm_sel(m01, x):
    a, b, c = _split3(x)
    return _mm(m01, a) + _mm(m01, b) + _mm(m01, c)


def _mm_sel_r(x, m01):
    a, b, c = _split3(x)
    return _mm(a, m01) + _mm(b, m01) + _mm(c, m01)


def _mm_nt_sel(m01, x):
    a, b, c = _split3(x)
    return _mm_nt(m01, a) + _mm_nt(m01, b) + _mm_nt(m01, c)


def _sigmoid(x):
    return 1.0 / (1.0 + jnp.exp(-x))


def _silu(x):
    return x * _sigmoid(x)


def _softplus(x):
    return jnp.maximum(x, 0.0) + jnp.log1p(jnp.exp(-jnp.abs(x)))


def _rms_rows(x, w):
    ms = jnp.mean(x * x, axis=-1, keepdims=True)
    return x * lax.rsqrt(ms + EPS) * w


def _gelu_tanh(x):
    c = math.sqrt(2.0 / math.pi)
    return x * (0.5 * (1.0 + jnp.tanh(c * (x + 0.044715 * (x * x * x)))))


def _const_spec(shape):
    nd = len(shape)
    return pl.BlockSpec(shape, lambda *_: (0,) * nd)


def _time_major_shape(B, T, nb):
    return (T, B * BRANCH_W)


def _time_major_spec(nb, lt):
    return pl.BlockSpec((lt, nb * BRANCH_W), lambda b, t: (t, b))


def _params(sem):
    return pltpu.CompilerParams(dimension_semantics=sem, vmem_limit_bytes=VMEM_LIMIT)


def _ada_kernel(c_ref, w_ref, b_ref, o_ref):
    c = c_ref[...]
    o_ref[0] = _mm(_silu(c), w_ref[0]) + b_ref[0]


def _ada_call(c_all, w_ada, b_ada):
    depth = w_ada.shape[0]
    nrow = c_all.shape[0]
    return pl.pallas_call(
        _ada_kernel,
        grid=(depth, 6),
        in_specs=[pl.BlockSpec((nrow, D_MODEL), lambda l, j: (0, 0)),
                  pl.BlockSpec((1, D_MODEL, D_MODEL), lambda l, j: (l, 0, j)),
                  pl.BlockSpec((1, 1, D_MODEL), lambda l, j: (l, 0, j))],
        out_specs=pl.BlockSpec((1, nrow, D_MODEL), lambda l, j: (l, 0, j)),
        out_shape=jax.ShapeDtypeStruct((depth, nrow, 6 * D_MODEL), F32),
        compiler_params=_params(("parallel", "parallel")),
        name="adaln",
    )(c_all, w_ada, b_ada.reshape(depth, 1, 6 * D_MODEL))


MXU_COLS = 256


def _inproj_kernel(x_ref, mod_ref, nw_ref, wg_ref, wd_ref, ws_ref, wsm_ref, wup_ref, bal_ref, sp_ref, cb_ref,
                   cw_ref, zg_ref, zd_ref, zs_ref, cbo_ref, *xbufs):
    nb, lt, d = x_ref.shape
    span = lt + SUBLANES

    @pl.when(pl.program_id(1) == 0)
    def _():
        for i in range(nb):
            for k, xb in enumerate(xbufs):
                xb[span * i + lt:span * (i + 1), :] = cb_ref[i, :, MXU_COLS * k:MXU_COLS * (k + 1)]

    x = x_ref[...]
    h = _rms_rows(x, nw_ref[...]) * (1.0 + mod_ref[:, 1:2, :]) + mod_ref[:, 0:1, :]
    h2 = h.reshape(nb * lt, d).astype(BF16)

    base = SUBLANES - (CONV_W - 1)
    for i in range(nb):
        for xb in xbufs:
            xb[span * i:span * i + SUBLANES, :] = xb[span * i + lt:span * (i + 1), :]
    def gla_block(j):
        dst = MXU_COLS * j if j < 2 else MXU_COLS * (j + 1)
        zg_ref[:, :, dst:dst + MXU_COLS] = _mm(h2, wg_ref[:, MXU_COLS * j:MXU_COLS * (j + 1)]).reshape(
            nb, lt, MXU_COLS)

    def gate_block(j):
        cs = slice(GDN_CONV_CH + MXU_COLS * j, GDN_CONV_CH + MXU_COLS * (j + 1))
        zd_ref[:, :, cs] = _mm(h2, wd_ref[:, cs]).reshape(nb, lt, MXU_COLS)

    def s5_block(j):
        zs = _mm(h2, ws_ref[:, MXU_COLS * j:MXU_COLS * (j + 1)])
        for i in range(nb):
            zs_ref[:, BRANCH_W * i + MXU_COLS * j:BRANCH_W * i + MXU_COLS * (j + 1)] = zs[lt * i:lt * (i + 1), :]

    def small_block(_):
        zsm = _mm(h2, wsm_ref[...])
        la_pre = _mm(zsm, wup_ref[...]) + bal_ref[...]
        log_a = (jnp.minimum(la_pre, 0.0) - jnp.log1p(jnp.exp(-jnp.abs(la_pre)))) * (1.0 / GLA_TAU)
        zg_ref[:, :, 512:768] = log_a.reshape(nb, lt, 256)
        lane = lax.broadcasted_iota(jnp.int32, zsm.shape, 1)
        beta = _sigmoid(zsm)
        log_g = -sp_ref[0:1, :] * _softplus(zsm + sp_ref[1:2, :])
        is_beta = (lane >= SMALL_BETA0) & (lane < SMALL_A0)
        is_a = (lane >= SMALL_A0) & (lane < SMALL_A0 + GDN_HEADS)
        small = jnp.where(is_beta, beta, jnp.where(is_a, log_g, 0.0))
        zd_ref[:, :, 2048:2176] = small.reshape(nb, lt, LANES)

    fillers = [[(gla_block, 0), (gate_block, 0)], [(gla_block, 1), (gate_block, 1)],
               [(gla_block, 2), (s5_block, 0)], [(gla_block, 3), (s5_block, 1)],
               [(gla_block, 4), (small_block, 0)], [(gla_block, 5)]]
    def conv_matmul(mblk):
        zd = _mm(h2, wd_ref[:, MXU_COLS * mblk:MXU_COLS * (mblk + 1)])
        for i in range(nb):
            xbufs[mblk][span * i + SUBLANES:span * (i + 1), :] = zd[lt * i:lt * (i + 1), :]
        for fn, arg in fillers[mblk]:
            fn(arg)

    n_blk = GDN_CONV_CH // MXU_COLS
    conv_matmul(0)
    for mblk in range(n_blk):
        if mblk + 1 < n_blk:
            conv_matmul(mblk + 1)
        xb = xbufs[mblk]
        for i in range(nb):
            r0 = span * i
            for cblk in range(MXU_COLS * mblk // LANES, MXU_COLS * (mblk + 1) // LANES):
                cs = slice(LANES * cblk, LANES * (cblk + 1))
                ls = slice(LANES * cblk - MXU_COLS * mblk, LANES * (cblk + 1) - MXU_COLS * mblk)
                conv = xb[r0 + base:r0 + base + lt, ls] * cw_ref[0:1, cs]
                for j in range(1, CONV_W):
                    conv = conv + xb[r0 + base + j:r0 + base + j + lt, ls] * cw_ref[j:j + 1, cs]
                act = _silu(conv)
                if cblk < 2 * GDN_HEADS:
                    scale = GDN_DK ** -0.5 if cblk < GDN_HEADS else 1.0
                    ss = jnp.sum(act * act, axis=-1, keepdims=True)
                    act = act * (lax.rsqrt(ss + EPS) * scale)
                zd_ref[i, :, cs] = act
    for i in range(nb):
        for k, xb in enumerate(xbufs):
            cbo_ref[i, :, MXU_COLS * k:MXU_COLS * (k + 1)] = xb[span * i + lt:span * (i + 1), :]


def _inproj_call(x, mod, cb8, p, nb, lt):
    B, T, D = x.shape
    assert nb == 1 or T == lt, "several sequences per step only when each is a single time block"
    grid = (B // nb, T // lt)
    tok = lambda w: pl.BlockSpec((nb, lt, w), lambda b, t: (b, t, 0))
    cb_spec = pl.BlockSpec((nb, SUBLANES, GDN_CONV_CH), lambda b, t: (b, 0, 0))
    return pl.pallas_call(
        _inproj_kernel,
        grid=grid,
        in_specs=[tok(D),
                  pl.BlockSpec((nb, 6, D), lambda b, t: (b, 0, 0)),
                  _const_spec((1, D)),
                  _const_spec(p["w_gla"].shape), _const_spec(p["w_gdn"].shape), _const_spec(p["w_s5"].shape),
                  _const_spec(p["w_small"].shape), _const_spec(p["w_up"].shape), _const_spec((1, GLA_QK)),
                  _const_spec((2, LANES)), cb_spec, _const_spec((CONV_W, GDN_CONV_CH))],
        out_specs=[tok(1792), tok(2176), _time_major_spec(nb, lt), cb_spec],
        out_shape=[jax.ShapeDtypeStruct((B, T, 1792), F32),
                   jax.ShapeDtypeStruct((B, T, 2176), F32),
                   jax.ShapeDtypeStruct(_time_major_shape(B, T, nb), F32),
                   jax.ShapeDtypeStruct((B, SUBLANES, GDN_CONV_CH), F32)],
        scratch_shapes=[pltpu.VMEM((nb * (lt + SUBLANES), MXU_COLS), F32)] * (GDN_CONV_CH // MXU_COLS),
        compiler_params=_params(("parallel", "arbitrary")),
        name="inproj",
    )(x, mod, p["norm1_w"], p["w_gla"], p["w_gdn"], p["w_s5"], p["w_small"], p["w_up"], p["b_alpha"],
      p["gdn_small"], cb8, p["conv_w"])


def _gla_stages(z_ref, nw_ref, o_ref, s_ref):
    L = CHUNK
    nb, lt, _ = z_ref.shape
    n_chunks = lt // L
    n_rows = nb * lt
    row = lax.broadcasted_iota(jnp.int32, (L, L), 0)
    col = lax.broadcasted_iota(jnp.int32, (L, L), 1)
    causal = row >= col
    lane = lax.broadcasted_iota(jnp.int32, (L, LANES), 1)
    own_half = (lane < GLA_DK, lane >= GLA_DK)
    nw = nw_ref[...]
    brow = lax.broadcasted_iota(jnp.int32, (n_rows, n_rows), 0)
    bcol = lax.broadcasted_iota(jnp.int32, (n_rows, n_rows), 1)
    shift = L.bit_length() - 1
    chunk_tril = ((brow >= bcol) & ((brow >> shift) == (bcol >> shift))).astype(BF16)
    b_all = _mm_sel(chunk_tril, z_ref[:, :, 512:768].reshape(n_rows, GLA_QK))

    def zcols(c, lo, width):
        i, cc = divmod(c, n_chunks)
        return z_ref[i, L * cc:L * (cc + 1), lo:lo + width]

    all_chunks = range(nb * n_chunks)
    probs = [(c, h) for c in all_chunks for h in range(GLA_HEADS)]
    rows_of = lambda c: slice(L * c, L * (c + 1))
    pair_of = lambda h: slice(LANES * (h // 2), LANES * (h // 2) + LANES)
    qi_, ki_, qs_, kl_, dl_ = {}, {}, {}, {}, {}
    for c in all_chunks:
        q = zcols(c, 0, GLA_QK)
        k = zcols(c, GLA_QK, GLA_QK)
        b = b_all[rows_of(c), :]
        b_mid = b[L // 2 - 1:L // 2, :]
        b_last = b[L - 1:L, :]
        qi_[c] = q * jnp.exp(b - b_mid)
        ki_[c] = k * jnp.exp(b_mid - b)
        qs_[c] = q * jnp.exp(b)
        kl_[c] = k * jnp.exp(b_last - b)
        dl_[c] = jnp.exp(b_last)
    v_ = {p: zcols(p[0], 768 + GLA_DV * p[1], GLA_DV) for p in probs}
    att_, kv_, av_ = {}, {}, {}

    def scores():
        for p in probs:
            ki_h = jnp.where(own_half[p[1] % 2], ki_[p[0]][:, pair_of(p[1])], 0.0)
            att_[p] = jnp.where(causal, _mm_nt(qi_[p[0]][:, pair_of(p[1])], ki_h), 0.0)

    def increments():
        for p in probs:
            kv_[p] = _mm_tn(v_[p], jnp.where(own_half[p[1] % 2], kl_[p[0]][:, pair_of(p[1])], 0.0))

    def intra():
        for p in probs:
            av_[p] = _mm(att_[p], v_[p])

    def outputs():
        st_ = {}
        for i in range(nb):
            for h in range(GLA_HEADS):
                st = s_ref[i, h]
                for cc in range(n_chunks):
                    c = n_chunks * i + cc
                    st_[(c, h)] = st
                    st = st * dl_[c][:, pair_of(h)] + kv_[(c, h)]
                s_ref[i, h] = st
        for p in probs:
            c, h = p
            i, cc = divmod(c, n_chunks)
            o_h = av_[p] + _mm_nt(qs_[c][:, pair_of(h)], st_[p])
            gate = zcols(c, 1280 + GLA_DV * h, GLA_DV)
            o_ref[i, L * cc:L * (cc + 1), GLA_DV * h:GLA_DV * (h + 1)] = _rms_rows(o_h, nw) * _silu(gate)

    return [scores, increments, intra, outputs]


def _gla_kernel(z_ref, s0_ref, nw_ref, o_ref, s_ref):
    @pl.when(pl.program_id(1) == 0)
    def _():
        s_ref[...] = s0_ref[...]

    for stage in _gla_stages(z_ref, nw_ref, o_ref, s_ref):
        stage()


def _gla_call(zg, s0t, norm_w, nb, lt):
    B, T, _ = zg.shape
    st_spec = pl.BlockSpec((nb, GLA_HEADS, GLA_DV, LANES), lambda b, t: (b, 0, 0, 0))
    return pl.pallas_call(
        _gla_kernel,
        grid=(B // nb, T // lt),
        in_specs=[pl.BlockSpec((nb, lt, 1792), lambda b, t: (b, t, 0)), st_spec, _const_spec((1, GLA_DV))],
        out_specs=[pl.BlockSpec((nb, lt, GLA_V), lambda b, t: (b, t, 0)), st_spec],
        out_shape=[jax.ShapeDtypeStruct((B, T, GLA_V), F32),
                   jax.ShapeDtypeStruct((B, GLA_HEADS, GLA_DV, LANES), F32)],
        compiler_params=_params(("parallel", "arbitrary")),
        name="gla",
    )(zg, s0t, norm_w)


def _gdn_kernel(z_ref, s0_ref, nw_ref, o_ref, s_ref, bfull_ref, gfull_ref, grow_ref, glast_ref):
    @pl.when(pl.program_id(1) == 0)
    def _():
        s_ref[...] = s0_ref[...]

    _gdn_body(z_ref, nw_ref, o_ref, s_ref, bfull_ref, gfull_ref, grow_ref, glast_ref, ())


def _mixers_kernel(zg_ref, zd_ref, sa0_ref, sb0_ref, nwa_ref, nwb_ref, oa_ref, ob_ref, sa_ref, sb_ref, bfull_ref,
                   gfull_ref, grow_ref, glast_ref):
    @pl.when(pl.program_id(1) == 0)
    def _():
        sa_ref[...] = sa0_ref[...]
        sb_ref[...] = sb0_ref[...]

    _gdn_body(zd_ref, nwb_ref, ob_ref, sb_ref, bfull_ref, gfull_ref, grow_ref, glast_ref,
              _gla_stages(zg_ref, nwa_ref, oa_ref, sa_ref))


def _gdn_body(z_ref, nw_ref, o_ref, s_ref, bfull_ref, gfull_ref, grow_ref, glast_ref, fillers):
    L = CHUNK
    nb, lt, _ = z_ref.shape
    n_chunks = lt // L
    n_rows = nb * lt
    fillers = list(fillers)
    row = lax.broadcasted_iota(jnp.int32, (L, L), 0)
    col = lax.broadcasted_iota(jnp.int32, (L, L), 1)
    causal = row >= col
    strict = row > col
    eye = (row == col).astype(F32)
    diag16 = (row >> 4) == (col >> 4)
    diag32 = (row >> 5) == (col >> 5)
    off32 = diag32 & jnp.logical_not(diag16)
    off64 = jnp.logical_not(diag32)
    sel_r = lax.broadcasted_iota(jnp.int32, (SUBLANES, LANES), 0)
    sel_c = lax.broadcasted_iota(jnp.int32, (SUBLANES, LANES), 1)
    sel = (sel_c == sel_r + SMALL_A0).astype(BF16)
    brow = lax.broadcasted_iota(jnp.int32, (n_rows, n_rows), 0)
    bcol = lax.broadcasted_iota(jnp.int32, (n_rows, n_rows), 1)
    shift = L.bit_length() - 1
    chunk_tril = ((brow >= bcol) & ((brow >> shift) == (bcol >> shift))).astype(BF16)
    small_all = z_ref[:, :, 2048:2176].reshape(n_rows, LANES)
    gs_all = _mm_sel(chunk_tril, small_all)
    grow_all = _mm_nt_sel(sel, gs_all)
    e_r = lax.broadcasted_iota(jnp.int32, (LANES, GDN_HEADS * LANES), 0)
    e_h = lax.broadcasted_iota(jnp.int32, (LANES, GDN_HEADS * LANES), 1) >> (LANES.bit_length() - 1)
    bfull_ref[...] = _mm_sel_r(small_all, (e_r == e_h + SMALL_BETA0).astype(BF16))
    gfull = _mm_sel_r(gs_all, (e_r == e_h + SMALL_A0).astype(BF16))
    gfull_ref[...] = gfull
    for c in range(nb * n_chunks):
        grow_ref[c] = grow_all[:, L * c:L * (c + 1)]
        glast_ref[c:c + 1, :] = gfull[L * (c + 1) - 1:L * (c + 1), :]
    nw = nw_ref[...]

    probs = [(c, h) for c in range(nb * n_chunks) for h in range(GDN_HEADS)]
    rows_of = lambda c: slice(L * c, L * (c + 1))
    head_of = lambda h: slice(LANES * h, LANES * (h + 1))

    def zcols(c, lo, width):
        i, cc = divmod(c, n_chunks)
        return z_ref[i, L * cc:L * (cc + 1), lo:lo + width]

    q_ = {p: zcols(p[0], GDN_DK * p[1], GDN_DK) for p in probs}
    k_ = {p: zcols(p[0], GDN_QK + GDN_DK * p[1], GDN_DK) for p in probs}
    beta_ = {p: bfull_ref[rows_of(p[0]), head_of(p[1])] for p in probs}
    g_ = {p: gfull_ref[rows_of(p[0]), head_of(p[1])] for p in probs}
    dec_ = {p: jnp.exp(jnp.where(causal, g_[p][:, 0:L] - grow_ref[p[0]][p[1]:p[1] + 1, :], -jnp.inf))
            for p in probs}
    kk_ = {p: _mm_nt(k_[p], k_[p]) for p in probs}
    qk_ = {p: _mm_nt(q_[p], k_[p]) for p in probs}
    a_ = {p: jnp.where(strict, beta_[p][:, 0:L] * kk_[p] * dec_[p], 0.0) for p in probs}
    qk_ = {p: jnp.where(causal, qk_[p] * dec_[p], 0.0) for p in probs}
    pw_ = {p: jnp.where(diag16, -a_[p], 0.0) for p in probs}
    tinv_ = {p: eye + pw_[p] for p in probs}
    for _ in range(3):
        pw_ = {p: _mm(pw_[p], pw_[p]) for p in probs}
        tinv_ = {p: tinv_[p] + _mm(tinv_[p], pw_[p]) for p in probs}
    for off in (off32, off64):
        te_ = {p: _mm(tinv_[p], jnp.where(off, a_[p], 0.0)) for p in probs}
        tinv_ = {p: tinv_[p] - _mm(te_[p], tinv_[p]) for p in probs}
    eg_ = {p: jnp.exp(g_[p]) for p in probs}
    sol_ = {}
    for p in probs:
        v_p = zcols(p[0], 2 * GDN_QK + GDN_DV * p[1], GDN_DV)
        rhs = jnp.concatenate([beta_[p] * v_p, (beta_[p] * eg_[p]) * k_[p]], axis=-1)
        sol_[p] = _mm(tinv_[p], rhs)

    st_ = {(i, h): s_ref[i, h] for i in range(nb) for h in range(GDN_HEADS)}
    for cc in range(n_chunks):
        heads = [(i, n_chunks * i + cc, h) for i in range(nb) for h in range(GDN_HEADS)]
        u_, o_, new_ = {}, {}, {}
        for i, c, h in heads:
            u_[(c, h)] = sol_[(c, h)][:, 0:GDN_DV] - _mm(sol_[(c, h)][:, GDN_DV:], st_[(i, h)])
        if fillers:
            fillers.pop(0)()
        for i, c, h in heads:
            o_[(c, h)] = _mm(qk_[(c, h)], u_[(c, h)]) + _mm(q_[(c, h)] * eg_[(c, h)], st_[(i, h)])
        for i, c, h in heads:
            g_last = glast_ref[c:c + 1, head_of(h)]
            new_[(i, h)] = (st_[(i, h)] * jnp.exp(g_last)
                            + _mm_tn(k_[(c, h)] * jnp.exp(g_last - g_[(c, h)]), u_[(c, h)]))
        st_ = new_
        for i, c, h in heads:
            gate = z_ref[i, L * cc:L * (cc + 1), GDN_CONV_CH + GDN_DV * h:GDN_CONV_CH + GDN_DV * (h + 1)]
            o_ref[i, L * cc:L * (cc + 1), GDN_DV * h:GDN_DV * (h + 1)] = _rms_rows(o_[(c, h)], nw) * _silu(gate)
    for i in range(nb):
        for h in range(GDN_HEADS):
            s_ref[i, h] = st_[(i, h)]
    for rest in fillers:
        rest()


def _mixers_call(zg, zd, sa0, sb0, nwa, nwb, nb, lt):
    B, T, _ = zd.shape
    n_rows = nb * lt
    last_rows = -(-(n_rows // CHUNK) // SUBLANES) * SUBLANES
    sa_spec = pl.BlockSpec((nb, GLA_HEADS, GLA_DV, LANES), lambda b, t: (b, 0, 0, 0))
    sb_spec = pl.BlockSpec((nb, GDN_HEADS, GDN_DK, GDN_DV), lambda b, t: (b, 0, 0, 0))
    tok = lambda w: pl.BlockSpec((nb, lt, w), lambda b, t: (b, t, 0))
    return pl.pallas_call(
        _mixers_kernel,
        grid=(B // nb, T // lt),
        in_specs=[tok(1792), tok(2176), sa_spec, sb_spec, _const_spec((1, GLA_DV)), _const_spec((1, GDN_DV))],
        out_specs=[tok(GLA_V), tok(GDN_V), sa_spec, sb_spec],
        out_shape=[jax.ShapeDtypeStruct((B, T, GLA_V), F32), jax.ShapeDtypeStruct((B, T, GDN_V), F32),
                   jax.ShapeDtypeStruct((B, GLA_HEADS, GLA_DV, LANES), F32),
                   jax.ShapeDtypeStruct((B, GDN_HEADS, GDN_DK, GDN_DV), F32)],
        scratch_shapes=[pltpu.VMEM((n_rows, GDN_HEADS * LANES), F32), pltpu.VMEM((n_rows, GDN_HEADS * LANES), F32),
                        pltpu.VMEM((n_rows // CHUNK, SUBLANES, CHUNK), F32),
                        pltpu.VMEM((last_rows, GDN_HEADS * LANES), F32)],
        compiler_params=_params(("parallel", "arbitrary")),
        name="mixers",
    )(zg, zd, sa0, sb0, nwa, nwb)


def _gdn_call(zd, s0, norm_w, nb, lt):
    B, T, _ = zd.shape
    n_rows = nb * lt
    last_rows = -(-(n_rows // CHUNK) // SUBLANES) * SUBLANES
    st_spec = pl.BlockSpec((nb, GDN_HEADS, GDN_DK, GDN_DV), lambda b, t: (b, 0, 0, 0))
    return pl.pallas_call(
        _gdn_kernel,
        grid=(B // nb, T // lt),
        in_specs=[pl.BlockSpec((nb, lt, 2176), lambda b, t: (b, t, 0)), st_spec, _const_spec((1, GDN_DV))],
        out_specs=[pl.BlockSpec((nb, lt, GDN_V), lambda b, t: (b, t, 0)), st_spec],
        out_shape=[jax.ShapeDtypeStruct((B, T, GDN_V), F32),
                   jax.ShapeDtypeStruct((B, GDN_HEADS, GDN_DK, GDN_DV), F32)],
        scratch_shapes=[pltpu.VMEM((n_rows, GDN_HEADS * LANES), F32), pltpu.VMEM((n_rows, GDN_HEADS * LANES), F32),
                        pltpu.VMEM((n_rows // CHUNK, SUBLANES, CHUNK), F32),
                        pltpu.VMEM((last_rows, GDN_HEADS * LANES), F32)],
        compiler_params=_params(("parallel", "arbitrary")),
        name="gdn",
    )(zd, s0, norm_w)


S5_LANE_CHUNK = 512


def _s5_kernel(u_ref, h0_ref, bb_ref, cc_ref, lam_ref, d_ref, y_ref, h_ref, bu_ref, *, lt):
    nb = SUBLANES

    @pl.when(pl.program_id(1) == 0)
    def _():
        h_ref[...] = h0_ref[...]

    u = u_ref[...].reshape(lt * nb, BRANCH_W)
    ch, sh = BRANCH_W // 2, S5_HALF // 2
    for part in range(2):
        for hb in range(2):
            bu_ref[:, part * S5_HALF + hb * sh:part * S5_HALF + (hb + 1) * sh] = _mm(
                u[:, hb * ch:(hb + 1) * ch], bb_ref[2 * part + hb])
    for j in range(S5_HALF // S5_LANE_CHUNK):
        re = slice(S5_LANE_CHUNK * j, S5_LANE_CHUNK * (j + 1))
        im = slice(S5_HALF + S5_LANE_CHUNK * j, S5_HALF + S5_LANE_CHUNK * (j + 1))
        lr = jnp.broadcast_to(lam_ref[0:1, re], (nb, S5_LANE_CHUNK))
        li = jnp.broadcast_to(lam_ref[1:2, re], (nb, S5_LANE_CHUNK))

        hr, hi = h_ref[:, re], h_ref[:, im]
        for tt in range(lt):
            rows = slice(tt * nb, (tt + 1) * nb)
            hr, hi = lr * hr - li * hi + bu_ref[rows, re], lr * hi + li * hr + bu_ref[rows, im]
            bu_ref[rows, re] = hr
            bu_ref[rows, im] = hi
        h_ref[:, re] = hr
        h_ref[:, im] = hi
    ys = [_mm(bu_ref[:, hb * sh:(hb + 1) * sh], cc_ref[hb])
          + _mm(bu_ref[:, S5_HALF + hb * sh:S5_HALF + (hb + 1) * sh], cc_ref[2 + hb]) for hb in range(2)]
    y = jnp.concatenate(ys, axis=-1) + d_ref[...] * u
    y_ref[...] = y.reshape(lt, nb, BRANCH_W)


def _s5_call(zs, h0, p, lt):
    T = zs.shape[0]
    B = h0.shape[0]
    nb = SUBLANES
    h_spec = pl.BlockSpec((nb, 2 * S5_HALF), lambda b, t: (b, 0))
    tok_spec = pl.BlockSpec((lt, None, nb, BRANCH_W), lambda b, t: (t, b, 0, 0))
    y, h = pl.pallas_call(
        functools.partial(_s5_kernel, lt=lt),
        grid=(B // nb, T // lt),
        in_specs=[tok_spec, h_spec,
                  _const_spec((4, BRANCH_W // 2, S5_HALF // 2)), _const_spec((4, S5_HALF // 2, BRANCH_W // 2)),
                  _const_spec((2, S5_HALF)), _const_spec((1, BRANCH_W))],
        out_specs=[tok_spec, h_spec],
        out_shape=[jax.ShapeDtypeStruct((T, B // nb, nb, BRANCH_W), F32),
                   jax.ShapeDtypeStruct((B, 2 * S5_HALF), F32)],
        scratch_shapes=[pltpu.VMEM((lt * nb, 2 * S5_HALF), F32)],
        compiler_params=_params(("parallel", "arbitrary")),
        name="s5",
    )(zs.reshape(T, B // nb, nb, BRANCH_W), h0, p["s5_bblk"], p["s5_cblk"], p["s5_lam"], p["s5_d"])
    return y.reshape(zs.shape), h


def _merge_kernel(x_ref, mod_ref, nw_ref, oa_ref, ob_ref, ys_ref, wgate_ref, bgate_ref, wa_ref, wb_ref, wc_ref,
                  wglu_ref, bglu_ref, wout_ref, o_ref):
    nb, lt, d = x_ref.shape
    n = nb * lt
    x = x_ref[...]
    h = _rms_rows(x, nw_ref[...]) * (1.0 + mod_ref[:, 1:2, :]) + mod_ref[:, 0:1, :]
    gate = _sigmoid(_mm(h.reshape(n, d), wgate_ref[...]) + bgate_ref[...])
    y_a = _mm(oa_ref[...].reshape(n, BRANCH_W), wa_ref[...])
    y_b = _mm(ob_ref[...].reshape(n, BRANCH_W), wb_ref[...])
    ys = [ys_ref[:, BRANCH_W * i:BRANCH_W * (i + 1)] for i in range(nb)]
    yc = _gelu_tanh(ys[0] if nb == 1 else jnp.concatenate(ys, axis=0))
    yc = yc * _sigmoid(_mm(yc, wglu_ref[...]) + bglu_ref[...])
    y_c = _mm(yc, wc_ref[...])
    mix = gate[:, 0:d] * y_a + gate[:, d:2 * d] * y_b + gate[:, 2 * d:3 * d] * y_c
    m = _mm(mix, wout_ref[...])
    o_ref[...] = x + mod_ref[:, 2:3, :] * m.reshape(nb, lt, d)


def _merge_call(x, mod, oa, ob, ys, p, nb, lt):
    B, T, D = x.shape
    tok = lambda w: pl.BlockSpec((nb, lt, w), lambda b, t: (b, t, 0))
    return pl.pallas_call(
        _merge_kernel,
        grid=(B // nb, T // lt),
        in_specs=[tok(D), pl.BlockSpec((nb, 6, D), lambda b, t: (b, 0, 0)), _const_spec((1, D)),
                  tok(BRANCH_W), tok(BRANCH_W), _time_major_spec(nb, lt),
                  _const_spec((D, 3 * D)), _const_spec((1, 3 * D)),
                  _const_spec((BRANCH_W, D)), _const_spec((BRANCH_W, D)), _const_spec((BRANCH_W, D)),
                  _const_spec((BRANCH_W, BRANCH_W)), _const_spec((1, BRANCH_W)), _const_spec((D, D))],
        out_specs=tok(D),
        out_shape=jax.ShapeDtypeStruct((B, T, D), F32),
        compiler_params=_params(("parallel", "parallel")),
        name="merge",
    )(x, mod, p["norm1_w"], oa, ob, ys, p["w_gate"], p["b_gate"], p["w_br_a"], p["w_br_b"], p["w_br_c"],
      p["w_glu"], p["b_glu"], p["w_out"])


MOE_ROWS = 160


def _moe_kernel(x_ref, mod_ref, nw_ref, wr_ref, br_ref, ex_ref, wg_ref, wu_ref, wd_ref, fw_ref, o_ref,
                hn_ref, comb_ref, krow_ref, acc_ref, *, final_norm):
    nb, lt, d = x_ref.shape
    R = MOE_ROWS
    gw = EXPERTS_PER_GROUP * D_FF
    halves = hn_ref.shape[0]
    if nb >= halves:
        hb, hl = nb // halves, lt
    else:
        hb, hl = nb, lt // halves
    n = hb * hl
    row_pos = lax.broadcasted_iota(jnp.int32, (R, n), 0).astype(F32)
    lane = lax.broadcasted_iota(jnp.int32, (n, LANES), 1)
    trow = lax.broadcasted_iota(jnp.int32, (n, n), 0)
    tcol = lax.broadcasted_iota(jnp.int32, (n, n), 1)
    before = (trow > tcol).astype(BF16)
    sel_r = lax.broadcasted_iota(jnp.int32, (SUBLANES, LANES), 0)
    sel_c = lax.broadcasted_iota(jnp.int32, (SUBLANES, LANES), 1)
    pick = (sel_r == sel_c).astype(BF16)
    neg = -jnp.inf

    def xs(k):
        return x_ref[hb * k:hb * (k + 1)] if nb >= halves else x_ref[:, hl * k:hl * (k + 1), :]

    def mods(k, row):
        m = mod_ref[hb * k:hb * (k + 1)] if nb >= halves else mod_ref[...]
        return m[:, row:row + 1, :]

    def route(k):
        hn = (_rms_rows(xs(k), nw_ref[...]) * (1.0 + mods(k, 4)) + mods(k, 3)).reshape(n, d)
        hn_hi = hn.astype(BF16)
        hn_ref[k] = hn_hi
        hn_lo = (hn - hn_hi.astype(F32)).astype(BF16)
        logits = _mm(hn_hi, wr_ref[0]) + (_mm(hn_hi, wr_ref[1]) + _mm(hn_lo, wr_ref[0])) + br_ref[...]
        gl = jnp.where(lane < MOE_GROUPS, logits, neg)
        gmax = jnp.max(gl, axis=-1, keepdims=True)
        g_sel = jnp.min(jnp.where(gl == gmax, lane, LANES), axis=-1, keepdims=True)
        p_sel = 1.0 / jnp.sum(jnp.exp(gl - gmax), axis=-1, keepdims=True)
        e_id = lane - ROUTE_E0
        in_group = (e_id >= 0) & (e_id < N_EXPERTS) & ((e_id >> 2) == g_sel)
        el = jnp.where(in_group, logits, neg)
        m1 = jnp.max(el, axis=-1, keepdims=True)
        i1 = jnp.min(jnp.where(el == m1, lane, LANES), axis=-1, keepdims=True)
        el2 = jnp.where(lane == i1, neg, el)
        m2 = jnp.max(el2, axis=-1, keepdims=True)
        i2 = jnp.min(jnp.where(el2 == m2, lane, LANES), axis=-1, keepdims=True)
        e2 = jnp.exp(m2 - m1)
        w1 = p_sel / (1.0 + e2)
        w2 = p_sel * e2 / (1.0 + e2)
        comb_ref[k] = jnp.where(lane == i1, w1, jnp.where(lane == i2, w2, 0.0)).astype(BF16)
        member = lane == g_sel
        rank = _mm(before, member.astype(BF16))
        key = jnp.where(member, rank, -1.0)
        krow_ref[k] = _mm_nt_sel(pick, key)
        return [(jnp.max(key[:, g:g + 1]).astype(jnp.int32) + R) // R for g in range(MOE_GROUPS)]

    def one_pass(k, g, s):
        base = float(s * R) if isinstance(s, int) else (s * R).astype(F32)
        gather = (krow_ref[k, g:g + 1, :] - base == row_pos).astype(BF16)
        xg = _mm(gather, hn_ref[k])
        cg = _mm(_mm(gather, comb_ref[k]), ex_ref[:, gw * g:gw * (g + 1)])
        experts = range(EXPERTS_PER_GROUP * g, EXPERTS_PER_GROUP * (g + 1))
        he = jnp.concatenate([_silu(_mm(xg, wg_ref[e])) * _mm(xg, wu_ref[e]) for e in experts], axis=-1)
        yg = _mm(he * cg, wd_ref[g])
        return _mm_tn(gather, yg)

    n_pass = [route(k) for k in range(halves)]
    for k in range(halves):
        acc = one_pass(k, 0, 0)
        for g in range(1, MOE_GROUPS):
            acc = acc + one_pass(k, g, 0)
        acc_ref[k] = acc
    for k in range(halves):
        for g in range(MOE_GROUPS):
            def extra(s, carry, k=k, g=g):
                acc_ref[k] += one_pass(k, g, s)
                return carry

            lax.fori_loop(1, n_pass[k][g], extra, 0)
    for k in range(halves):
        out = xs(k) + mods(k, 5) * acc_ref[k].reshape(hb, hl, d)
        if final_norm:
            out = _rms_rows(out, fw_ref[...])
        if nb >= halves:
            o_ref[hb * k:hb * (k + 1)] = out
        else:
            o_ref[:, hl * k:hl * (k + 1), :] = out


def _moe_call(x, mod, p, final_w, nb, lt, final_norm):
    B, T, D = x.shape
    halves = MOE_HALVES if (nb * lt) % (MOE_HALVES * SUBLANES * 2) == 0 and max(nb, lt) % MOE_HALVES == 0 else 1
    n = nb * lt // halves
    gw = EXPERTS_PER_GROUP * D_FF
    tok = pl.BlockSpec((nb, lt, D), lambda b, t: (b, t, 0))
    resident = lambda shape: pl.BlockSpec(shape, lambda *_: (0,) * len(shape), pipeline_mode=pl.Buffered(1))
    return pl.pallas_call(
        functools.partial(_moe_kernel, final_norm=final_norm),
        grid=(B // nb, T // lt),
        in_specs=[tok, pl.BlockSpec((nb, 6, D), lambda b, t: (b, 0, 0)), _const_spec((1, D)),
                  _const_spec((2, D, LANES)), _const_spec((1, LANES)), _const_spec((LANES, N_EXPERTS * D_FF)),
                  resident((N_EXPERTS, D, D_FF)), resident((N_EXPERTS, D, D_FF)), resident((MOE_GROUPS, gw, D)),
                  _const_spec((1, D))],
        out_specs=tok,
        out_shape=jax.ShapeDtypeStruct((B, T, D), F32),
        scratch_shapes=[pltpu.VMEM((halves, n, D), BF16), pltpu.VMEM((halves, n, LANES), BF16),
                        pltpu.VMEM((halves, SUBLANES, n), F32), pltpu.VMEM((halves, n, D), F32)],
        compiler_params=_params(("parallel", "parallel")),
        name="moe",
    )(x, mod, p["norm2_w"], p["w_route"], p["b_route"], p["route_expand"], p["w_eg"], p["w_eu"], p["w_ed"],
      final_w)


def _prep_layer(l, norm1_w, norm2_w, w_in, w_alpha_up, b_alpha, gla_norm_w, conv_w, gdn_a_log, gdn_dt_bias,
                gdn_norm_w, s5_a_re, s5_a_im, s5_log_dt, s5_b_re, s5_b_im, s5_c_re, s5_c_im, s5_d, w_glu, b_glu,
                w_br_a, w_br_b, w_br_c, w_gate, b_gate, w_out, w_route_group, b_route_group, w_route_expert,
                b_route_expert, w_exp_gate, w_exp_up, w_exp_down):
    D = D_MODEL
    w = w_in[l]
    o = 0
    cols = {}
    for name, size in (("qa", GLA_QK), ("ka", GLA_QK), ("va", GLA_V), ("ga", GLA_V), ("aa", GLA_LOWRANK),
                       ("qkv", GDN_CONV_CH), ("beta", GDN_HEADS), ("a", GDN_HEADS), ("gb", GDN_V),
                       ("u", BRANCH_W)):
        cols[name] = w[:, o:o + size]
        o += size
    p = {}
    p["norm1_w"] = norm1_w[l].reshape(1, D)
    p["norm2_w"] = norm2_w[l].reshape(1, D)
    p["w_gla"] = jnp.concatenate([cols["qa"] * (GLA_DK ** -0.5), cols["ka"], cols["va"], cols["ga"]],
                                 axis=1).astype(BF16)
    p["w_gdn"] = jnp.concatenate([cols["qkv"], cols["gb"]], axis=1).astype(BF16)
    p["w_s5"] = cols["u"].astype(BF16)
    assert SMALL_BETA0 == GLA_LOWRANK and SMALL_A0 == SMALL_BETA0 + GDN_HEADS
    used = SMALL_A0 + GDN_HEADS
    p["w_small"] = jnp.concatenate([cols["aa"], cols["beta"], cols["a"], jnp.zeros((D, LANES - used), F32)],
                                   axis=1).astype(BF16)
    p["w_up"] = jnp.pad(w_alpha_up[l], ((0, LANES - GLA_LOWRANK), (0, 0))).astype(BF16)
    p["b_alpha"] = b_alpha[l].reshape(1, GLA_QK)
    p["gdn_small"] = jnp.pad(jnp.stack([jnp.exp(gdn_a_log[l]), gdn_dt_bias[l]]),
                             ((0, 0), (SMALL_A0, LANES - used)))
    p["gla_norm_w"] = gla_norm_w[l].reshape(1, GLA_DV)
    p["gdn_norm_w"] = gdn_norm_w[l].reshape(1, GDN_DV)
    p["conv_w"] = conv_w[l]

    a_re, a_im = s5_a_re[l], s5_a_im[l]
    dt = jnp.exp(s5_log_dt[l])[:, None]
    mag = jnp.exp(a_re * dt)
    lr, li = mag * jnp.cos(a_im * dt), mag * jnp.sin(a_im * dt)
    den = a_re * a_re + a_im * a_im
    f_re = ((lr - 1.0) * a_re + li * a_im) / den
    f_im = (li * a_re - (lr - 1.0) * a_im) / den
    bb_re = f_re[..., None] * s5_b_re[l] - f_im[..., None] * s5_b_im[l]
    bb_im = f_re[..., None] * s5_b_im[l] + f_im[..., None] * s5_b_re[l]
    gh = S5_GROUPS // 2
    eye_h = jnp.eye(gh, dtype=F32)
    blk = lambda m: (jnp.transpose(m, (0, 2, 1))[:, :, None, :] * eye_h[:, None, :, None]
                     ).reshape(gh * S5_GROUP, gh * S5_STATE)
    cblk = lambda m: (jnp.transpose(m, (2, 0, 1))[None, :, :, :] * eye_h[:, None, :, None]
                      ).reshape(gh * S5_STATE, gh * S5_GROUP)
    p["s5_bblk"] = jnp.stack([blk(m[hb * gh:(hb + 1) * gh]) for m in (bb_re, bb_im) for hb in range(2)]
                             ).astype(BF16)
    p["s5_cblk"] = jnp.stack([cblk(m[hb * gh:(hb + 1) * gh]) for m in (s5_c_re[l], -s5_c_im[l])
                              for hb in range(2)]).astype(BF16)
    p["s5_lam"] = jnp.stack([lr.reshape(S5_HALF), li.reshape(S5_HALF)])
    p["s5_d"] = s5_d[l].reshape(1, BRANCH_W)

    p["w_gate"] = w_gate[l].astype(BF16)
    p["b_gate"] = b_gate[l].reshape(1, 3 * D)
    p["w_br_a"] = w_br_a[l].astype(BF16)
    p["w_br_b"] = w_br_b[l].astype(BF16)
    p["w_br_c"] = w_br_c[l].astype(BF16)
    p["w_glu"] = w_glu[l].astype(BF16)
    p["b_glu"] = b_glu[l].reshape(1, BRANCH_W)
    p["w_out"] = w_out[l].astype(BF16)

    assert ROUTE_E0 == MOE_GROUPS
    n_route = MOE_GROUPS + N_EXPERTS
    wr = jnp.concatenate([w_route_group[l], w_route_expert[l], jnp.zeros((D, LANES - n_route), F32)], axis=1)
    br = jnp.pad(jnp.concatenate([b_route_group[l], b_route_expert[l]]), (0, LANES - n_route)).reshape(1, LANES)
    wr_hi = wr.astype(BF16)
    p["w_route"] = jnp.stack([wr_hi, (wr - wr_hi.astype(F32)).astype(BF16)])
    p["b_route"] = br
    e_of_col = jnp.arange(N_EXPERTS * D_FF) // D_FF
    p["route_expand"] = (jnp.arange(LANES)[:, None] == (e_of_col[None, :] + ROUTE_E0)).astype(BF16)
    gw = EXPERTS_PER_GROUP * D_FF
    p["w_eg"] = w_exp_gate[l].astype(BF16)
    p["w_eu"] = w_exp_up[l].astype(BF16)
    p["w_ed"] = w_exp_down[l].reshape(MOE_GROUPS, gw, D).astype(BF16)
    return p


def _gla_state_in(s):
    st = jnp.swapaxes(s, -1, -2)
    z = jnp.zeros_like(st)
    halves = [jnp.concatenate([st[:, h], z[:, h]] if h % 2 == 0 else [z[:, h], st[:, h]], axis=-1)
              for h in range(GLA_HEADS)]
    return jnp.stack(halves, axis=1)


def _gla_state_out(st):
    parts = [st[:, h, :, GLA_DK * (h % 2):GLA_DK * (h % 2 + 1)] for h in range(GLA_HEADS)]
    return jnp.swapaxes(jnp.stack(parts, axis=1), -1, -2)


TOKEN_TILE = 512
MIXER_TILE = 256
GDN_CHUNKS = 8
GLA_CHUNKS = 4
MOE_HALVES = 2
MOE_TILE = 512 * MOE_HALVES


def _tiles(B, T):
    lt = min(TOKEN_TILE, T)
    nb = max(1, min(B, TOKEN_TILE // lt))
    lt_moe = min(MOE_TILE, T)
    nb_moe = max(1, min(B, MOE_TILE // lt_moe))
    lt_rec = min(MIXER_TILE, T)
    nb_gdn = max(1, min(B, GDN_CHUNKS // (lt_rec // CHUNK)))
    nb_gla = max(1, min(B, GLA_CHUNKS // (lt_rec // CHUNK)))
    return dict(nb=nb, lt=lt, nb_gdn=nb_gdn, nb_gla=nb_gla, lt_rec=lt_rec, lt_s5=min(CHUNK, T), nb_moe=nb_moe,
                lt_moe=lt_moe)


def _trunk(x, mods, params, states, final_w):
    tiles = _tiles(x.shape[0], x.shape[1])
    nb, lt, lt_rec, lt_s5 = tiles["nb"], tiles["lt"], tiles["lt_rec"], tiles["lt_s5"]
    new = []
    for l, p in enumerate(params):
        s_gla, s_gdn, cbuf, s5_re, s5_im = states[l]
        B = x.shape[0]
        cb8 = jnp.concatenate([jnp.zeros((B, SUBLANES - (CONV_W - 1), GDN_CONV_CH), F32), cbuf], axis=1)
        zg, zd, zs, cb_new = _inproj_call(x, mods[l], cb8, p, nb, lt)
        oa, ob, gla_t, gdn_new = _mixers_call(zg, zd, _gla_state_in(s_gla), s_gdn, p["gla_norm_w"],
                                              p["gdn_norm_w"], tiles["nb_gdn"], lt_rec)
        h0 = jnp.concatenate([s5_re.reshape(B, S5_HALF), s5_im.reshape(B, S5_HALF)], axis=1)
        ys, h_new = _s5_call(zs, h0, p, lt_s5)
        x = _merge_call(x, mods[l], oa, ob, ys, p, nb, lt)
        x = _moe_call(x, mods[l], p, final_w, tiles["nb_moe"], tiles["lt_moe"],
                      final_norm=(l == len(params) - 1))
        new.append((_gla_state_out(gla_t), gdn_new, cb_new[:, SUBLANES - (CONV_W - 1):],
                    h_new[:, :S5_HALF].reshape(B, S5_GROUPS, S5_STATE),
                    h_new[:, S5_HALF:].reshape(B, S5_GROUPS, S5_STATE)))
    return x, [jnp.stack([new[l][i] for l in range(len(params))]) for i in range(5)]


def kernel(x_prompt, x_sample, c_prompt, c_sample, state_gla, state_gdn, cache_gdn_conv, state_s5_re, state_s5_im, norm1_w, norm2_w, final_norm_w, w_ada, b_ada, w_in, w_alpha_up, b_alpha, gla_norm_w, conv_w, gdn_a_log, gdn_dt_bias, gdn_norm_w, s5_a_re, s5_a_im, s5_log_dt, s5_b_re, s5_b_im, s5_c_re, s5_c_im, s5_d, w_glu, b_glu, w_br_a, w_br_b, w_br_c, w_gate, b_gate, w_out, w_route_group, b_route_group, w_route_expert, b_route_expert, w_exp_gate, w_exp_up, w_exp_down):
    depth = w_in.shape[0]
    bp, tp, _ = x_prompt.shape
    bs, ts, _ = x_sample.shape
    params = [_prep_layer(l, norm1_w, norm2_w, w_in, w_alpha_up, b_alpha, gla_norm_w, conv_w, gdn_a_log,
                          gdn_dt_bias, gdn_norm_w, s5_a_re, s5_a_im, s5_log_dt, s5_b_re, s5_b_im, s5_c_re,
                          s5_c_im, s5_d, w_glu, b_glu, w_br_a, w_br_b, w_br_c, w_gate, b_gate, w_out,
                          w_route_group, b_route_group, w_route_expert, b_route_expert, w_exp_gate, w_exp_up,
                          w_exp_down) for l in range(depth)]
    final_w = final_norm_w.reshape(1, D_MODEL)

    mod = _ada_call(jnp.concatenate([c_prompt, c_sample], axis=0), w_ada, b_ada)
    mod = mod.reshape(depth, bp + bs, 6, D_MODEL)
    mods_p = [mod[l, :bp] for l in range(depth)]
    mods_s = [mod[l, bp:] for l in range(depth)]

    zeros_p = [(jnp.zeros((bp, GLA_HEADS, GLA_DK, GLA_DV), F32), jnp.zeros((bp, GDN_HEADS, GDN_DK, GDN_DV), F32),
                jnp.zeros((bp, CONV_W - 1, GDN_CONV_CH), F32), jnp.zeros((bp, S5_GROUPS, S5_STATE), F32),
                jnp.zeros((bp, S5_GROUPS, S5_STATE), F32)) for _ in range(depth)]
    y_p, new_p = _trunk(x_prompt, mods_p, params, zeros_p, final_w)

    states_s = [(state_gla[l], state_gdn[l], cache_gdn_conv[l], state_s5_re[l], state_s5_im[l])
                for l in range(depth)]
    y_s, new_s = _trunk(x_sample, mods_s, params, states_s, final_w)
    return (y_p, y_s, *new_p, *new_s)
```

```python
import functools
import math

import jax
import jax.numpy as jnp
from jax import lax
from jax.experimental import pallas as pl
from jax.experimental.pallas import tpu as pltpu

F32 = jnp.float32
BF16 = jnp.bfloat16

D_MODEL = 1024
EPS = 1e-6
CHUNK = 64
BRANCH_W = 512
GLA_HEADS, GLA_DK, GLA_DV, GLA_LOWRANK, GLA_TAU = 4, 64, 128, 16, 16.0
GDN_HEADS, GDN_DK, GDN_DV, CONV_W = 4, 128, 128, 4
S5_GROUP, S5_GROUPS, S5_STATE = 16, 32, 64
GLA_QK, GLA_V, GDN_QK, GDN_V = 256, 512, 512, 512
GDN_CONV_CH = 1536
MOE_GROUPS, EXPERTS_PER_GROUP, N_EXPERTS, D_FF = 4, 4, 16, 256

LANES = 128
SUBLANES = 8
VMEM_LIMIT = 56 * 1024 * 1024
S5_HALF = S5_GROUPS * S5_STATE
SMALL_BETA0, SMALL_A0 = 16, 20
ROUTE_E0 = 4


def _mm(a, b):
    return jnp.dot(a.astype(BF16), b.astype(BF16), preferred_element_type=F32)


def _mm_nt(a, b):
    return lax.dot_general(a.astype(BF16), b.astype(BF16), (((1,), (1,)), ((), ())), preferred_element_type=F32)


def _mm_tn(a, b):
    return lax.dot_general(a.astype(BF16), b.astype(BF16), (((0,), (0,)), ((), ())), preferred_element_type=F32)


def _split3(x):
    a = x.astype(BF16)
    r = x - a.astype(F32)
    b = r.astype(BF16)
    c = (r - b.astype(F32)).astype(BF16)
    return a, b, c


def _mm_sel(m01, x):
    a, b, c = _split3(x)
    return _mm(m01, a) + _mm(m01, b) + _mm(m01, c)


def _mm_sel_r(x, m01):
    a, b, c = _split3(x)
    return _mm(a, m01) + _mm(b, m01) + _mm(c, m01)


def _mm_nt_sel(m01, x):
    a, b, c = _split3(x)
    return _mm_nt(m01, a) + _mm_nt(m01, b) + _mm_nt(m01, c)


def _sigmoid(x):
    return 1.0 / (1.0 + jnp.exp(-x))


def _silu(x):
    return x * _sigmoid(x)


def _softplus(x):
    return jnp.maximum(x, 0.0) + jnp.log1p(jnp.exp(-jnp.abs(x)))


def _rms_rows(x, w):
    ms = jnp.mean(x * x, axis=-1, keepdims=True)
    return x * lax.rsqrt(ms + EPS) * w


def _gelu_tanh(x):
    c = math.sqrt(2.0 / math.pi)
    return x * (0.5 * (1.0 + jnp.tanh(c * (x + 0.044715 * (x * x * x)))))


def _const_spec(shape):
    nd = len(shape)
    return pl.BlockSpec(shape, lambda *_: (0,) * nd)


def _time_major_shape(B, T, nb):
    return (T, B * BRANCH_W)


def _time_major_spec(nb, lt):
    return pl.BlockSpec((lt, nb * BRANCH_W), lambda b, t: (t, b))


def _params(sem):
    return pltpu.CompilerParams(dimension_semantics=sem, vmem_limit_bytes=VMEM_LIMIT)


def _ada_kernel(c_ref, w_ref, b_ref, o_ref):
    c = c_ref[...]
    o_ref[0] = _mm(_silu(c), w_ref[0]) + b_ref[0]


def _ada_call(c_all, w_ada, b_ada):
    depth = w_ada.shape[0]
    nrow = c_all.shape[0]
    return pl.pallas_call(
        _ada_kernel,
        grid=(depth, 6),
        in_specs=[pl.BlockSpec((nrow, D_MODEL), lambda l, j: (0, 0)),
                  pl.BlockSpec((1, D_MODEL, D_MODEL), lambda l, j: (l, 0, j)),
                  pl.BlockSpec((1, 1, D_MODEL), lambda l, j: (l, 0, j))],
        out_specs=pl.BlockSpec((1, nrow, D_MODEL), lambda l, j: (l, 0, j)),
        out_shape=jax.ShapeDtypeStruct((depth, nrow, 6 * D_MODEL), F32),
        compiler_params=_params(("parallel", "parallel")),
        name="adaln",
    )(c_all, w_ada, b_ada.reshape(depth, 1, 6 * D_MODEL))


MXU_COLS = 256


def _inproj_kernel(x_ref, mod_ref, nw_ref, wg_ref, wd_ref, ws_ref, wsm_ref, wup_ref, bal_ref, sp_ref, cb_ref,
                   cw_ref, zg_ref, zd_ref, zs_ref, cbo_ref, *xbufs):
    nb, lt, d = x_ref.shape
    span = lt + SUBLANES

    @pl.when(pl.program_id(1) == 0)
    def _():
        for i in range(nb):
            for k, xb in enumerate(xbufs):
                xb[span * i + lt:span * (i + 1), :] = cb_ref[i, :, MXU_COLS * k:MXU_COLS * (k + 1)]

    x = x_ref[...]
    h = _rms_rows(x, nw_ref[...]) * (1.0 + mod_ref[:, 1:2, :]) + mod_ref[:, 0:1, :]
    h2 = h.reshape(nb * lt, d).astype(BF16)

    base = SUBLANES - (CONV_W - 1)
    for i in range(nb):
        for xb in xbufs:
            xb[span * i:span * i + SUBLANES, :] = xb[span * i + lt:span * (i + 1), :]

    def gla_block(j):
        dst = MXU_COLS * j if j < 2 else MXU_COLS * (j + 1)
        zg_ref[:, :, dst:dst + MXU_COLS] = _mm(h2, wg_ref[:, MXU_COLS * j:MXU_COLS * (j + 1)]).reshape(
            nb, lt, MXU_COLS)

    def gate_block(j):
        cs = slice(GDN_CONV_CH + MXU_COLS * j, GDN_CONV_CH + MXU_COLS * (j + 1))
        zd_ref[:, :, cs] = _mm(h2, wd_ref[:, cs]).reshape(nb, lt, MXU_COLS)

    def s5_block(j):
        zs = _mm(h2, ws_ref[:, MXU_COLS * j:MXU_COLS * (j + 1)])
        for i in range(nb):
            zs_ref[:, BRANCH_W * i + MXU_COLS * j:BRANCH_W * i + MXU_COLS * (j + 1)] = zs[lt * i:lt * (i + 1), :]

    def small_block(_):
        zsm = _mm(h2, wsm_ref[...])
        la_pre = _mm(zsm, wup_ref[...]) + bal_ref[...]
        log_a = (jnp.minimum(la_pre, 0.0) - jnp.log1p(jnp.exp(-jnp.abs(la_pre)))) * (1.0 / GLA_TAU)
        zg_ref[:, :, 512:768] = log_a.reshape(nb, lt, 256)
        lane = lax.broadcasted_iota(jnp.int32, zsm.shape, 1)
        beta = _sigmoid(zsm)
        log_g = -sp_ref[0:1, :] * _softplus(zsm + sp_ref[1:2, :])
        is_beta = (lane >= SMALL_BETA0) & (lane < SMALL_A0)
        is_a = (lane >= SMALL_A0) & (lane < SMALL_A0 + GDN_HEADS)
        small = jnp.where(is_beta, beta, jnp.where(is_a, log_g, 0.0))
        zd_ref[:, :, 2048:2176] = small.reshape(nb, lt, LANES)

    fillers = [[(gla_block, 0), (gate_block, 0)], [(gla_block, 1), (gate_block, 1)],
               [(gla_block, 2), (s5_block, 0)], [(gla_block, 3), (s5_block, 1)],
               [(gla_block, 4), (small_block, 0)], [(gla_block, 5)]]
    def conv_matmul(mblk):
        zd = _mm(h2, wd_ref[:, MXU_COLS * mblk:MXU_COLS * (mblk + 1)])
        for i in range(nb):
            xbufs[mblk][span * i + SUBLANES:span * (i + 1), :] = zd[lt * i:lt * (i + 1), :]
        for fn, arg in fillers[mblk]:
            fn(arg)

    n_blk = GDN_CONV_CH // MXU_COLS
    conv_matmul(0)
    for mblk in range(n_blk):
        if mblk + 1 < n_blk:
            conv_matmul(mblk + 1)
        xb = xbufs[mblk]
        for i in range(nb):
            r0 = span * i
            for cblk in range(MXU_COLS * mblk // LANES, MXU_COLS * (mblk + 1) // LANES):
                cs = slice(LANES * cblk, LANES * (cblk + 1))
                ls = slice(LANES * cblk - MXU_COLS * mblk, LANES * (cblk + 1) - MXU_COLS * mblk)
                conv = xb[r0 + base:r0 + base + lt, ls] * cw_ref[0:1, cs]
                for j in range(1, CONV_W):
                    conv = conv + xb[r0 + base + j:r0 + base + j + lt, ls] * cw_ref[j:j + 1, cs]
                act = _silu(conv)
                if cblk < 2 * GDN_HEADS:
                    scale = GDN_DK ** -0.5 if cblk < GDN_HEADS else 1.0
                    ss = jnp.sum(act * act, axis=-1, keepdims=True)
                    act = act * (lax.rsqrt(ss + EPS) * scale)
                zd_ref[i, :, cs] = act
    for i in range(nb):
        for k, xb in enumerate(xbufs):
            cbo_ref[i, :, MXU_COLS * k:MXU_COLS * (k + 1)] = xb[span * i + lt:span * (i + 1), :]


def _inproj_call(x, mod, cb8, p, nb, lt):
    B, T, D = x.shape
    assert nb == 1 or T == lt, "several sequences per step only when each is a single time block"
    grid = (B // nb, T // lt)
    tok = lambda w: pl.BlockSpec((nb, lt, w), lambda b, t: (b, t, 0))
    cb_spec = pl.BlockSpec((nb, SUBLANES, GDN_CONV_CH), lambda b, t: (b, 0, 0))
    return pl.pallas_call(
        _inproj_kernel,
        grid=grid,
        in_specs=[tok(D),
                  pl.BlockSpec((nb, 6, D), lambda b, t: (b, 0, 0)),
                  _const_spec((1, D)),
                  _const_spec(p["w_gla"].shape), _const_spec(p["w_gdn"].shape), _const_spec(p["w_s5"].shape),
                  _const_spec(p["w_small"].shape), _const_spec(p["w_up"].shape), _const_spec((1, GLA_QK)),
                  _const_spec((2, LANES)), cb_spec, _const_spec((CONV_W, GDN_CONV_CH))],
        out_specs=[tok(1792), tok(2176), _time_major_spec(nb, lt), cb_spec],
        out_shape=[jax.ShapeDtypeStruct((B, T, 1792), F32),
                   jax.ShapeDtypeStruct((B, T, 2176), F32),
                   jax.ShapeDtypeStruct(_time_major_shape(B, T, nb), F32),
                   jax.ShapeDtypeStruct((B, SUBLANES, GDN_CONV_CH), F32)],
        scratch_shapes=[pltpu.VMEM((nb * (lt + SUBLANES), MXU_COLS), F32)] * (GDN_CONV_CH // MXU_COLS),
        compiler_params=_params(("parallel", "arbitrary")),
        name="inproj",
    )(x, mod, p["norm1_w"], p["w_gla"], p["w_gdn"], p["w_s5"], p["w_small"], p["w_up"], p["b_alpha"],
      p["gdn_small"], cb8, p["conv_w"])


def _gla_stages(z_ref, nw_ref, o_ref, s_ref):
    L = CHUNK
    nb, lt, _ = z_ref.shape
    n_chunks = lt // L
    n_rows = nb * lt
    row = lax.broadcasted_iota(jnp.int32, (L, L), 0)
    col = lax.broadcasted_iota(jnp.int32, (L, L), 1)
    causal = row >= col
    lane = lax.broadcasted_iota(jnp.int32, (L, LANES), 1)
    own_half = (lane < GLA_DK, lane >= GLA_DK)
    nw = nw_ref[...]
    brow = lax.broadcasted_iota(jnp.int32, (n_rows, n_rows), 0)
    bcol = lax.broadcasted_iota(jnp.int32, (n_rows, n_rows), 1)
    shift = L.bit_length() - 1
    chunk_tril = ((brow >= bcol) & ((brow >> shift) == (bcol >> shift))).astype(BF16)
    b_all = _mm_sel(chunk_tril, z_ref[:, :, 512:768].reshape(n_rows, GLA_QK))

    def zcols(c, lo, width):
        i, cc = divmod(c, n_chunks)
        return z_ref[i, L * cc:L * (cc + 1), lo:lo + width]

    all_chunks = range(nb * n_chunks)
    probs = [(c, h) for c in all_chunks for h in range(GLA_HEADS)]
    rows_of = lambda c: slice(L * c, L * (c + 1))
    pair_of = lambda h: slice(LANES * (h // 2), LANES * (h // 2) + LANES)
    qi_, ki_, qs_, kl_, dl_ = {}, {}, {}, {}, {}
    for c in all_chunks:
        q = zcols(c, 0, GLA_QK)
        k = zcols(c, GLA_QK, GLA_QK)
        b = b_all[rows_of(c), :]
        b_mid = b[L // 2 - 1:L // 2, :]
        b_last = b[L - 1:L, :]
        qi_[c] = q * jnp.exp(b - b_mid)
        ki_[c] = k * jnp.exp(b_mid - b)
        qs_[c] = q * jnp.exp(b)
        kl_[c] = k * jnp.exp(b_last - b)
        dl_[c] = jnp.exp(b_last)
    v_ = {p: zcols(p[0], 768 + GLA_DV * p[1], GLA_DV) for p in probs}
    att_, kv_, av_ = {}, {}, {}

    def scores():
        for p in probs:
            ki_h = jnp.where(own_half[p[1] % 2], ki_[p[0]][:, pair_of(p[1])], 0.0)
            att_[p] = jnp.where(causal, _mm_nt(qi_[p[0]][:, pair_of(p[1])], ki_h), 0.0)

    def increments():
        for p in probs:
            kv_[p] = _mm_tn(v_[p], jnp.where(own_half[p[1] % 2], kl_[p[0]][:, pair_of(p[1])], 0.0))

    def intra():
        for p in probs:
            av_[p] = _mm(att_[p], v_[p])

    def outputs():
        st_ = {}
        for i in range(nb):
            for h in range(GLA_HEADS):
                st = s_ref[i, h]
                for cc in range(n_chunks):
                    c = n_chunks * i + cc
                    st_[(c, h)] = st
                    st = st * dl_[c][:, pair_of(h)] + kv_[(c, h)]
                s_ref[i, h] = st
        for p in probs:
            c, h = p
            i, cc = divmod(c, n_chunks)
            o_h = av_[p] + _mm_nt(qs_[c][:, pair_of(h)], st_[p])
            gate = zcols(c, 1280 + GLA_DV * h, GLA_DV)
            o_ref[i, L * cc:L * (cc + 1), GLA_DV * h:GLA_DV * (h + 1)] = _rms_rows(o_h, nw) * _silu(gate)

    return [scores, increments, intra, outputs]


def _mixers_kernel(zg_ref, zd_ref, sa0_ref, sb0_ref, nwa_ref, nwb_ref, oa_ref, ob_ref, sa_ref, sb_ref, bfull_ref,
                   gfull_ref, grow_ref, glast_ref):
    @pl.when(pl.program_id(1) == 0)
    def _():
        sa_ref[...] = sa0_ref[...]
        sb_ref[...] = sb0_ref[...]

    _gdn_body(zd_ref, nwb_ref, ob_ref, sb_ref, bfull_ref, gfull_ref, grow_ref, glast_ref,
              _gla_stages(zg_ref, nwa_ref, oa_ref, sa_ref))


def _gdn_body(z_ref, nw_ref, o_ref, s_ref, bfull_ref, gfull_ref, grow_ref, glast_ref, fillers):
    L = CHUNK
    nb, lt, _ = z_ref.shape
    n_chunks = lt // L
    n_rows = nb * lt
    fillers = list(fillers)
    row = lax.broadcasted_iota(jnp.int32, (L, L), 0)
    col = lax.broadcasted_iota(jnp.int32, (L, L), 1)
    causal = row >= col
    strict = row > col
    eye = (row == col).astype(F32)
    diag16 = (row >> 4) == (col >> 4)
    diag32 = (row >> 5) == (col >> 5)
    off32 = diag32 & jnp.logical_not(diag16)
    off64 = jnp.logical_not(diag32)
    sel_r = lax.broadcasted_iota(jnp.int32, (SUBLANES, LANES), 0)
    sel_c = lax.broadcasted_iota(jnp.int32, (SUBLANES, LANES), 1)
    sel = (sel_c == sel_r + SMALL_A0).astype(BF16)
    brow = lax.broadcasted_iota(jnp.int32, (n_rows, n_rows), 0)
    bcol = lax.broadcasted_iota(jnp.int32, (n_rows, n_rows), 1)
    shift = L.bit_length() - 1
    chunk_tril = ((brow >= bcol) & ((brow >> shift) == (bcol >> shift))).astype(BF16)
    small_all = z_ref[:, :, 2048:2176].reshape(n_rows, LANES)
    gs_all = _mm_sel(chunk_tril, small_all)
    grow_all = _mm_nt_sel(sel, gs_all)
    e_r = lax.broadcasted_iota(jnp.int32, (LANES, GDN_HEADS * LANES), 0)
    e_h = lax.broadcasted_iota(jnp.int32, (LANES, GDN_HEADS * LANES), 1) >> (LANES.bit_length() - 1)
    bfull_ref[...] = _mm_sel_r(small_all, (e_r == e_h + SMALL_BETA0).astype(BF16))
    gfull = _mm_sel_r(gs_all, (e_r == e_h + SMALL_A0).astype(BF16))
    gfull_ref[...] = gfull
    for c in range(nb * n_chunks):
        grow_ref[c] = grow_all[:, L * c:L * (c + 1)]
        glast_ref[c:c + 1, :] = gfull[L * (c + 1) - 1:L * (c + 1), :]
    nw = nw_ref[...]

    probs = [(c, h) for c in range(nb * n_chunks) for h in range(GDN_HEADS)]
    rows_of = lambda c: slice(L * c, L * (c + 1))
    head_of = lambda h: slice(LANES * h, LANES * (h + 1))

    def zcols(c, lo, width):
        i, cc = divmod(c, n_chunks)
        return z_ref[i, L * cc:L * (cc + 1), lo:lo + width]

    q_ = {p: zcols(p[0], GDN_DK * p[1], GDN_DK) for p in probs}
    k_ = {p: zcols(p[0], GDN_QK + GDN_DK * p[1], GDN_DK) for p in probs}
    beta_ = {p: bfull_ref[rows_of(p[0]), head_of(p[1])] for p in probs}
    g_ = {p: gfull_ref[rows_of(p[0]), head_of(p[1])] for p in probs}
    dec_ = {p: jnp.exp(jnp.where(causal, g_[p][:, 0:L] - grow_ref[p[0]][p[1]:p[1] + 1, :], -jnp.inf))
            for p in probs}
    kk_ = {p: _mm_nt(k_[p], k_[p]) for p in probs}
    qk_ = {p: _mm_nt(q_[p], k_[p]) for p in probs}
    a_ = {p: jnp.where(strict, beta_[p][:, 0:L] * kk_[p] * dec_[p], 0.0) for p in probs}
    qk_ = {p: jnp.where(causal, qk_[p] * dec_[p], 0.0) for p in probs}
    pw_ = {p: jnp.where(diag16, -a_[p], 0.0) for p in probs}
    tinv_ = {p: eye + pw_[p] for p in probs}
    for _ in range(3):
        pw_ = {p: _mm(pw_[p], pw_[p]) for p in probs}
        tinv_ = {p: tinv_[p] + _mm(tinv_[p], pw_[p]) for p in probs}
    for off in (off32, off64):
        te_ = {p: _mm(tinv_[p], jnp.where(off, a_[p], 0.0)) for p in probs}
        tinv_ = {p: tinv_[p] - _mm(te_[p], tinv_[p]) for p in probs}
    eg_ = {p: jnp.exp(g_[p]) for p in probs}
    sol_ = {}
    for p in probs:
        v_p = zcols(p[0], 2 * GDN_QK + GDN_DV * p[1], GDN_DV)
        rhs = jnp.concatenate([beta_[p] * v_p, (beta_[p] * eg_[p]) * k_[p]], axis=-1)
        sol_[p] = _mm(tinv_[p], rhs)

    st_ = {(i, h): s_ref[i, h] for i in range(nb) for h in range(GDN_HEADS)}
    for cc in range(n_chunks):
        heads = [(i, n_chunks * i + cc, h) for i in range(nb) for h in range(GDN_HEADS)]
        u_, o_, new_ = {}, {}, {}
        for i, c, h in heads:
            u_[(c, h)] = sol_[(c, h)][:, 0:GDN_DV] - _mm(sol_[(c, h)][:, GDN_DV:], st_[(i, h)])
        if fillers:
            fillers.pop(0)()
        for i, c, h in heads:
            o_[(c, h)] = _mm(qk_[(c, h)], u_[(c, h)]) + _mm(q_[(c, h)] * eg_[(c, h)], st_[(i, h)])
        for i, c, h in heads:
            g_last = glast_ref[c:c + 1, head_of(h)]
            new_[(i, h)] = (st_[(i, h)] * jnp.exp(g_last)
                            + _mm_tn(k_[(c, h)] * jnp.exp(g_last - g_[(c, h)]), u_[(c, h)]))
        st_ = new_
        for i, c, h in heads:
            gate = z_ref[i, L * cc:L * (cc + 1), GDN_CONV_CH + GDN_DV * h:GDN_CONV_CH + GDN_DV * (h + 1)]
            o_ref[i, L * cc:L * (cc + 1), GDN_DV * h:GDN_DV * (h + 1)] = _rms_rows(o_[(c, h)], nw) * _silu(gate)
    for i in range(nb):
        for h in range(GDN_HEADS):
            s_ref[i, h] = st_[(i, h)]
    for rest in fillers:
        rest()


def _mixers_call(zg, zd, sa0, sb0, nwa, nwb, nb, lt):
    B, T, _ = zd.shape
    n_rows = nb * lt
    last_rows = -(-(n_rows // CHUNK) // SUBLANES) * SUBLANES
    sa_spec = pl.BlockSpec((nb, GLA_HEADS, GLA_DV, LANES), lambda b, t: (b, 0, 0, 0))
    sb_spec = pl.BlockSpec((nb, GDN_HEADS, GDN_DK, GDN_DV), lambda b, t: (b, 0, 0, 0))
    tok = lambda w: pl.BlockSpec((nb, lt, w), lambda b, t: (b, t, 0))
    return pl.pallas_call(
        _mixers_kernel,
        grid=(B // nb, T // lt),
        in_specs=[tok(1792), tok(2176), sa_spec, sb_spec, _const_spec((1, GLA_DV)), _const_spec((1, GDN_DV))],
        out_specs=[tok(GLA_V), tok(GDN_V), sa_spec, sb_spec],
        out_shape=[jax.ShapeDtypeStruct((B, T, GLA_V), F32), jax.ShapeDtypeStruct((B, T, GDN_V), F32),
                   jax.ShapeDtypeStruct((B, GLA_HEADS, GLA_DV, LANES), F32),
                   jax.ShapeDtypeStruct((B, GDN_HEADS, GDN_DK, GDN_DV), F32)],
        scratch_shapes=[pltpu.VMEM((n_rows, GDN_HEADS * LANES), F32), pltpu.VMEM((n_rows, GDN_HEADS * LANES), F32),
                        pltpu.VMEM((n_rows // CHUNK, SUBLANES, CHUNK), F32),
                        pltpu.VMEM((last_rows, GDN_HEADS * LANES), F32)],
        compiler_params=_params(("parallel", "arbitrary")),
        name="mixers",
    )(zg, zd, sa0, sb0, nwa, nwb)


S5_LANE_CHUNK = 512


def _s5_kernel(u_ref, h0_ref, bb_ref, cc_ref, lam_ref, d_ref, y_ref, h_ref, bu_ref, *, lt):
    nb = SUBLANES

    @pl.when(pl.program_id(1) == 0)
    def _():
        h_ref[...] = h0_ref[...]

    u = u_ref[...].reshape(lt * nb, BRANCH_W)
    ch, sh = BRANCH_W // 2, S5_HALF // 2
    for part in range(2):
        for hb in range(2):
            bu_ref[:, part * S5_HALF + hb * sh:part * S5_HALF + (hb + 1) * sh] = _mm(
                u[:, hb * ch:(hb + 1) * ch], bb_ref[2 * part + hb])
    for j in range(S5_HALF // S5_LANE_CHUNK):
        re = slice(S5_LANE_CHUNK * j, S5_LANE_CHUNK * (j + 1))
        im = slice(S5_HALF + S5_LANE_CHUNK * j, S5_HALF + S5_LANE_CHUNK * (j + 1))
        lr = jnp.broadcast_to(lam_ref[0:1, re], (nb, S5_LANE_CHUNK))
        li = jnp.broadcast_to(lam_ref[1:2, re], (nb, S5_LANE_CHUNK))

        hr, hi = h_ref[:, re], h_ref[:, im]
        for tt in range(lt):
            rows = slice(tt * nb, (tt + 1) * nb)
            hr, hi = lr * hr - li * hi + bu_ref[rows, re], lr * hi + li * hr + bu_ref[rows, im]
            bu_ref[rows, re] = hr
            bu_ref[rows, im] = hi
        h_ref[:, re] = hr
        h_ref[:, im] = hi
    ys = [_mm(bu_ref[:, hb * sh:(hb + 1) * sh], cc_ref[hb])
          + _mm(bu_ref[:, S5_HALF + hb * sh:S5_HALF + (hb + 1) * sh], cc_ref[2 + hb]) for hb in range(2)]
    y = jnp.concatenate(ys, axis=-1) + d_ref[...] * u
    y_ref[...] = y.reshape(lt, nb, BRANCH_W)


def _s5_call(zs, h0, p, lt):
    T = zs.shape[0]
    B = h0.shape[0]
    nb = SUBLANES
    h_spec = pl.BlockSpec((nb, 2 * S5_HALF), lambda b, t: (b, 0))
    tok_spec = pl.BlockSpec((lt, None, nb, BRANCH_W), lambda b, t: (t, b, 0, 0))
    y, h = pl.pallas_call(
        functools.partial(_s5_kernel, lt=lt),
        grid=(B // nb, T // lt),
        in_specs=[tok_spec, h_spec,
                  _const_spec((4, BRANCH_W // 2, S5_HALF // 2)), _const_spec((4, S5_HALF // 2, BRANCH_W // 2)),
                  _const_spec((2, S5_HALF)), _const_spec((1, BRANCH_W))],
        out_specs=[tok_spec, h_spec],
        out_shape=[jax.ShapeDtypeStruct((T, B // nb, nb, BRANCH_W), F32),
                   jax.ShapeDtypeStruct((B, 2 * S5_HALF), F32)],
        scratch_shapes=[pltpu.VMEM((lt * nb, 2 * S5_HALF), F32)],
        compiler_params=_params(("parallel", "arbitrary")),
        name="s5",
    )(zs.reshape(T, B // nb, nb, BRANCH_W), h0, p["s5_bblk"], p["s5_cblk"], p["s5_lam"], p["s5_d"])
    return y.reshape(zs.shape), h


def _merge_kernel(x_ref, mod_ref, nw_ref, oa_ref, ob_ref, ys_ref, wgate_ref, bgate_ref, wa_ref, wb_ref, wc_ref,
                  wglu_ref, bglu_ref, wout_ref, o_ref):
    nb, lt, d = x_ref.shape
    n = nb * lt
    x = x_ref[...]
    h = _rms_rows(x, nw_ref[...]) * (1.0 + mod_ref[:, 1:2, :]) + mod_ref[:, 0:1, :]
    gate = _sigmoid(_mm(h.reshape(n, d), wgate_ref[...]) + bgate_ref[...])
    y_a = _mm(oa_ref[...].reshape(n, BRANCH_W), wa_ref[...])
    y_b = _mm(ob_ref[...].reshape(n, BRANCH_W), wb_ref[...])
    ys = [ys_ref[:, BRANCH_W * i:BRANCH_W * (i + 1)] for i in range(nb)]
    yc = _gelu_tanh(ys[0] if nb == 1 else jnp.concatenate(ys, axis=0))
    yc = yc * _sigmoid(_mm(yc, wglu_ref[...]) + bglu_ref[...])
    y_c = _mm(yc, wc_ref[...])
    mix = gate[:, 0:d] * y_a + gate[:, d:2 * d] * y_b + gate[:, 2 * d:3 * d] * y_c
    m = _mm(mix, wout_ref[...])
    o_ref[...] = x + mod_ref[:, 2:3, :] * m.reshape(nb, lt, d)


def _merge_call(x, mod, oa, ob, ys, p, nb, lt):
    B, T, D = x.shape
    tok = lambda w: pl.BlockSpec((nb, lt, w), lambda b, t: (b, t, 0))
    return pl.pallas_call(
        _merge_kernel,
        grid=(B // nb, T // lt),
        in_specs=[tok(D), pl.BlockSpec((nb, 6, D), lambda b, t: (b, 0, 0)), _const_spec((1, D)),
                  tok(BRANCH_W), tok(BRANCH_W), _time_major_spec(nb, lt),
                  _const_spec((D, 3 * D)), _const_spec((1, 3 * D)),
                  _const_spec((BRANCH_W, D)), _const_spec((BRANCH_W, D)), _const_spec((BRANCH_W, D)),
                  _const_spec((BRANCH_W, BRANCH_W)), _const_spec((1, BRANCH_W)), _const_spec((D, D))],
        out_specs=tok(D),
        out_shape=jax.ShapeDtypeStruct((B, T, D), F32),
        compiler_params=_params(("parallel", "parallel")),
        name="merge",
    )(x, mod, p["norm1_w"], oa, ob, ys, p["w_gate"], p["b_gate"], p["w_br_a"], p["w_br_b"], p["w_br_c"],
      p["w_glu"], p["b_glu"], p["w_out"])


MOE_ROWS = 160


def _moe_kernel(x_ref, mod_ref, nw_ref, wr_ref, br_ref, ex_ref, wg_ref, wu_ref, wd_ref, fw_ref, o_ref,
                hn_ref, comb_ref, krow_ref, acc_ref, *, final_norm):
    nb, lt, d = x_ref.shape
    R = MOE_ROWS
    gw = EXPERTS_PER_GROUP * D_FF
    halves = hn_ref.shape[0]
    if nb >= halves:
        hb, hl = nb // halves, lt
    else:
        hb, hl = nb, lt // halves
    n = hb * hl
    row_pos = lax.broadcasted_iota(jnp.int32, (R, n), 0).astype(F32)
    lane = lax.broadcasted_iota(jnp.int32, (n, LANES), 1)
    trow = lax.broadcasted_iota(jnp.int32, (n, n), 0)
    tcol = lax.broadcasted_iota(jnp.int32, (n, n), 1)
    before = (trow > tcol).astype(BF16)
    sel_r = lax.broadcasted_iota(jnp.int32, (SUBLANES, LANES), 0)
    sel_c = lax.broadcasted_iota(jnp.int32, (SUBLANES, LANES), 1)
    pick = (sel_r == sel_c).astype(BF16)
    neg = -jnp.inf

    def xs(k):
        return x_ref[hb * k:hb * (k + 1)] if nb >= halves else x_ref[:, hl * k:hl * (k + 1), :]

    def mods(k, row):
        m = mod_ref[hb * k:hb * (k + 1)] if nb >= halves else mod_ref[...]
        return m[:, row:row + 1, :]

    def route(k):
        hn = (_rms_rows(xs(k), nw_ref[...]) * (1.0 + mods(k, 4)) + mods(k, 3)).reshape(n, d)
        hn_hi = hn.astype(BF16)
        hn_ref[k] = hn_hi
        hn_lo = (hn - hn_hi.astype(F32)).astype(BF16)
        logits = _mm(hn_hi, wr_ref[0]) + (_mm(hn_hi, wr_ref[1]) + _mm(hn_lo, wr_ref[0])) + br_ref[...]
        gl = jnp.where(lane < MOE_GROUPS, logits, neg)
        gmax = jnp.max(gl, axis=-1, keepdims=True)
        g_sel = jnp.min(jnp.where(gl == gmax, lane, LANES), axis=-1, keepdims=True)
        p_sel = 1.0 / jnp.sum(jnp.exp(gl - gmax), axis=-1, keepdims=True)
        e_id = lane - ROUTE_E0
        in_group = (e_id >= 0) & (e_id < N_EXPERTS) & ((e_id >> 2) == g_sel)
        el = jnp.where(in_group, logits, neg)
        m1 = jnp.max(el, axis=-1, keepdims=True)
        i1 = jnp.min(jnp.where(el == m1, lane, LANES), axis=-1, keepdims=True)
        el2 = jnp.where(lane == i1, neg, el)
        m2 = jnp.max(el2, axis=-1, keepdims=True)
        i2 = jnp.min(jnp.where(el2 == m2, lane, LANES), axis=-1, keepdims=True)
        e2 = jnp.exp(m2 - m1)
        w1 = p_sel / (1.0 + e2)
        w2 = p_sel * e2 / (1.0 + e2)
        comb_ref[k] = jnp.where(lane == i1, w1, jnp.where(lane == i2, w2, 0.0)).astype(BF16)
        member = lane == g_sel
        rank = _mm(before, member.astype(BF16))
        key = jnp.where(member, rank, -1.0)
        krow_ref[k] = _mm_nt_sel(pick, key)
        return [(jnp.max(key[:, g:g + 1]).astype(jnp.int32) + R) // R for g in range(MOE_GROUPS)]

    def one_pass(k, g, s):
        base = float(s * R) if isinstance(s, int) else (s * R).astype(F32)
        gather = (krow_ref[k, g:g + 1, :] - base == row_pos).astype(BF16)
        xg = _mm(gather, hn_ref[k])
        cg = _mm(_mm(gather, comb_ref[k]), ex_ref[:, gw * g:gw * (g + 1)])
        experts = range(EXPERTS_PER_GROUP * g, EXPERTS_PER_GROUP * (g + 1))
        he = jnp.concatenate([_silu(_mm(xg, wg_ref[e])) * _mm(xg, wu_ref[e]) for e in experts], axis=-1)
        yg = _mm(he * cg, wd_ref[g])
        return _mm_tn(gather, yg)

    n_pass = [route(k) for k in range(halves)]
    for k in range(halves):
        acc = one_pass(k, 0, 0)
        for g in range(1, MOE_GROUPS):
            acc = acc + one_pass(k, g, 0)
        acc_ref[k] = acc
    for k in range(halves):
        for g in range(MOE_GROUPS):
            def extra(s, carry, k=k, g=g):
                acc_ref[k] += one_pass(k, g, s)
                return carry

            lax.fori_loop(1, n_pass[k][g], extra, 0)
    for k in range(halves):
        out = xs(k) + mods(k, 5) * acc_ref[k].reshape(hb, hl, d)
        if final_norm:
            out = _rms_rows(out, fw_ref[...])
        if nb >= halves:
            o_ref[hb * k:hb * (k + 1)] = out
        else:
            o_ref[:, hl * k:hl * (k + 1), :] = out


def _moe_call(x, mod, p, final_w, nb, lt, final_norm):
    B, T, D = x.shape
    halves = MOE_HALVES if (nb * lt) % (MOE_HALVES * SUBLANES * 2) == 0 and max(nb, lt) % MOE_HALVES == 0 else 1
    n = nb * lt // halves
    gw = EXPERTS_PER_GROUP * D_FF
    tok = pl.BlockSpec((nb, lt, D), lambda b, t: (b, t, 0))
    resident = lambda shape: pl.BlockSpec(shape, lambda *_: (0,) * len(shape), pipeline_mode=pl.Buffered(1))
    return pl.pallas_call(
        functools.partial(_moe_kernel, final_norm=final_norm),
        grid=(B // nb, T // lt),
        in_specs=[tok, pl.BlockSpec((nb, 6, D), lambda b, t: (b, 0, 0)), _const_spec((1, D)),
                  _const_spec((2, D, LANES)), _const_spec((1, LANES)), _const_spec((LANES, N_EXPERTS * D_FF)),
                  resident((N_EXPERTS, D, D_FF)), resident((N_EXPERTS, D, D_FF)), resident((MOE_GROUPS, gw, D)),
                  _const_spec((1, D))],
        out_specs=tok,
        out_shape=jax.ShapeDtypeStruct((B, T, D), F32),
        scratch_shapes=[pltpu.VMEM((halves, n, D), BF16), pltpu.VMEM((halves, n, LANES), BF16),
                        pltpu.VMEM((halves, SUBLANES, n), F32), pltpu.VMEM((halves, n, D), F32)],
        compiler_params=_params(("parallel", "parallel")),
        name="moe",
    )(x, mod, p["norm2_w"], p["w_route"], p["b_route"], p["route_expand"], p["w_eg"], p["w_eu"], p["w_ed"],
      final_w)


def _prep_layer(l, norm1_w, norm2_w, w_in, w_alpha_up, b_alpha, gla_norm_w, conv_w, gdn_a_log, gdn_dt_bias,
                gdn_norm_w, s5_a_re, s5_a_im, s5_log_dt, s5_b_re, s5_b_im, s5_c_re, s5_c_im, s5_d, w_glu, b_glu,
                w_br_a, w_br_b, w_br_c, w_gate, b_gate, w_out, w_route_group, b_route_group, w_route_expert,
                b_route_expert, w_exp_gate, w_exp_up, w_exp_down):
    D = D_MODEL
    w = w_in[l]
    o = 0
    cols = {}
    for name, size in (("qa", GLA_QK), ("ka", GLA_QK), ("va", GLA_V), ("ga", GLA_V), ("aa", GLA_LOWRANK),
                       ("qkv", GDN_CONV_CH), ("beta", GDN_HEADS), ("a", GDN_HEADS), ("gb", GDN_V),
                       ("u", BRANCH_W)):
        cols[name] = w[:, o:o + size]
        o += size
    p = {}
    p["norm1_w"] = norm1_w[l].reshape(1, D)
    p["norm2_w"] = norm2_w[l].reshape(1, D)
    p["w_gla"] = jnp.concatenate([cols["qa"] * (GLA_DK ** -0.5), cols["ka"], cols["va"], cols["ga"]],
                                 axis=1).astype(BF16)
    p["w_gdn"] = jnp.concatenate([cols["qkv"], cols["gb"]], axis=1).astype(BF16)
    p["w_s5"] = cols["u"].astype(BF16)
    assert SMALL_BETA0 == GLA_LOWRANK and SMALL_A0 == SMALL_BETA0 + GDN_HEADS
    used = SMALL_A0 + GDN_HEADS
    p["w_small"] = jnp.concatenate([cols["aa"], cols["beta"], cols["a"], jnp.zeros((D, LANES - used), F32)],
                                   axis=1).astype(BF16)
    p["w_up"] = jnp.pad(w_alpha_up[l], ((0, LANES - GLA_LOWRANK), (0, 0))).astype(BF16)
    p["b_alpha"] = b_alpha[l].reshape(1, GLA_QK)
    p["gdn_small"] = jnp.pad(jnp.stack([jnp.exp(gdn_a_log[l]), gdn_dt_bias[l]]),
                             ((0, 0), (SMALL_A0, LANES - used)))
    p["gla_norm_w"] = gla_norm_w[l].reshape(1, GLA_DV)
    p["gdn_norm_w"] = gdn_norm_w[l].reshape(1, GDN_DV)
    p["conv_w"] = conv_w[l]

    a_re, a_im = s5_a_re[l], s5_a_im[l]
    dt = jnp.exp(s5_log_dt[l])[:, None]
    mag = jnp.exp(a_re * dt)
    lr, li = mag * jnp.cos(a_im * dt), mag * jnp.sin(a_im * dt)
    den = a_re * a_re + a_im * a_im
    f_re = ((lr - 1.0) * a_re + li * a_im) / den
    f_im = (li * a_re - (lr - 1.0) * a_im) / den
    bb_re = f_re[..., None] * s5_b_re[l] - f_im[..., None] * s5_b_im[l]
    bb_im = f_re[..., None] * s5_b_im[l] + f_im[..., None] * s5_b_re[l]
    gh = S5_GROUPS // 2
    eye_h = jnp.eye(gh, dtype=F32)
    blk = lambda m: (jnp.transpose(m, (0, 2, 1))[:, :, None, :] * eye_h[:, None, :, None]
                     ).reshape(gh * S5_GROUP, gh * S5_STATE)
    cblk = lambda m: (jnp.transpose(m, (2, 0, 1))[None, :, :, :] * eye_h[:, None, :, None]
                      ).reshape(gh * S5_STATE, gh * S5_GROUP)
    p["s5_bblk"] = jnp.stack([blk(m[hb * gh:(hb + 1) * gh]) for m in (bb_re, bb_im) for hb in range(2)]
                             ).astype(BF16)
    p["s5_cblk"] = jnp.stack([cblk(m[hb * gh:(hb + 1) * gh]) for m in (s5_c_re[l], -s5_c_im[l])
                              for hb in range(2)]).astype(BF16)
    p["s5_lam"] = jnp.stack([lr.reshape(S5_HALF), li.reshape(S5_HALF)])
    p["s5_d"] = s5_d[l].reshape(1, BRANCH_W)

    p["w_gate"] = w_gate[l].astype(BF16)
    p["b_gate"] = b_gate[l].reshape(1, 3 * D)
    p["w_br_a"] = w_br_a[l].astype(BF16)
    p["w_br_b"] = w_br_b[l].astype(BF16)
    p["w_br_c"] = w_br_c[l].astype(BF16)
    p["w_glu"] = w_glu[l].astype(BF16)
    p["b_glu"] = b_glu[l].reshape(1, BRANCH_W)
    p["w_out"] = w_out[l].astype(BF16)

    assert ROUTE_E0 == MOE_GROUPS
    n_route = MOE_GROUPS + N_EXPERTS
    wr = jnp.concatenate([w_route_group[l], w_route_expert[l], jnp.zeros((D, LANES - n_route), F32)], axis=1)
    br = jnp.pad(jnp.concatenate([b_route_group[l], b_route_expert[l]]), (0, LANES - n_route)).reshape(1, LANES)
    wr_hi = wr.astype(BF16)
    p["w_route"] = jnp.stack([wr_hi, (wr - wr_hi.astype(F32)).astype(BF16)])
    p["b_route"] = br
    e_of_col = jnp.arange(N_EXPERTS * D_FF) // D_FF
    p["route_expand"] = (jnp.arange(LANES)[:, None] == (e_of_col[None, :] + ROUTE_E0)).astype(BF16)
    gw = EXPERTS_PER_GROUP * D_FF
    p["w_eg"] = w_exp_gate[l].astype(BF16)
    p["w_eu"] = w_exp_up[l].astype(BF16)
    p["w_ed"] = w_exp_down[l].reshape(MOE_GROUPS, gw, D).astype(BF16)
    return p


def _gla_state_in(s):
    st = jnp.swapaxes(s, -1, -2)
    z = jnp.zeros_like(st)
    halves = [jnp.concatenate([st[:, h], z[:, h]] if h % 2 == 0 else [z[:, h], st[:, h]], axis=-1)
              for h in range(GLA_HEADS)]
    return jnp.stack(halves, axis=1)


def _gla_state_out(st):
    parts = [st[:, h, :, GLA_DK * (h % 2):GLA_DK * (h % 2 + 1)] for h in range(GLA_HEADS)]
    return jnp.swapaxes(jnp.stack(parts, axis=1), -1, -2)


TOKEN_TILE = 512
MIXER_TILE = 256
MIXER_CHUNKS = 8
MOE_HALVES = 2
MOE_TILE = 512 * MOE_HALVES


def _tiles(B, T):
    lt = min(TOKEN_TILE, T)
    nb = max(1, min(B, TOKEN_TILE // lt))
    lt_moe = min(MOE_TILE, T)
    nb_moe = max(1, min(B, MOE_TILE // lt_moe))
    lt_rec = min(MIXER_TILE, T)
    nb_rec = max(1, min(B, MIXER_CHUNKS // (lt_rec // CHUNK)))
    return dict(nb=nb, lt=lt, nb_rec=nb_rec, lt_rec=lt_rec, lt_s5=min(CHUNK, T), nb_moe=nb_moe, lt_moe=lt_moe)


def _trunk(x, mods, params, states, final_w):
    tiles = _tiles(x.shape[0], x.shape[1])
    nb, lt, lt_rec, lt_s5 = tiles["nb"], tiles["lt"], tiles["lt_rec"], tiles["lt_s5"]
    new = []
    for l, p in enumerate(params):
        s_gla, s_gdn, cbuf, s5_re, s5_im = states[l]
        B = x.shape[0]
        cb8 = jnp.concatenate([jnp.zeros((B, SUBLANES - (CONV_W - 1), GDN_CONV_CH), F32), cbuf], axis=1)
        zg, zd, zs, cb_new = _inproj_call(x, mods[l], cb8, p, nb, lt)
        oa, ob, gla_t, gdn_new = _mixers_call(zg, zd, _gla_state_in(s_gla), s_gdn, p["gla_norm_w"],
                                              p["gdn_norm_w"], tiles["nb_rec"], lt_rec)
        h0 = jnp.concatenate([s5_re.reshape(B, S5_HALF), s5_im.reshape(B, S5_HALF)], axis=1)
        ys, h_new = _s5_call(zs, h0, p, lt_s5)
        x = _merge_call(x, mods[l], oa, ob, ys, p, nb, lt)
        x = _moe_call(x, mods[l], p, final_w, tiles["nb_moe"], tiles["lt_moe"],
                      final_norm=(l == len(params) - 1))
        new.append((_gla_state_out(gla_t), gdn_new, cb_new[:, SUBLANES - (CONV_W - 1):],
                    h_new[:, :S5_HALF].reshape(B, S5_GROUPS, S5_STATE),
                    h_new[:, S5_HALF:].reshape(B, S5_GROUPS, S5_STATE)))
    return x, [jnp.stack([new[l][i] for l in range(len(params))]) for i in range(5)]


def kernel(x_prompt, x_sample, c_prompt, c_sample, state_gla, state_gdn, cache_gdn_conv, state_s5_re, state_s5_im, norm1_w, norm2_w, final_norm_w, w_ada, b_ada, w_in, w_alpha_up, b_alpha, gla_norm_w, conv_w, gdn_a_log, gdn_dt_bias, gdn_norm_w, s5_a_re, s5_a_im, s5_log_dt, s5_b_re, s5_b_im, s5_c_re, s5_c_im, s5_d, w_glu, b_glu, w_br_a, w_br_b, w_br_c, w_gate, b_gate, w_out, w_route_group, b_route_group, w_route_expert, b_route_expert, w_exp_gate, w_exp_up, w_exp_down):
    depth = w_in.shape[0]
    bp, tp, _ = x_prompt.shape
    bs, ts, _ = x_sample.shape
    params = [_prep_layer(l, norm1_w, norm2_w, w_in, w_alpha_up, b_alpha, gla_norm_w, conv_w, gdn_a_log,
                          gdn_dt_bias, gdn_norm_w, s5_a_re, s5_a_im, s5_log_dt, s5_b_re, s5_b_im, s5_c_re,
                          s5_c_im, s5_d, w_glu, b_glu, w_br_a, w_br_b, w_br_c, w_gate, b_gate, w_out,
                          w_route_group, b_route_group, w_route_expert, b_route_expert, w_exp_gate, w_exp_up,
                          w_exp_down) for l in range(depth)]
    final_w = final_norm_w.reshape(1, D_MODEL)

    mod = _ada_call(jnp.concatenate([c_prompt, c_sample], axis=0), w_ada, b_ada)
    mod = mod.reshape(depth, bp + bs, 6, D_MODEL)
    mods_p = [mod[l, :bp] for l in range(depth)]
    mods_s = [mod[l, bp:] for l in range(depth)]

    zeros_p = [(jnp.zeros((bp, GLA_HEADS, GLA_DK, GLA_DV), F32), jnp.zeros((bp, GDN_HEADS, GDN_DK, GDN_DV), F32),
                jnp.zeros((bp, CONV_W - 1, GDN_CONV_CH), F32), jnp.zeros((bp, S5_GROUPS, S5_STATE), F32),
                jnp.zeros((bp, S5_GROUPS, S5_STATE), F32)) for _ in range(depth)]
    y_p, new_p = _trunk(x_prompt, mods_p, params, zeros_p, final_w)

    states_s = [(state_gla[l], state_gdn[l], cache_gdn_conv[l], state_s5_re[l], state_s5_im[l])
                for l in range(depth)]
    y_s, new_s = _trunk(x_sample, mods_s, params, states_s, final_w)
    return (y_p, y_s, *new_p, *new_s)
```

```python
import functools
import math

import jax
import jax.numpy as jnp
from jax import lax
from jax.experimental import pallas as pl
from jax.experimental.pallas import tpu as pltpu

F32 = jnp.float32
BF16 = jnp.bfloat16

D_MODEL = 1024
EPS = 1e-6
CHUNK = 64
BRANCH_W = 512
GLA_HEADS, GLA_DK, GLA_DV, GLA_LOWRANK, GLA_TAU = 4, 64, 128, 16, 16.0
GDN_HEADS, GDN_DK, GDN_DV, CONV_W = 4, 128, 128, 4
S5_GROUP, S5_GROUPS, S5_STATE = 16, 32, 64
GLA_QK, GLA_V, GDN_QK, GDN_V = 256, 512, 512, 512
GDN_CONV_CH = 1536
MOE_GROUPS, EXPERTS_PER_GROUP, N_EXPERTS, D_FF = 4, 4, 16, 256

LANES = 128
SUBLANES = 8
VMEM_LIMIT = 56 * 1024 * 1024
S5_HALF = S5_GROUPS * S5_STATE
SMALL_BETA0, SMALL_A0 = 16, 20
ROUTE_E0 = 4


def _mm(a, b):
    return jnp.dot(a.astype(BF16), b.astype(BF16), preferred_element_type=F32)


def _mm_nt(a, b):
    return lax.dot_general(a.astype(BF16), b.astype(BF16), (((1,), (1,)), ((), ())), preferred_element_type=F32)


def _mm_tn(a, b):
    return lax.dot_general(a.astype(BF16), b.astype(BF16), (((0,), (0,)), ((), ())), preferred_element_type=F32)


def _split3(x):
    a = x.astype(BF16)
    r = x - a.astype(F32)
    b = r.astype(BF16)
    c = (r - b.astype(F32)).astype(BF16)
    return a, b, c


def _mm_sel(m01, x):
    a, b, c = _split3(x)
    return _mm(m01, a) + _mm(m01, b) + _mm(m01, c)


def _mm_sel_r(x, m01):
    a, b, c = _split3(x)
    return _mm(a, m01) + _mm(b, m01) + _mm(c, m01)


def _mm_nt_sel(m01, x):
    a, b, c = _split3(x)
    return _mm_nt(m01, a) + _mm_nt(m01, b) + _mm_nt(m01, c)


def _sigmoid(x):
    return 1.0 / (1.0 + jnp.exp(-x))


def _silu(x):
    return x * _sigmoid(x)


def _softplus(x):
    return jnp.maximum(x, 0.0) + jnp.log1p(jnp.exp(-jnp.abs(x)))


def _rms_rows(x, w):
    ms = jnp.mean(x * x, axis=-1, keepdims=True)
    return x * lax.rsqrt(ms + EPS) * w


def _gelu_tanh(x):
    c = math.sqrt(2.0 / math.pi)
    return x * (0.5 * (1.0 + jnp.tanh(c * (x + 0.044715 * (x * x * x)))))


def _const_spec(shape):
    nd = len(shape)
    return pl.BlockSpec(shape, lambda *_: (0,) * nd)


def _time_major_shape(B, T, nb):
    return (T, B * BRANCH_W)


def _time_major_spec(nb, lt):
    return pl.BlockSpec((lt, nb * BRANCH_W), lambda b, t: (t, b))


def _params(sem):
    return pltpu.CompilerParams(dimension_semantics=sem, vmem_limit_bytes=VMEM_LIMIT)


def _ada_kernel(c_ref, w_ref, b_ref, o_ref):
    c = c_ref[...]
    o_ref[0] = _mm(_silu(c), w_ref[0]) + b_ref[0]


def _ada_call(c_all, w_ada, b_ada):
    depth = w_ada.shape[0]
    nrow = c_all.shape[0]
    return pl.pallas_call(
        _ada_kernel,
        grid=(depth, 6),
        in_specs=[pl.BlockSpec((nrow, D_MODEL), lambda l, j: (0, 0)),
                  pl.BlockSpec((1, D_MODEL, D_MODEL), lambda l, j: (l, 0, j)),
                  pl.BlockSpec((1, 1, D_MODEL), lambda l, j: (l, 0, j))],
        out_specs=pl.BlockSpec((1, nrow, D_MODEL), lambda l, j: (l, 0, j)),
        out_shape=jax.ShapeDtypeStruct((depth, nrow, 6 * D_MODEL), F32),
        compiler_params=_params(("parallel", "parallel")),
        name="adaln",
    )(c_all, w_ada, b_ada.reshape(depth, 1, 6 * D_MODEL))


MXU_COLS = 256


def _inproj_kernel(x_ref, mod_ref, nw_ref, wg_ref, wd_ref, ws_ref, wsm_ref, wup_ref, bal_ref, sp_ref, cb_ref,
                   cw_ref, zg_ref, zd_ref, zs_ref, cbo_ref, *xbufs):
    nb, lt, d = x_ref.shape
    span = lt + SUBLANES

    @pl.when(pl.program_id(1) == 0)
    def _():
        for i in range(nb):
            for k, xb in enumerate(xbufs):
                xb[span * i + lt:span * (i + 1), :] = cb_ref[i, :, MXU_COLS * k:MXU_COLS * (k + 1)]

    x = x_ref[...]
    h = _rms_rows(x, nw_ref[...]) * (1.0 + mod_ref[:, 1:2, :]) + mod_ref[:, 0:1, :]
    h2 = h.reshape(nb * lt, d).astype(BF16)

    base = SUBLANES - (CONV_W - 1)
    for i in range(nb):
        for xb in xbufs:
            xb[span * i:span * i + SUBLANES, :] = xb[span * i + lt:span * (i + 1), :]

    def gla_block(j):
        dst = MXU_COLS * j if j < 2 else MXU_COLS * (j + 1)
        zg_ref[:, :, dst:dst + MXU_COLS] = _mm(h2, wg_ref[:, MXU_COLS * j:MXU_COLS * (j + 1)]).reshape(
            nb, lt, MXU_COLS)

    def gate_block(j):
        cs = slice(GDN_CONV_CH + MXU_COLS * j, GDN_CONV_CH + MXU_COLS * (j + 1))
        zd_ref[:, :, cs] = _mm(h2, wd_ref[:, cs]).reshape(nb, lt, MXU_COLS)

    def s5_block(j):
        zs = _mm(h2, ws_ref[:, MXU_COLS * j:MXU_COLS * (j + 1)])
        for i in range(nb):
            zs_ref[:, BRANCH_W * i + MXU_COLS * j:BRANCH_W * i + MXU_COLS * (j + 1)] = zs[lt * i:lt * (i + 1), :]

    def small_block(_):
        zsm = _mm(h2, wsm_ref[...])
        la_pre = _mm(zsm, wup_ref[...]) + bal_ref[...]
        log_a = (jnp.minimum(la_pre, 0.0) - jnp.log1p(jnp.exp(-jnp.abs(la_pre)))) * (1.0 / GLA_TAU)
        zg_ref[:, :, 512:768] = log_a.reshape(nb, lt, 256)
        lane = lax.broadcasted_iota(jnp.int32, zsm.shape, 1)
        beta = _sigmoid(zsm)
        log_g = -sp_ref[0:1, :] * _softplus(zsm + sp_ref[1:2, :])
        is_beta = (lane >= SMALL_BETA0) & (lane < SMALL_A0)
        is_a = (lane >= SMALL_A0) & (lane < SMALL_A0 + GDN_HEADS)
        small = jnp.where(is_beta, beta, jnp.where(is_a, log_g, 0.0))
        zd_ref[:, :, 2048:2176] = small.reshape(nb, lt, LANES)

    fillers = [[(gla_block, 0), (gate_block, 0)], [(gla_block, 1), (gate_block, 1)],
               [(gla_block, 2), (s5_block, 0)], [(gla_block, 3), (s5_block, 1)],
               [(gla_block, 4), (small_block, 0)], [(gla_block, 5)]]
    def conv_matmul(mblk):
        zd = _mm(h2, wd_ref[:, MXU_COLS * mblk:MXU_COLS * (mblk + 1)])
        for i in range(nb):
            xbufs[mblk][span * i + SUBLANES:span * (i + 1), :] = zd[lt * i:lt * (i + 1), :]
        for fn, arg in fillers[mblk]:
            fn(arg)

    n_blk = GDN_CONV_CH // MXU_COLS
    conv_matmul(0)
    for mblk in range(n_blk):
        if mblk + 1 < n_blk:
            conv_matmul(mblk + 1)
        xb = xbufs[mblk]
        for i in range(nb):
            r0 = span * i
            for cblk in range(MXU_COLS * mblk // LANES, MXU_COLS * (mblk + 1) // LANES):
                cs = slice(LANES * cblk, LANES * (cblk + 1))
                ls = slice(LANES * cblk - MXU_COLS * mblk, LANES * (cblk + 1) - MXU_COLS * mblk)
                conv = xb[r0 + base:r0 + base + lt, ls] * cw_ref[0:1, cs]
                for j in range(1, CONV_W):
                    conv = conv + xb[r0 + base + j:r0 + base + j + lt, ls] * cw_ref[j:j + 1, cs]
                act = _silu(conv)
                if cblk < 2 * GDN_HEADS:
                    scale = GDN_DK ** -0.5 if cblk < GDN_HEADS else 1.0
                    ss = jnp.sum(act * act, axis=-1, keepdims=True)
                    act = act * (lax.rsqrt(ss + EPS) * scale)
                zd_ref[i, :, cs] = act
    for i in range(nb):
        for k, xb in enumerate(xbufs):
            cbo_ref[i, :, MXU_COLS * k:MXU_COLS * (k + 1)] = xb[span * i + lt:span * (i + 1), :]


def _inproj_call(x, mod, cb8, p, nb, lt):
    B, T, D = x.shape
    assert nb == 1 or T == lt, "several sequences per step only when each is a single time block"
    grid = (B // nb, T // lt)
    tok = lambda w: pl.BlockSpec((nb, lt, w), lambda b, t: (b, t, 0))
    cb_spec = pl.BlockSpec((nb, SUBLANES, GDN_CONV_CH), lambda b, t: (b, 0, 0))
    return pl.pallas_call(
        _inproj_kernel,
        grid=grid,
        in_specs=[tok(D),
                  pl.BlockSpec((nb, 6, D), lambda b, t: (b, 0, 0)),
                  _const_spec((1, D)),
                  _const_spec(p["w_gla"].shape), _const_spec(p["w_gdn"].shape), _const_spec(p["w_s5"].shape),
                  _const_spec(p["w_small"].shape), _const_spec(p["w_up"].shape), _const_spec((1, GLA_QK)),
                  _const_spec((2, LANES)), cb_spec, _const_spec((CONV_W, GDN_CONV_CH))],
        out_specs=[tok(1792), tok(2176), _time_major_spec(nb, lt), cb_spec],
        out_shape=[jax.ShapeDtypeStruct((B, T, 1792), F32),
                   jax.ShapeDtypeStruct((B, T, 2176), F32),
                   jax.ShapeDtypeStruct(_time_major_shape(B, T, nb), F32),
                   jax.ShapeDtypeStruct((B, SUBLANES, GDN_CONV_CH), F32)],
        scratch_shapes=[pltpu.VMEM((nb * (lt + SUBLANES), MXU_COLS), F32)] * (GDN_CONV_CH // MXU_COLS),
        compiler_params=_params(("parallel", "arbitrary")),
        name="inproj",
    )(x, mod, p["norm1_w"], p["w_gla"], p["w_gdn"], p["w_s5"], p["w_small"], p["w_up"], p["b_alpha"],
      p["gdn_small"], cb8, p["conv_w"])


def _gla_stages(z_ref, nw_ref, o_ref, s_ref):
    L = CHUNK
    nb, lt, _ = z_ref.shape
    n_chunks = lt // L
    n_rows = nb * lt
    row = lax.broadcasted_iota(jnp.int32, (L, L), 0)
    col = lax.broadcasted_iota(jnp.int32, (L, L), 1)
    causal = row >= col
    lane = lax.broadcasted_iota(jnp.int32, (L, LANES), 1)
    own_half = (lane < GLA_DK, lane >= GLA_DK)
    nw = nw_ref[...]
    brow = lax.broadcasted_iota(jnp.int32, (n_rows, n_rows), 0)
    bcol = lax.broadcasted_iota(jnp.int32, (n_rows, n_rows), 1)
    shift = L.bit_length() - 1
    chunk_tril = ((brow >= bcol) & ((brow >> shift) == (bcol >> shift))).astype(BF16)
    b_all = _mm_sel(chunk_tril, z_ref[:, :, 512:768].reshape(n_rows, GLA_QK))

    def zcols(c, lo, width):
        i, cc = divmod(c, n_chunks)
        return z_ref[i, L * cc:L * (cc + 1), lo:lo + width]

    all_chunks = range(nb * n_chunks)
    probs = [(c, h) for c in all_chunks for h in range(GLA_HEADS)]
    rows_of = lambda c: slice(L * c, L * (c + 1))
    pair_of = lambda h: slice(LANES * (h // 2), LANES * (h // 2) + LANES)
    qi_, ki_, qs_, kl_, dl_ = {}, {}, {}, {}, {}
    for c in all_chunks:
        q = zcols(c, 0, GLA_QK)
        k = zcols(c, GLA_QK, GLA_QK)
        b = b_all[rows_of(c), :]
        b_mid = b[L // 2 - 1:L // 2, :]
        b_last = b[L - 1:L, :]
        qi_[c] = q * jnp.exp(b - b_mid)
        ki_[c] = k * jnp.exp(b_mid - b)
        qs_[c] = q * jnp.exp(b)
        kl_[c] = k * jnp.exp(b_last - b)
        dl_[c] = jnp.exp(b_last)
    v_ = {p: zcols(p[0], 768 + GLA_DV * p[1], GLA_DV) for p in probs}
    att_, kv_, av_ = {}, {}, {}

    def scores():
        for p in probs:
            ki_h = jnp.where(own_half[p[1] % 2], ki_[p[0]][:, pair_of(p[1])], 0.0)
            att_[p] = jnp.where(causal, _mm_nt(qi_[p[0]][:, pair_of(p[1])], ki_h), 0.0)

    def increments():
        for p in probs:
            kv_[p] = _mm_tn(v_[p], jnp.where(own_half[p[1] % 2], kl_[p[0]][:, pair_of(p[1])], 0.0))

    def intra():
        for p in probs:
            av_[p] = _mm(att_[p], v_[p])

    def outputs():
        st_ = {}
        for i in range(nb):
            for h in range(GLA_HEADS):
                st = s_ref[i, h]
                for cc in range(n_chunks):
                    c = n_chunks * i + cc
                    st_[(c, h)] = st
                    st = st * dl_[c][:, pair_of(h)] + kv_[(c, h)]
                s_ref[i, h] = st
        for p in probs:
            c, h = p
            i, cc = divmod(c, n_chunks)
            o_h = av_[p] + _mm_nt(qs_[c][:, pair_of(h)], st_[p])
            gate = zcols(c, 1280 + GLA_DV * h, GLA_DV)
            o_ref[i, L * cc:L * (cc + 1), GLA_DV * h:GLA_DV * (h + 1)] = _rms_rows(o_h, nw) * _silu(gate)

    return [scores, increments, intra, outputs]


def _mixers_kernel(zg_ref, zd_ref, sa0_ref, sb0_ref, nwa_ref, nwb_ref, oa_ref, ob_ref, sa_ref, sb_ref, bfull_ref,
                   gfull_ref, grow_ref, glast_ref):
    @pl.when(pl.program_id(1) == 0)
    def _():
        sa_ref[...] = sa0_ref[...]
        sb_ref[...] = sb0_ref[...]

    _gdn_body(zd_ref, nwb_ref, ob_ref, sb_ref, bfull_ref, gfull_ref, grow_ref, glast_ref,
              _gla_stages(zg_ref, nwa_ref, oa_ref, sa_ref))


def _gdn_body(z_ref, nw_ref, o_ref, s_ref, bfull_ref, gfull_ref, grow_ref, glast_ref, fillers):
    L = CHUNK
    nb, lt, _ = z_ref.shape
    n_chunks = lt // L
    n_rows = nb * lt
    fillers = list(fillers)
    row = lax.broadcasted_iota(jnp.int32, (L, L), 0)
    col = lax.broadcasted_iota(jnp.int32, (L, L), 1)
    causal = row >= col
    strict = row > col
    eye = (row == col).astype(F32)
    diag16 = (row >> 4) == (col >> 4)
    diag32 = (row >> 5) == (col >> 5)
    off32 = diag32 & jnp.logical_not(diag16)
    off64 = jnp.logical_not(diag32)
    sel_r = lax.broadcasted_iota(jnp.int32, (SUBLANES, LANES), 0)
    sel_c = lax.broadcasted_iota(jnp.int32, (SUBLANES, LANES), 1)
    sel = (sel_c == sel_r + SMALL_A0).astype(BF16)
    brow = lax.broadcasted_iota(jnp.int32, (n_rows, n_rows), 0)
    bcol = lax.broadcasted_iota(jnp.int32, (n_rows, n_rows), 1)
    shift = L.bit_length() - 1
    chunk_tril = ((brow >= bcol) & ((brow >> shift) == (bcol >> shift))).astype(BF16)
    small_all = z_ref[:, :, 2048:2176].reshape(n_rows, LANES)
    gs_all = _mm_sel(chunk_tril, small_all)
    grow_all = _mm_nt_sel(sel, gs_all)
    e_r = lax.broadcasted_iota(jnp.int32, (LANES, GDN_HEADS * LANES), 0)
    e_h = lax.broadcasted_iota(jnp.int32, (LANES, GDN_HEADS * LANES), 1) >> (LANES.bit_length() - 1)
    bfull_ref[...] = _mm_sel_r(small_all, (e_r == e_h + SMALL_BETA0).astype(BF16))
    gfull = _mm_sel_r(gs_all, (e_r == e_h + SMALL_A0).astype(BF16))
    gfull_ref[...] = gfull
    for c in range(nb * n_chunks):
        grow_ref[c] = grow_all[:, L * c:L * (c + 1)]
        glast_ref[c:c + 1, :] = gfull[L * (c + 1) - 1:L * (c + 1), :]
    nw = nw_ref[...]

    probs = [(c, h) for c in range(nb * n_chunks) for h in range(GDN_HEADS)]
    rows_of = lambda c: slice(L * c, L * (c + 1))
    head_of = lambda h: slice(LANES * h, LANES * (h + 1))

    def zcols(c, lo, width):
        i, cc = divmod(c, n_chunks)
        return z_ref[i, L * cc:L * (cc + 1), lo:lo + width]

    q_ = {p: zcols(p[0], GDN_DK * p[1], GDN_DK) for p in probs}
    k_ = {p: zcols(p[0], GDN_QK + GDN_DK * p[1], GDN_DK) for p in probs}
    beta_ = {p: bfull_ref[rows_of(p[0]), head_of(p[1])] for p in probs}
    g_ = {p: gfull_ref[rows_of(p[0]), head_of(p[1])] for p in probs}
    dec_ = {p: jnp.exp(jnp.where(causal, g_[p][:, 0:L] - grow_ref[p[0]][p[1]:p[1] + 1, :], -jnp.inf))
            for p in probs}
    kk_ = {p: _mm_nt(k_[p], k_[p]) for p in probs}
    qk_ = {p: _mm_nt(q_[p], k_[p]) for p in probs}
    a_ = {p: jnp.where(strict, beta_[p][:, 0:L] * kk_[p] * dec_[p], 0.0) for p in probs}
    qk_ = {p: jnp.where(causal, qk_[p] * dec_[p], 0.0) for p in probs}
    pw_ = {p: jnp.where(diag16, -a_[p], 0.0) for p in probs}
    tinv_ = {p: eye + pw_[p] for p in probs}
    for _ in range(3):
        pw_ = {p: _mm(pw_[p], pw_[p]) for p in probs}
        tinv_ = {p: tinv_[p] + _mm(tinv_[p], pw_[p]) for p in probs}
    for off in (off32, off64):
        te_ = {p: _mm(tinv_[p], jnp.where(off, a_[p], 0.0)) for p in probs}
        tinv_ = {p: tinv_[p] - _mm(te_[p], tinv_[p]) for p in probs}
    eg_ = {p: jnp.exp(g_[p]) for p in probs}
    sol_ = {}
    for p in probs:
        v_p = zcols(p[0], 2 * GDN_QK + GDN_DV * p[1], GDN_DV)
        rhs = jnp.concatenate([beta_[p] * v_p, (beta_[p] * eg_[p]) * k_[p]], axis=-1)
        sol_[p] = _mm(tinv_[p], rhs)

    st_ = {(i, h): s_ref[i, h] for i in range(nb) for h in range(GDN_HEADS)}
    for cc in range(n_chunks):
        heads = [(i, n_chunks * i + cc, h) for i in range(nb) for h in range(GDN_HEADS)]
        u_, o_, new_ = {}, {}, {}
        for i, c, h in heads:
            u_[(c, h)] = sol_[(c, h)][:, 0:GDN_DV] - _mm(sol_[(c, h)][:, GDN_DV:], st_[(i, h)])
        if fillers:
            fillers.pop(0)()
        for i, c, h in heads:
            o_[(c, h)] = _mm(qk_[(c, h)], u_[(c, h)]) + _mm(q_[(c, h)] * eg_[(c, h)], st_[(i, h)])
        for i, c, h in heads:
            g_last = glast_ref[c:c + 1, head_of(h)]
            new_[(i, h)] = (st_[(i, h)] * jnp.exp(g_last)
                            + _mm_tn(k_[(c, h)] * jnp.exp(g_last - g_[(c, h)]), u_[(c, h)]))
        st_ = new_
        for i, c, h in heads:
            gate = z_ref[i, L * cc:L * (cc + 1), GDN_CONV_CH + GDN_DV * h:GDN_CONV_CH + GDN_DV * (h + 1)]
            o_ref[i, L * cc:L * (cc + 1), GDN_DV * h:GDN_DV * (h + 1)] = _rms_rows(o_[(c, h)], nw) * _silu(gate)
    for i in range(nb):
        for h in range(GDN_HEADS):
            s_ref[i, h] = st_[(i, h)]
    for rest in fillers:
        rest()


def _mixers_call(zg, zd, sa0, sb0, nwa, nwb, nb, lt):
    B, T, _ = zd.shape
    n_rows = nb * lt
    last_rows = -(-(n_rows // CHUNK) // SUBLANES) * SUBLANES
    sa_spec = pl.BlockSpec((nb, GLA_HEADS, GLA_DV, LANES), lambda b, t: (b, 0, 0, 0))
    sb_spec = pl.BlockSpec((nb, GDN_HEADS, GDN_DK, GDN_DV), lambda b, t: (b, 0, 0, 0))
    tok = lambda w: pl.BlockSpec((nb, lt, w), lambda b, t: (b, t, 0))
    return pl.pallas_call(
        _mixers_kernel,
        grid=(B // nb, T // lt),
        in_specs=[tok(1792), tok(2176), sa_spec, sb_spec, _const_spec((1, GLA_DV)), _const_spec((1, GDN_DV))],
        out_specs=[tok(GLA_V), tok(GDN_V), sa_spec, sb_spec],
        out_shape=[jax.ShapeDtypeStruct((B, T, GLA_V), F32), jax.ShapeDtypeStruct((B, T, GDN_V), F32),
                   jax.ShapeDtypeStruct((B, GLA_HEADS, GLA_DV, LANES), F32),
                   jax.ShapeDtypeStruct((B, GDN_HEADS, GDN_DK, GDN_DV), F32)],
        scratch_shapes=[pltpu.VMEM((n_rows, GDN_HEADS * LANES), F32), pltpu.VMEM((n_rows, GDN_HEADS * LANES), F32),
                        pltpu.VMEM((n_rows // CHUNK, SUBLANES, CHUNK), F32),
                        pltpu.VMEM((last_rows, GDN_HEADS * LANES), F32)],
        compiler_params=_params(("parallel", "arbitrary")),
        name="mixers",
    )(zg, zd, sa0, sb0, nwa, nwb)


S5_LANE_CHUNK = 512


def _s5_kernel(u_ref, h0_ref, bb_ref, cc_ref, lam_ref, d_ref, y_ref, h_ref, bu_ref, *, lt):
    nb = SUBLANES

    @pl.when(pl.program_id(1) == 0)
    def _():
        h_ref[...] = h0_ref[...]

    u = u_ref[...].reshape(lt * nb, BRANCH_W)
    ch, sh = BRANCH_W // 2, S5_HALF // 2
    for part in range(2):
        for hb in range(2):
            bu_ref[:, part * S5_HALF + hb * sh:part * S5_HALF + (hb + 1) * sh] = _mm(
                u[:, hb * ch:(hb + 1) * ch], bb_ref[2 * part + hb])
    for j in range(S5_HALF // S5_LANE_CHUNK):
        re = slice(S5_LANE_CHUNK * j, S5_LANE_CHUNK * (j + 1))
        im = slice(S5_HALF + S5_LANE_CHUNK * j, S5_HALF + S5_LANE_CHUNK * (j + 1))
        lr = jnp.broadcast_to(lam_ref[0:1, re], (nb, S5_LANE_CHUNK))
        li = jnp.broadcast_to(lam_ref[1:2, re], (nb, S5_LANE_CHUNK))

        hr, hi = h_ref[:, re], h_ref[:, im]
        for tt in range(lt):
            rows = slice(tt * nb, (tt + 1) * nb)
            hr, hi = lr * hr - li * hi + bu_ref[rows, re], lr * hi + li * hr + bu_ref[rows, im]
            bu_ref[rows, re] = hr
            bu_ref[rows, im] = hi
        h_ref[:, re] = hr
        h_ref[:, im] = hi
    ys = [_mm(bu_ref[:, hb * sh:(hb + 1) * sh], cc_ref[hb])
          + _mm(bu_ref[:, S5_HALF + hb * sh:S5_HALF + (hb + 1) * sh], cc_ref[2 + hb]) for hb in range(2)]
    y = jnp.concatenate(ys, axis=-1) + d_ref[...] * u
    y_ref[...] = y.reshape(lt, nb, BRANCH_W)


def _s5_call(zs, h0, p, lt):
    T = zs.shape[0]
    B = h0.shape[0]
    nb = SUBLANES
    h_spec = pl.BlockSpec((nb, 2 * S5_HALF), lambda b, t: (b, 0))
    tok_spec = pl.BlockSpec((lt, None, nb, BRANCH_W), lambda b, t: (t, b, 0, 0))
    y, h = pl.pallas_call(
        functools.partial(_s5_kernel, lt=lt),
        grid=(B // nb, T // lt),
        in_specs=[tok_spec, h_spec,
                  _const_spec((4, BRANCH_W // 2, S5_HALF // 2)), _const_spec((4, S5_HALF // 2, BRANCH_W // 2)),
                  _const_spec((2, S5_HALF)), _const_spec((1, BRANCH_W))],
        out_specs=[tok_spec, h_spec],
        out_shape=[jax.ShapeDtypeStruct((T, B // nb, nb, BRANCH_W), F32),
                   jax.ShapeDtypeStruct((B, 2 * S5_HALF), F32)],
        scratch_shapes=[pltpu.VMEM((lt * nb, 2 * S5_HALF), F32)],
        compiler_params=_params(("parallel", "arbitrary")),
        name="s5",
    )(zs.reshape(T, B // nb, nb, BRANCH_W), h0, p["s5_bblk"], p["s5_cblk"], p["s5_lam"], p["s5_d"])
    return y.reshape(zs.shape), h


def _merge_kernel(x_ref, mod_ref, nw_ref, oa_ref, ob_ref, ys_ref, wgate_ref, bgate_ref, wa_ref, wb_ref, wc_ref,
                  wglu_ref, bglu_ref, wout_ref, o_ref):
    nb, lt, d = x_ref.shape
    n = nb * lt
    x = x_ref[...]
    h = _rms_rows(x, nw_ref[...]) * (1.0 + mod_ref[:, 1:2, :]) + mod_ref[:, 0:1, :]
    gate = _sigmoid(_mm(h.reshape(n, d), wgate_ref[...]) + bgate_ref[...])
    y_a = _mm(oa_ref[...].reshape(n, BRANCH_W), wa_ref[...])
    y_b = _mm(ob_ref[...].reshape(n, BRANCH_W), wb_ref[...])
    ys = [ys_ref[:, BRANCH_W * i:BRANCH_W * (i + 1)] for i in range(nb)]
    yc = _gelu_tanh(ys[0] if nb == 1 else jnp.concatenate(ys, axis=0))
    yc = yc * _sigmoid(_mm(yc, wglu_ref[...]) + bglu_ref[...])
    y_c = _mm(yc, wc_ref[...])
    mix = gate[:, 0:d] * y_a + gate[:, d:2 * d] * y_b + gate[:, 2 * d:3 * d] * y_c
    m = _mm(mix, wout_ref[...])
    o_ref[...] = x + mod_ref[:, 2:3, :] * m.reshape(nb, lt, d)


def _merge_call(x, mod, oa, ob, ys, p, nb, lt):
    B, T, D = x.shape
    tok = lambda w: pl.BlockSpec((nb, lt, w), lambda b, t: (b, t, 0))
    return pl.pallas_call(
        _merge_kernel,
        grid=(B // nb, T // lt),
        in_specs=[tok(D), pl.BlockSpec((nb, 6, D), lambda b, t: (b, 0, 0)), _const_spec((1, D)),
                  tok(BRANCH_W), tok(BRANCH_W), _time_major_spec(nb, lt),
                  _const_spec((D, 3 * D)), _const_spec((1, 3 * D)),
                  _const_spec((BRANCH_W, D)), _const_spec((BRANCH_W, D)), _const_spec((BRANCH_W, D)),
                  _const_spec((BRANCH_W, BRANCH_W)), _const_spec((1, BRANCH_W)), _const_spec((D, D))],
        out_specs=tok(D),
        out_shape=jax.ShapeDtypeStruct((B, T, D), F32),
        compiler_params=_params(("parallel", "parallel")),
        name="merge",
    )(x, mod, p["norm1_w"], oa, ob, ys, p["w_gate"], p["b_gate"], p["w_br_a"], p["w_br_b"], p["w_br_c"],
      p["w_glu"], p["b_glu"], p["w_out"])


MOE_ROWS = 192


def _moe_kernel(x_ref, mod_ref, nw_ref, wr_ref, br_ref, ex_ref, wg_ref, wu_ref, wd_ref, fw_ref, o_ref,
                hn_ref, comb_ref, krow_ref, acc_ref, *, final_norm):
    nb, lt, d = x_ref.shape
    R = MOE_ROWS
    gw = EXPERTS_PER_GROUP * D_FF
    halves = hn_ref.shape[0]
    if nb >= halves:
        hb, hl = nb // halves, lt
    else:
        hb, hl = nb, lt // halves
    n = hb * hl
    row_pos = lax.broadcasted_iota(jnp.int32, (R, n), 0).astype(F32)
    lane = lax.broadcasted_iota(jnp.int32, (n, LANES), 1)
    trow = lax.broadcasted_iota(jnp.int32, (n, n), 0)
    tcol = lax.broadcasted_iota(jnp.int32, (n, n), 1)
    before = (trow > tcol).astype(BF16)
    sel_r = lax.broadcasted_iota(jnp.int32, (SUBLANES, LANES), 0)
    sel_c = lax.broadcasted_iota(jnp.int32, (SUBLANES, LANES), 1)
    pick = (sel_r == sel_c).astype(BF16)
    neg = -jnp.inf

    def xs(k):
        return x_ref[hb * k:hb * (k + 1)] if nb >= halves else x_ref[:, hl * k:hl * (k + 1), :]

    def mods(k, row):
        m = mod_ref[hb * k:hb * (k + 1)] if nb >= halves else mod_ref[...]
        return m[:, row:row + 1, :]

    def route(k):
        hn = (_rms_rows(xs(k), nw_ref[...]) * (1.0 + mods(k, 4)) + mods(k, 3)).reshape(n, d)
        hn_hi = hn.astype(BF16)
        hn_ref[k] = hn_hi
        hn_lo = (hn - hn_hi.astype(F32)).astype(BF16)
        logits = _mm(hn_hi, wr_ref[0]) + (_mm(hn_hi, wr_ref[1]) + _mm(hn_lo, wr_ref[0])) + br_ref[...]
        gl = jnp.where(lane < MOE_GROUPS, logits, neg)
        gmax = jnp.max(gl, axis=-1, keepdims=True)
        g_sel = jnp.min(jnp.where(gl == gmax, lane, LANES), axis=-1, keepdims=True)
        p_sel = 1.0 / jnp.sum(jnp.exp(gl - gmax), axis=-1, keepdims=True)
        e_id = lane - ROUTE_E0
        in_group = (e_id >= 0) & (e_id < N_EXPERTS) & ((e_id >> 2) == g_sel)
        el = jnp.where(in_group, logits, neg)
        m1 = jnp.max(el, axis=-1, keepdims=True)
        i1 = jnp.min(jnp.where(el == m1, lane, LANES), axis=-1, keepdims=True)
        el2 = jnp.where(lane == i1, neg, el)
        m2 = jnp.max(el2, axis=-1, keepdims=True)
        i2 = jnp.min(jnp.where(el2 == m2, lane, LANES), axis=-1, keepdims=True)
        e2 = jnp.exp(m2 - m1)
        w1 = p_sel / (1.0 + e2)
        w2 = p_sel * e2 / (1.0 + e2)
        comb_ref[k] = jnp.where(lane == i1, w1, jnp.where(lane == i2, w2, 0.0)).astype(BF16)
        member = lane == g_sel
        rank = _mm(before, member.astype(BF16))
        key = jnp.where(member, rank, -1.0)
        krow_ref[k] = _mm_nt_sel(pick, key)
        return [(jnp.max(key[:, g:g + 1]).astype(jnp.int32) + R) // R for g in range(MOE_GROUPS)]

    def one_pass(k, g, s):
        base = float(s * R) if isinstance(s, int) else (s * R).astype(F32)
        gather = (krow_ref[k, g:g + 1, :] - base == row_pos).astype(BF16)
        xg = _mm(gather, hn_ref[k])
        cg = _mm(_mm(gather, comb_ref[k]), ex_ref[:, gw * g:gw * (g + 1)])
        experts = range(EXPERTS_PER_GROUP * g, EXPERTS_PER_GROUP * (g + 1))
        he = jnp.concatenate([_silu(_mm(xg, wg_ref[e])) * _mm(xg, wu_ref[e]) for e in experts], axis=-1)
        yg = _mm(he * cg, wd_ref[g])
        return _mm_tn(gather, yg)

    n_pass = [route(k) for k in range(halves)]
    for k in range(halves):
        acc = one_pass(k, 0, 0)
        for g in range(1, MOE_GROUPS):
            acc = acc + one_pass(k, g, 0)
        acc_ref[k] = acc
    for k in range(halves):
        for g in range(MOE_GROUPS):
            def extra(s, carry, k=k, g=g):
                acc_ref[k] += one_pass(k, g, s)
                return carry

            lax.fori_loop(1, n_pass[k][g], extra, 0)
    for k in range(halves):
        out = xs(k) + mods(k, 5) * acc_ref[k].reshape(hb, hl, d)
        if final_norm:
            out = _rms_rows(out, fw_ref[...])
        if nb >= halves:
            o_ref[hb * k:hb * (k + 1)] = out
        else:
            o_ref[:, hl * k:hl * (k + 1), :] = out


def _moe_call(x, mod, p, final_w, nb, lt, final_norm):
    B, T, D = x.shape
    halves = MOE_HALVES if (nb * lt) % (MOE_HALVES * SUBLANES * 2) == 0 and max(nb, lt) % MOE_HALVES == 0 else 1
    n = nb * lt // halves
    gw = EXPERTS_PER_GROUP * D_FF
    tok = pl.BlockSpec((nb, lt, D), lambda b, t: (b, t, 0))
    resident = lambda shape: pl.BlockSpec(shape, lambda *_: (0,) * len(shape), pipeline_mode=pl.Buffered(1))
    return pl.pallas_call(
        functools.partial(_moe_kernel, final_norm=final_norm),
        grid=(B // nb, T // lt),
        in_specs=[tok, pl.BlockSpec((nb, 6, D), lambda b, t: (b, 0, 0)), _const_spec((1, D)),
                  _const_spec((2, D, LANES)), _const_spec((1, LANES)), _const_spec((LANES, N_EXPERTS * D_FF)),
                  resident((N_EXPERTS, D, D_FF)), resident((N_EXPERTS, D, D_FF)), resident((MOE_GROUPS, gw, D)),
                  _const_spec((1, D))],
        out_specs=tok,
        out_shape=jax.ShapeDtypeStruct((B, T, D), F32),
        scratch_shapes=[pltpu.VMEM((halves, n, D), BF16), pltpu.VMEM((halves, n, LANES), BF16),
                        pltpu.VMEM((halves, SUBLANES, n), F32), pltpu.VMEM((halves, n, D), F32)],
        compiler_params=_params(("parallel", "parallel")),
        name="moe",
    )(x, mod, p["norm2_w"], p["w_route"], p["b_route"], p["route_expand"], p["w_eg"], p["w_eu"], p["w_ed"],
      final_w)


def _prep_layer(l, norm1_w, norm2_w, w_in, w_alpha_up, b_alpha, gla_norm_w, conv_w, gdn_a_log, gdn_dt_bias,
                gdn_norm_w, s5_a_re, s5_a_im, s5_log_dt, s5_b_re, s5_b_im, s5_c_re, s5_c_im, s5_d, w_glu, b_glu,
                w_br_a, w_br_b, w_br_c, w_gate, b_gate, w_out, w_route_group, b_route_group, w_route_expert,
                b_route_expert, w_exp_gate, w_exp_up, w_exp_down):
    D = D_MODEL
    w = w_in[l]
    o = 0
    cols = {}
    for name, size in (("qa", GLA_QK), ("ka", GLA_QK), ("va", GLA_V), ("ga", GLA_V), ("aa", GLA_LOWRANK),
                       ("qkv", GDN_CONV_CH), ("beta", GDN_HEADS), ("a", GDN_HEADS), ("gb", GDN_V),
                       ("u", BRANCH_W)):
        cols[name] = w[:, o:o + size]
        o += size
    p = {}
    p["norm1_w"] = norm1_w[l].reshape(1, D)
    p["norm2_w"] = norm2_w[l].reshape(1, D)
    p["w_gla"] = jnp.concatenate([cols["qa"] * (GLA_DK ** -0.5), cols["ka"], cols["va"], cols["ga"]],
                                 axis=1).astype(BF16)
    p["w_gdn"] = jnp.concatenate([cols["qkv"], cols["gb"]], axis=1).astype(BF16)
    p["w_s5"] = cols["u"].astype(BF16)
    assert SMALL_BETA0 == GLA_LOWRANK and SMALL_A0 == SMALL_BETA0 + GDN_HEADS
    used = SMALL_A0 + GDN_HEADS
    p["w_small"] = jnp.concatenate([cols["aa"], cols["beta"], cols["a"], jnp.zeros((D, LANES - used), F32)],
                                   axis=1).astype(BF16)
    p["w_up"] = jnp.pad(w_alpha_up[l], ((0, LANES - GLA_LOWRANK), (0, 0))).astype(BF16)
    p["b_alpha"] = b_alpha[l].reshape(1, GLA_QK)
    p["gdn_small"] = jnp.pad(jnp.stack([jnp.exp(gdn_a_log[l]), gdn_dt_bias[l]]),
                             ((0, 0), (SMALL_A0, LANES - used)))
    p["gla_norm_w"] = gla_norm_w[l].reshape(1, GLA_DV)
    p["gdn_norm_w"] = gdn_norm_w[l].reshape(1, GDN_DV)
    p["conv_w"] = conv_w[l]

    a_re, a_im = s5_a_re[l], s5_a_im[l]
    dt = jnp.exp(s5_log_dt[l])[:, None]
    mag = jnp.exp(a_re * dt)
    lr, li = mag * jnp.cos(a_im * dt), mag * jnp.sin(a_im * dt)
    den = a_re * a_re + a_im * a_im
    f_re = ((lr - 1.0) * a_re + li * a_im) / den
    f_im = (li * a_re - (lr - 1.0) * a_im) / den
    bb_re = f_re[..., None] * s5_b_re[l] - f_im[..., None] * s5_b_im[l]
    bb_im = f_re[..., None] * s5_b_im[l] + f_im[..., None] * s5_b_re[l]
    gh = S5_GROUPS // 2
    eye_h = jnp.eye(gh, dtype=F32)
    blk = lambda m: (jnp.transpose(m, (0, 2, 1))[:, :, None, :] * eye_h[:, None, :, None]
                     ).reshape(gh * S5_GROUP, gh * S5_STATE)
    cblk = lambda m: (jnp.transpose(m, (2, 0, 1))[None, :, :, :] * eye_h[:, None, :, None]
                      ).reshape(gh * S5_STATE, gh * S5_GROUP)
    p["s5_bblk"] = jnp.stack([blk(m[hb * gh:(hb + 1) * gh]) for m in (bb_re, bb_im) for hb in range(2)]
                             ).astype(BF16)
    p["s5_cblk"] = jnp.stack([cblk(m[hb * gh:(hb + 1) * gh]) for m in (s5_c_re[l], -s5_c_im[l])
                              for hb in range(2)]).astype(BF16)
    p["s5_lam"] = jnp.stack([lr.reshape(S5_HALF), li.reshape(S5_HALF)])
    p["s5_d"] = s5_d[l].reshape(1, BRANCH_W)

    p["w_gate"] = w_gate[l].astype(BF16)
    p["b_gate"] = b_gate[l].reshape(1, 3 * D)
    p["w_br_a"] = w_br_a[l].astype(BF16)
    p["w_br_b"] = w_br_b[l].astype(BF16)
    p["w_br_c"] = w_br_c[l].astype(BF16)
    p["w_glu"] = w_glu[l].astype(BF16)
    p["b_glu"] = b_glu[l].reshape(1, BRANCH_W)
    p["w_out"] = w_out[l].astype(BF16)

    assert ROUTE_E0 == MOE_GROUPS
    n_route = MOE_GROUPS + N_EXPERTS
    wr = jnp.concatenate([w_route_group[l], w_route_expert[l], jnp.zeros((D, LANES - n_route), F32)], axis=1)
    br = jnp.pad(jnp.concatenate([b_route_group[l], b_route_expert[l]]), (0, LANES - n_route)).reshape(1, LANES)
    wr_hi = wr.astype(BF16)
    p["w_route"] = jnp.stack([wr_hi, (wr - wr_hi.astype(F32)).astype(BF16)])
    p["b_route"] = br
    e_of_col = jnp.arange(N_EXPERTS * D_FF) // D_FF
    p["route_expand"] = (jnp.arange(LANES)[:, None] == (e_of_col[None, :] + ROUTE_E0)).astype(BF16)
    gw = EXPERTS_PER_GROUP * D_FF
    p["w_eg"] = w_exp_gate[l].astype(BF16)
    p["w_eu"] = w_exp_up[l].astype(BF16)
    p["w_ed"] = w_exp_down[l].reshape(MOE_GROUPS, gw, D).astype(BF16)
    return p


def _gla_state_in(s):
    st = jnp.swapaxes(s, -1, -2)
    z = jnp.zeros_like(st)
    halves = [jnp.concatenate([st[:, h], z[:, h]] if h % 2 == 0 else [z[:, h], st[:, h]], axis=-1)
              for h in range(GLA_HEADS)]
    return jnp.stack(halves, axis=1)


def _gla_state_out(st):
    parts = [st[:, h, :, GLA_DK * (h % 2):GLA_DK * (h % 2 + 1)] for h in range(GLA_HEADS)]
    return jnp.swapaxes(jnp.stack(parts, axis=1), -1, -2)


TOKEN_TILE = 512
MIXER_TILE = 256
MIXER_CHUNKS = 8
MOE_HALVES = 2
MOE_TILE = 512 * MOE_HALVES


def _tiles(B, T):
    lt = min(TOKEN_TILE, T)
    nb = max(1, min(B, TOKEN_TILE // lt))
    lt_moe = min(MOE_TILE, T)
    nb_moe = max(1, min(B, MOE_TILE // lt_moe))
    lt_rec = min(MIXER_TILE, T)
    nb_rec = max(1, min(B, MIXER_CHUNKS // (lt_rec // CHUNK)))
    return dict(nb=nb, lt=lt, nb_rec=nb_rec, lt_rec=lt_rec, lt_s5=min(CHUNK, T), nb_moe=nb_moe, lt_moe=lt_moe)


def _trunk(x, mods, params, states, final_w):
    tiles = _tiles(x.shape[0], x.shape[1])
    nb, lt, lt_rec, lt_s5 = tiles["nb"], tiles["lt"], tiles["lt_rec"], tiles["lt_s5"]
    new = []
    for l, p in enumerate(params):
        s_gla, s_gdn, cbuf, s5_re, s5_im = states[l]
        B = x.shape[0]
        cb8 = jnp.concatenate([jnp.zeros((B, SUBLANES - (CONV_W - 1), GDN_CONV_CH), F32), cbuf], axis=1)
        zg, zd, zs, cb_new = _inproj_call(x, mods[l], cb8, p, nb, lt)
        oa, ob, gla_t, gdn_new = _mixers_call(zg, zd, _gla_state_in(s_gla), s_gdn, p["gla_norm_w"],
                                              p["gdn_norm_w"], tiles["nb_rec"], lt_rec)
        h0 = jnp.concatenate([s5_re.reshape(B, S5_HALF), s5_im.reshape(B, S5_HALF)], axis=1)
        ys, h_new = _s5_call(zs, h0, p, lt_s5)
        x = _merge_call(x, mods[l], oa, ob, ys, p, nb, lt)
        x = _moe_call(x, mods[l], p, final_w, tiles["nb_moe"], tiles["lt_moe"],
                      final_norm=(l == len(params) - 1))
        new.append((_gla_state_out(gla_t), gdn_new, cb_new[:, SUBLANES - (CONV_W - 1):],
                    h_new[:, :S5_HALF].reshape(B, S5_GROUPS, S5_STATE),
                    h_new[:, S5_HALF:].reshape(B, S5_GROUPS, S5_STATE)))
    return x, [jnp.stack([new[l][i] for l in range(len(params))]) for i in range(5)]


def kernel(x_prompt, x_sample, c_prompt, c_sample, state_gla, state_gdn, cache_gdn_conv, state_s5_re, state_s5_im, norm1_w, norm2_w, final_norm_w, w_ada, b_ada, w_in, w_alpha_up, b_alpha, gla_norm_w, conv_w, gdn_a_log, gdn_dt_bias, gdn_norm_w, s5_a_re, s5_a_im, s5_log_dt, s5_b_re, s5_b_im, s5_c_re, s5_c_im, s5_d, w_glu, b_glu, w_br_a, w_br_b, w_br_c, w_gate, b_gate, w_out, w_route_group, b_route_group, w_route_expert, b_route_expert, w_exp_gate, w_exp_up, w_exp_down):
    depth = w_in.shape[0]
    bp, tp, _ = x_prompt.shape
    bs, ts, _ = x_sample.shape
    params = [_prep_layer(l, norm1_w, norm2_w, w_in, w_alpha_up, b_alpha, gla_norm_w, conv_w, gdn_a_log,
                          gdn_dt_bias, gdn_norm_w, s5_a_re, s5_a_im, s5_log_dt, s5_b_re, s5_b_im, s5_c_re,
                          s5_c_im, s5_d, w_glu, b_glu, w_br_a, w_br_b, w_br_c, w_gate, b_gate, w_out,
                          w_route_group, b_route_group, w_route_expert, b_route_expert, w_exp_gate, w_exp_up,
                          w_exp_down) for l in range(depth)]
    final_w = final_norm_w.reshape(1, D_MODEL)

    mod = _ada_call(jnp.concatenate([c_prompt, c_sample], axis=0), w_ada, b_ada)
    mod = mod.reshape(depth, bp + bs, 6, D_MODEL)
    mods_p = [mod[l, :bp] for l in range(depth)]
    mods_s = [mod[l, bp:] for l in range(depth)]

    zeros_p = [(jnp.zeros((bp, GLA_HEADS, GLA_DK, GLA_DV), F32), jnp.zeros((bp, GDN_HEADS, GDN_DK, GDN_DV), F32),
                jnp.zeros((bp, CONV_W - 1, GDN_CONV_CH), F32), jnp.zeros((bp, S5_GROUPS, S5_STATE), F32),
                jnp.zeros((bp, S5_GROUPS, S5_STATE), F32)) for _ in range(depth)]
    y_p, new_p = _trunk(x_prompt, mods_p, params, zeros_p, final_w)

    states_s = [(state_gla[l], state_gdn[l], cache_gdn_conv[l], state_s5_re[l], state_s5_im[l])
                for l in range(depth)]
    y_s, new_s = _trunk(x_sample, mods_s, params, states_s, final_w)
    return (y_p, y_s, *new_p, *new_s)
```

```python
import functools
import math

import jax
import jax.numpy as jnp
from jax import lax
from jax.experimental import pallas as pl
from jax.experimental.pallas import tpu as pltpu

F32 = jnp.float32
BF16 = jnp.bfloat16

D_MODEL = 1024
EPS = 1e-6
CHUNK = 64
BRANCH_W = 512
GLA_HEADS, GLA_DK, GLA_DV, GLA_LOWRANK, GLA_TAU = 4, 64, 128, 16, 16.0
GDN_HEADS, GDN_DK, GDN_DV, CONV_W = 4, 128, 128, 4
S5_GROUP, S5_GROUPS, S5_STATE = 16, 32, 64
GLA_QK, GLA_V, GDN_QK, GDN_V = 256, 512, 512, 512
GDN_CONV_CH = 1536
MOE_GROUPS, EXPERTS_PER_GROUP, N_EXPERTS, D_FF = 4, 4, 16, 256

LANES = 128
SUBLANES = 8
VMEM_LIMIT = 56 * 1024 * 1024
S5_HALF = S5_GROUPS * S5_STATE
SMALL_BETA0, SMALL_A0 = 16, 20
ROUTE_E0 = 4


def _mm(a, b):
    return jnp.dot(a.astype(BF16), b.astype(BF16), preferred_element_type=F32)


def _mm_nt(a, b):
    return lax.dot_general(a.astype(BF16), b.astype(BF16), (((1,), (1,)), ((), ())), preferred_element_type=F32)


def _mm_tn(a, b):
    return lax.dot_general(a.astype(BF16), b.astype(BF16), (((0,), (0,)), ((), ())), preferred_element_type=F32)


def _split3(x):
    a = x.astype(BF16)
    r = x - a.astype(F32)
    b = r.astype(BF16)
    c = (r - b.astype(F32)).astype(BF16)
    return a, b, c


def _mm_sel(m01, x):
    a, b, c = _split3(x)
    return _mm(m01, a) + _mm(m01, b) + _mm(m01, c)


def _mm_sel_r(x, m01):
    a, b, c = _split3(x)
    return _mm(a, m01) + _mm(b, m01) + _mm(c, m01)


def _mm_nt_sel(m01, x):
    a, b, c = _split3(x)
    return _mm_nt(m01, a) + _mm_nt(m01, b) + _mm_nt(m01, c)


def _sigmoid(x):
    return 1.0 / (1.0 + jnp.exp(-x))


def _silu(x):
    return x * _sigmoid(x)


def _softplus(x):
    return jnp.maximum(x, 0.0) + jnp.log1p(jnp.exp(-jnp.abs(x)))


def _rms_rows(x, w):
    ms = jnp.mean(x * x, axis=-1, keepdims=True)
    return x * lax.rsqrt(ms + EPS) * w


def _gelu_tanh(x):
    c = math.sqrt(2.0 / math.pi)
    return x * (0.5 * (1.0 + jnp.tanh(c * (x + 0.044715 * (x * x * x)))))


def _const_spec(shape):
    nd = len(shape)
    return pl.BlockSpec(shape, lambda *_: (0,) * nd)


def _time_major_shape(B, T, nb):
    return (T, B * BRANCH_W)


def _time_major_spec(nb, lt):
    return pl.BlockSpec((lt, nb * BRANCH_W), lambda b, t: (t, b))


def _params(sem):
    return pltpu.CompilerParams(dimension_semantics=sem, vmem_limit_bytes=VMEM_LIMIT)


def _ada_kernel(c_ref, w_ref, b_ref, o_ref):
    c = c_ref[...]
    o_ref[0] = _mm(_silu(c), w_ref[0]) + b_ref[0]


def _ada_call(c_all, w_ada, b_ada):
    depth = w_ada.shape[0]
    nrow = c_all.shape[0]
    return pl.pallas_call(
        _ada_kernel,
        grid=(depth, 6),
        in_specs=[pl.BlockSpec((nrow, D_MODEL), lambda l, j: (0, 0)),
                  pl.BlockSpec((1, D_MODEL, D_MODEL), lambda l, j: (l, 0, j)),
                  pl.BlockSpec((1, 1, D_MODEL), lambda l, j: (l, 0, j))],
        out_specs=pl.BlockSpec((1, nrow, D_MODEL), lambda l, j: (l, 0, j)),
        out_shape=jax.ShapeDtypeStruct((depth, nrow, 6 * D_MODEL), F32),
        compiler_params=_params(("parallel", "parallel")),
        name="adaln",
    )(c_all, w_ada, b_ada.reshape(depth, 1, 6 * D_MODEL))


MXU_COLS = 256


def _inproj_kernel(x_ref, mod_ref, nw_ref, wg_ref, wd_ref, ws_ref, wsm_ref, wup_ref, bal_ref, sp_ref, cb_ref,
                   cw_ref, zg_ref, zd_ref, zs_ref, cbo_ref, *xbufs):
    nb, lt, d = x_ref.shape
    span = lt + SUBLANES

    @pl.when(pl.program_id(1) == 0)
    def _():
        for i in range(nb):
            for k, xb in enumerate(xbufs):
                xb[span * i + lt:span * (i + 1), :] = cb_ref[i, :, MXU_COLS * k:MXU_COLS * (k + 1)]

    x = x_ref[...]
    h = _rms_rows(x, nw_ref[...]) * (1.0 + mod_ref[:, 1:2, :]) + mod_ref[:, 0:1, :]
    h2 = h.reshape(nb * lt, d).astype(BF16)

    base = SUBLANES - (CONV_W - 1)
    for i in range(nb):
        for xb in xbufs:
            xb[span * i:span * i + SUBLANES, :] = xb[span * i + lt:span * (i + 1), :]

    def gla_block(j):
        dst = MXU_COLS * j if j < 2 else MXU_COLS * (j + 1)
        zg_ref[:, :, dst:dst + MXU_COLS] = _mm(h2, wg_ref[:, MXU_COLS * j:MXU_COLS * (j + 1)]).reshape(
            nb, lt, MXU_COLS)

    def gate_block(j):
        cs = slice(GDN_CONV_CH + MXU_COLS * j, GDN_CONV_CH + MXU_COLS * (j + 1))
        zd_ref[:, :, cs] = _mm(h2, wd_ref[:, cs]).reshape(nb, lt, MXU_COLS)

    def s5_block(j):
        zs = _mm(h2, ws_ref[:, MXU_COLS * j:MXU_COLS * (j + 1)])
        for i in range(nb):
            zs_ref[:, BRANCH_W * i + MXU_COLS * j:BRANCH_W * i + MXU_COLS * (j + 1)] = zs[lt * i:lt * (i + 1), :]

    def small_block(_):
        zsm = _mm(h2, wsm_ref[...])
        la_pre = _mm(zsm, wup_ref[...]) + bal_ref[...]
        log_a = (jnp.minimum(la_pre, 0.0) - jnp.log1p(jnp.exp(-jnp.abs(la_pre)))) * (1.0 / GLA_TAU)
        zg_ref[:, :, 512:768] = log_a.reshape(nb, lt, 256)
        lane = lax.broadcasted_iota(jnp.int32, zsm.shape, 1)
        beta = _sigmoid(zsm)
        log_g = -sp_ref[0:1, :] * _softplus(zsm + sp_ref[1:2, :])
        is_beta = (lane >= SMALL_BETA0) & (lane < SMALL_A0)
        is_a = (lane >= SMALL_A0) & (lane < SMALL_A0 + GDN_HEADS)
        small = jnp.where(is_beta, beta, jnp.where(is_a, log_g, 0.0))
        zd_ref[:, :, 2048:2176] = small.reshape(nb, lt, LANES)

    fillers = [[(gla_block, 0), (gate_block, 0)], [(gla_block, 1), (gate_block, 1)],
               [(gla_block, 2), (s5_block, 0)], [(gla_block, 3), (s5_block, 1)],
               [(gla_block, 4), (small_block, 0)], [(gla_block, 5)]]
    def conv_matmul(mblk):
        zd = _mm(h2, wd_ref[:, MXU_COLS * mblk:MXU_COLS * (mblk + 1)])
        for i in range(nb):
            xbufs[mblk][span * i + SUBLANES:span * (i + 1), :] = zd[lt * i:lt * (i + 1), :]
        for fn, arg in fillers[mblk]:
            fn(arg)

    n_blk = GDN_CONV_CH // MXU_COLS
    conv_matmul(0)
    for mblk in range(n_blk):
        if mblk + 1 < n_blk:
            conv_matmul(mblk + 1)
        xb = xbufs[mblk]
        for i in range(nb):
            r0 = span * i
            for cblk in range(MXU_COLS * mblk // LANES, MXU_COLS * (mblk + 1) // LANES):
                cs = slice(LANES * cblk, LANES * (cblk + 1))
                ls = slice(LANES * cblk - MXU_COLS * mblk, LANES * (cblk + 1) - MXU_COLS * mblk)
                conv = xb[r0 + base:r0 + base + lt, ls] * cw_ref[0:1, cs]
                for j in range(1, CONV_W):
                    conv = conv + xb[r0 + base + j:r0 + base + j + lt, ls] * cw_ref[j:j + 1, cs]
                act = _silu(conv)
                if cblk < 2 * GDN_HEADS:
                    scale = GDN_DK ** -0.5 if cblk < GDN_HEADS else 1.0
                    ss = jnp.sum(act * act, axis=-1, keepdims=True)
                    act = act * (lax.rsqrt(ss + EPS) * scale)
                zd_ref[i, :, cs] = act
    for i in range(nb):
        for k, xb in enumerate(xbufs):
            cbo_ref[i, :, MXU_COLS * k:MXU_COLS * (k + 1)] = xb[span * i + lt:span * (i + 1), :]


def _inproj_call(x, mod, cb8, p, nb, lt):
    B, T, D = x.shape
    assert nb == 1 or T == lt, "several sequences per step only when each is a single time block"
    grid = (B // nb, T // lt)
    tok = lambda w: pl.BlockSpec((nb, lt, w), lambda b, t: (b, t, 0))
    cb_spec = pl.BlockSpec((nb, SUBLANES, GDN_CONV_CH), lambda b, t: (b, 0, 0))
    return pl.pallas_call(
        _inproj_kernel,
        grid=grid,
        in_specs=[tok(D),
                  pl.BlockSpec((nb, 6, D), lambda b, t: (b, 0, 0)),
                  _const_spec((1, D)),
                  _const_spec(p["w_gla"].shape), _const_spec(p["w_gdn"].shape), _const_spec(p["w_s5"].shape),
                  _const_spec(p["w_small"].shape), _const_spec(p["w_up"].shape), _const_spec((1, GLA_QK)),
                  _const_spec((2, LANES)), cb_spec, _const_spec((CONV_W, GDN_CONV_CH))],
        out_specs=[tok(1792), tok(2176), _time_major_spec(nb, lt), cb_spec],
        out_shape=[jax.ShapeDtypeStruct((B, T, 1792), F32),
                   jax.ShapeDtypeStruct((B, T, 2176), F32),
                   jax.ShapeDtypeStruct(_time_major_shape(B, T, nb), F32),
                   jax.ShapeDtypeStruct((B, SUBLANES, GDN_CONV_CH), F32)],
        scratch_shapes=[pltpu.VMEM((nb * (lt + SUBLANES), MXU_COLS), F32)] * (GDN_CONV_CH // MXU_COLS),
        compiler_params=_params(("parallel", "arbitrary")),
        name="inproj",
    )(x, mod, p["norm1_w"], p["w_gla"], p["w_gdn"], p["w_s5"], p["w_small"], p["w_up"], p["b_alpha"],
      p["gdn_small"], cb8, p["conv_w"])


def _gla_stages(z_ref, nw_ref, o_ref, s_ref):
    L = CHUNK
    nb, lt, _ = z_ref.shape
    n_chunks = lt // L
    n_rows = nb * lt
    row = lax.broadcasted_iota(jnp.int32, (L, L), 0)
    col = lax.broadcasted_iota(jnp.int32, (L, L), 1)
    causal = row >= col
    lane = lax.broadcasted_iota(jnp.int32, (L, LANES), 1)
    own_half = (lane < GLA_DK, lane >= GLA_DK)
    nw = nw_ref[...]
    brow = lax.broadcasted_iota(jnp.int32, (n_rows, n_rows), 0)
    bcol = lax.broadcasted_iota(jnp.int32, (n_rows, n_rows), 1)
    shift = L.bit_length() - 1
    chunk_tril = ((brow >= bcol) & ((brow >> shift) == (bcol >> shift))).astype(BF16)
    b_all = _mm_sel(chunk_tril, z_ref[:, :, 512:768].reshape(n_rows, GLA_QK))

    def zcols(c, lo, width):
        i, cc = divmod(c, n_chunks)
        return z_ref[i, L * cc:L * (cc + 1), lo:lo + width]

    all_chunks = range(nb * n_chunks)
    probs = [(c, h) for c in all_chunks for h in range(GLA_HEADS)]
    rows_of = lambda c: slice(L * c, L * (c + 1))
    pair_of = lambda h: slice(LANES * (h // 2), LANES * (h // 2) + LANES)
    qi_, ki_, qs_, kl_, dl_ = {}, {}, {}, {}, {}
    for c in all_chunks:
        q = zcols(c, 0, GLA_QK)
        k = zcols(c, GLA_QK, GLA_QK)
        b = b_all[rows_of(c), :]
        b_mid = b[L // 2 - 1:L // 2, :]
        b_last = b[L - 1:L, :]
        qi_[c] = q * jnp.exp(b - b_mid)
        ki_[c] = k * jnp.exp(b_mid - b)
        qs_[c] = q * jnp.exp(b)
        kl_[c] = k * jnp.exp(b_last - b)
        dl_[c] = jnp.exp(b_last)
    v_ = {p: zcols(p[0], 768 + GLA_DV * p[1], GLA_DV) for p in probs}
    att_, kv_, av_ = {}, {}, {}

    def scores():
        for p in probs:
            ki_h = jnp.where(own_half[p[1] % 2], ki_[p[0]][:, pair_of(p[1])], 0.0)
            att_[p] = jnp.where(causal, _mm_nt(qi_[p[0]][:, pair_of(p[1])], ki_h), 0.0)

    def increments():
        for p in probs:
            kv_[p] = _mm_tn(v_[p], jnp.where(own_half[p[1] % 2], kl_[p[0]][:, pair_of(p[1])], 0.0))

    def intra():
        for p in probs:
            av_[p] = _mm(att_[p], v_[p])

    def outputs():
        st_ = {}
        for i in range(nb):
            for h in range(GLA_HEADS):
                st = s_ref[i, h]
                for cc in range(n_chunks):
                    c = n_chunks * i + cc
                    st_[(c, h)] = st
                    st = st * dl_[c][:, pair_of(h)] + kv_[(c, h)]
                s_ref[i, h] = st
        for p in probs:
            c, h = p
            i, cc = divmod(c, n_chunks)
            o_h = av_[p] + _mm_nt(qs_[c][:, pair_of(h)], st_[p])
            gate = zcols(c, 1280 + GLA_DV * h, GLA_DV)
            o_ref[i, L * cc:L * (cc + 1), GLA_DV * h:GLA_DV * (h + 1)] = _rms_rows(o_h, nw) * _silu(gate)

    return [scores, increments, intra, outputs]


def _mixers_kernel(zg_ref, zd_ref, sa0_ref, sb0_ref, nwa_ref, nwb_ref, oa_ref, ob_ref, sa_ref, sb_ref, bfull_ref,
                   gfull_ref, grow_ref, glast_ref):
    @pl.when(pl.program_id(1) == 0)
    def _():
        sa_ref[...] = sa0_ref[...]
        sb_ref[...] = sb0_ref[...]

    _gdn_body(zd_ref, nwb_ref, ob_ref, sb_ref, bfull_ref, gfull_ref, grow_ref, glast_ref,
              _gla_stages(zg_ref, nwa_ref, oa_ref, sa_ref))


def _gdn_body(z_ref, nw_ref, o_ref, s_ref, bfull_ref, gfull_ref, grow_ref, glast_ref, fillers):
    L = CHUNK
    nb, lt, _ = z_ref.shape
    n_chunks = lt // L
    n_rows = nb * lt
    fillers = list(fillers)
    row = lax.broadcasted_iota(jnp.int32, (L, L), 0)
    col = lax.broadcasted_iota(jnp.int32, (L, L), 1)
    causal = row >= col
    strict = row > col
    eye = (row == col).astype(F32)
    diag16 = (row >> 4) == (col >> 4)
    diag32 = (row >> 5) == (col >> 5)
    off32 = diag32 & jnp.logical_not(diag16)
    off64 = jnp.logical_not(diag32)
    sel_r = lax.broadcasted_iota(jnp.int32, (SUBLANES, LANES), 0)
    sel_c = lax.broadcasted_iota(jnp.int32, (SUBLANES, LANES), 1)
    sel = (sel_c == sel_r + SMALL_A0).astype(BF16)
    brow = lax.broadcasted_iota(jnp.int32, (n_rows, n_rows), 0)
    bcol = lax.broadcasted_iota(jnp.int32, (n_rows, n_rows), 1)
    shift = L.bit_length() - 1
    chunk_tril = ((brow >= bcol) & ((brow >> shift) == (bcol >> shift))).astype(BF16)
    small_all = z_ref[:, :, 2048:2176].reshape(n_rows, LANES)
    gs_all = _mm_sel(chunk_tril, small_all)
    grow_all = _mm_nt_sel(sel, gs_all)
    e_r = lax.broadcasted_iota(jnp.int32, (LANES, GDN_HEADS * LANES), 0)
    e_h = lax.broadcasted_iota(jnp.int32, (LANES, GDN_HEADS * LANES), 1) >> (LANES.bit_length() - 1)
    bfull_ref[...] = _mm_sel_r(small_all, (e_r == e_h + SMALL_BETA0).astype(BF16))
    gfull = _mm_sel_r(gs_all, (e_r == e_h + SMALL_A0).astype(BF16))
    gfull_ref[...] = gfull
    for c in range(nb * n_chunks):
        grow_ref[c] = grow_all[:, L * c:L * (c + 1)]
        glast_ref[c:c + 1, :] = gfull[L * (c + 1) - 1:L * (c + 1), :]
    nw = nw_ref[...]

    probs = [(c, h) for c in range(nb * n_chunks) for h in range(GDN_HEADS)]
    rows_of = lambda c: slice(L * c, L * (c + 1))
    head_of = lambda h: slice(LANES * h, LANES * (h + 1))

    def zcols(c, lo, width):
        i, cc = divmod(c, n_chunks)
        return z_ref[i, L * cc:L * (cc + 1), lo:lo + width]

    q_ = {p: zcols(p[0], GDN_DK * p[1], GDN_DK) for p in probs}
    k_ = {p: zcols(p[0], GDN_QK + GDN_DK * p[1], GDN_DK) for p in probs}
    beta_ = {p: bfull_ref[rows_of(p[0]), head_of(p[1])] for p in probs}
    g_ = {p: gfull_ref[rows_of(p[0]), head_of(p[1])] for p in probs}
    dec_ = {p: jnp.exp(jnp.where(causal, g_[p][:, 0:L] - grow_ref[p[0]][p[1]:p[1] + 1, :], -jnp.inf))
            for p in probs}
    kk_ = {p: _mm_nt(k_[p], k_[p]) for p in probs}
    qk_ = {p: _mm_nt(q_[p], k_[p]) for p in probs}
    a_ = {p: jnp.where(strict, beta_[p][:, 0:L] * kk_[p] * dec_[p], 0.0) for p in probs}
    qk_ = {p: jnp.where(causal, qk_[p] * dec_[p], 0.0) for p in probs}
    pw_ = {p: jnp.where(diag16, -a_[p], 0.0) for p in probs}
    tinv_ = {p: eye + pw_[p] for p in probs}
    for _ in range(3):
        pw_ = {p: _mm(pw_[p], pw_[p]) for p in probs}
        tinv_ = {p: tinv_[p] + _mm(tinv_[p], pw_[p]) for p in probs}
    for off in (off32, off64):
        te_ = {p: _mm(tinv_[p], jnp.where(off, a_[p], 0.0)) for p in probs}
        tinv_ = {p: tinv_[p] - _mm(te_[p], tinv_[p]) for p in probs}
    eg_ = {p: jnp.exp(g_[p]) for p in probs}
    sol_ = {}
    for p in probs:
        v_p = zcols(p[0], 2 * GDN_QK + GDN_DV * p[1], GDN_DV)
        rhs = jnp.concatenate([beta_[p] * v_p, (beta_[p] * eg_[p]) * k_[p]], axis=-1)
        sol_[p] = _mm(tinv_[p], rhs)

    st_ = {(i, h): s_ref[i, h] for i in range(nb) for h in range(GDN_HEADS)}
    for cc in range(n_chunks):
        heads = [(i, n_chunks * i + cc, h) for i in range(nb) for h in range(GDN_HEADS)]
        u_, o_, new_ = {}, {}, {}
        for i, c, h in heads:
            u_[(c, h)] = sol_[(c, h)][:, 0:GDN_DV] - _mm(sol_[(c, h)][:, GDN_DV:], st_[(i, h)])
        if fillers:
            fillers.pop(0)()
        for i, c, h in heads:
            o_[(c, h)] = _mm(qk_[(c, h)], u_[(c, h)]) + _mm(q_[(c, h)] * eg_[(c, h)], st_[(i, h)])
        for i, c, h in heads:
            g_last = glast_ref[c:c + 1, head_of(h)]
            new_[(i, h)] = (st_[(i, h)] * jnp.exp(g_last)
                            + _mm_tn(k_[(c, h)] * jnp.exp(g_last - g_[(c, h)]), u_[(c, h)]))
        st_ = new_
        for i, c, h in heads:
            gate = z_ref[i, L * cc:L * (cc + 1), GDN_CONV_CH + GDN_DV * h:GDN_CONV_CH + GDN_DV * (h + 1)]
            o_ref[i, L * cc:L * (cc + 1), GDN_DV * h:GDN_DV * (h + 1)] = _rms_rows(o_[(c, h)], nw) * _silu(gate)
    for i in range(nb):
        for h in range(GDN_HEADS):
            s_ref[i, h] = st_[(i, h)]
    for rest in fillers:
        rest()


def _mixers_call(zg, zd, sa0, sb0, nwa, nwb, nb, lt):
    B, T, _ = zd.shape
    n_rows = nb * lt
    last_rows = -(-(n_rows // CHUNK) // SUBLANES) * SUBLANES
    sa_spec = pl.BlockSpec((nb, GLA_HEADS, GLA_DV, LANES), lambda b, t: (b, 0, 0, 0))
    sb_spec = pl.BlockSpec((nb, GDN_HEADS, GDN_DK, GDN_DV), lambda b, t: (b, 0, 0, 0))
    tok = lambda w: pl.BlockSpec((nb, lt, w), lambda b, t: (b, t, 0))
    return pl.pallas_call(
        _mixers_kernel,
        grid=(B // nb, T // lt),
        in_specs=[tok(1792), tok(2176), sa_spec, sb_spec, _const_spec((1, GLA_DV)), _const_spec((1, GDN_DV))],
        out_specs=[tok(GLA_V), tok(GDN_V), sa_spec, sb_spec],
        out_shape=[jax.ShapeDtypeStruct((B, T, GLA_V), F32), jax.ShapeDtypeStruct((B, T, GDN_V), F32),
                   jax.ShapeDtypeStruct((B, GLA_HEADS, GLA_DV, LANES), F32),
                   jax.ShapeDtypeStruct((B, GDN_HEADS, GDN_DK, GDN_DV), F32)],
        scratch_shapes=[pltpu.VMEM((n_rows, GDN_HEADS * LANES), F32), pltpu.VMEM((n_rows, GDN_HEADS * LANES), F32),
                        pltpu.VMEM((n_rows // CHUNK, SUBLANES, CHUNK), F32),
                        pltpu.VMEM((last_rows, GDN_HEADS * LANES), F32)],
        compiler_params=_params(("parallel", "arbitrary")),
        name="mixers",
    )(zg, zd, sa0, sb0, nwa, nwb)


S5_LANE_CHUNK = 512


def _s5_kernel(u_ref, h0_ref, bb_ref, cc_ref, lam_ref, d_ref, y_ref, h_ref, bu_ref, *, lt):
    nb = SUBLANES

    @pl.when(pl.program_id(1) == 0)
    def _():
        h_ref[...] = h0_ref[...]

    u = u_ref[...].reshape(lt * nb, BRANCH_W)
    ch, sh = BRANCH_W // 2, S5_HALF // 2
    for part in range(2):
        for hb in range(2):
            bu_ref[:, part * S5_HALF + hb * sh:part * S5_HALF + (hb + 1) * sh] = _mm(
                u[:, hb * ch:(hb + 1) * ch], bb_ref[2 * part + hb])
    for j in range(S5_HALF // S5_LANE_CHUNK):
        re = slice(S5_LANE_CHUNK * j, S5_LANE_CHUNK * (j + 1))
        im = slice(S5_HALF + S5_LANE_CHUNK * j, S5_HALF + S5_LANE_CHUNK * (j + 1))
        lr = jnp.broadcast_to(lam_ref[0:1, re], (nb, S5_LANE_CHUNK))
        li = jnp.broadcast_to(lam_ref[1:2, re], (nb, S5_LANE_CHUNK))

        hr, hi = h_ref[:, re], h_ref[:, im]
        for tt in range(lt):
            rows = slice(tt * nb, (tt + 1) * nb)
            hr, hi = lr * hr - li * hi + bu_ref[rows, re], lr * hi + li * hr + bu_ref[rows, im]
            bu_ref[rows, re] = hr
            bu_ref[rows, im] = hi
        h_ref[:, re] = hr
        h_ref[:, im] = hi
    ys = [_mm(bu_ref[:, hb * sh:(hb + 1) * sh], cc_ref[hb])
          + _mm(bu_ref[:, S5_HALF + hb * sh:S5_HALF + (hb + 1) * sh], cc_ref[2 + hb]) for hb in range(2)]
    y = jnp.concatenate(ys, axis=-1) + d_ref[...] * u
    y_ref[...] = y.reshape(lt, nb, BRANCH_W)


def _s5_call(zs, h0, p, lt):
    T = zs.shape[0]
    B = h0.shape[0]
    nb = SUBLANES
    h_spec = pl.BlockSpec((nb, 2 * S5_HALF), lambda b, t: (b, 0))
    tok_spec = pl.BlockSpec((lt, None, nb, BRANCH_W), lambda b, t: (t, b, 0, 0))
    y, h = pl.pallas_call(
        functools.partial(_s5_kernel, lt=lt),
        grid=(B // nb, T // lt),
        in_specs=[tok_spec, h_spec,
                  _const_spec((4, BRANCH_W // 2, S5_HALF // 2)), _const_spec((4, S5_HALF // 2, BRANCH_W // 2)),
                  _const_spec((2, S5_HALF)), _const_spec((1, BRANCH_W))],
        out_specs=[tok_spec, h_spec],
        out_shape=[jax.ShapeDtypeStruct((T, B // nb, nb, BRANCH_W), F32),
                   jax.ShapeDtypeStruct((B, 2 * S5_HALF), F32)],
        scratch_shapes=[pltpu.VMEM((lt * nb, 2 * S5_HALF), F32)],
        compiler_params=_params(("parallel", "arbitrary")),
        name="s5",
    )(zs.reshape(T, B // nb, nb, BRANCH_W), h0, p["s5_bblk"], p["s5_cblk"], p["s5_lam"], p["s5_d"])
    return y.reshape(zs.shape), h


def _merge_kernel(x_ref, mod_ref, nw_ref, oa_ref, ob_ref, ys_ref, wgate_ref, bgate_ref, wa_ref, wb_ref, wc_ref,
                  wglu_ref, bglu_ref, wout_ref, o_ref):
    nb, lt, d = x_ref.shape
    n = nb * lt
    x = x_ref[...]
    h = _rms_rows(x, nw_ref[...]) * (1.0 + mod_ref[:, 1:2, :]) + mod_ref[:, 0:1, :]
    gate = _sigmoid(_mm(h.reshape(n, d), wgate_ref[...]) + bgate_ref[...])
    y_a = _mm(oa_ref[...].reshape(n, BRANCH_W), wa_ref[...])
    y_b = _mm(ob_ref[...].reshape(n, BRANCH_W), wb_ref[...])
    ys = [ys_ref[:, BRANCH_W * i:BRANCH_W * (i + 1)] for i in range(nb)]
    yc = _gelu_tanh(ys[0] if nb == 1 else jnp.concatenate(ys, axis=0))
    yc = yc * _sigmoid(_mm(yc, wglu_ref[...]) + bglu_ref[...])
    y_c = _mm(yc, wc_ref[...])
    mix = gate[:, 0:d] * y_a + gate[:, d:2 * d] * y_b + gate[:, 2 * d:3 * d] * y_c
    m = _mm(mix, wout_ref[...])
    o_ref[...] = x + mod_ref[:, 2:3, :] * m.reshape(nb, lt, d)


def _merge_call(x, mod, oa, ob, ys, p, nb, lt):
    B, T, D = x.shape
    tok = lambda w: pl.BlockSpec((nb, lt, w), lambda b, t: (b, t, 0))
    return pl.pallas_call(
        _merge_kernel,
        grid=(B // nb, T // lt),
        in_specs=[tok(D), pl.BlockSpec((nb, 6, D), lambda b, t: (b, 0, 0)), _const_spec((1, D)),
                  tok(BRANCH_W), tok(BRANCH_W), _time_major_spec(nb, lt),
                  _const_spec((D, 3 * D)), _const_spec((1, 3 * D)),
                  _const_spec((BRANCH_W, D)), _const_spec((BRANCH_W, D)), _const_spec((BRANCH_W, D)),
                  _const_spec((BRANCH_W, BRANCH_W)), _const_spec((1, BRANCH_W)), _const_spec((D, D))],
        out_specs=tok(D),
        out_shape=jax.ShapeDtypeStruct((B, T, D), F32),
        compiler_params=_params(("parallel", "parallel")),
        name="merge",
    )(x, mod, p["norm1_w"], oa, ob, ys, p["w_gate"], p["b_gate"], p["w_br_a"], p["w_br_b"], p["w_br_c"],
      p["w_glu"], p["b_glu"], p["w_out"])


MOE_ROWS = 192


def _moe_kernel(x_ref, mod_ref, nw_ref, wr_ref, br_ref, ex_ref, wg_ref, wu_ref, wd_ref, fw_ref, o_ref,
                hn_ref, comb_ref, krow_ref, acc_ref, *, final_norm):
    nb, lt, d = x_ref.shape
    R = MOE_ROWS
    gw = EXPERTS_PER_GROUP * D_FF
    halves = hn_ref.shape[0]
    if nb >= halves:
        hb, hl = nb // halves, lt
    else:
        hb, hl = nb, lt // halves
    n = hb * hl
    row_pos = lax.broadcasted_iota(jnp.int32, (R, n), 0).astype(F32)
    lane = lax.broadcasted_iota(jnp.int32, (n, LANES), 1)
    trow = lax.broadcasted_iota(jnp.int32, (n, n), 0)
    tcol = lax.broadcasted_iota(jnp.int32, (n, n), 1)
    before = (trow > tcol).astype(BF16)
    sel_r = lax.broadcasted_iota(jnp.int32, (SUBLANES, LANES), 0)
    sel_c = lax.broadcasted_iota(jnp.int32, (SUBLANES, LANES), 1)
    pick = (sel_r == sel_c).astype(BF16)
    neg = -jnp.inf

    def xs(k):
        return x_ref[hb * k:hb * (k + 1)] if nb >= halves else x_ref[:, hl * k:hl * (k + 1), :]

    def mods(k, row):
        m = mod_ref[hb * k:hb * (k + 1)] if nb >= halves else mod_ref[...]
        return m[:, row:row + 1, :]

    def route(k):
        hn = (_rms_rows(xs(k), nw_ref[...]) * (1.0 + mods(k, 4)) + mods(k, 3)).reshape(n, d)
        hn_hi = hn.astype(BF16)
        hn_ref[k] = hn_hi
        hn_lo = (hn - hn_hi.astype(F32)).astype(BF16)
        logits = _mm(hn_hi, wr_ref[0]) + (_mm(hn_hi, wr_ref[1]) + _mm(hn_lo, wr_ref[0])) + br_ref[...]
        gl = jnp.where(lane < MOE_GROUPS, logits, neg)
        gmax = jnp.max(gl, axis=-1, keepdims=True)
        g_sel = jnp.min(jnp.where(gl == gmax, lane, LANES), axis=-1, keepdims=True)
        p_sel = 1.0 / jnp.sum(jnp.exp(gl - gmax), axis=-1, keepdims=True)
        e_id = lane - ROUTE_E0
        in_group = (e_id >= 0) & (e_id < N_EXPERTS) & ((e_id >> 2) == g_sel)
        el = jnp.where(in_group, logits, neg)
        m1 = jnp.max(el, axis=-1, keepdims=True)
        i1 = jnp.min(jnp.where(el == m1, lane, LANES), axis=-1, keepdims=True)
        el2 = jnp.where(lane == i1, neg, el)
        m2 = jnp.max(el2, axis=-1, keepdims=True)
        i2 = jnp.min(jnp.where(el2 == m2, lane, LANES), axis=-1, keepdims=True)
        e2 = jnp.exp(m2 - m1)
        w1 = p_sel / (1.0 + e2)
        w2 = p_sel * e2 / (1.0 + e2)
        comb_ref[k] = jnp.where(lane == i1, w1, jnp.where(lane == i2, w2, 0.0)).astype(BF16)
        member = lane == g_sel
        rank = _mm(before, member.astype(BF16))
        key = jnp.where(member, rank, -1.0)
        krow_ref[k] = _mm_nt_sel(pick, key)
        return [(jnp.max(key[:, g:g + 1]).astype(jnp.int32) + R) // R for g in range(MOE_GROUPS)]

    def one_pass(k, g, s):
        base = float(s * R) if isinstance(s, int) else (s * R).astype(F32)
        gather = (krow_ref[k, g:g + 1, :] - base == row_pos).astype(BF16)
        xg = _mm(gather, hn_ref[k])
        cg = _mm(_mm(gather, comb_ref[k]), ex_ref[:, gw * g:gw * (g + 1)])
        experts = range(EXPERTS_PER_GROUP * g, EXPERTS_PER_GROUP * (g + 1))
        he = jnp.concatenate([_silu(_mm(xg, wg_ref[e])) * _mm(xg, wu_ref[e]) for e in experts], axis=-1)
        yg = _mm(he * cg, wd_ref[g])
        return _mm_tn(gather, yg)

    n_pass = [route(k) for k in range(halves)]
    for k in range(halves):
        acc = one_pass(k, 0, 0)
        for g in range(1, MOE_GROUPS):
            acc = acc + one_pass(k, g, 0)
        acc_ref[k] = acc
    for k in range(halves):
        for g in range(MOE_GROUPS):
            def extra(s, carry, k=k, g=g):
                acc_ref[k] += one_pass(k, g, s)
                return carry

            lax.fori_loop(1, n_pass[k][g], extra, 0)
    for k in range(halves):
        out = xs(k) + mods(k, 5) * acc_ref[k].reshape(hb, hl, d)
        if final_norm:
            out = _rms_rows(out, fw_ref[...])
        if nb >= halves:
            o_ref[hb * k:hb * (k + 1)] = out
        else:
            o_ref[:, hl * k:hl * (k + 1), :] = out


def _moe_call(x, mod, p, final_w, nb, lt, final_norm):
    B, T, D = x.shape
    halves = MOE_HALVES if (nb * lt) % (MOE_HALVES * SUBLANES * 2) == 0 and max(nb, lt) % MOE_HALVES == 0 else 1
    n = nb * lt // halves
    gw = EXPERTS_PER_GROUP * D_FF
    tok = pl.BlockSpec((nb, lt, D), lambda b, t: (b, t, 0))
    resident = lambda shape: pl.BlockSpec(shape, lambda *_: (0,) * len(shape), pipeline_mode=pl.Buffered(1))
    return pl.pallas_call(
        functools.partial(_moe_kernel, final_norm=final_norm),
        grid=(B // nb, T // lt),
        in_specs=[tok, pl.BlockSpec((nb, 6, D), lambda b, t: (b, 0, 0)), _const_spec((1, D)),
                  _const_spec((2, D, LANES)), _const_spec((1, LANES)), _const_spec((LANES, N_EXPERTS * D_FF)),
                  resident((N_EXPERTS, D, D_FF)), resident((N_EXPERTS, D, D_FF)), resident((MOE_GROUPS, gw, D)),
                  _const_spec((1, D))],
        out_specs=tok,
        out_shape=jax.ShapeDtypeStruct((B, T, D), F32),
        scratch_shapes=[pltpu.VMEM((halves, n, D), BF16), pltpu.VMEM((halves, n, LANES), BF16),
                        pltpu.VMEM((halves, SUBLANES, n), F32), pltpu.VMEM((halves, n, D), F32)],
        compiler_params=_params(("parallel", "parallel")),
        name="moe",
    )(x, mod, p["norm2_w"], p["w_route"], p["b_route"], p["route_expand"], p["w_eg"], p["w_eu"], p["w_ed"],
      final_w)


def _prep_layer(l, norm1_w, norm2_w, w_in, w_alpha_up, b_alpha, gla_norm_w, conv_w, gdn_a_log, gdn_dt_bias,
                gdn_norm_w, s5_a_re, s5_a_im, s5_log_dt, s5_b_re, s5_b_im, s5_c_re, s5_c_im, s5_d, w_glu, b_glu,
                w_br_a, w_br_b, w_br_c, w_gate, b_gate, w_out, w_route_group, b_route_group, w_route_expert,
                b_route_expert, w_exp_gate, w_exp_up, w_exp_down):
    D = D_MODEL
    w = w_in[l]
    o = 0
    cols = {}
    for name, size in (("qa", GLA_QK), ("ka", GLA_QK), ("va", GLA_V), ("ga", GLA_V), ("aa", GLA_LOWRANK),
                       ("qkv", GDN_CONV_CH), ("beta", GDN_HEADS), ("a", GDN_HEADS), ("gb", GDN_V),
                       ("u", BRANCH_W)):
        cols[name] = w[:, o:o + size]
        o += size
    p = {}
    p["norm1_w"] = norm1_w[l].reshape(1, D)
    p["norm2_w"] = norm2_w[l].reshape(1, D)
    p["w_gla"] = jnp.concatenate([cols["qa"] * (GLA_DK ** -0.5), cols["ka"], cols["va"], cols["ga"]],
                                 axis=1).astype(BF16)
    p["w_gdn"] = jnp.concatenate([cols["qkv"], cols["gb"]], axis=1).astype(BF16)
    p["w_s5"] = cols["u"].astype(BF16)
    assert SMALL_BETA0 == GLA_LOWRANK and SMALL_A0 == SMALL_BETA0 + GDN_HEADS
    used = SMALL_A0 + GDN_HEADS
    p["w_small"] = jnp.concatenate([cols["aa"], cols["beta"], cols["a"], jnp.zeros((D, LANES - used), F32)],
                                   axis=1).astype(BF16)
    p["w_up"] = jnp.pad(w_alpha_up[l], ((0, LANES - GLA_LOWRANK), (0, 0))).astype(BF16)
    p["b_alpha"] = b_alpha[l].reshape(1, GLA_QK)
    p["gdn_small"] = jnp.pad(jnp.stack([jnp.exp(gdn_a_log[l]), gdn_dt_bias[l]]),
                             ((0, 0), (SMALL_A0, LANES - used)))
    p["gla_norm_w"] = gla_norm_w[l].reshape(1, GLA_DV)
    p["gdn_norm_w"] = gdn_norm_w[l].reshape(1, GDN_DV)
    p["conv_w"] = conv_w[l]

    a_re, a_im = s5_a_re[l], s5_a_im[l]
    dt = jnp.exp(s5_log_dt[l])[:, None]
    mag = jnp.exp(a_re * dt)
    lr, li = mag * jnp.cos(a_im * dt), mag * jnp.sin(a_im * dt)
    den = a_re * a_re + a_im * a_im
    f_re = ((lr - 1.0) * a_re + li * a_im) / den
    f_im = (li * a_re - (lr - 1.0) * a_im) / den
    bb_re = f_re[..., None] * s5_b_re[l] - f_im[..., None] * s5_b_im[l]
    bb_im = f_re[..., None] * s5_b_im[l] + f_im[..., None] * s5_b_re[l]
    gh = S5_GROUPS // 2
    eye_h = jnp.eye(gh, dtype=F32)
    blk = lambda m: (jnp.transpose(m, (0, 2, 1))[:, :, None, :] * eye_h[:, None, :, None]
                     ).reshape(gh * S5_GROUP, gh * S5_STATE)
    cblk = lambda m: (jnp.transpose(m, (2, 0, 1))[None, :, :, :] * eye_h[:, None, :, None]
                      ).reshape(gh * S5_STATE, gh * S5_GROUP)
    p["s5_bblk"] = jnp.stack([blk(m[hb * gh:(hb + 1) * gh]) for m in (bb_re, bb_im) for hb in range(2)]
                             ).astype(BF16)
    p["s5_cblk"] = jnp.stack([cblk(m[hb * gh:(hb + 1) * gh]) for m in (s5_c_re[l], -s5_c_im[l])
                              for hb in range(2)]).astype(BF16)
    p["s5_lam"] = jnp.stack([lr.reshape(S5_HALF), li.reshape(S5_HALF)])
    p["s5_d"] = s5_d[l].reshape(1, BRANCH_W)

    p["w_gate"] = w_gate[l].astype(BF16)
    p["b_gate"] = b_gate[l].reshape(1, 3 * D)
    p["w_br_a"] = w_br_a[l].astype(BF16)
    p["w_br_b"] = w_br_b[l].astype(BF16)
    p["w_br_c"] = w_br_c[l].astype(BF16)
    p["w_glu"] = w_glu[l].astype(BF16)
    p["b_glu"] = b_glu[l].reshape(1, BRANCH_W)
    p["w_out"] = w_out[l].astype(BF16)

    assert ROUTE_E0 == MOE_GROUPS
    n_route = MOE_GROUPS + N_EXPERTS
    wr = jnp.concatenate([w_route_group[l], w_route_expert[l], jnp.zeros((D, LANES - n_route), F32)], axis=1)
    br = jnp.pad(jnp.concatenate([b_route_group[l], b_route_expert[l]]), (0, LANES - n_route)).reshape(1, LANES)
    wr_hi = wr.astype(BF16)
    p["w_route"] = jnp.stack([wr_hi, (wr - wr_hi.astype(F32)).astype(BF16)])
    p["b_route"] = br
    e_of_col = jnp.arange(N_EXPERTS * D_FF) // D_FF
    p["route_expand"] = (jnp.arange(LANES)[:, None] == (e_of_col[None, :] + ROUTE_E0)).astype(BF16)
    gw = EXPERTS_PER_GROUP * D_FF
    p["w_eg"] = w_exp_gate[l].astype(BF16)
    p["w_eu"] = w_exp_up[l].astype(BF16)
    p["w_ed"] = w_exp_down[l].reshape(MOE_GROUPS, gw, D).astype(BF16)
    return p


def _gla_state_in(s):
    st = jnp.swapaxes(s, -1, -2)
    z = jnp.zeros_like(st)
    halves = [jnp.concatenate([st[:, h], z[:, h]] if h % 2 == 0 else [z[:, h], st[:, h]], axis=-1)
              for h in range(GLA_HEADS)]
    return jnp.stack(halves, axis=1)


def _gla_state_out(st):
    parts = [st[:, h, :, GLA_DK * (h % 2):GLA_DK * (h % 2 + 1)] for h in range(GLA_HEADS)]
    return jnp.swapaxes(jnp.stack(parts, axis=1), -1, -2)


TOKEN_TILE = 512
MIXER_TILE = 256
S5_TILE = 128
MIXER_CHUNKS = 8
MOE_HALVES = 2
MOE_TILE = 512 * MOE_HALVES


def _tiles(B, T):
    lt = min(TOKEN_TILE, T)
    nb = max(1, min(B, TOKEN_TILE // lt))
    lt_moe = min(MOE_TILE, T)
    nb_moe = max(1, min(B, MOE_TILE // lt_moe))
    lt_rec = min(MIXER_TILE, T)
    nb_rec = max(1, min(B, MIXER_CHUNKS // (lt_rec // CHUNK)))
    return dict(nb=nb, lt=lt, nb_rec=nb_rec, lt_rec=lt_rec, lt_s5=min(S5_TILE, T), nb_moe=nb_moe, lt_moe=lt_moe)


def _trunk(x, mods, params, states, final_w):
    tiles = _tiles(x.shape[0], x.shape[1])
    nb, lt, lt_rec, lt_s5 = tiles["nb"], tiles["lt"], tiles["lt_rec"], tiles["lt_s5"]
    new = []
    for l, p in enumerate(params):
        s_gla, s_gdn, cbuf, s5_re, s5_im = states[l]
        B = x.shape[0]
        cb8 = jnp.concatenate([jnp.zeros((B, SUBLANES - (CONV_W - 1), GDN_CONV_CH), F32), cbuf], axis=1)
        zg, zd, zs, cb_new = _inproj_call(x, mods[l], cb8, p, nb, lt)
        oa, ob, gla_t, gdn_new = _mixers_call(zg, zd, _gla_state_in(s_gla), s_gdn, p["gla_norm_w"],
                                              p["gdn_norm_w"], tiles["nb_rec"], lt_rec)
        h0 = jnp.concatenate([s5_re.reshape(B, S5_HALF), s5_im.reshape(B, S5_HALF)], axis=1)
        ys, h_new = _s5_call(zs, h0, p, lt_s5)
        x = _merge_call(x, mods[l], oa, ob, ys, p, nb, lt)
        x = _moe_call(x, mods[l], p, final_w, tiles["nb_moe"], tiles["lt_moe"],
                      final_norm=(l == len(params) - 1))
        new.append((_gla_state_out(gla_t), gdn_new, cb_new[:, SUBLANES - (CONV_W - 1):],
                    h_new[:, :S5_HALF].reshape(B, S5_GROUPS, S5_STATE),
                    h_new[:, S5_HALF:].reshape(B, S5_GROUPS, S5_STATE)))
    return x, [jnp.stack([new[l][i] for l in range(len(params))]) for i in range(5)]


def kernel(x_prompt, x_sample, c_prompt, c_sample, state_gla, state_gdn, cache_gdn_conv, state_s5_re, state_s5_im, norm1_w, norm2_w, final_norm_w, w_ada, b_ada, w_in, w_alpha_up, b_alpha, gla_norm_w, conv_w, gdn_a_log, gdn_dt_bias, gdn_norm_w, s5_a_re, s5_a_im, s5_log_dt, s5_b_re, s5_b_im, s5_c_re, s5_c_im, s5_d, w_glu, b_glu, w_br_a, w_br_b, w_br_c, w_gate, b_gate, w_out, w_route_group, b_route_group, w_route_expert, b_route_expert, w_exp_gate, w_exp_up, w_exp_down):
    depth = w_in.shape[0]
    bp, tp, _ = x_prompt.shape
    bs, ts, _ = x_sample.shape
    params = [_prep_layer(l, norm1_w, norm2_w, w_in, w_alpha_up, b_alpha, gla_norm_w, conv_w, gdn_a_log,
                          gdn_dt_bias, gdn_norm_w, s5_a_re, s5_a_im, s5_log_dt, s5_b_re, s5_b_im, s5_c_re,
                          s5_c_im, s5_d, w_glu, b_glu, w_br_a, w_br_b, w_br_c, w_gate, b_gate, w_out,
                          w_route_group, b_route_group, w_route_expert, b_route_expert, w_exp_gate, w_exp_up,
                          w_exp_down) for l in range(depth)]
    final_w = final_norm_w.reshape(1, D_MODEL)

    mod = _ada_call(jnp.concatenate([c_prompt, c_sample], axis=0), w_ada, b_ada)
    mod = mod.reshape(depth, bp + bs, 6, D_MODEL)
    mods_p = [mod[l, :bp] for l in range(depth)]
    mods_s = [mod[l, bp:] for l in range(depth)]

    zeros_p = [(jnp.zeros((bp, GLA_HEADS, GLA_DK, GLA_DV), F32), jnp.zeros((bp, GDN_HEADS, GDN_DK, GDN_DV), F32),
                jnp.zeros((bp, CONV_W - 1, GDN_CONV_CH), F32), jnp.zeros((bp, S5_GROUPS, S5_STATE), F32),
                jnp.zeros((bp, S5_GROUPS, S5_STATE), F32)) for _ in range(depth)]
    y_p, new_p = _trunk(x_prompt, mods_p, params, zeros_p, final_w)

    states_s = [(state_gla[l], state_gdn[l], cache_gdn_conv[l], state_s5_re[l], state_s5_im[l])
                for l in range(depth)]
    y_s, new_s = _trunk(x_sample, mods_s, params, states_s, final_w)
    return (y_p, y_s, *new_p, *new_s)
```
